```python
import functools
import jax
import jax.numpy as jnp
from jax import lax
import numpy as np

D_MODEL = 1024
BATCH = 32
SEQ = 2048
DEPTH = 4

GRID_W = 64
CTX_LEN = 256
D_MIX = D_MODEL
GLA_W = D_MIX // 4
RET_W = D_MIX // 4
MLA_W = D_MIX - GLA_W - RET_W
GLA_DV = 64
GLA_HEADS = GLA_W // GLA_DV
GLA_DK = GLA_DV // 2
GLA_QK = GLA_HEADS * GLA_DK
GLA_GATE_RANK = 16
GLA_TAU = 16.0
RET_DV = 64
RET_HEADS = RET_W // RET_DV
RET_DK = RET_DV // 2
RET_QK = RET_HEADS * RET_DK
MLA_DV = 64
MLA_HEADS = MLA_W // MLA_DV
MLA_D_NOPE = 64
MLA_D_ROPE = 32
MLA_Q_RANK = D_MODEL // 4
MLA_KV_RANK = D_MODEL // 8
MLA_SCALE = (MLA_D_NOPE + MLA_D_ROPE) ** -0.5
CHUNK = 64
Q_BLOCK = 128
D_FF = 128 * ((8 * D_MODEL // 3 + 127) // 128)
CONV_W = 3
ROPE_BASE = 10000.0
EPS = 1e-6
ALPHA = (2 * DEPTH) ** 0.25
BETA = (8 * DEPTH) ** -0.25
ADA_INIT = 0.5
IN_SIZES = (GLA_QK, GLA_QK, GLA_W, 2 * GLA_GATE_RANK, GLA_W,
            RET_QK, RET_QK, RET_W, RET_W,
            MLA_Q_RANK, MLA_KV_RANK, MLA_D_ROPE)
D_IN = sum(IN_SIZES)

kernel_name = "hybrid_gla_retnet_mla_prefix_dit"


def _layer_norm(x, g, b):
    xf = x.astype(jnp.float32)
    mu = jnp.mean(xf, axis=-1, keepdims=True)
    var = jnp.mean(jnp.square(xf - mu), axis=-1, keepdims=True)
    return ((xf - mu) * lax.rsqrt(var + EPS) * g + b).astype(x.dtype)


def _rms_norm(x, g):
    xf = x.astype(jnp.float32)
    return xf * lax.rsqrt(jnp.mean(jnp.square(xf), axis=-1, keepdims=True) + EPS) * g


def _group_norm(x):
    xf = x.astype(jnp.float32)
    mu = jnp.mean(xf, axis=-1, keepdims=True)
    var = jnp.mean(jnp.square(xf - mu), axis=-1, keepdims=True)
    return (xf - mu) * lax.rsqrt(var + EPS)


def _heads(t, n):
    return t.reshape(t.shape[0], t.shape[1], n, -1)


def _merge(t):
    return t.reshape(t.shape[0], t.shape[1], -1)


def _rot_half(x, cos, sin):
    cos = cos.astype(x.dtype)
    sin = sin.astype(x.dtype)
    x1, x2 = jnp.split(x, 2, axis=-1)
    return jnp.concatenate([x1 * cos - x2 * sin, x1 * sin + x2 * cos], axis=-1)


def _axial_rope(x, row_cos, row_sin, col_cos, col_sin):
    xr, xc = jnp.split(x, 2, axis=-1)
    return jnp.concatenate([_rot_half(xr, row_cos, row_sin), _rot_half(xc, col_cos, col_sin)], axis=-1)


def _project(h, w_in):
    p = jnp.einsum('bld,de->ble', h, w_in)
    return jnp.split(p, np.cumsum(IN_SIZES)[:-1].tolist(), axis=-1)


def _to_chunks(t):
    b, l, h, d = t.shape
    return t.reshape(b, l // CHUNK, CHUNK, h, d).transpose(1, 0, 3, 2, 4)


def _from_chunks(t):
    n, b, h, c, d = t.shape
    return t.transpose(1, 0, 3, 2, 4).reshape(b, n * c, h, d)


def _gla_log_gate(lr, w2, b2):
    z = jnp.einsum('blr,re->ble', lr, w2) + b2
    return _heads(jax.nn.log_sigmoid(z.astype(jnp.float32)) / GLA_TAU, GLA_HEADS)


def _gla_scan(q, k, v, log_a, s0):
    f32 = jnp.float32
    qc, kc, vc, ac = tuple(_to_chunks(t.astype(f32)) for t in (q, k, v, log_a))
    tri = jnp.tril(jnp.ones((CHUNK, CHUNK), dtype=bool))[:, :, None]

    def step(s, inp):
        qi, ki, vi, ai = inp
        b = jnp.cumsum(ai, axis=2)
        diff = b[:, :, :, None, :] - b[:, :, None, :, :]
        decay = jnp.exp(jnp.where(tri, diff, -jnp.inf))
        att = jnp.einsum('bhtd,bhsd,bhtsd->bhts', qi, ki, decay)
        o = (jnp.einsum('bhtd,bhde->bhte', qi * jnp.exp(b), s)
             + jnp.einsum('bhts,bhse->bhte', att, vi))
        b_end = b[:, :, -1:, :]
        s = (s * jnp.exp(b_end)[:, :, 0, :, None]
             + jnp.einsum('bhsd,bhse->bhde', ki * jnp.exp(b_end - b), vi))
        return s, o

    s_fin, o = lax.scan(step, s0, (qc, kc, vc, ac))
    return _from_chunks(o), s_fin


def _ret_scan(log_g, q, k, v, s0):
    f32 = jnp.float32
    qc, kc, vc = tuple(_to_chunks(t.astype(f32)) for t in (q, k, v))
    idx = jnp.arange(CHUNK, dtype=f32)
    rel = idx[:, None] - idx[None, :]
    dmat = jnp.where(rel >= 0, jnp.exp(jnp.maximum(rel, 0.0) * log_g[:, None, None]), 0.0)
    q_dec = jnp.exp((idx + 1.0) * log_g[:, None])[..., None]
    k_dec = jnp.exp((CHUNK - 1.0 - idx) * log_g[:, None])[..., None]
    s_dec = jnp.exp(CHUNK * log_g)[:, None, None]

    def step(s, inp):
        qi, ki, vi = inp
        att = jnp.einsum('bhtd,bhsd->bhts', qi, ki) * dmat
        o = (jnp.einsum('bhtd,bhde->bhte', qi * q_dec, s)
             + jnp.einsum('bhts,bhse->bhte', att, vi))
        s = s * s_dec + jnp.einsum('bhsd,bhse->bhde', ki * k_dec, vi)
        return s, o

    s_fin, o = lax.scan(step, s0, (qc, kc, vc))
    return _from_chunks(o), s_fin


def _bidir_prefix(scan_f, scan_b, ctx_f, lat_f, ctx_b, lat_b, s0):
    flip = lambda ts: [jnp.flip(t, axis=1) for t in ts]
    oc_f, sc_f = scan_f(*ctx_f, s0)
    ol_f, _ = scan_f(*lat_f, sc_f)
    oc_b, sc_b = scan_b(*flip(ctx_b), s0)
    ol_b, _ = scan_b(*flip(lat_b), sc_b)
    return oc_f + jnp.flip(oc_b, axis=1), ol_f + jnp.flip(ol_b, axis=1)


def _mla_attend(qn, qr, kn, kr, v):
    s = (jnp.einsum('bqhd,bkhd->bhqk', qn, kn)
         + jnp.einsum('bqhr,bkr->bhqk', qr, kr)).astype(jnp.float32) * MLA_SCALE
    p = jax.nn.softmax(s, axis=-1).astype(v.dtype)
    return jnp.einsum('bhqk,bkhe->bqhe', p, v)


def _mla_blocks(qn, qr, kn, kr, v):
    b, s = qn.shape[0], qn.shape[1]
    nb = s // Q_BLOCK
    blk = lambda t: jnp.moveaxis(t.reshape(b, nb, Q_BLOCK, *t.shape[2:]), 1, 0)
    out = lax.map(lambda qs: _mla_attend(qs[0], qs[1], kn, kr, v), (blk(qn), blk(qr)))
    return jnp.moveaxis(out, 0, 1).reshape(b, s, MLA_HEADS, MLA_DV)


def _token_mixers(h_c, h_l, rope, w_in, gla_gate_w, gla_gate_b, gla_norm_g, ret_decay,
                  mla_q_norm_g, mla_kv_norm_g, mla_w_uq, mla_w_uk, mla_w_uv, need_ctx):
    ret_cos, ret_sin, row_cos, row_sin, col_cos, col_sin = rope
    b = h_l.shape[0]
    pc = _project(h_c, w_in)
    pl = _project(h_l, w_in)

    def gla_inputs(p):
        q = _heads(p[0], GLA_HEADS) * (GLA_DK ** -0.5)
        k = _heads(p[1], GLA_HEADS)
        v = _heads(p[2], GLA_HEADS)
        lr_f, lr_b = jnp.split(p[3], 2, axis=-1)
        a_f = _gla_log_gate(lr_f, gla_gate_w[0], gla_gate_b[0])
        a_b = _gla_log_gate(lr_b, gla_gate_w[1], gla_gate_b[1])
        return (q, k, v, a_f), (q, k, v, a_b)

    gc_f, gc_b = gla_inputs(pc)
    gl_f, gl_b = gla_inputs(pl)
    s0_gla = jnp.zeros((b, GLA_HEADS, GLA_DK, GLA_DV), jnp.float32)
    gla_c, gla_l = _bidir_prefix(_gla_scan, _gla_scan, gc_f, gl_f, gc_b, gl_b, s0_gla)
    gla_out = lambda o, p: _merge(_rms_norm(o, gla_norm_g) * jax.nn.silu(_heads(p[4], GLA_HEADS)))

    log_g = jax.nn.log_sigmoid(ret_decay.astype(jnp.float32))

    def ret_inputs(p, rotate):
        q = _heads(p[5], RET_HEADS)
        k = _heads(p[6], RET_HEADS) * (RET_DK ** -0.5)
        v = _heads(p[7], RET_HEADS)
        if rotate:
            q = _rot_half(q, ret_cos[:, None], ret_sin[:, None])
            k = _rot_half(k, ret_cos[:, None], ret_sin[:, None])
        return (q, k, v)

    rc = ret_inputs(pc, False)
    rl = ret_inputs(pl, True)
    s0_ret = jnp.zeros((b, RET_HEADS, RET_DK, RET_DV), jnp.float32)
    ret_c, ret_l = _bidir_prefix(functools.partial(_ret_scan, log_g[0]),
                                 functools.partial(_ret_scan, log_g[1]),
                                 rc, rl, rc, rl, s0_ret)
    ret_out = lambda o, p: _merge(_group_norm(o) * jax.nn.silu(_heads(p[8], RET_HEADS)))

    def mla_inputs(p, rotate):
        cq = _rms_norm(p[9], mla_q_norm_g)
        q = _heads(jnp.einsum('blr,re->ble', cq, mla_w_uq), MLA_HEADS)
        qn, qr = q[..., :MLA_D_NOPE], q[..., MLA_D_NOPE:]
        ckv = _rms_norm(p[10], mla_kv_norm_g)
        kn = _heads(jnp.einsum('blr,re->ble', ckv, mla_w_uk), MLA_HEADS)
        v = _heads(jnp.einsum('blr,re->ble', ckv, mla_w_uv), MLA_HEADS)
        kr = p[11]
        if rotate:
            qr = _axial_rope(qr, row_cos[:, None], row_sin[:, None], col_cos[:, None], col_sin[:, None])
            kr = _axial_rope(kr, row_cos, row_sin, col_cos, col_sin)
        return qn, qr, kn, kr, v

    qn_c, qr_c, kn_c, kr_c, v_c = mla_inputs(pc, False)
    qn_l, qr_l, kn_l, kr_l, v_l = mla_inputs(pl, True)
    mla_l = _mla_blocks(qn_l, qr_l,
                        jnp.concatenate([kn_l, kn_c], axis=1),
                        jnp.concatenate([kr_l, kr_c], axis=1),
                        jnp.concatenate([v_l, v_c], axis=1))
    m_l = jnp.concatenate([gla_out(gla_l, pl), ret_out(ret_l, pl), _merge(mla_l)], axis=-1).astype(h_l.dtype)
    if not need_ctx:
        return None, m_l
    mla_c = _mla_attend(qn_c, qr_c, kn_c, kr_c, v_c)
    m_c = jnp.concatenate([gla_out(gla_c, pc), ret_out(ret_c, pc), _merge(mla_c)], axis=-1).astype(h_c.dtype)
    return m_c, m_l


def _conv_ffn(h, w_up, conv_w, conv_b, w_down):
    u = jnp.einsum('bld,df->blf', h, w_up)
    l = u.shape[1]
    pad = CONV_W // 2
    up = jnp.pad(u, ((0, 0), (pad, pad), (0, 0)))
    u = sum(up[:, j:j + l] * conv_w[j] for j in range(CONV_W)) + conv_b
    a, g = jnp.split(u, 2, axis=-1)
    return jnp.einsum('blf,fd->bld', jax.nn.silu(a) * g, w_down)


def _fwd_setup_inputs(seed: int = 0) -> dict:
    key = jax.random.key(seed)
    ks = jax.random.split(key, 28)
    f32 = jnp.float32
    nrm = lambda k, shape, s: jax.random.normal(k, shape, f32) * s
    L = DEPTH
    ret_base = jnp.log(2.0 ** (5.0 + jnp.arange(RET_HEADS, dtype=f32)) - 1.0)
    return {
        "x": nrm(ks[0], (BATCH, SEQ, D_MODEL), 1.0),
        "c": nrm(ks[1], (BATCH, D_MODEL), 1.0),
        "ctx": nrm(ks[2], (BATCH, CTX_LEN, D_MODEL), 1.0),
        "c_ctx": nrm(ks[3], (D_MODEL,), 1.0),
        "ada_w": nrm(ks[4], (L, D_MODEL, 6 * D_MODEL), ADA_INIT * D_MODEL ** -0.5),
        "ada_b": nrm(ks[5], (L, 6 * D_MODEL), 0.01),
        "w_in": nrm(ks[6], (L, D_MODEL, D_IN), D_MODEL ** -0.5),
        "gla_gate_w": nrm(ks[7], (L, 2, GLA_GATE_RANK, GLA_QK), GLA_GATE_RANK ** -0.5),
        "gla_gate_b": nrm(ks[8], (L, 2, GLA_QK), 0.1),
        "gla_norm_g": 1.0 + nrm(ks[9], (L, GLA_DV), 0.02),
        "ret_decay": ret_base + nrm(ks[10], (L, 2, RET_HEADS), 0.1),
        "mla_q_norm_g": 1.0 + nrm(ks[11], (L, MLA_Q_RANK), 0.02),
        "mla_kv_norm_g": 1.0 + nrm(ks[12], (L, MLA_KV_RANK), 0.02),
        "mla_w_uq": nrm(ks[13], (L, MLA_Q_RANK, MLA_HEADS * (MLA_D_NOPE + MLA_D_ROPE)), MLA_Q_RANK ** -0.5),
        "mla_w_uk": nrm(ks[14], (L, MLA_KV_RANK, MLA_HEADS * MLA_D_NOPE), MLA_KV_RANK ** -0.5),
        "mla_w_uv": nrm(ks[15], (L, MLA_KV_RANK, MLA_HEADS * MLA_DV), MLA_KV_RANK ** -0.5),
        "w_out": nrm(ks[16], (L, D_MIX, D_MODEL), BETA * D_MIX ** -0.5),
        "ln1_g": 1.0 + nrm(ks[17], (L, D_MODEL), 0.02),
        "ln1_b": nrm(ks[18], (L, D_MODEL), 0.02),
        "ffn_up": nrm(ks[19], (L, D_MODEL, 2 * D_FF), D_MODEL ** -0.5),
        "ffn_conv_w": nrm(ks[20], (L, CONV_W, 2 * D_FF), CONV_W ** -0.5),
        "ffn_conv_b": nrm(ks[21], (L, 2 * D_FF), 0.01),
        "ffn_down": nrm(ks[22], (L, D_FF, D_MODEL), BETA * D_FF ** -0.5),
        "ln2_g": 1.0 + nrm(ks[23], (L, D_MODEL), 0.02),
        "ln2_b": nrm(ks[24], (L, D_MODEL), 0.02),
    }


def _fwd_reference(x, c, ctx, c_ctx, ada_w, ada_b, w_in, gla_gate_w, gla_gate_b, gla_norm_g, ret_decay,
              mla_q_norm_g, mla_kv_norm_g, mla_w_uq, mla_w_uk, mla_w_uv, w_out, ln1_g, ln1_b,
              ffn_up, ffn_conv_w, ffn_conv_b, ffn_down, ln2_g, ln2_b):
    f32 = jnp.float32
    seq = x.shape[1]
    rows_n = seq // GRID_W
    rows = jnp.repeat(jnp.arange(rows_n, dtype=f32), GRID_W)
    cols = jnp.tile(jnp.arange(GRID_W, dtype=f32), rows_n)
    pos = jnp.arange(seq, dtype=f32)
    ret_inv = 1.0 / (ROPE_BASE ** jnp.linspace(0.0, 1.0, RET_DK // 2, dtype=f32))
    ret_ang = pos[:, None] * ret_inv
    n_ax = MLA_D_ROPE // 4
    ax_inv = ROPE_BASE ** (-jnp.arange(n_ax, dtype=f32) / n_ax)
    row_ang = rows[:, None] * ax_inv
    col_ang = cols[:, None] * ax_inv
    rope = (jnp.cos(ret_ang), jnp.sin(ret_ang), jnp.cos(row_ang), jnp.sin(row_ang),
            jnp.cos(col_ang), jnp.sin(col_ang))

    for i in range(DEPTH):
        need_ctx = i < DEPTH - 1
        mod_l = jnp.einsum('bd,de->be', jax.nn.silu(c), ada_w[i]) + ada_b[i]
        mod_c = jnp.einsum('d,de->e', jax.nn.silu(c_ctx), ada_w[i]) + ada_b[i]
        sh1_l, sc1_l, g1_l, sh2_l, sc2_l, g2_l = [m[:, None, :] for m in jnp.split(mod_l, 6, axis=-1)]
        sh1_c, sc1_c, g1_c, sh2_c, sc2_c, g2_c = jnp.split(mod_c, 6, axis=-1)

        h_l = x * (1.0 + sc1_l) + sh1_l
        h_c = ctx * (1.0 + sc1_c) + sh1_c
        m_c, m_l = _token_mixers(h_c, h_l, rope, w_in[i], gla_gate_w[i], gla_gate_b[i], gla_norm_g[i],
                                 ret_decay[i], mla_q_norm_g[i], mla_kv_norm_g[i], mla_w_uq[i],
                                 mla_w_uk[i], mla_w_uv[i], need_ctx)
        x = _layer_norm(ALPHA * x + g1_l * jnp.einsum('ble,ed->bld', m_l, w_out[i]), ln1_g[i], ln1_b[i])
        f_l = _conv_ffn(x * (1.0 + sc2_l) + sh2_l, ffn_up[i], ffn_conv_w[i], ffn_conv_b[i], ffn_down[i])
        x = _layer_norm(ALPHA * x + g2_l * f_l, ln2_g[i], ln2_b[i])
        if need_ctx:
            ctx = _layer_norm(ALPHA * ctx + g1_c * jnp.einsum('ble,ed->bld', m_c, w_out[i]), ln1_g[i], ln1_b[i])
            f_c = _conv_ffn(ctx * (1.0 + sc2_c) + sh2_c, ffn_up[i], ffn_conv_w[i], ffn_conv_b[i], ffn_down[i])
            ctx = _layer_norm(ALPHA * ctx + g2_c * f_c, ln2_g[i], ln2_b[i])
    return x


import jax as _jax
import jax.numpy as _jnp

TWIN_FORMAT = 'train_step'
FWD_PARAMS = ['x', 'c', 'ctx', 'c_ctx', 'ada_w', 'ada_b', 'w_in', 'gla_gate_w', 'gla_gate_b', 'gla_norm_g', 'ret_decay', 'mla_q_norm_g', 'mla_kv_norm_g', 'mla_w_uq', 'mla_w_uk', 'mla_w_uv', 'w_out', 'ln1_g', 'ln1_b', 'ffn_up', 'ffn_conv_w', 'ffn_conv_b', 'ffn_down', 'ln2_g', 'ln2_b']
TWIN_WEIGHTS = ['c_ctx', 'ada_w', 'ada_b', 'w_in', 'gla_gate_w', 'gla_gate_b', 'gla_norm_g', 'ret_decay', 'mla_q_norm_g', 'mla_kv_norm_g', 'mla_w_uq', 'mla_w_uk', 'mla_w_uv', 'w_out', 'ln1_g', 'ln1_b', 'ffn_up', 'ffn_conv_w', 'ffn_conv_b', 'ffn_down', 'ln2_g', 'ln2_b']
TWIN_DIFF_INPUT = 'x'
TWIN_INPUTS = ['x', 'c', 'ctx', 'c_ctx', 'ada_w', 'ada_b', 'w_in', 'gla_gate_w', 'gla_gate_b', 'gla_norm_g', 'ret_decay', 'mla_q_norm_g', 'mla_kv_norm_g', 'mla_w_uq', 'mla_w_uk', 'mla_w_uv', 'w_out', 'ln1_g', 'ln1_b', 'ffn_up', 'ffn_conv_w', 'ffn_conv_b', 'ffn_down', 'ln2_g', 'ln2_b', 'loss_target', 'm_c_ctx', 'm_ada_w', 'm_ada_b', 'm_w_in', 'm_gla_gate_w', 'm_gla_gate_b', 'm_gla_norm_g', 'm_ret_decay', 'm_mla_q_norm_g', 'm_mla_kv_norm_g', 'm_mla_w_uq', 'm_mla_w_uk', 'm_mla_w_uv', 'm_w_out', 'm_ln1_g', 'm_ln1_b', 'm_ffn_up', 'm_ffn_conv_w', 'm_ffn_conv_b', 'm_ffn_down', 'm_ln2_g', 'm_ln2_b', 'v_c_ctx', 'v_ada_w', 'v_ada_b', 'v_w_in', 'v_gla_gate_w', 'v_gla_gate_b', 'v_gla_norm_g', 'v_ret_decay', 'v_mla_q_norm_g', 'v_mla_kv_norm_g', 'v_mla_w_uq', 'v_mla_w_uk', 'v_mla_w_uv', 'v_w_out', 'v_ln1_g', 'v_ln1_b', 'v_ffn_up', 'v_ffn_conv_w', 'v_ffn_conv_b', 'v_ffn_down', 'v_ln2_g', 'v_ln2_b']
TWIN_OUTPUTS = ['loss', 'grad_x', 'grad_c_ctx', 'grad_ada_w', 'grad_ada_b', 'grad_w_in', 'grad_gla_gate_w', 'grad_gla_gate_b', 'grad_gla_norm_g', 'grad_ret_decay', 'grad_mla_q_norm_g', 'grad_mla_kv_norm_g', 'grad_mla_w_uq', 'grad_mla_w_uk', 'grad_mla_w_uv', 'grad_w_out', 'grad_ln1_g', 'grad_ln1_b', 'grad_ffn_up', 'grad_ffn_conv_w', 'grad_ffn_conv_b', 'grad_ffn_down', 'grad_ln2_g', 'grad_ln2_b', 'delta_c_ctx', 'delta_ada_w', 'delta_ada_b', 'delta_w_in', 'delta_gla_gate_w', 'delta_gla_gate_b', 'delta_gla_norm_g', 'delta_ret_decay', 'delta_mla_q_norm_g', 'delta_mla_kv_norm_g', 'delta_mla_w_uq', 'delta_mla_w_uk', 'delta_mla_w_uv', 'delta_w_out', 'delta_ln1_g', 'delta_ln1_b', 'delta_ffn_up', 'delta_ffn_conv_w', 'delta_ffn_conv_b', 'delta_ffn_down', 'delta_ln2_g', 'delta_ln2_b', 'new_m_c_ctx', 'new_m_ada_w', 'new_m_ada_b', 'new_m_w_in', 'new_m_gla_gate_w', 'new_m_gla_gate_b', 'new_m_gla_norm_g', 'new_m_ret_decay', 'new_m_mla_q_norm_g', 'new_m_mla_kv_norm_g', 'new_m_mla_w_uq', 'new_m_mla_w_uk', 'new_m_mla_w_uv', 'new_m_w_out', 'new_m_ln1_g', 'new_m_ln1_b', 'new_m_ffn_up', 'new_m_ffn_conv_w', 'new_m_ffn_conv_b', 'new_m_ffn_down', 'new_m_ln2_g', 'new_m_ln2_b', 'new_v_c_ctx', 'new_v_ada_w', 'new_v_ada_b', 'new_v_w_in', 'new_v_gla_gate_w', 'new_v_gla_gate_b', 'new_v_gla_norm_g', 'new_v_ret_decay', 'new_v_mla_q_norm_g', 'new_v_mla_kv_norm_g', 'new_v_mla_w_uq', 'new_v_mla_w_uk', 'new_v_mla_w_uv', 'new_v_w_out', 'new_v_ln1_g', 'new_v_ln1_b', 'new_v_ffn_up', 'new_v_ffn_conv_w', 'new_v_ffn_conv_b', 'new_v_ffn_down', 'new_v_ln2_g', 'new_v_ln2_b']
TWIN_LEAF_KINDS = {'loss': 'loss', 'grad_x': 'grad_x', 'grad_c_ctx': 'grad_w', 'grad_ada_w': 'grad_w', 'grad_ada_b': 'grad_w', 'grad_w_in': 'grad_w', 'grad_gla_gate_w': 'grad_w', 'grad_gla_gate_b': 'grad_w', 'grad_gla_norm_g': 'grad_w', 'grad_ret_decay': 'grad_w', 'grad_mla_q_norm_g': 'grad_w', 'grad_mla_kv_norm_g': 'grad_w', 'grad_mla_w_uq': 'grad_w', 'grad_mla_w_uk': 'grad_w', 'grad_mla_w_uv': 'grad_w', 'grad_w_out': 'grad_w', 'grad_ln1_g': 'grad_w', 'grad_ln1_b': 'grad_w', 'grad_ffn_up': 'grad_w', 'grad_ffn_conv_w': 'grad_w', 'grad_ffn_conv_b': 'grad_w', 'grad_ffn_down': 'grad_w', 'grad_ln2_g': 'grad_w', 'grad_ln2_b': 'grad_w', 'delta_c_ctx': 'delta_w', 'delta_ada_w': 'delta_w', 'delta_ada_b': 'delta_w', 'delta_w_in': 'delta_w', 'delta_gla_gate_w': 'delta_w', 'delta_gla_gate_b': 'delta_w', 'delta_gla_norm_g': 'delta_w', 'delta_ret_decay': 'delta_w', 'delta_mla_q_norm_g': 'delta_w', 'delta_mla_kv_norm_g': 'delta_w', 'delta_mla_w_uq': 'delta_w', 'delta_mla_w_uk': 'delta_w', 'delta_mla_w_uv': 'delta_w', 'delta_w_out': 'delta_w', 'delta_ln1_g': 'delta_w', 'delta_ln1_b': 'delta_w', 'delta_ffn_up': 'delta_w', 'delta_ffn_conv_w': 'delta_w', 'delta_ffn_conv_b': 'delta_w', 'delta_ffn_down': 'delta_w', 'delta_ln2_g': 'delta_w', 'delta_ln2_b': 'delta_w', 'new_m_c_ctx': 'new_m', 'new_m_ada_w': 'new_m', 'new_m_ada_b': 'new_m', 'new_m_w_in': 'new_m', 'new_m_gla_gate_w': 'new_m', 'new_m_gla_gate_b': 'new_m', 'new_m_gla_norm_g': 'new_m', 'new_m_ret_decay': 'new_m', 'new_m_mla_q_norm_g': 'new_m', 'new_m_mla_kv_norm_g': 'new_m', 'new_m_mla_w_uq': 'new_m', 'new_m_mla_w_uk': 'new_m', 'new_m_mla_w_uv': 'new_m', 'new_m_w_out': 'new_m', 'new_m_ln1_g': 'new_m', 'new_m_ln1_b': 'new_m', 'new_m_ffn_up': 'new_m', 'new_m_ffn_conv_w': 'new_m', 'new_m_ffn_conv_b': 'new_m', 'new_m_ffn_down': 'new_m', 'new_m_ln2_g': 'new_m', 'new_m_ln2_b': 'new_m', 'new_v_c_ctx': 'new_v', 'new_v_ada_w': 'new_v', 'new_v_ada_b': 'new_v', 'new_v_w_in': 'new_v', 'new_v_gla_gate_w': 'new_v', 'new_v_gla_gate_b': 'new_v', 'new_v_gla_norm_g': 'new_v', 'new_v_ret_decay': 'new_v', 'new_v_mla_q_norm_g': 'new_v', 'new_v_mla_kv_norm_g': 'new_v', 'new_v_mla_w_uq': 'new_v', 'new_v_mla_w_uk': 'new_v', 'new_v_mla_w_uv': 'new_v', 'new_v_w_out': 'new_v', 'new_v_ln1_g': 'new_v', 'new_v_ln1_b': 'new_v', 'new_v_ffn_up': 'new_v', 'new_v_ffn_conv_w': 'new_v', 'new_v_ffn_conv_b': 'new_v', 'new_v_ffn_down': 'new_v', 'new_v_ln2_g': 'new_v', 'new_v_ln2_b': 'new_v'}


def _forward(args):
    return _fwd_reference(*[args[k] for k in FWD_PARAMS])


def _output_shape():
    out = _jax.eval_shape(lambda: _forward(_fwd_setup_inputs(0)))
    return out.shape, out.dtype

N_MICROBATCH = 1
ADAM_LR = 0.001
ADAM_B1 = 0.9
ADAM_B2 = 0.999
ADAM_EPS = 1e-08
ADAM_WD = 0.01
ADAM_STEP = 10
PER_EXAMPLE_BATCH_AXIS = {'x': 0, 'c': 0, 'ctx': 0, 'loss_target': 0}
SHARED_INPUTS = []
_WEIGHT_DTYPES = {'c_ctx': _jnp.float32, 'ada_w': _jnp.float32, 'ada_b': _jnp.float32, 'w_in': _jnp.float32, 'gla_gate_w': _jnp.float32, 'gla_gate_b': _jnp.float32, 'gla_norm_g': _jnp.float32, 'ret_decay': _jnp.float32, 'mla_q_norm_g': _jnp.float32, 'mla_kv_norm_g': _jnp.float32, 'mla_w_uq': _jnp.float32, 'mla_w_uk': _jnp.float32, 'mla_w_uv': _jnp.float32, 'w_out': _jnp.float32, 'ln1_g': _jnp.float32, 'ln1_b': _jnp.float32, 'ffn_up': _jnp.float32, 'ffn_conv_w': _jnp.float32, 'ffn_conv_b': _jnp.float32, 'ffn_down': _jnp.float32, 'ln2_g': _jnp.float32, 'ln2_b': _jnp.float32}
MOMENT_SCALE = {'c_ctx': 8.472362e-03, 'ada_w': 1.883217e-02, 'ada_b': 3.016223e-02, 'w_in': 1.446092e-02, 'gla_gate_w': 2.293683e-03, 'gla_gate_b': 6.059715e-03, 'gla_norm_g': 2.739126e-02, 'ret_decay': 3.432027e-02, 'mla_q_norm_g': 2.986757e-03, 'mla_kv_norm_g': 1.465217e-02, 'mla_w_uq': 1.711477e-03, 'mla_w_uk': 1.855676e-03, 'mla_w_uv': 8.316282e-03, 'w_out': 2.470911e-02, 'ln1_g': 2.018725e+00, 'ln1_b': 7.591032e-01, 'ffn_up': 8.925437e-03, 'ffn_conv_w': 8.832568e-03, 'ffn_conv_b': 8.624091e-03, 'ffn_down': 3.480278e-02, 'ln2_g': 3.214565e+01, 'ln2_b': 1.497402e+00}


def _to_microbatches(a, axis):
    t = _jnp.moveaxis(a, axis, 0)
    t = t.reshape((N_MICROBATCH, t.shape[0] // N_MICROBATCH) + t.shape[1:])
    return _jnp.moveaxis(t, 1, axis + 1)


def setup_inputs(seed: int = 0) -> dict:
    inp = _fwd_setup_inputs(seed)
    key = _jax.random.fold_in(_jax.random.key(seed), 7919)
    shape, _ = _output_shape()
    out = dict(inp)
    out["loss_target"] = _jax.random.normal(_jax.random.fold_in(key, 0), shape, _jnp.float32)
    for i, name in enumerate(TWIN_WEIGHTS):
        w = inp[name].astype(_jnp.float32)
        if MOMENT_SCALE is None:
            s = _jnp.sqrt(_jnp.mean(_jnp.square(w)) + 1e-30)
        else:
            s = MOMENT_SCALE[name]
        km, kv = _jax.random.split(_jax.random.fold_in(key, i + 1))
        out[name] = w
        out["m_" + name] = s * _jax.random.normal(km, w.shape, _jnp.float32)
        out["v_" + name] = (s * s) * _jax.random.uniform(kv, w.shape, _jnp.float32, 0.5, 1.5)
    if N_MICROBATCH > 1:
        for name, axis in PER_EXAMPLE_BATCH_AXIS.items():
            out[name] = _to_microbatches(out[name], axis)
    return {'x': out['x'], 'c': out['c'], 'ctx': out['ctx'], 'c_ctx': out['c_ctx'], 'ada_w': out['ada_w'], 'ada_b': out['ada_b'], 'w_in': out['w_in'], 'gla_gate_w': out['gla_gate_w'], 'gla_gate_b': out['gla_gate_b'], 'gla_norm_g': out['gla_norm_g'], 'ret_decay': out['ret_decay'], 'mla_q_norm_g': out['mla_q_norm_g'], 'mla_kv_norm_g': out['mla_kv_norm_g'], 'mla_w_uq': out['mla_w_uq'], 'mla_w_uk': out['mla_w_uk'], 'mla_w_uv': out['mla_w_uv'], 'w_out': out['w_out'], 'ln1_g': out['ln1_g'], 'ln1_b': out['ln1_b'], 'ffn_up': out['ffn_up'], 'ffn_conv_w': out['ffn_conv_w'], 'ffn_conv_b': out['ffn_conv_b'], 'ffn_down': out['ffn_down'], 'ln2_g': out['ln2_g'], 'ln2_b': out['ln2_b'], 'loss_target': out['loss_target'], 'm_c_ctx': out['m_c_ctx'], 'm_ada_w': out['m_ada_w'], 'm_ada_b': out['m_ada_b'], 'm_w_in': out['m_w_in'], 'm_gla_gate_w': out['m_gla_gate_w'], 'm_gla_gate_b': out['m_gla_gate_b'], 'm_gla_norm_g': out['m_gla_norm_g'], 'm_ret_decay': out['m_ret_decay'], 'm_mla_q_norm_g': out['m_mla_q_norm_g'], 'm_mla_kv_norm_g': out['m_mla_kv_norm_g'], 'm_mla_w_uq': out['m_mla_w_uq'], 'm_mla_w_uk': out['m_mla_w_uk'], 'm_mla_w_uv': out['m_mla_w_uv'], 'm_w_out': out['m_w_out'], 'm_ln1_g': out['m_ln1_g'], 'm_ln1_b': out['m_ln1_b'], 'm_ffn_up': out['m_ffn_up'], 'm_ffn_conv_w': out['m_ffn_conv_w'], 'm_ffn_conv_b': out['m_ffn_conv_b'], 'm_ffn_down': out['m_ffn_down'], 'm_ln2_g': out['m_ln2_g'], 'm_ln2_b': out['m_ln2_b'], 'v_c_ctx': out['v_c_ctx'], 'v_ada_w': out['v_ada_w'], 'v_ada_b': out['v_ada_b'], 'v_w_in': out['v_w_in'], 'v_gla_gate_w': out['v_gla_gate_w'], 'v_gla_gate_b': out['v_gla_gate_b'], 'v_gla_norm_g': out['v_gla_norm_g'], 'v_ret_decay': out['v_ret_decay'], 'v_mla_q_norm_g': out['v_mla_q_norm_g'], 'v_mla_kv_norm_g': out['v_mla_kv_norm_g'], 'v_mla_w_uq': out['v_mla_w_uq'], 'v_mla_w_uk': out['v_mla_w_uk'], 'v_mla_w_uv': out['v_mla_w_uv'], 'v_w_out': out['v_w_out'], 'v_ln1_g': out['v_ln1_g'], 'v_ln1_b': out['v_ln1_b'], 'v_ffn_up': out['v_ffn_up'], 'v_ffn_conv_w': out['v_ffn_conv_w'], 'v_ffn_conv_b': out['v_ffn_conv_b'], 'v_ffn_down': out['v_ffn_down'], 'v_ln2_g': out['v_ln2_g'], 'v_ln2_b': out['v_ln2_b']}


def _loss(weights, diff, rest, loss_target):
    with _jax.named_scope("forward"):
        args = {**rest, TWIN_DIFF_INPUT: diff, **{k: w.astype(_WEIGHT_DTYPES[k]) for k, w in weights.items()}}
        y = _forward(args)
    with _jax.named_scope("loss_head"):
        err = _jnp.square(y.astype(_jnp.float32) - loss_target)
        return 0.5 * _jnp.sum(_jnp.mean(err, axis=-1)) if err.ndim else 0.5 * err


def _adamw(w, g, m, v):
    m = ADAM_B1 * m + (1.0 - ADAM_B1) * g
    v = ADAM_B2 * v + (1.0 - ADAM_B2) * _jnp.square(g)
    m_hat = m / (1.0 - ADAM_B1 ** ADAM_STEP)
    v_hat = v / (1.0 - ADAM_B2 ** ADAM_STEP)
    delta = -ADAM_LR * (m_hat / (_jnp.sqrt(v_hat) + ADAM_EPS) + ADAM_WD * w)
    return delta, m, v


def reference(x, c, ctx, c_ctx, ada_w, ada_b, w_in, gla_gate_w, gla_gate_b, gla_norm_g, ret_decay, mla_q_norm_g, mla_kv_norm_g, mla_w_uq, mla_w_uk, mla_w_uv, w_out, ln1_g, ln1_b, ffn_up, ffn_conv_w, ffn_conv_b, ffn_down, ln2_g, ln2_b, loss_target, m_c_ctx, m_ada_w, m_ada_b, m_w_in, m_gla_gate_w, m_gla_gate_b, m_gla_norm_g, m_ret_decay, m_mla_q_norm_g, m_mla_kv_norm_g, m_mla_w_uq, m_mla_w_uk, m_mla_w_uv, m_w_out, m_ln1_g, m_ln1_b, m_ffn_up, m_ffn_conv_w, m_ffn_conv_b, m_ffn_down, m_ln2_g, m_ln2_b, v_c_ctx, v_ada_w, v_ada_b, v_w_in, v_gla_gate_w, v_gla_gate_b, v_gla_norm_g, v_ret_decay, v_mla_q_norm_g, v_mla_kv_norm_g, v_mla_w_uq, v_mla_w_uk, v_mla_w_uv, v_w_out, v_ln1_g, v_ln1_b, v_ffn_up, v_ffn_conv_w, v_ffn_conv_b, v_ffn_down, v_ln2_g, v_ln2_b):
    given = dict(x=x, c=c, ctx=ctx, c_ctx=c_ctx, ada_w=ada_w, ada_b=ada_b, w_in=w_in, gla_gate_w=gla_gate_w, gla_gate_b=gla_gate_b, gla_norm_g=gla_norm_g, ret_decay=ret_decay, mla_q_norm_g=mla_q_norm_g, mla_kv_norm_g=mla_kv_norm_g, mla_w_uq=mla_w_uq, mla_w_uk=mla_w_uk, mla_w_uv=mla_w_uv, w_out=w_out, ln1_g=ln1_g, ln1_b=ln1_b, ffn_up=ffn_up, ffn_conv_w=ffn_conv_w, ffn_conv_b=ffn_conv_b, ffn_down=ffn_down, ln2_g=ln2_g, ln2_b=ln2_b, loss_target=loss_target, m_c_ctx=m_c_ctx, m_ada_w=m_ada_w, m_ada_b=m_ada_b, m_w_in=m_w_in, m_gla_gate_w=m_gla_gate_w, m_gla_gate_b=m_gla_gate_b, m_gla_norm_g=m_gla_norm_g, m_ret_decay=m_ret_decay, m_mla_q_norm_g=m_mla_q_norm_g, m_mla_kv_norm_g=m_mla_kv_norm_g, m_mla_w_uq=m_mla_w_uq, m_mla_w_uk=m_mla_w_uk, m_mla_w_uv=m_mla_w_uv, m_w_out=m_w_out, m_ln1_g=m_ln1_g, m_ln1_b=m_ln1_b, m_ffn_up=m_ffn_up, m_ffn_conv_w=m_ffn_conv_w, m_ffn_conv_b=m_ffn_conv_b, m_ffn_down=m_ffn_down, m_ln2_g=m_ln2_g, m_ln2_b=m_ln2_b, v_c_ctx=v_c_ctx, v_ada_w=v_ada_w, v_ada_b=v_ada_b, v_w_in=v_w_in, v_gla_gate_w=v_gla_gate_w, v_gla_gate_b=v_gla_gate_b, v_gla_norm_g=v_gla_norm_g, v_ret_decay=v_ret_decay, v_mla_q_norm_g=v_mla_q_norm_g, v_mla_kv_norm_g=v_mla_kv_norm_g, v_mla_w_uq=v_mla_w_uq, v_mla_w_uk=v_mla_w_uk, v_mla_w_uv=v_mla_w_uv, v_w_out=v_w_out, v_ln1_g=v_ln1_g, v_ln1_b=v_ln1_b, v_ffn_up=v_ffn_up, v_ffn_conv_w=v_ffn_conv_w, v_ffn_conv_b=v_ffn_conv_b, v_ffn_down=v_ffn_down, v_ln2_g=v_ln2_g, v_ln2_b=v_ln2_b)
    weights = {n: given[n] for n in TWIN_WEIGHTS}
    shared = {n: given[n] for n in SHARED_INPUTS}
    per_example = {n: given[n] for n in ['x', 'c', 'ctx']}
    grad_fn = _jax.value_and_grad(_loss, argnums=(0, 1))

    def one_microbatch(ex, loss_target):
        ex = dict(ex)
        diff = ex.pop(TWIN_DIFF_INPUT)
        return grad_fn(weights, diff, {**shared, **ex}, loss_target)

    if N_MICROBATCH == 1:
        loss, (grad_w, grad_x) = one_microbatch(per_example, given["loss_target"])
    else:
        def body(carry, xs):
            loss_sum, grad_sum = carry
            l_k, (gw_k, gx_k) = one_microbatch(xs[0], xs[1])
            with _jax.named_scope("update"):
                return (loss_sum + l_k, _jax.tree.map(_jnp.add, grad_sum, gw_k)), gx_k

        init = (_jnp.zeros((), _jnp.float32), _jax.tree.map(_jnp.zeros_like, weights))
        (loss, grad_w), grad_x = _jax.lax.scan(body, init, (per_example, given["loss_target"]))
    with _jax.named_scope("update"):
        delta_w, new_m, new_v = {}, {}, {}
        for n in TWIN_WEIGHTS:
            delta_w[n], new_m[n], new_v[n] = _adamw(weights[n], grad_w[n], given["m_" + n], given["v_" + n])
    return (loss, grad_x, *[grad_w[n] for n in TWIN_WEIGHTS], *[delta_w[n] for n in TWIN_WEIGHTS],
            *[new_m[n] for n in TWIN_WEIGHTS], *[new_v[n] for n in TWIN_WEIGHTS])
```

```python
import functools
import math

import numpy as np
import jax
import jax.numpy as jnp
from jax import lax
from jax.experimental import pallas as pl
from jax.experimental.pallas import tpu as pltpu

F32 = jnp.float32
BF16 = jnp.bfloat16
HI = lax.Precision.HIGHEST
MESH = pl.DeviceIdType.MESH

N_DEV = 8
D = 1024
DEPTH = 4
CHUNK = 64
EPS = 1e-6
ALPHA = (2 * DEPTH) ** 0.25
GLA_TAU = 16.0
ROPE_BASE = 10000.0
MLA_SCALE = 96 ** -0.5
D_FF = 2816
FF_CHUNK = 1408
P_PAD = 2048
VMEM_LIMIT_BYTES = 56 << 20

ADAM_LR, ADAM_B1, ADAM_B2, ADAM_EPS, ADAM_WD, ADAM_STEP = 0.001, 0.9, 0.999, 1e-08, 0.01, 10

D_IN = 1984
W_IN_SEGS = [(0, 512), (544, 1408), (512, 32), (1952, 32)]
W_IN_INV_SEGS = [(0, 512), (1920, 32), (512, 1408), (1952, 32)]
FF_SEGS = [(0, FF_CHUNK), (D_FF, FF_CHUNK), (FF_CHUNK, FF_CHUNK), (D_FF + FF_CHUNK, FF_CHUNK)]


def _cols(a, segs):
    return jnp.concatenate([a[..., s:s + n] for s, n in segs], axis=-1)


def _pad_heads(w, per_head):
    r = w.shape[0]
    w = w.reshape(r, 8, per_head)
    return jnp.concatenate([w, jnp.zeros((r, 8, 128 - per_head), w.dtype)], axis=2).reshape(r, 1024)


def _unpad_heads(g, per_head):
    r = g.shape[0]
    return g.reshape(r, 8, 128)[:, :, :per_head].reshape(r, 8 * per_head)


def _cparams(sem=None):
    return pltpu.CompilerParams(vmem_limit_bytes=VMEM_LIMIT_BYTES, dimension_semantics=sem)


@jax.custom_vjp
def bdot(a, w):
    return jnp.dot(a.astype(BF16), w.astype(BF16), preferred_element_type=F32)


def _bdot_fwd(a, w):
    return bdot(a, w), (a, w)


def _bdot_bwd(res, ct):
    a, w = res
    ctb = ct.astype(BF16)
    da = lax.dot_general(ctb, w.astype(BF16), (((1,), (1,)), ((), ())), preferred_element_type=F32)
    dw = lax.dot_general(a.astype(BF16), ctb, (((0,), (0,)), ((), ())), preferred_element_type=F32)
    return da.astype(a.dtype), dw.astype(w.dtype)


bdot.defvjp(_bdot_fwd, _bdot_bwd)


def hdot(a, b):
    return jnp.dot(a, b, precision=HI, preferred_element_type=F32)


def _iota(shape, axis):
    return lax.broadcasted_iota(jnp.int32, shape, axis)


def _group_avg(n, g):
    return (_iota((n, n), 0) // g == _iota((n, n), 1) // g).astype(F32) * (1.0 / g)


def _swap_matrix(n, half):
    i, j = _iota((n, n), 0), _iota((n, n), 1)
    partner = jnp.where((j // half) % 2 == 0, j + half, j - half)
    return (i == partner).astype(F32)


def _silu(x):
    return x * jax.nn.sigmoid(x)


def _layer_norm(z, g, b):
    mu = jnp.mean(z, axis=-1, keepdims=True)
    zc = z - mu
    var = jnp.mean(zc * zc, axis=-1, keepdims=True)
    return zc * lax.rsqrt(var + EPS) * g + b


def _rms(x, g):
    return x * lax.rsqrt(jnp.mean(x * x, axis=-1, keepdims=True) + EPS) * g


def mm(name, a, w, out_dtype, tn, tr, mod=None, sel=None):
    n, k = a.shape
    nw = w.shape[1]

    def body(*refs):
        if mod is not None:
            a_ref, m_ref, w_ref, o_ref = refs
            m = m_ref[0]
            av = a_ref[...] * (1.0 + m[sel[0]:sel[0] + 1]) + m[sel[1]:sel[1] + 1]
        else:
            a_ref, w_ref, o_ref = refs
            av = a_ref[...]
        o_ref[...] = jnp.dot(av.astype(BF16), w_ref[...], preferred_element_type=F32).astype(o_ref.dtype)

    in_specs = [pl.BlockSpec((tr, k), lambda c, j: (j, 0))]
    args = [a]
    if mod is not None:
        in_specs.append(pl.BlockSpec((1, 8, k), lambda c, j: (j, 0, 0)))
        args.append(mod)
    in_specs.append(pl.BlockSpec((k, tn), lambda c, j: (0, c)))
    args.append(w)
    return pl.pallas_call(
        body, name=name, grid=(nw // tn, n // tr), in_specs=in_specs,
        out_specs=pl.BlockSpec((tr, tn), lambda c, j: (j, c)),
        out_shape=jax.ShapeDtypeStruct((n, nw), out_dtype), compiler_params=_cparams(("parallel", "arbitrary")),
    )(*args)


def mm_tn(name, a, dc, tn, tr, mod=None, sel=None):
    n, k = a.shape
    nw = dc.shape[1]

    def body(*refs):
        if mod is not None:
            a_ref, m_ref, d_ref, o_ref = refs
            m = m_ref[0]
            av = a_ref[...] * (1.0 + m[sel[0]:sel[0] + 1]) + m[sel[1]:sel[1] + 1]
        else:
            a_ref, d_ref, o_ref = refs
            av = a_ref[...]

        @pl.when(pl.program_id(1) == 0)
        def _():
            o_ref[...] = jnp.zeros_like(o_ref)

        o_ref[...] += lax.dot_general(av.astype(BF16), d_ref[...].astype(BF16), (((0,), (0,)), ((), ())),
                                      preferred_element_type=F32)

    in_specs = [pl.BlockSpec((tr, k), lambda c, j: (j, 0))]
    args = [a]
    if mod is not None:
        in_specs.append(pl.BlockSpec((1, 8, k), lambda c, j: (j, 0, 0)))
        args.append(mod)
    in_specs.append(pl.BlockSpec((tr, tn), lambda c, j: (j, c)))
    args.append(dc)
    return pl.pallas_call(
        body, name=name, grid=(nw // tn, n // tr), in_specs=in_specs,
        out_specs=pl.BlockSpec((k, tn), lambda c, j: (0, c)),
        out_shape=jax.ShapeDtypeStruct((k, nw), F32), compiler_params=_cparams(("parallel", "arbitrary")),
    )(*args)


def mm_modbwd(name, dc, wt, x, mod, add, sel, tr):
    n, k = dc.shape
    dm = wt.shape[1]

    def body(dc_ref, wt_ref, x_ref, m_ref, add_ref, dx_ref, dm_ref):
        dh = jnp.dot(dc_ref[...].astype(BF16), wt_ref[...], preferred_element_type=F32)
        m = m_ref[0]
        dx_ref[...] = add_ref[...] + dh * (1.0 + m[sel[0]:sel[0] + 1])
        dsc = jnp.sum(dh * x_ref[...], axis=0, keepdims=True)
        dsh = jnp.sum(dh, axis=0, keepdims=True)
        rows = _iota((8, dm), 0)
        dm_ref[0] = jnp.where(rows == sel[0], dsc, 0.0) + jnp.where(rows == sel[1], dsh, 0.0)

    return pl.pallas_call(
        body, name=name, grid=(n // tr,),
        in_specs=[pl.BlockSpec((tr, k), lambda j: (j, 0)), pl.BlockSpec((k, dm), lambda j: (0, 0)),
                  pl.BlockSpec((tr, dm), lambda j: (j, 0)), pl.BlockSpec((1, 8, dm), lambda j: (j, 0, 0)),
                  pl.BlockSpec((tr, dm), lambda j: (j, 0))],
        out_specs=[pl.BlockSpec((tr, dm), lambda j: (j, 0)), pl.BlockSpec((1, 8, dm), lambda j: (j, 0, 0))],
        out_shape=[jax.ShapeDtypeStruct((n, dm), F32), jax.ShapeDtypeStruct((n // tr, 8, dm), F32)],
        compiler_params=_cparams(("arbitrary",)),
    )(dc, wt, x, mod, add)


def _spec(kind, arr, width, cb, tr, tps):
    if kind == "row":
        return pl.BlockSpec((tr, width), lambda j: (j, cb))
    if kind == "pos":
        return pl.BlockSpec((tr, width), lambda j: (j % tps, cb))
    if kind == "tile":
        return pl.BlockSpec((1,) + arr.shape[1:], lambda j: (j, 0, 0))
    if kind == "par":
        return pl.BlockSpec(arr.shape, lambda j: (0, 0))
    raise ValueError(kind)


def _load(kind, ref):
    v = ref[0] if kind == "tile" else ref[...]
    return v.astype(F32)


def tile_fwd(name, fn, ins, outs, n, tr, tps):
    kinds = [i[0] for i in ins]

    def body(*refs):
        vals = [_load(k, r) for k, r in zip(kinds, refs[:len(ins)])]
        res = fn(*vals)
        for r, o in zip(refs[len(ins):], res):
            r[...] = o.astype(r.dtype)

    return pl.pallas_call(
        body, name=name, grid=(n // tr,),
        in_specs=[_spec(k, a, w, cb, tr, tps) for k, a, w, cb in ins],
        out_specs=[pl.BlockSpec((tr, w), lambda j: (j, 0)) for w, _ in outs],
        out_shape=[jax.ShapeDtypeStruct((n, w), dt) for w, dt in outs],
        compiler_params=_cparams(("arbitrary",)),
    )(*[i[1] for i in ins])


def tile_bwd(name, fn, ins, cots, want, n, tr, tps):
    kinds = [i[0] for i in ins]
    widx = [w[0] for w in want]
    ni, nc = len(ins), len(cots)

    def body(*refs):
        vals = [_load(k, r) for k, r in zip(kinds, refs[:ni])]
        cvals = tuple(r[...].astype(F32) for r in refs[ni:ni + nc])

        def f(*dv):
            full = list(vals)
            for i, v in zip(widx, dv):
                full[i] = v
            return tuple(fn(*full))

        _, vjp = jax.vjp(f, *[vals[i] for i in widx])
        grads = vjp(cvals)
        first = pl.program_id(0) == 0
        for r, g, i in zip(refs[ni + nc:], grads, widx):
            if kinds[i] == "par":
                @pl.when(first)
                def _(r=r):
                    r[...] = jnp.zeros_like(r)
                r[...] += g
            elif kinds[i] == "tile":
                r[0] = g.astype(r.dtype)
            else:
                r[...] = g.astype(r.dtype)

    out_specs, out_shape = [], []
    for i, dt in want:
        k, a, w, cb = ins[i]
        if k == "par":
            out_specs.append(pl.BlockSpec(a.shape, lambda j: (0, 0)))
            out_shape.append(jax.ShapeDtypeStruct(a.shape, F32))
        elif k == "tile":
            out_specs.append(pl.BlockSpec((1,) + a.shape[1:], lambda j: (j, 0, 0)))
            out_shape.append(jax.ShapeDtypeStruct(a.shape, F32))
        else:
            out_specs.append(pl.BlockSpec((tr, w), lambda j: (j, 0)))
            out_shape.append(jax.ShapeDtypeStruct((n, w), dt))
    return pl.pallas_call(
        body, name=name, grid=(n // tr,),
        in_specs=[_spec(k, a, w, cb, tr, tps) for k, a, w, cb in ins]
        + [pl.BlockSpec((tr, c.shape[1]), lambda j: (j, 0)) for c in cots],
        out_specs=out_specs, out_shape=out_shape, compiler_params=_cparams(("arbitrary",)),
    )(*[i[1] for i in ins], *cots)


def _rope(x, cos, sin, swap):
    return x * cos + hdot(x, swap) * sin


def pre_fn(p_gq, p_rq, p_rk, p_cq, p_ckv, p_misc, rcos, rsin, mcos, msin,
           w2f, w2b, b2f, b2b, retf, retb, qg, kvg, wuq, wuk, wuv, e2):
    tr = p_gq.shape[0]
    gq = p_gq * (32 ** -0.5)
    af = jax.nn.log_sigmoid(hdot(p_misc, w2f) + b2f) * (1.0 / GLA_TAU)
    ab = jax.nn.log_sigmoid(hdot(p_misc, w2b) + b2b) * (1.0 / GLA_TAU)
    arf = jnp.zeros((tr, 128), F32) + retf
    arb = jnp.zeros((tr, 128), F32) + retb
    sw32 = _swap_matrix(128, 16)
    rq = _rope(p_rq, rcos, rsin, sw32)
    rk = _rope(p_rk * (32 ** -0.5), rcos, rsin, sw32)
    sw16 = _swap_matrix(128, 8)
    qp = bdot(_rms(p_cq, qg), wuq) * MLA_SCALE
    ckvn = _rms(p_ckv, kvg)
    kp = bdot(ckvn, wuk) + hdot(p_misc, e2)
    qs = [_rope(qp[:, h * 128:(h + 1) * 128], mcos, msin, sw16) for h in range(8)]
    ks = [_rope(kp[:, h * 128:(h + 1) * 128], mcos, msin, sw16) for h in range(8)]
    v = bdot(ckvn, wuv)
    return gq, af, ab, arf, arb, rq, rk, jnp.concatenate(qs, axis=1), jnp.concatenate(ks, axis=1), v


def post_fn(ogf, ogb, orf, orb, om, gg, rg, x, mod, gng, wout, lng, lnb):
    avg = _group_avg(256, 64)
    og = ogf + ogb
    mg = og * lax.rsqrt(hdot(og * og, avg) + EPS) * gng * _silu(gg)
    orr = orf + orb
    oc = orr - hdot(orr, avg)
    mr = oc * lax.rsqrt(hdot(oc * oc, avg) + EPS) * _silu(rg)
    m = jnp.concatenate([mg, mr, om], axis=1)
    y = bdot(m, wout)
    return (_layer_norm(ALPHA * x + mod[2:3] * y, lng, lnb),)


def ln2_fn(x1, f, mod, lng, lnb):
    return (_layer_norm(ALPHA * x1 + mod[5:6] * f, lng, lnb),)


def scan_step(q, k, v, a, s, rev):
    ii, jj = _iota((CHUNK, CHUNK), 0), _iota((CHUNK, CHUNK), 1)
    tri = ((jj >= ii) if rev else (jj <= ii)).astype(F32)
    b = hdot(tri, a)
    btot = jnp.sum(a, axis=0, keepdims=True)
    qe = q * jnp.exp(b - btot)
    ke = k * jnp.exp(btot - b)
    lane = _iota((1, 128), 1)
    q4 = jnp.concatenate([qe * (lane // 32 == h).astype(F32) for h in range(4)], axis=0)
    att = lax.dot_general(q4, ke, (((1,), (1,)), ((), ())), precision=HI, preferred_element_type=F32)
    att = jnp.where(jnp.concatenate([tri] * 4, axis=0) > 0, att, 0.0)
    r = hdot(att, v)
    col = _iota((1, 256), 1)
    o = hdot(q * jnp.exp(b), s)
    for h in range(4):
        o = o + r[h * CHUNK:(h + 1) * CHUNK] * (col // 64 == h).astype(F32)
    dg = (_iota((128, 128), 0) == _iota((128, 128), 1)).astype(F32) * jnp.exp(btot)
    kv = lax.dot_general(ke, v, (((0,), (0,)), ((), ())), precision=HI, preferred_element_type=F32)
    bd = (_iota((128, 256), 0) // 32 == _iota((128, 256), 1) // 64).astype(F32)
    return o, hdot(dg, s) + kv * bd


def _chunk_maps(nch, nctx):
    def fwd(b, s):
        return b * nch + s

    def bwd(b, s):
        return b * nch + jnp.where(s < nctx, nctx - 1 - s, nch - 1 - (s - nctx))
    return fwd, bwd


def scan_fwd(name, q, k, v, af, ab, nb, nch, nctx):
    n = af.shape[0]
    fmap, bmap = _chunk_maps(nch, nctx)

    def body(qf, kf, vf, a_f, qb, kb, vb, a_b, of_ref, ob_ref, stf_ref, stb_ref, s_scr):
        @pl.when(pl.program_id(1) == 0)
        def _():
            s_scr[...] = jnp.zeros_like(s_scr)

        stf_ref[0] = s_scr[0]
        stb_ref[0] = s_scr[1]
        o, sn = scan_step(qf[...], kf[...], vf[...], a_f[...], s_scr[0], False)
        of_ref[...] = o
        s_scr[0] = sn
        o, sn = scan_step(qb[...], kb[...], vb[...], a_b[...], s_scr[1], True)
        ob_ref[...] = o
        s_scr[1] = sn

    def specs(m):
        return [pl.BlockSpec((CHUNK, w), lambda b, s, cb=cb: (m(b, s), cb)) for _, w, cb in (q, k, v)] + \
               [pl.BlockSpec((CHUNK, 128), lambda b, s: (m(b, s), 0))]

    return pl.pallas_call(
        body, name=name, grid=(nb, nch), in_specs=specs(fmap) + specs(bmap),
        out_specs=[pl.BlockSpec((CHUNK, 256), lambda b, s: (fmap(b, s), 0)),
                   pl.BlockSpec((CHUNK, 256), lambda b, s: (bmap(b, s), 0)),
                   pl.BlockSpec((1, 128, 256), lambda b, s: (b * nch + s, 0, 0)),
                   pl.BlockSpec((1, 128, 256), lambda b, s: (b * nch + s, 0, 0))],
        out_shape=[jax.ShapeDtypeStruct((n, 256), F32)] * 2 + [jax.ShapeDtypeStruct((nb * nch, 128, 256), F32)] * 2,
        scratch_shapes=[pltpu.VMEM((2, 128, 256), F32)], compiler_params=_cparams(("arbitrary", "arbitrary")),
    )(q[0], k[0], v[0], af, q[0], k[0], v[0], ab)


def scan_bwd(name, q, k, v, af, ab, stf, stb, do, nb, nch, nctx):
    n = af.shape[0]
    fmap0, bmap0 = _chunk_maps(nch, nctx)
    fmap = lambda b, r: fmap0(b, nch - 1 - r)
    bmap = lambda b, r: bmap0(b, nch - 1 - r)

    def body(qf, kf, vf, a_f, sf, dof, qb, kb, vb, a_b, sb, dob,
             dqf, dkf, dvf, daf, dqb, dkb, dvb, dab, ds_scr):
        @pl.when(pl.program_id(1) == 0)
        def _():
            ds_scr[...] = jnp.zeros_like(ds_scr)

        for d, (qr, kr, vr, ar, sr, dor, outs) in enumerate(((qf, kf, vf, a_f, sf, dof, (dqf, dkf, dvf, daf)),
                                                               (qb, kb, vb, a_b, sb, dob, (dqb, dkb, dvb, dab)))):
            _, vjp = jax.vjp(functools.partial(scan_step, rev=bool(d)), qr[...], kr[...], vr[...], ar[...], sr[0])
            dq, dk, dv, da, ds = vjp((dor[...], ds_scr[d]))
            outs[0][...] = dq
            outs[1][...] = dk
            outs[2][...] = dv
            outs[3][...] = da
            ds_scr[d] = ds

    def specs(m):
        return [pl.BlockSpec((CHUNK, w), lambda b, r, cb=cb: (m(b, r), cb)) for _, w, cb in (q, k, v)] + \
               [pl.BlockSpec((CHUNK, 128), lambda b, r: (m(b, r), 0)),
                pl.BlockSpec((1, 128, 256), lambda b, r: (b * nch + nch - 1 - r, 0, 0)),
                pl.BlockSpec((CHUNK, 256), lambda b, r: (m(b, r), 0))]

    def ospecs(m):
        return [pl.BlockSpec((CHUNK, w), lambda b, r: (m(b, r), 0)) for w in (128, 128, 256, 128)]

    oshape = [jax.ShapeDtypeStruct((n, w), F32) for w in (128, 128, 256, 128)]
    return pl.pallas_call(
        body, name=name, grid=(nb, nch), in_specs=specs(fmap) + specs(bmap),
        out_specs=ospecs(fmap) + ospecs(bmap), out_shape=oshape + oshape,
        scratch_shapes=[pltpu.VMEM((2, 128, 256), F32)], compiler_params=_cparams(("arbitrary", "arbitrary")),
    )(q[0], k[0], v[0], af, stf, do, q[0], k[0], v[0], ab, stb, do)


def _softmax_rows(s):
    m = jnp.max(s, axis=-1, keepdims=True)
    e = jnp.exp(s - m)
    return e, jnp.sum(e, axis=-1, keepdims=True)


def mla_fwd(name, qa, ka, va, nb, tps, tr, nctx_rows):
    n = qa.shape[0]
    t = tps * tr

    def body(q_ref, k_ref, v_ref, o_ref):
        def attend(nk):
            outs = []
            for h in range(2):
                s = lax.dot_general(q_ref[:, h * 128:(h + 1) * 128], k_ref[0:nk, h * 128:(h + 1) * 128],
                                    (((1,), (1,)), ((), ())), preferred_element_type=F32)
                e, l = _softmax_rows(s)
                outs.append(jnp.dot(e.astype(BF16), v_ref[0:nk, h * 64:(h + 1) * 64], preferred_element_type=F32) / l)
            o_ref[...] = jnp.concatenate(outs, axis=1)

        @pl.when(pl.program_id(2) == 0)
        def _():
            attend(nctx_rows)

        @pl.when(pl.program_id(2) > 0)
        def _():
            attend(t)

    return pl.pallas_call(
        body, name=name, grid=(nb, 4, tps),
        in_specs=[pl.BlockSpec((tr, 256), lambda b, h, j: (b * tps + j, h)), pl.BlockSpec((t, 256), lambda b, h, j: (b, h)),
                  pl.BlockSpec((t, 128), lambda b, h, j: (b, h))],
        out_specs=pl.BlockSpec((tr, 128), lambda b, h, j: (b * tps + j, h)),
        out_shape=jax.ShapeDtypeStruct((n, 512), F32), compiler_params=_cparams(("parallel", "parallel", "arbitrary")),
    )(qa, ka, va)


def mla_bwd(name, qa, ka, va, do, nb, tps, tr, nctx_rows):
    n = qa.shape[0]
    t = tps * tr

    def body(q_ref, k_ref, v_ref, do_ref, dq_ref, dk_ref, dv_ref):
        @pl.when(pl.program_id(2) == 0)
        def _():
            dk_ref[...] = jnp.zeros_like(dk_ref)
            dv_ref[...] = jnp.zeros_like(dv_ref)

        def attend(nk):
            dqs, dvs = [], []
            for h in range(2):
                hs, vs = slice(h * 128, (h + 1) * 128), slice(h * 64, (h + 1) * 64)
                qh, kh, vh = q_ref[:, hs], k_ref[0:nk, hs], v_ref[0:nk, vs]
                doh = do_ref[:, vs].astype(BF16)
                s = lax.dot_general(qh, kh, (((1,), (1,)), ((), ())), preferred_element_type=F32)
                e, l = _softmax_rows(s)
                p = e / l
                dp = lax.dot_general(doh, vh, (((1,), (1,)), ((), ())), preferred_element_type=F32)
                ds = (p * (dp - jnp.sum(p * dp, axis=-1, keepdims=True))).astype(BF16)
                dqs.append(jnp.dot(ds, kh, preferred_element_type=F32))
                dk_ref[0:nk, hs] += lax.dot_general(ds, qh, (((0,), (0,)), ((), ())), preferred_element_type=F32)
                dvs.append(lax.dot_general(p.astype(BF16), doh, (((0,), (0,)), ((), ())), preferred_element_type=F32))
            dq_ref[...] = jnp.concatenate(dqs, axis=1)
            dv_ref[0:nk, :] += jnp.concatenate(dvs, axis=1)

        @pl.when(pl.program_id(2) == 0)
        def _():
            attend(nctx_rows)

        @pl.when(pl.program_id(2) > 0)
        def _():
            attend(t)

    return pl.pallas_call(
        body, name=name, grid=(nb, 4, tps),
        in_specs=[pl.BlockSpec((tr, 256), lambda b, h, j: (b * tps + j, h)), pl.BlockSpec((t, 256), lambda b, h, j: (b, h)),
                  pl.BlockSpec((t, 128), lambda b, h, j: (b, h)), pl.BlockSpec((tr, 128), lambda b, h, j: (b * tps + j, h))],
        out_specs=[pl.BlockSpec((tr, 256), lambda b, h, j: (b * tps + j, h)), pl.BlockSpec((t, 256), lambda b, h, j: (b, h)),
                   pl.BlockSpec((t, 128), lambda b, h, j: (b, h))],
        out_shape=[jax.ShapeDtypeStruct((n, 1024), F32), jax.ShapeDtypeStruct((n, 1024), F32),
                   jax.ShapeDtypeStruct((n, 512), F32)],
        compiler_params=_cparams(("parallel", "parallel", "arbitrary")),
    )(qa, ka, va, do)


HALO = 16


def _halo_specs(tr, width, tps, nt):
    r = tr // HALO
    return [pl.BlockSpec((tr, width), lambda j, c: (j, c)),
            pl.BlockSpec((HALO, width), lambda j, c: (jnp.maximum(j * r - 1, 0), c)),
            pl.BlockSpec((HALO, width), lambda j, c: (jnp.minimum((j + 1) * r, nt * r - 1), c))]


def _shifted(u, prev, nxt, j, tps):
    tr = u.shape[0]
    t = j % tps
    has_prev = (t >= 2).astype(F32)
    has_next = jnp.logical_and(t >= 1, t <= tps - 2).astype(F32)
    rows = _iota(u.shape, 0)
    dn = jnp.where(rows == 0, prev[HALO - 1:HALO] * has_prev, pltpu.roll(u, 1, 0))
    up = jnp.where(rows == tr - 1, nxt[0:1] * has_next, pltpu.roll(u, tr - 1, 0))
    return dn, up


def _ffn_act(ucv):
    return _silu(ucv[:, :FF_CHUNK]) * ucv[:, FF_CHUNK:]


def ffn2_fwd(name, u, cw, cb, wd, x1, mod, lng, lnb, tr, tps):
    n = u.shape[0]
    nt = n // tr
    w2 = 2 * FF_CHUNK

    def body(u_ref, up_ref, un_ref, cw_ref, cb_ref, wd_ref, x1_ref, m_ref, g_ref, b_ref, f_ref, x2_ref, acc):
        j, c = pl.program_id(0), pl.program_id(1)
        uu = u_ref[...].astype(F32)
        dn, up = _shifted(uu, up_ref[...].astype(F32), un_ref[...].astype(F32), j, tps)
        cwv = cw_ref[...]
        ucv = cwv[0:1] * dn + cwv[1:2] * uu + cwv[2:3] * up + cb_ref[...]
        part = bdot(_ffn_act(ucv), wd_ref[...])

        @pl.when(c == 0)
        def _():
            acc[...] = part

        @pl.when(c == 1)
        def _():
            f = acc[...] + part
            f_ref[...] = f
            x2_ref[...] = ln2_fn(x1_ref[...], f, m_ref[0], g_ref[...], b_ref[...])[0]

    return pl.pallas_call(
        body, name=name, grid=(nt, 2),
        in_specs=_halo_specs(tr, w2, tps, nt) + [
            pl.BlockSpec((8, w2), lambda j, c: (0, c)), pl.BlockSpec((1, w2), lambda j, c: (0, c)),
            pl.BlockSpec((FF_CHUNK, D), lambda j, c: (c, 0)), pl.BlockSpec((tr, D), lambda j, c: (j, 0)),
            pl.BlockSpec((1, 8, D), lambda j, c: (j, 0, 0)), pl.BlockSpec((1, D), lambda j, c: (0, 0)),
            pl.BlockSpec((1, D), lambda j, c: (0, 0))],
        out_specs=[pl.BlockSpec((tr, D), lambda j, c: (j, 0)), pl.BlockSpec((tr, D), lambda j, c: (j, 0))],
        out_shape=[jax.ShapeDtypeStruct((n, D), F32)] * 2, scratch_shapes=[pltpu.VMEM((tr, D), F32)],
        compiler_params=_cparams(("arbitrary", "arbitrary")),
    )(u, u, u, cw, cb, wd, x1, mod, lng, lnb)


def ffn2_bwd(name, u, cw, cb, wd, df, tr, tps):
    n = u.shape[0]
    nt = n // tr
    w2 = 2 * FF_CHUNK

    def body(u_ref, up_ref, un_ref, cw_ref, cb_ref, wd_ref, df_ref, ducv_ref, dwd_ref):
        c, j = pl.program_id(0), pl.program_id(1)
        uu = u_ref[...].astype(F32)
        dn, up = _shifted(uu, up_ref[...].astype(F32), un_ref[...].astype(F32), j, tps)
        cwv = cw_ref[...]
        ucv = cwv[0:1] * dn + cwv[1:2] * uu + cwv[2:3] * up + cb_ref[...]
        _, vjp = jax.vjp(lambda z, w: bdot(_ffn_act(z), w), ucv, wd_ref[...].astype(F32))
        ducv, dwd = vjp(df_ref[...])
        ducv_ref[...] = ducv.astype(ducv_ref.dtype)

        @pl.when(j == 0)
        def _():
            dwd_ref[...] = jnp.zeros_like(dwd_ref)

        dwd_ref[...] += dwd

    hs = _halo_specs(tr, w2, tps, nt)
    swap = lambda spec: pl.BlockSpec(spec.block_shape, lambda c, j, f=spec.index_map: f(j, c))
    return pl.pallas_call(
        body, name=name, grid=(2, nt),
        in_specs=[swap(s) for s in hs] + [
            pl.BlockSpec((8, w2), lambda c, j: (0, c)), pl.BlockSpec((1, w2), lambda c, j: (0, c)),
            pl.BlockSpec((FF_CHUNK, D), lambda c, j: (c, 0)), pl.BlockSpec((tr, D), lambda c, j: (j, 0))],
        out_specs=[pl.BlockSpec((tr, w2), lambda c, j: (j, c)), pl.BlockSpec((FF_CHUNK, D), lambda c, j: (c, 0))],
        out_shape=[jax.ShapeDtypeStruct((n, 2 * w2), BF16), jax.ShapeDtypeStruct((D_FF, D), F32)],
        compiler_params=_cparams(("parallel", "arbitrary")),
    )(u, u, u, cw, cb, wd, df)


def conv_bwd(name, ducv, u, cw, tr, tps):
    n = u.shape[0]
    nt = n // tr
    w2 = 2 * FF_CHUNK

    def body(g_ref, gp_ref, gn_ref, u_ref, up_ref, un_ref, cw_ref, du_ref, dcw_ref, dcb_ref):
        c, j = pl.program_id(0), pl.program_id(1)
        g = g_ref[...].astype(F32)
        gdn, gup = _shifted(g, gp_ref[...].astype(F32), gn_ref[...].astype(F32), j, tps)
        uu = u_ref[...].astype(F32)
        udn, uup = _shifted(uu, up_ref[...].astype(F32), un_ref[...].astype(F32), j, tps)
        cwv = cw_ref[...]
        du_ref[...] = (cwv[0:1] * gup + cwv[1:2] * g + cwv[2:3] * gdn).astype(du_ref.dtype)
        rows = _iota((8, w2), 0)
        s = lambda z: jnp.sum(z, axis=0, keepdims=True)
        dcw = (jnp.where(rows == 0, s(g * udn), 0.0) + jnp.where(rows == 1, s(g * uu), 0.0)
               + jnp.where(rows == 2, s(g * uup), 0.0))

        @pl.when(j == 0)
        def _():
            dcw_ref[...] = jnp.zeros_like(dcw_ref)
            dcb_ref[...] = jnp.zeros_like(dcb_ref)

        dcw_ref[...] += dcw
        dcb_ref[...] += s(g)

    hs = _halo_specs(tr, w2, tps, nt)
    swap = lambda spec: pl.BlockSpec(spec.block_shape, lambda c, j, f=spec.index_map: f(j, c))
    return pl.pallas_call(
        body, name=name, grid=(2, nt),
        in_specs=[swap(s) for s in hs] * 2 + [pl.BlockSpec((8, w2), lambda c, j: (0, c))],
        out_specs=[pl.BlockSpec((tr, w2), lambda c, j: (j, c)), pl.BlockSpec((8, w2), lambda c, j: (0, c)),
                   pl.BlockSpec((1, w2), lambda c, j: (0, c))],
        out_shape=[jax.ShapeDtypeStruct((n, 2 * w2), BF16), jax.ShapeDtypeStruct((8, 2 * w2), F32),
                   jax.ShapeDtypeStruct((1, 2 * w2), F32)],
        compiler_params=_cparams(("parallel", "arbitrary")),
    )(ducv, ducv, ducv, u, u, u, cw)


def loss_head(name, xf, target, nb, tps, tr):
    n = xf.shape[0]

    def body(x_ref, t_ref, dy_ref, l_ref):
        lat = (pl.program_id(0) % tps > 0).astype(F32)
        err = (x_ref[...] - t_ref[...]) * lat
        dy_ref[...] = err * (1.0 / D)
        l_ref[...] = jnp.zeros_like(l_ref) + 0.5 * jnp.sum(err * err) * (1.0 / D)

    def tmap(j):
        return ((j // tps) * (tps - 1) + jnp.maximum(j % tps - 1, 0), 0)

    return pl.pallas_call(
        body, name=name, grid=(n // tr,),
        in_specs=[pl.BlockSpec((tr, D), lambda j: (j, 0)), pl.BlockSpec((tr, D), tmap)],
        out_specs=[pl.BlockSpec((tr, D), lambda j: (j, 0)), pl.BlockSpec((1, 8, 128), lambda j: (j, 0, 0))],
        out_shape=[jax.ShapeDtypeStruct((n, D), F32), jax.ShapeDtypeStruct((n // tr, 8, 128), F32)],
        compiler_params=_cparams(("arbitrary",)),
    )(xf, target)


ADAM_MAX_ROWS = 512


def adamw(name, w, m, v, g8):
    r, c = w.shape
    k = g8.shape[0]
    rows = max(b for b in range(8, ADAM_MAX_ROWS + 1, 8) if r % b == 0)
    bc1 = 1.0 - ADAM_B1 ** ADAM_STEP
    bc2 = 1.0 - ADAM_B2 ** ADAM_STEP

    def body(w_ref, m_ref, v_ref, g_ref, go_ref, d_ref, mo_ref, vo_ref):
        g = g_ref[0].astype(F32)
        for i in range(1, k):
            g = g + g_ref[i].astype(F32)
        mn = ADAM_B1 * m_ref[...] + (1.0 - ADAM_B1) * g
        vn = ADAM_B2 * v_ref[...] + (1.0 - ADAM_B2) * (g * g)
        go_ref[...] = g
        mo_ref[...] = mn
        vo_ref[...] = vn
        d_ref[...] = -ADAM_LR * ((mn / bc1) / (jnp.sqrt(vn / bc2) + ADAM_EPS) + ADAM_WD * w_ref[...])

    blk = pl.BlockSpec((rows, c), lambda i: (i, 0))
    return pl.pallas_call(
        body, name=name, grid=(r // rows,),
        in_specs=[blk, blk, blk, pl.BlockSpec((k, rows, c), lambda i: (0, i, 0))],
        out_specs=[blk] * 4, out_shape=[jax.ShapeDtypeStruct((r, c), F32)] * 4,
        compiler_params=_cparams(("parallel",)),
    )(w, m, v, g8)


def ada_fwd(name, s, aw, ab):
    nl, _, cw = aw.shape

    def body(s_ref, w_ref, b_ref, o_ref):
        o_ref[0] = hdot(s_ref[...], w_ref[0]) + b_ref[0]

    return pl.pallas_call(
        body, name=name, grid=(nl,),
        in_specs=[pl.BlockSpec(s.shape, lambda l: (0, 0)), pl.BlockSpec((1, D, cw), lambda l: (l, 0, 0)),
                  pl.BlockSpec((1, 1, cw), lambda l: (l, 0, 0))],
        out_specs=pl.BlockSpec((1, s.shape[0], cw), lambda l: (l, 0, 0)),
        out_shape=jax.ShapeDtypeStruct((nl, s.shape[0], cw), F32), compiler_params=_cparams(("arbitrary",)),
    )(s, aw, ab)


def ada_bwd(name, s, aw, dmod):
    nl, _, cw = aw.shape

    def body(s_ref, w_ref, d_ref, dw_ref, ds_ref):
        dw_ref[0] = lax.dot_general(s_ref[...], d_ref[0], (((0,), (0,)), ((), ())), precision=HI,
                                    preferred_element_type=F32)
        ds_ref[0] = lax.dot_general(d_ref[0], w_ref[0], (((1,), (1,)), ((), ())), precision=HI,
                                    preferred_element_type=F32)

    return pl.pallas_call(
        body, name=name, grid=(nl,),
        in_specs=[pl.BlockSpec(s.shape, lambda l: (0, 0)), pl.BlockSpec((1, D, cw), lambda l: (l, 0, 0)),
                  pl.BlockSpec((1, s.shape[0], cw), lambda l: (l, 0, 0))],
        out_specs=[pl.BlockSpec((1, D, cw), lambda l: (l, 0, 0)), pl.BlockSpec((1, s.shape[0], D), lambda l: (l, 0, 0))],
        out_shape=[jax.ShapeDtypeStruct((nl, D, cw), F32), jax.ShapeDtypeStruct((nl, s.shape[0], D), F32)],
        compiler_params=_cparams(("arbitrary",)),
    )(s, aw, dmod)


def _place():
    return lax.axis_index("x"), lax.axis_index("y"), lax.axis_index("c")


def all_gather(name, x, in_vmem):
    r, c = x.shape

    def body(x_ref, out_ref, send_sems, recv_sems, local_sem):
        px, py, pc = _place()
        me, sibling = (px, py, pc), (px, py, 1 - pc)
        chips = [(1 - px, py), (px, 1 - py), (1 - px, 1 - py)]

        def rows(qx, qy, qc):
            return out_ref.at[pl.ds((4 * qx + 2 * qy + qc) * r, r), :]

        def copy(k, block, to, src=None):
            return pltpu.make_async_remote_copy(
                src_ref=rows(*block) if src is None else src, dst_ref=rows(*block),
                send_sem=send_sems.at[k], recv_sem=recv_sems.at[k], device_id=to, device_id_type=MESH)

        mine = pltpu.make_async_copy(x_ref, rows(*me), local_sem)
        mine.start()
        first = [copy(0, me, sibling, src=x_ref)]
        first += [copy(1 + j, me, (*chip, pc), src=x_ref) for j, chip in enumerate(chips)]
        for cp in first:
            cp.start()
        passed = [copy(4 + j, (*chip, pc), sibling) for j, chip in enumerate(chips)]
        for j, chip in enumerate(chips):
            copy(1 + j, (*chip, pc), me).wait_recv()
            passed[j].start()
        copy(0, sibling, me).wait_recv()
        for j, chip in enumerate(chips):
            copy(4 + j, (*chip, 1 - pc), me).wait_recv()
        for cp in first + passed:
            cp.wait_send()
        mine.wait()

    space = pltpu.VMEM if in_vmem else pl.ANY
    return pl.pallas_call(
        body, name=name, out_shape=jax.ShapeDtypeStruct((N_DEV * r, c), x.dtype),
        in_specs=[pl.BlockSpec(memory_space=space)], out_specs=pl.BlockSpec(memory_space=space),
        scratch_shapes=[pltpu.SemaphoreType.DMA((7,)), pltpu.SemaphoreType.DMA((7,)), pltpu.SemaphoreType.DMA],
        compiler_params=pltpu.CompilerParams(vmem_limit_bytes=VMEM_LIMIT_BYTES),
    )(x)


def all_to_all(name, x):
    _, r, c = x.shape

    def body(x_ref, out_ref, send_sems, recv_sems, local_sem):
        px, py, pc = _place()
        my = 4 * px + 2 * py + pc
        mine = pltpu.make_async_copy(x_ref.at[my], out_ref.at[my], local_sem)
        mine.start()
        copies = []
        for k in range(1, N_DEV):
            qx, qy, qc = px ^ (k >> 2 & 1), py ^ (k >> 1 & 1), pc ^ (k & 1)
            copies.append(pltpu.make_async_remote_copy(
                src_ref=x_ref.at[4 * qx + 2 * qy + qc], dst_ref=out_ref.at[my],
                send_sem=send_sems.at[k - 1], recv_sem=recv_sems.at[k - 1],
                device_id=(qx, qy, qc), device_id_type=MESH))
        for cp in copies:
            cp.start()
        for k, cp in enumerate(copies):
            cp.wait_send()
        for k in range(1, N_DEV):
            qx, qy, qc = px ^ (k >> 2 & 1), py ^ (k >> 1 & 1), pc ^ (k & 1)
            q = 4 * qx + 2 * qy + qc
            pltpu.make_async_remote_copy(
                src_ref=x_ref.at[q], dst_ref=out_ref.at[q], send_sem=send_sems.at[k - 1],
                recv_sem=recv_sems.at[k - 1], device_id=(qx, qy, qc), device_id_type=MESH).wait_recv()
        mine.wait()

    return pl.pallas_call(
        body, name=name, out_shape=jax.ShapeDtypeStruct(x.shape, x.dtype),
        in_specs=[pl.BlockSpec(memory_space=pl.ANY)], out_specs=pl.BlockSpec(memory_space=pl.ANY),
        scratch_shapes=[pltpu.SemaphoreType.DMA((7,)), pltpu.SemaphoreType.DMA((7,)), pltpu.SemaphoreType.DMA],
    )(x)


def _tables(seq, nctx_rows):
    f32 = np.float32
    pos = np.arange(seq, dtype=f32)
    ret_inv = (1.0 / (ROPE_BASE ** np.linspace(0.0, 1.0, 16, dtype=f32))).astype(f32)
    ang = pos[:, None] * ret_inv
    rc, rs = np.cos(ang).astype(f32), np.sin(ang).astype(f32)
    rcos = np.tile(np.concatenate([rc, rc], 1), (1, 4))
    rsin = np.tile(np.concatenate([-rs, rs], 1), (1, 4))
    rows = np.repeat(np.arange(seq // 64, dtype=f32), 64)
    cols = np.tile(np.arange(64, dtype=f32), seq // 64)
    ax_inv = (ROPE_BASE ** (-np.arange(8, dtype=f32) / 8)).astype(f32)
    ra, ca = rows[:, None] * ax_inv, cols[:, None] * ax_inv
    one, zero = np.ones((seq, 64), f32), np.zeros((seq, 64), f32)
    mcos = np.concatenate([one, np.cos(ra), np.cos(ra), np.cos(ca), np.cos(ca), one[:, :32]], 1)
    msin = np.concatenate([zero, -np.sin(ra), np.sin(ra), -np.sin(ca), np.sin(ca), zero[:, :32]], 1)
    ident = lambda t, v: np.concatenate([np.full((nctx_rows, 128), v, f32), t.astype(f32)], 0)
    return [jnp.asarray(ident(rcos, 1.0)), jnp.asarray(ident(rsin, 0.0)),
            jnp.asarray(ident(mcos, 1.0)), jnp.asarray(ident(msin, 0.0))]


def _prep_layer(w, l):
    z = lambda *s: jnp.zeros(s, F32)
    p = {}
    win = _cols(w["w_in"][l], W_IN_SEGS)
    p["w_in"] = jnp.concatenate([win, jnp.zeros((D, P_PAD - D_IN), win.dtype)], axis=1)
    p["w_in_t"] = p["w_in"].T
    p["w_up"] = _cols(w["ffn_up"][l], FF_SEGS)
    p["w_up_t"] = p["w_up"].T
    p["w_down"] = w["ffn_down"][l]
    p["w_out"] = w["w_out"][l]
    p["wuq"] = _pad_heads(w["mla_w_uq"][l], 96)
    p["wuk"] = _pad_heads(w["mla_w_uk"][l], 64)
    p["wuv"] = w["mla_w_uv"][l]
    gw = w["gla_gate_w"][l]
    p["w2f"] = z(128, 128).at[0:16].set(gw[0])
    p["w2b"] = z(128, 128).at[16:32].set(gw[1])
    p["b2f"], p["b2b"] = w["gla_gate_b"][l][0:1], w["gla_gate_b"][l][1:2]
    lg = jax.nn.log_sigmoid(w["ret_decay"][l])
    p["retf"], p["retb"] = jnp.repeat(lg[0], 32)[None], jnp.repeat(lg[1], 32)[None]
    p["qg"], p["kvg"] = w["mla_q_norm_g"][l][None], w["mla_kv_norm_g"][l][None]
    p["gng"] = jnp.tile(w["gla_norm_g"][l], 4)[None]
    p["ln1g"], p["ln1b"] = w["ln1_g"][l][None], w["ln1_b"][l][None]
    p["ln2g"], p["ln2b"] = w["ln2_g"][l][None], w["ln2_b"][l][None]
    p["cw"] = jnp.concatenate([_cols(w["ffn_conv_w"][l], FF_SEGS), z(5, 2 * D_FF)], axis=0)
    p["cb"] = _cols(w["ffn_conv_b"][l], FF_SEGS)[None]
    e2 = np.zeros((128, 1024), np.float32)
    for h in range(8):
        e2[32 + np.arange(32), h * 128 + 64 + np.arange(32)] = 1.0
    p["e2"] = jnp.asarray(e2)
    return p


def _pre_ins(pa, tabs, p):
    row = lambda w, cb: ("row", pa, w, cb)
    return [row(128, 0), row(128, 6), row(128, 7), row(256, 6), row(128, 14), row(128, 15)] + \
           [("pos", t, 128, 0) for t in tabs] + \
           [("par", p[k], 0, 0) for k in ("w2f", "w2b", "b2f", "b2b", "retf", "retb", "qg", "kvg", "wuq", "wuk", "wuv", "e2")]


_PRE_OUTS = [(128, F32)] * 7 + [(1024, BF16), (1024, BF16), (512, BF16)]
_PRE_WANT = [(i, F32) for i in range(6)] + [(i, F32) for i in range(10, 21)]


def _post_ins(ogf, ogb, orf, orb, om, pa, x, mod, p):
    return [("row", ogf, 256, 0), ("row", ogb, 256, 0), ("row", orf, 256, 0), ("row", orb, 256, 0),
            ("row", om, 512, 0), ("row", pa, 256, 2), ("row", pa, 256, 5), ("row", x, D, 0), ("tile", mod, 0, 0),
            ("par", p["gng"], 0, 0), ("par", p["w_out"], 0, 0), ("par", p["ln1g"], 0, 0), ("par", p["ln1b"], 0, 0)]


def layer_fwd(l, x, mod, p, tabs, dims):
    nb, tps, tr, nch, nctx = dims
    n = x.shape[0]
    pa = mm("proj", x, p["w_in"], F32, P_PAD, tr, mod=mod, sel=(1, 0))
    gq, af, ab, arf, arb, rq, rk, qa, ka, va = tile_fwd("mix_pre", pre_fn, _pre_ins(pa, tabs, p), _PRE_OUTS, n, tr, tps)
    ogf, ogb, gstf, gstb = scan_fwd("gla_scan", (gq, 128, 0), (pa, 128, 1), (pa, 256, 1), af, ab, nb, nch, nctx)
    orf, orb, rstf, rstb = scan_fwd("ret_scan", (rq, 128, 0), (rk, 128, 0), (pa, 256, 4), arf, arb, nb, nch, nctx)
    om = mla_fwd("mla_attn", qa, ka, va, nb, tps, tr, nctx * CHUNK)
    (x1,) = tile_fwd("mix_post", post_fn, _post_ins(ogf, ogb, orf, orb, om, pa, x, mod, p), [(D, F32)], n, tr, tps)
    u = mm("ffn_up", x1, p["w_up"], BF16, FF_CHUNK, tr, mod=mod, sel=(4, 3))
    f, x2 = ffn2_fwd("ffn_down", u, p["cw"], p["cb"], p["w_down"], x1, mod, p["ln2g"], p["ln2b"], tr, tps)
    saved = dict(x=x, pa=pa, gq=gq, af=af, ab=ab, arf=arf, arb=arb, rq=rq, rk=rk, qa=qa, ka=ka, va=va,
                 ogf=ogf, ogb=ogb, gstf=gstf, gstb=gstb, orf=orf, orb=orb, rstf=rstf, rstb=rstb, om=om,
                 x1=x1, u=u, f=f)
    return x2, saved


def layer_bwd(l, dx2, s, mod, p, tabs, dims):
    nb, tps, tr, nch, nctx = dims
    n = dx2.shape[0]
    g = {}
    ln2_ins = [("row", s["x1"], D, 0), ("row", s["f"], D, 0), ("tile", mod, 0, 0),
               ("par", p["ln2g"], 0, 0), ("par", p["ln2b"], 0, 0)]
    dx1a, df, dmod_a, g["ln2g"], g["ln2b"] = tile_bwd(
        "ln2_bwd", ln2_fn, ln2_ins, [dx2], [(0, F32), (1, F32), (2, F32), (3, F32), (4, F32)], n, tr, tps)
    ducv, g["w_down"] = ffn2_bwd("ffn_down_bwd", s["u"], p["cw"], p["cb"], p["w_down"], df, tr, tps)
    du, g["cw"], g["cb"] = conv_bwd("conv_bwd", ducv, s["u"], p["cw"], tr, tps)
    g["w_up"] = mm_tn("ffn_up_dw", s["x1"], du, FF_CHUNK, tr, mod=mod, sel=(4, 3))
    dx1, dmod_b = mm_modbwd("ffn_up_dx", du, p["w_up_t"], s["x1"], mod, dx1a, (4, 3), tr)

    post_ins = _post_ins(s["ogf"], s["ogb"], s["orf"], s["orb"], s["om"], s["pa"], s["x"], mod, p)
    want = [(0, F32), (2, F32), (4, F32), (5, F32), (6, F32), (7, F32), (8, F32), (9, F32), (10, F32), (11, F32), (12, F32)]
    dog, dor, dom, dgg, drg, dxa, dmod_c, g["gng"], g["w_out"], g["ln1g"], g["ln1b"] = tile_bwd(
        "mix_post_bwd", post_fn, post_ins, [dx1], want, n, tr, tps)
    dqa, dka, dva = mla_bwd("mla_attn_bwd", s["qa"], s["ka"], s["va"], dom, nb, tps, tr, nctx * CHUNK)
    pa = s["pa"]
    gdqf, gdkf, gdvf, gdaf, gdqb, gdkb, gdvb, gdab = scan_bwd(
        "gla_scan_bwd", (s["gq"], 128, 0), (pa, 128, 1), (pa, 256, 1), s["af"], s["ab"], s["gstf"], s["gstb"], dog,
        nb, nch, nctx)
    rdqf, rdkf, rdvf, rdaf, rdqb, rdkb, rdvb, rdab = scan_bwd(
        "ret_scan_bwd", (s["rq"], 128, 0), (s["rk"], 128, 0), (pa, 256, 4), s["arf"], s["arb"], s["rstf"], s["rstb"],
        dor, nb, nch, nctx)

    pre_ins = _pre_ins(pa, tabs, p)
    extra = [gdqf, gdqb, rdqf, rdqb, rdkf, rdkb, gdkf, gdkb, gdvf, gdvb, rdvf, rdvb, dgg, drg]
    kinds = [i[0] for i in pre_ins]
    widx = [w[0] for w in _PRE_WANT]
    npre = len(pre_ins)

    def body(*refs):
        vals = [_load(k, r) for k, r in zip(kinds, refs[:npre])]
        rd = lambda i: refs[npre + i][...].astype(F32)
        cots = (rd(0) + rd(1), rd(14), rd(15), rd(16), rd(17), rd(2) + rd(3), rd(4) + rd(5), rd(18), rd(19), rd(20))

        def f(*dv):
            full = list(vals)
            for i, v in zip(widx, dv):
                full[i] = v
            return tuple(pre_fn(*full))

        _, vjp = jax.vjp(f, *[vals[i] for i in widx])
        grads = vjp(cots)
        dgq, drq, drk, dcq, dckv, dmisc = grads[:6]
        dp = jnp.concatenate([dgq, rd(6) + rd(7), rd(8) + rd(9), rd(12), drq, drk, rd(10) + rd(11), rd(13),
                              dcq, dckv, dmisc], axis=1)
        outs = refs[npre + 21:]
        outs[0][...] = dp.astype(BF16)
        first = pl.program_id(0) == 0
        for r, gr in zip(outs[1:], grads[6:]):
            @pl.when(first)
            def _(r=r):
                r[...] = jnp.zeros_like(r)
            r[...] += gr

    cot_arrays = extra + [gdaf, gdab, rdaf, rdab, dqa, dka, dva]
    par_arrays = [pre_ins[i][1] for i in range(10, 21)]
    res = pl.pallas_call(
        body, name="mix_pre_bwd", grid=(n // tr,),
        in_specs=[_spec(k, a, w, cb, tr, tps) for k, a, w, cb in pre_ins]
        + [pl.BlockSpec((tr, c.shape[1]), lambda j: (j, 0)) for c in cot_arrays],
        out_specs=[pl.BlockSpec((tr, P_PAD), lambda j: (j, 0))] + [pl.BlockSpec(a.shape, lambda j: (0, 0)) for a in par_arrays],
        out_shape=[jax.ShapeDtypeStruct((n, P_PAD), BF16)] + [jax.ShapeDtypeStruct(a.shape, F32) for a in par_arrays],
        compiler_params=_cparams(("arbitrary",)),
    )(*[i[1] for i in pre_ins], *cot_arrays)
    dp = res[0]
    for k, v in zip(("w2f", "w2b", "b2f", "b2b", "retf", "retb", "qg", "kvg", "wuq", "wuk", "wuv"), res[1:]):
        g[k] = v
    g["w_in"] = mm_tn("proj_dw", s["x"], dp, P_PAD, tr, mod=mod, sel=(1, 0))
    dx, dmod_d = mm_modbwd("proj_dx", dp, p["w_in_t"], s["x"], mod, dxa, (1, 0), tr)
    return dx, dmod_a + dmod_b + dmod_c + dmod_d, g


def _unprep_grads(g, w, l):
    o = {}
    o["w_in"] = _cols(g["w_in"], W_IN_INV_SEGS)
    o["ffn_up"] = _cols(g["w_up"], FF_SEGS)
    o["ffn_down"] = g["w_down"]
    o["w_out"] = g["w_out"]
    o["mla_w_uq"] = _unpad_heads(g["wuq"], 96)
    o["mla_w_uk"] = _unpad_heads(g["wuk"], 64)
    o["mla_w_uv"] = g["wuv"]
    o["gla_gate_w"] = jnp.stack([g["w2f"][0:16], g["w2b"][16:32]])
    o["gla_gate_b"] = jnp.concatenate([g["b2f"], g["b2b"]], axis=0)
    dlg = jnp.stack([g["retf"].reshape(4, 32).sum(-1), g["retb"].reshape(4, 32).sum(-1)])
    o["ret_decay"] = dlg * jax.nn.sigmoid(-w["ret_decay"][l])
    o["mla_q_norm_g"], o["mla_kv_norm_g"] = g["qg"][0], g["kvg"][0]
    o["gla_norm_g"] = g["gng"].reshape(4, 64).sum(0)
    o["ln1_g"], o["ln1_b"], o["ln2_g"], o["ln2_b"] = g["ln1g"][0], g["ln1b"][0], g["ln2g"][0], g["ln2b"][0]
    o["ffn_conv_w"] = _cols(g["cw"][0:3], FF_SEGS)
    o["ffn_conv_b"] = _cols(g["cb"][0], FF_SEGS)
    return o


def local_step(xs, target, modtab, w, dims):
    nb, tps, tr, nch, nctx = dims
    tabs = _tables((tps - 1) * tr, tr)
    x = xs
    saved, preps = [], []
    for l in range(DEPTH):
        p = _prep_layer(w, l)
        x, s = layer_fwd(l, x, modtab[l], p, tabs, dims)
        saved.append(s)
        preps.append(p)
    dy, lpart = loss_head("loss_head", x, target, nb, tps, tr)
    loss = jnp.sum(lpart[:, 0, 0])
    dx = dy
    dmods, grads = [None] * DEPTH, [None] * DEPTH
    for l in reversed(range(DEPTH)):
        dx, dmods[l], g = layer_bwd(l, dx, saved[l], modtab[l], preps[l], tabs, dims)
        grads[l] = _unprep_grads(g, w, l)
    gstack = {k: jnp.stack([grads[l][k] for l in range(DEPTH)]) for k in grads[0]}
    return loss, dx, jnp.stack(dmods), gstack


BIG = ["w_in", "mla_w_uq", "mla_w_uk", "mla_w_uv", "w_out", "ffn_up", "ffn_down"]
BIG_AXIS = dict(w_in=2, mla_w_uq=2, mla_w_uk=2, mla_w_uv=2, w_out=1, ffn_up=2, ffn_down=1)
SMALL = ["ada_b", "gla_gate_w", "gla_gate_b", "gla_norm_g", "ret_decay", "mla_q_norm_g", "mla_kv_norm_g",
         "ln1_g", "ln1_b", "ffn_conv_b", "ln2_g", "ln2_b"]
PACK_C = 1024


def _pack(arrs, dtype):
    flat = jnp.concatenate([a.reshape(-1).astype(dtype) for a in arrs])
    pad = (-flat.shape[0]) % (8 * PACK_C)
    return jnp.concatenate([flat, jnp.zeros((pad,), dtype)]).reshape(-1, PACK_C)


def _unpack(flat2d, shapes):
    flat = flat2d.reshape(-1)
    out, off = [], 0
    for s in shapes:
        sz = int(np.prod(s))
        out.append(flat[off:off + sz].reshape(s))
        off += sz
    return out


def _shards_for_peers(gfull, axis):
    if axis == 2:
        l, r, c = gfull.shape
        return gfull.reshape(l, r, N_DEV, c // N_DEV).transpose(2, 0, 1, 3)
    l, r, c = gfull.shape
    return gfull.reshape(l, N_DEV, r // N_DEV, c).transpose(1, 0, 2, 3)


def _whole_from_shards(sh, axis):
    if axis == 2:
        _, l, r, c = sh.shape
        return sh.transpose(1, 2, 0, 3).reshape(l, r, N_DEV * c)
    _, l, r, c = sh.shape
    return sh.transpose(1, 0, 2, 3).reshape(l, N_DEV * r, c)


def kernel(x, c, ctx, c_ctx, ada_w, ada_b, w_in, gla_gate_w, gla_gate_b, gla_norm_g, ret_decay, mla_q_norm_g, mla_kv_norm_g, mla_w_uq, mla_w_uk, mla_w_uv, w_out, ln1_g, ln1_b, ffn_up, ffn_conv_w, ffn_conv_b, ffn_down, ln2_g, ln2_b, loss_target, m_c_ctx, m_ada_w, m_ada_b, m_w_in, m_gla_gate_w, m_gla_gate_b, m_gla_norm_g, m_ret_decay, m_mla_q_norm_g, m_mla_kv_norm_g, m_mla_w_uq, m_mla_w_uk, m_mla_w_uv, m_w_out, m_ln1_g, m_ln1_b, m_ffn_up, m_ffn_conv_w, m_ffn_conv_b, m_ffn_down, m_ln2_g, m_ln2_b, v_c_ctx, v_ada_w, v_ada_b, v_w_in, v_gla_gate_w, v_gla_gate_b, v_gla_norm_g, v_ret_decay, v_mla_q_norm_g, v_mla_kv_norm_g, v_mla_w_uq, v_mla_w_uk, v_mla_w_uv, v_w_out, v_ln1_g, v_ln1_b, v_ffn_up, v_ffn_conv_w, v_ffn_conv_b, v_ffn_down, v_ln2_g, v_ln2_b):
    names = ["c_ctx", "ada_w", "ada_b", "w_in", "gla_gate_w", "gla_gate_b", "gla_norm_g", "ret_decay", "mla_q_norm_g",
             "mla_kv_norm_g", "mla_w_uq", "mla_w_uk", "mla_w_uv", "w_out", "ln1_g", "ln1_b", "ffn_up", "ffn_conv_w",
             "ffn_conv_b", "ffn_down", "ln2_g", "ln2_b"]
    loc = locals()
    W = {k: loc[k] for k in names}
    M = {k: loc["m_" + k] for k in names}
    V = {k: loc["v_" + k] for k in names}

    nb, seq, _ = x.shape
    tr = ctx.shape[1]
    tps = 1 + seq // tr
    t = tps * tr
    n = nb * t
    nt = nb * tps
    dims = (nb, tps, tr, t // CHUNK, tr // CHUNK)
    px, py, pc = _place()
    me = 4 * px + 2 * py + pc
    ncol = ada_w.shape[2]

    cw_loc = ffn_conv_w.reshape(-1)
    g1 = jnp.concatenate([c.reshape(-1), cw_loc])
    g1 = jnp.concatenate([g1, jnp.zeros(((-g1.shape[0]) % (8 * PACK_C),), F32)]).reshape(-1, PACK_C)
    r1 = g1.shape[0]
    g1a = all_gather("gather_cond", g1, True).reshape(N_DEV, -1)
    c_all = g1a[:, :nb * D].reshape(N_DEV * nb, D)
    cw_all = g1a[:, nb * D:nb * D + cw_loc.shape[0]].reshape(N_DEV, DEPTH, 3, -1).transpose(1, 2, 0, 3).reshape(DEPTH, 3, -1)

    wpack = _pack([W[k] for k in BIG], BF16)
    wall = all_gather("gather_weights", wpack, False).reshape(N_DEV, -1)
    shapes = [W[k].shape for k in BIG]
    full, off = {}, 0
    for k, s in zip(BIG, shapes):
        sz = int(np.prod(s))
        full[k] = _whole_from_shards(wall[:, off:off + sz].reshape((N_DEV,) + s), BIG_AXIS[k])
        off += sz
    wl = dict(full)
    for k in SMALL[1:]:
        wl[k] = W[k]
    wl["ffn_conv_w"] = cw_all

    srows = 40
    s_in = jnp.concatenate([c_all, c_ctx[None], jnp.zeros((srows - N_DEV * nb - 1, D), F32)], axis=0)
    s_act = _silu(s_in)
    ab_loc = lax.dynamic_slice_in_dim(ada_b, me * ncol, ncol, axis=1)[:, None, :]
    mod_part = ada_fwd("ada_fwd", s_act, ada_w, ab_loc)
    mod_all = all_gather("gather_mod", mod_part.reshape(-1, ncol), True).reshape(N_DEV, DEPTH, srows, ncol)
    mod_rows = mod_all.transpose(1, 2, 0, 3).reshape(DEPTH, srows, N_DEV * ncol)
    mod_l = lax.dynamic_slice_in_dim(mod_rows, me * nb, nb, axis=1).reshape(DEPTH, nb, 6, D)
    mod_c = mod_rows[:, N_DEV * nb].reshape(DEPTH, 1, 6, D)
    tile_is_ctx = (jnp.arange(tps) == 0)[None, None, :, None, None]
    modtab = jnp.where(tile_is_ctx, mod_c[:, :, None], mod_l[:, :, None])
    modtab = jnp.concatenate([modtab, jnp.zeros((DEPTH, nb, tps, 2, D), F32)], axis=3).reshape(DEPTH, nt, 8, D)

    xs = jnp.concatenate([ctx, x], axis=1).reshape(n, D)
    loss_loc, dxs, dmodtab, gl = local_step(xs, loss_target.reshape(nb * seq, D), modtab, wl, dims)
    loss = lax.psum(loss_loc, ("x", "y", "c"))
    grad_x = dxs.reshape(nb, t, D)[:, tr:]

    dm = dmodtab.reshape(DEPTH, nb, tps, 8, D)[:, :, :, :6]
    dmod_l = dm[:, :, 1:].sum(2).reshape(DEPTH, nb, 6 * D)
    dmod_c = dm[:, :, 0].sum(1).reshape(DEPTH, 1, 6 * D)
    gl["ada_b"] = dmod_l.sum(1) + dmod_c[:, 0]
    small_list = [gl[k] for k in SMALL] + [gl["ffn_conv_w"]]
    small_shapes = [a.shape for a in small_list]
    spack = _pack(small_list + [jnp.concatenate([dmod_l, dmod_c], axis=1)], F32)
    rs = spack.shape[0]
    sall = all_gather("gather_small_grads", spack, True).reshape(N_DEV, rs, PACK_C)
    nsmall = sum(int(np.prod(s)) for s in small_shapes)
    dmo = sall.reshape(N_DEV, -1)[:, nsmall:nsmall + DEPTH * (nb + 1) * 6 * D].reshape(N_DEV, DEPTH, nb + 1, 6 * D)
    dl_all = dmo[:, :, :nb].transpose(1, 0, 2, 3).reshape(DEPTH, N_DEV * nb, 6 * D)
    dc_all = dmo[:, :, nb].sum(0)[:, None]
    dmod_rows = jnp.concatenate([dl_all, dc_all, jnp.zeros((DEPTH, srows - N_DEV * nb - 1, 6 * D), F32)], axis=1)
    dmod_loc = lax.dynamic_slice_in_dim(dmod_rows.reshape(DEPTH, srows, N_DEV, ncol), me, 1, axis=2)[:, :, 0]
    d_ada_w, d_s = ada_bwd("ada_bwd", s_act, ada_w, dmod_loc)
    sg = jax.nn.sigmoid(c_ctx)
    dcc = d_s[:, N_DEV * nb].sum(0) * (sg * (1.0 + c_ctx * (1.0 - sg)))
    ccp = jnp.concatenate([dcc[None], jnp.zeros((7, D), F32)], axis=0)
    ccall = all_gather("gather_cctx", ccp, True).reshape(N_DEV, 8, D)

    gsend = jnp.concatenate([_shards_for_peers(gl[k], BIG_AXIS[k]).reshape(N_DEV, -1) for k in BIG], axis=1)
    padc = (-gsend.shape[1]) % (8 * PACK_C)
    gsend = jnp.concatenate([gsend, jnp.zeros((N_DEV, padc), F32)], axis=1).astype(BF16).reshape(N_DEV, -1, PACK_C)
    grecv = all_to_all("grad_all_to_all", gsend)
    res = {}

    def update(tag, keys, g8):
        outs = adamw(tag, _pack([W[k] for k in keys], F32), _pack([M[k] for k in keys], F32),
                     _pack([V[k] for k in keys], F32), g8)
        for i, arr in enumerate(outs):
            for k, a in zip(keys, _unpack(arr, [W[k].shape for k in keys])):
                res.setdefault(k, [None] * 4)[i] = a

    update("adamw_big", BIG, grecv)
    nrep = sum(int(np.prod(W[k].shape)) for k in SMALL)
    sflat = sall.reshape(N_DEV, -1)
    def pack8(a):
        a = a.reshape(N_DEV, -1)
        pad = (-a.shape[1]) % (8 * PACK_C)
        return jnp.concatenate([a, jnp.zeros((N_DEV, pad), F32)], axis=1).reshape(N_DEV, -1, PACK_C)

    update("adamw_small", SMALL, pack8(sflat[:, :nrep]))
    ncw = ffn_conv_w.shape[2]
    cw8 = sflat[:, nrep:nsmall].reshape(N_DEV, DEPTH, 3, N_DEV * ncw)
    cw8 = lax.dynamic_slice_in_dim(cw8, me * ncw, ncw, axis=3)
    update("adamw_conv", ["ffn_conv_w"], pack8(cw8))
    update("adamw_ada", ["ada_w"], _pack([d_ada_w], F32)[None])
    update("adamw_cctx", ["c_ctx"], ccall)

    out = [loss, grad_x]
    for i in range(4):
        out += [res[k][i] for k in names]
    return tuple(out)
```

```python
import functools
import math

import numpy as np
import jax
import jax.numpy as jnp
from jax import lax
from jax.experimental import pallas as pl
from jax.experimental.pallas import tpu as pltpu

F32 = jnp.float32
BF16 = jnp.bfloat16
HI = lax.Precision.HIGHEST
MESH = pl.DeviceIdType.MESH

N_DEV = 8
D = 1024
DEPTH = 4
CHUNK = 64
EPS = 1e-6
ALPHA = (2 * DEPTH) ** 0.25
GLA_TAU = 16.0
ROPE_BASE = 10000.0
MLA_SCALE = 96 ** -0.5
D_FF = 2816
FF_CHUNK = 1408
P_PAD = 2048
VMEM_LIMIT_BYTES = 56 << 20

ADAM_LR, ADAM_B1, ADAM_B2, ADAM_EPS, ADAM_WD, ADAM_STEP = 0.001, 0.9, 0.999, 1e-08, 0.01, 10

D_IN = 1984
W_IN_SEGS = [(0, 512), (544, 1408), (512, 32), (1952, 32)]
W_IN_INV_SEGS = [(0, 512), (1920, 32), (512, 1408), (1952, 32)]
FF_SEGS = [(0, FF_CHUNK), (D_FF, FF_CHUNK), (FF_CHUNK, FF_CHUNK), (D_FF + FF_CHUNK, FF_CHUNK)]


def _cols(a, segs):
    return jnp.concatenate([a[..., s:s + n] for s, n in segs], axis=-1)


def _rows(a, segs):
    return jnp.concatenate([a[s:s + n] for s, n in segs], axis=0)


def _pad_heads(wt, per_head):
    c = wt.shape[1]
    wt = wt.reshape(8, per_head, c)
    return jnp.concatenate([wt, jnp.zeros((8, 128 - per_head, c), wt.dtype)], axis=1).reshape(1024, c)


def _unpad_heads(g, per_head):
    c = g.shape[1]
    return g.reshape(8, 128, c)[:, :per_head].reshape(8 * per_head, c)


def _cparams(sem=None):
    return pltpu.CompilerParams(vmem_limit_bytes=VMEM_LIMIT_BYTES, dimension_semantics=sem)


@jax.custom_vjp
def bdot(a, w):
    return jnp.dot(a.astype(BF16), w.astype(BF16), preferred_element_type=F32)


def _bdot_fwd(a, w):
    return bdot(a, w), (a, w)


def _bdot_bwd(res, ct):
    a, w = res
    ctb = ct.astype(BF16)
    da = lax.dot_general(ctb, w.astype(BF16), (((1,), (1,)), ((), ())), preferred_element_type=F32)
    dw = lax.dot_general(a.astype(BF16), ctb, (((0,), (0,)), ((), ())), preferred_element_type=F32)
    return da.astype(a.dtype), dw.astype(w.dtype)


bdot.defvjp(_bdot_fwd, _bdot_bwd)


@jax.custom_vjp
def bdot_nt(a, wt):
    return lax.dot_general(a.astype(BF16), wt.astype(BF16), (((1,), (1,)), ((), ())), preferred_element_type=F32)


def _bdot_nt_fwd(a, wt):
    return bdot_nt(a, wt), (a, wt)


def _bdot_nt_bwd(res, ct):
    a, wt = res
    ctb = ct.astype(BF16)
    da = jnp.dot(ctb, wt.astype(BF16), preferred_element_type=F32)
    dwt = lax.dot_general(ctb, a.astype(BF16), (((0,), (0,)), ((), ())), preferred_element_type=F32)
    return da.astype(a.dtype), dwt.astype(wt.dtype)


bdot_nt.defvjp(_bdot_nt_fwd, _bdot_nt_bwd)


def hdot(a, b):
    return jnp.dot(a, b, precision=HI, preferred_element_type=F32)


def _iota(shape, axis):
    return lax.broadcasted_iota(jnp.int32, shape, axis)


def _group_avg(n, g):
    return (_iota((n, n), 0) // g == _iota((n, n), 1) // g).astype(F32) * (1.0 / g)


def _swap_matrix(n, half):
    i, j = _iota((n, n), 0), _iota((n, n), 1)
    partner = jnp.where((j // half) % 2 == 0, j + half, j - half)
    return (i == partner).astype(F32)


def _silu(x):
    return x * jax.nn.sigmoid(x)


def _layer_norm(z, g, b):
    mu = jnp.mean(z, axis=-1, keepdims=True)
    zc = z - mu
    var = jnp.mean(zc * zc, axis=-1, keepdims=True)
    return zc * lax.rsqrt(var + EPS) * g + b


def _rms(x, g):
    return x * lax.rsqrt(jnp.mean(x * x, axis=-1, keepdims=True) + EPS) * g


def mm(name, a, wt, out_dtype, tn, tr, mod=None, sel=None):
    n, k = a.shape
    nw = wt.shape[0]

    def body(*refs):
        if mod is not None:
            a_ref, m_ref, w_ref, o_ref = refs
            m = m_ref[0]
            av = a_ref[...] * (1.0 + m[sel[0]:sel[0] + 1]) + m[sel[1]:sel[1] + 1]
        else:
            a_ref, w_ref, o_ref = refs
            av = a_ref[...]
        o_ref[...] = lax.dot_general(av.astype(BF16), w_ref[...], (((1,), (1,)), ((), ())),
                                     preferred_element_type=F32).astype(o_ref.dtype)

    in_specs = [pl.BlockSpec((tr, k), lambda c, j: (j, 0))]
    args = [a]
    if mod is not None:
        in_specs.append(pl.BlockSpec((1, 8, k), lambda c, j: (j, 0, 0)))
        args.append(mod)
    in_specs.append(pl.BlockSpec((tn, k), lambda c, j: (c, 0)))
    args.append(wt)
    return pl.pallas_call(
        body, name=name, grid=(nw // tn, n // tr), in_specs=in_specs,
        out_specs=pl.BlockSpec((tr, tn), lambda c, j: (j, c)),
        out_shape=jax.ShapeDtypeStruct((n, nw), out_dtype), compiler_params=_cparams(("parallel", "arbitrary")),
    )(*args)


def mm_tn(name, dc, a, tn, tr, mod=None, sel=None):
    n, k = a.shape
    nw = dc.shape[1]

    def body(*refs):
        if mod is not None:
            d_ref, a_ref, m_ref, o_ref = refs
            m = m_ref[0]
            av = a_ref[...] * (1.0 + m[sel[0]:sel[0] + 1]) + m[sel[1]:sel[1] + 1]
        else:
            d_ref, a_ref, o_ref = refs
            av = a_ref[...]

        @pl.when(pl.program_id(1) == 0)
        def _():
            o_ref[...] = jnp.zeros_like(o_ref)

        o_ref[...] += lax.dot_general(d_ref[...].astype(BF16), av.astype(BF16), (((0,), (0,)), ((), ())),
                                      preferred_element_type=F32)

    in_specs = [pl.BlockSpec((tr, tn), lambda c, j: (j, c)), pl.BlockSpec((tr, k), lambda c, j: (j, 0))]
    args = [dc, a]
    if mod is not None:
        in_specs.append(pl.BlockSpec((1, 8, k), lambda c, j: (j, 0, 0)))
        args.append(mod)
    return pl.pallas_call(
        body, name=name, grid=(nw // tn, n // tr), in_specs=in_specs,
        out_specs=pl.BlockSpec((tn, k), lambda c, j: (c, 0)),
        out_shape=jax.ShapeDtypeStruct((nw, k), F32), compiler_params=_cparams(("parallel", "arbitrary")),
    )(*args)


def mm_modbwd(name, dc, wt, x, mod, add, sel, tr):
    n, k = dc.shape
    dm = wt.shape[1]

    def body(dc_ref, wt_ref, x_ref, m_ref, add_ref, dx_ref, dm_ref):
        dh = jnp.dot(dc_ref[...].astype(BF16), wt_ref[...], preferred_element_type=F32)
        m = m_ref[0]
        dx_ref[...] = add_ref[...] + dh * (1.0 + m[sel[0]:sel[0] + 1])
        dsc = jnp.sum(dh * x_ref[...], axis=0, keepdims=True)
        dsh = jnp.sum(dh, axis=0, keepdims=True)
        rows = _iota((8, dm), 0)
        dm_ref[0] = jnp.where(rows == sel[0], dsc, 0.0) + jnp.where(rows == sel[1], dsh, 0.0)

    return pl.pallas_call(
        body, name=name, grid=(n // tr,),
        in_specs=[pl.BlockSpec((tr, k), lambda j: (j, 0)), pl.BlockSpec((k, dm), lambda j: (0, 0)),
                  pl.BlockSpec((tr, dm), lambda j: (j, 0)), pl.BlockSpec((1, 8, dm), lambda j: (j, 0, 0)),
                  pl.BlockSpec((tr, dm), lambda j: (j, 0))],
        out_specs=[pl.BlockSpec((tr, dm), lambda j: (j, 0)), pl.BlockSpec((1, 8, dm), lambda j: (j, 0, 0))],
        out_shape=[jax.ShapeDtypeStruct((n, dm), F32), jax.ShapeDtypeStruct((n // tr, 8, dm), F32)],
        compiler_params=_cparams(("arbitrary",)),
    )(dc, wt, x, mod, add)


def _spec(kind, arr, width, cb, tr, tps):
    if kind == "row":
        return pl.BlockSpec((tr, width), lambda j: (j, cb))
    if kind == "pos":
        return pl.BlockSpec((tr, width), lambda j: (j % tps, cb))
    if kind == "tile":
        return pl.BlockSpec((1,) + arr.shape[1:], lambda j: (j, 0, 0))
    if kind == "par":
        return pl.BlockSpec(arr.shape, lambda j: (0, 0))
    raise ValueError(kind)


def _load(kind, ref):
    v = ref[0] if kind == "tile" else ref[...]
    return v.astype(F32)


def tile_fwd(name, fn, ins, outs, n, tr, tps):
    kinds = [i[0] for i in ins]

    def body(*refs):
        vals = [_load(k, r) for k, r in zip(kinds, refs[:len(ins)])]
        res = fn(*vals)
        for r, o in zip(refs[len(ins):], res):
            r[...] = o.astype(r.dtype)

    return pl.pallas_call(
        body, name=name, grid=(n // tr,),
        in_specs=[_spec(k, a, w, cb, tr, tps) for k, a, w, cb in ins],
        out_specs=[pl.BlockSpec((tr, w), lambda j: (j, 0)) for w, _ in outs],
        out_shape=[jax.ShapeDtypeStruct((n, w), dt) for w, dt in outs],
        compiler_params=_cparams(("arbitrary",)),
    )(*[i[1] for i in ins])


def tile_bwd(name, fn, ins, cots, want, n, tr, tps):
    kinds = [i[0] for i in ins]
    widx = [w[0] for w in want]
    ni, nc = len(ins), len(cots)

    def body(*refs):
        vals = [_load(k, r) for k, r in zip(kinds, refs[:ni])]
        cvals = tuple(r[...].astype(F32) for r in refs[ni:ni + nc])

        def f(*dv):
            full = list(vals)
            for i, v in zip(widx, dv):
                full[i] = v
            return tuple(fn(*full))

        _, vjp = jax.vjp(f, *[vals[i] for i in widx])
        grads = vjp(cvals)
        first = pl.program_id(0) == 0
        for r, g, i in zip(refs[ni + nc:], grads, widx):
            if kinds[i] == "par":
                @pl.when(first)
                def _(r=r):
                    r[...] = jnp.zeros_like(r)
                r[...] += g
            elif kinds[i] == "tile":
                r[0] = g.astype(r.dtype)
            else:
                r[...] = g.astype(r.dtype)

    out_specs, out_shape = [], []
    for i, dt in want:
        k, a, w, cb = ins[i]
        if k == "par":
            out_specs.append(pl.BlockSpec(a.shape, lambda j: (0, 0)))
            out_shape.append(jax.ShapeDtypeStruct(a.shape, F32))
        elif k == "tile":
            out_specs.append(pl.BlockSpec((1,) + a.shape[1:], lambda j: (j, 0, 0)))
            out_shape.append(jax.ShapeDtypeStruct(a.shape, F32))
        else:
            out_specs.append(pl.BlockSpec((tr, w), lambda j: (j, 0)))
            out_shape.append(jax.ShapeDtypeStruct((n, w), dt))
    return pl.pallas_call(
        body, name=name, grid=(n // tr,),
        in_specs=[_spec(k, a, w, cb, tr, tps) for k, a, w, cb in ins]
        + [pl.BlockSpec((tr, c.shape[1]), lambda j: (j, 0)) for c in cots],
        out_specs=out_specs, out_shape=out_shape, compiler_params=_cparams(("arbitrary",)),
    )(*[i[1] for i in ins], *cots)


def _rope(x, cos, sin, swap):
    return x * cos + hdot(x, swap) * sin


def pre_fn(p_gq, p_rq, p_rk, p_cq, p_ckv, p_misc, rcos, rsin, mcos, msin,
           w2f, w2b, b2f, b2b, retf, retb, qg, kvg, wuq, wuk, wuv, e2):
    tr = p_gq.shape[0]
    gq = p_gq * (32 ** -0.5)
    af = jax.nn.log_sigmoid(hdot(p_misc, w2f) + b2f) * (1.0 / GLA_TAU)
    ab = jax.nn.log_sigmoid(hdot(p_misc, w2b) + b2b) * (1.0 / GLA_TAU)
    arf = jnp.zeros((tr, 128), F32) + retf
    arb = jnp.zeros((tr, 128), F32) + retb
    sw32 = _swap_matrix(128, 16)
    rq = _rope(p_rq, rcos, rsin, sw32)
    rk = _rope(p_rk * (32 ** -0.5), rcos, rsin, sw32)
    sw16 = _swap_matrix(128, 8)
    qp = bdot_nt(_rms(p_cq, qg), wuq) * MLA_SCALE
    ckvn = _rms(p_ckv, kvg)
    kp = bdot_nt(ckvn, wuk) + hdot(p_misc, e2)
    qs = [_rope(qp[:, h * 128:(h + 1) * 128], mcos, msin, sw16) for h in range(8)]
    ks = [_rope(kp[:, h * 128:(h + 1) * 128], mcos, msin, sw16) for h in range(8)]
    v = bdot_nt(ckvn, wuv)
    return gq, af, ab, arf, arb, rq, rk, jnp.concatenate(qs, axis=1), jnp.concatenate(ks, axis=1), v


def post_fn(ogf, ogb, orf, orb, om, gg, rg, x, mod, gng, wout, lng, lnb):
    avg = _group_avg(256, 64)
    og = ogf + ogb
    mg = og * lax.rsqrt(hdot(og * og, avg) + EPS) * gng * _silu(gg)
    orr = orf + orb
    oc = orr - hdot(orr, avg)
    mr = oc * lax.rsqrt(hdot(oc * oc, avg) + EPS) * _silu(rg)
    m = jnp.concatenate([mg, mr, om], axis=1)
    y = bdot(m, wout)
    return (_layer_norm(ALPHA * x + mod[2:3] * y, lng, lnb),)


def ln2_fn(x1, f, mod, lng, lnb):
    return (_layer_norm(ALPHA * x1 + mod[5:6] * f, lng, lnb),)


def scan_step(q, k, v, a, s, rev):
    ii, jj = _iota((CHUNK, CHUNK), 0), _iota((CHUNK, CHUNK), 1)
    tri = ((jj >= ii) if rev else (jj <= ii)).astype(F32)
    b = hdot(tri, a)
    btot = jnp.sum(a, axis=0, keepdims=True)
    qe = q * jnp.exp(b - btot)
    ke = k * jnp.exp(btot - b)
    lane = _iota((1, 128), 1)
    q4 = jnp.concatenate([qe * (lane // 32 == h).astype(F32) for h in range(4)], axis=0)
    att = lax.dot_general(q4, ke, (((1,), (1,)), ((), ())), precision=HI, preferred_element_type=F32)
    att = jnp.where(jnp.concatenate([tri] * 4, axis=0) > 0, att, 0.0)
    r = hdot(att, v)
    col = _iota((1, 256), 1)
    o = hdot(q * jnp.exp(b), s)
    for h in range(4):
        o = o + r[h * CHUNK:(h + 1) * CHUNK] * (col // 64 == h).astype(F32)
    bcol = lax.dot_general(a, jnp.ones((CHUNK, 1), F32), (((0,), (0,)), ((), ())), precision=HI,
                           preferred_element_type=F32)
    kv = lax.dot_general(ke, v, (((0,), (0,)), ((), ())), precision=HI, preferred_element_type=F32)
    bd = (_iota((128, 256), 0) // 32 == _iota((128, 256), 1) // 64).astype(F32)
    return o, s * jnp.exp(bcol) + kv * bd


def _chunk_maps(nch, nctx):
    def fwd(b, s):
        return b * nch + s

    def bwd(b, s):
        return b * nch + jnp.where(s < nctx, nctx - 1 - s, nch - 1 - (s - nctx))
    return fwd, bwd


def scan_fwd(name, q, k, v, af, ab, nb, nch, nctx):
    n = af.shape[0]
    fmap, bmap = _chunk_maps(nch, nctx)

    def body(qf, kf, vf, a_f, qb, kb, vb, a_b, of_ref, ob_ref, stf_ref, stb_ref, s_scr):
        @pl.when(pl.program_id(1) == 0)
        def _():
            s_scr[...] = jnp.zeros_like(s_scr)

        stf_ref[0] = s_scr[0]
        stb_ref[0] = s_scr[1]
        o, sn = scan_step(qf[...], kf[...], vf[...], a_f[...], s_scr[0], False)
        of_ref[...] = o
        s_scr[0] = sn
        o, sn = scan_step(qb[...], kb[...], vb[...], a_b[...], s_scr[1], True)
        ob_ref[...] = o
        s_scr[1] = sn

    def specs(m):
        return [pl.BlockSpec((CHUNK, w), lambda b, s, cb=cb: (m(b, s), cb)) for _, w, cb in (q, k, v)] + \
               [pl.BlockSpec((CHUNK, 128), lambda b, s: (m(b, s), 0))]

    return pl.pallas_call(
        body, name=name, grid=(nb, nch), in_specs=specs(fmap) + specs(bmap),
        out_specs=[pl.BlockSpec((CHUNK, 256), lambda b, s: (fmap(b, s), 0)),
                   pl.BlockSpec((CHUNK, 256), lambda b, s: (bmap(b, s), 0)),
                   pl.BlockSpec((1, 128, 256), lambda b, s: (b * nch + s, 0, 0)),
                   pl.BlockSpec((1, 128, 256), lambda b, s: (b * nch + s, 0, 0))],
        out_shape=[jax.ShapeDtypeStruct((n, 256), F32)] * 2 + [jax.ShapeDtypeStruct((nb * nch, 128, 256), F32)] * 2,
        scratch_shapes=[pltpu.VMEM((2, 128, 256), F32)], compiler_params=_cparams(("arbitrary", "arbitrary")),
    )(q[0], k[0], v[0], af, q[0], k[0], v[0], ab)


def scan_bwd(name, q, k, v, af, ab, stf, stb, do, nb, nch, nctx):
    n = af.shape[0]
    fmap0, bmap0 = _chunk_maps(nch, nctx)
    fmap = lambda b, r: fmap0(b, nch - 1 - r)
    bmap = lambda b, r: bmap0(b, nch - 1 - r)

    def body(qf, kf, vf, a_f, sf, dof, qb, kb, vb, a_b, sb, dob,
             dqf, dkf, dvf, daf, dqb, dkb, dvb, dab, ds_scr):
        @pl.when(pl.program_id(1) == 0)
        def _():
            ds_scr[...] = jnp.zeros_like(ds_scr)

        for d, (qr, kr, vr, ar, sr, dor, outs) in enumerate(((qf, kf, vf, a_f, sf, dof, (dqf, dkf, dvf, daf)),
                                                               (qb, kb, vb, a_b, sb, dob, (dqb, dkb, dvb, dab)))):
            _, vjp = jax.vjp(functools.partial(scan_step, rev=bool(d)), qr[...], kr[...], vr[...], ar[...], sr[0])
            dq, dk, dv, da, ds = vjp((dor[...], ds_scr[d]))
            outs[0][...] = dq
            outs[1][...] = dk
            outs[2][...] = dv
            outs[3][...] = da
            ds_scr[d] = ds

    def specs(m):
        return [pl.BlockSpec((CHUNK, w), lambda b, r, cb=cb: (m(b, r), cb)) for _, w, cb in (q, k, v)] + \
               [pl.BlockSpec((CHUNK, 128), lambda b, r: (m(b, r), 0)),
                pl.BlockSpec((1, 128, 256), lambda b, r: (b * nch + nch - 1 - r, 0, 0)),
                pl.BlockSpec((CHUNK, 256), lambda b, r: (m(b, r), 0))]

    def ospecs(m):
        return [pl.BlockSpec((CHUNK, w), lambda b, r: (m(b, r), 0)) for w in (128, 128, 256, 128)]

    oshape = [jax.ShapeDtypeStruct((n, w), F32) for w in (128, 128, 256, 128)]
    return pl.pallas_call(
        body, name=name, grid=(nb, nch), in_specs=specs(fmap) + specs(bmap),
        out_specs=ospecs(fmap) + ospecs(bmap), out_shape=oshape + oshape,
        scratch_shapes=[pltpu.VMEM((2, 128, 256), F32)], compiler_params=_cparams(("arbitrary", "arbitrary")),
    )(q[0], k[0], v[0], af, stf, do, q[0], k[0], v[0], ab, stb, do)


def mla_fwd(name, qa, ka, va, nb, tps, tr, nctx_rows):
    n = qa.shape[0]
    t = tps * tr

    def body(q_ref, k_ref, v_ref, o_ref, lse_ref):
        def attend(nk):
            vv = v_ref[0:nk, :]
            first = _iota(vv.shape, 1) < 64
            one = jnp.ones_like(vv)
            res, lses = [], []
            for h in range(2):
                s = lax.dot_general(q_ref[:, h * 128:(h + 1) * 128], k_ref[0:nk, h * 128:(h + 1) * 128],
                                    (((1,), (1,)), ((), ())), preferred_element_type=F32)
                m = jnp.max(s, axis=-1, keepdims=True)
                e = jnp.exp((s - m).astype(BF16))
                r = jnp.dot(e, jnp.where(first == (h == 0), vv, one), preferred_element_type=F32)
                l = r[:, 64:65] if h == 0 else r[:, 0:1]
                res.append(r / l)
                lses.append(m + jnp.log(l))
            lane = _iota((tr, 128), 1) < 64
            o_ref[...] = jnp.where(lane, res[0], res[1])
            lse_ref[...] = jnp.where(lane, lses[0], lses[1])

        @pl.when(pl.program_id(2) == 0)
        def _():
            attend(nctx_rows)

        @pl.when(pl.program_id(2) > 0)
        def _():
            attend(t)

    return pl.pallas_call(
        body, name=name, grid=(nb, 4, tps),
        in_specs=[pl.BlockSpec((tr, 256), lambda b, h, j: (b * tps + j, h)), pl.BlockSpec((t, 256), lambda b, h, j: (b, h)),
                  pl.BlockSpec((t, 128), lambda b, h, j: (b, h))],
        out_specs=[pl.BlockSpec((tr, 128), lambda b, h, j: (b * tps + j, h))] * 2,
        out_shape=[jax.ShapeDtypeStruct((n, 512), F32)] * 2,
        compiler_params=_cparams(("parallel", "parallel", "arbitrary")),
    )(qa, ka, va)


def mla_bwd(name, qa, ka, va, o, lse, do, nb, tps, tr, nctx_rows):
    n = qa.shape[0]
    t = tps * tr

    def body(q_ref, k_ref, v_ref, o_ref, lse_ref, do_ref, dq_ref, dk_ref, dv_ref):
        @pl.when(pl.program_id(2) == 0)
        def _():
            dk_ref[...] = jnp.zeros_like(dk_ref)
            dv_ref[...] = jnp.zeros_like(dv_ref)

        def attend(nk):
            dov = do_ref[...]
            oo = dov * o_ref[...]
            dob = dov.astype(BF16)
            first = _iota(dob.shape, 1) < 64
            dqs = []
            dv = None
            for h in range(2):
                hs = slice(h * 128, (h + 1) * 128)
                qh, kh = q_ref[:, hs], k_ref[0:nk, hs]
                mine = first == (h == 0)
                delta = jnp.sum(jnp.where(mine, oo, 0.0), axis=-1, keepdims=True)
                doh = jnp.where(mine, dob, jnp.zeros_like(dob))
                s = lax.dot_general(qh, kh, (((1,), (1,)), ((), ())), preferred_element_type=F32)
                p = jnp.exp((s - lse_ref[:, h * 64:h * 64 + 1]).astype(BF16))
                dp = lax.dot_general(doh, v_ref[0:nk, :], (((1,), (1,)), ((), ())), preferred_element_type=F32)
                ds = p * (dp - delta).astype(BF16)
                dqs.append(jnp.dot(ds, kh, preferred_element_type=F32))
                dk_ref[0:nk, hs] += lax.dot_general(ds, qh, (((0,), (0,)), ((), ())), preferred_element_type=F32)
                dvh = lax.dot_general(p, doh, (((0,), (0,)), ((), ())), preferred_element_type=F32)
                dv = dvh if dv is None else dv + dvh
            dq_ref[...] = jnp.concatenate(dqs, axis=1)
            dv_ref[0:nk, :] += dv

        @pl.when(pl.program_id(2) == 0)
        def _():
            attend(nctx_rows)

        @pl.when(pl.program_id(2) > 0)
        def _():
            attend(t)

    qtile = pl.BlockSpec((tr, 128), lambda b, h, j: (b * tps + j, h))
    return pl.pallas_call(
        body, name=name, grid=(nb, 4, tps),
        in_specs=[pl.BlockSpec((tr, 256), lambda b, h, j: (b * tps + j, h)), pl.BlockSpec((t, 256), lambda b, h, j: (b, h)),
                  pl.BlockSpec((t, 128), lambda b, h, j: (b, h)), qtile, qtile, qtile],
        out_specs=[pl.BlockSpec((tr, 256), lambda b, h, j: (b * tps + j, h)), pl.BlockSpec((t, 256), lambda b, h, j: (b, h)),
                   pl.BlockSpec((t, 128), lambda b, h, j: (b, h))],
        out_shape=[jax.ShapeDtypeStruct((n, 1024), F32), jax.ShapeDtypeStruct((n, 1024), F32),
                   jax.ShapeDtypeStruct((n, 512), F32)],
        compiler_params=_cparams(("parallel", "parallel", "arbitrary")),
    )(qa, ka, va, o, lse, do)


HALO = 16


def _halo_specs(tr, width, tps, nt):
    r = tr // HALO
    return [pl.BlockSpec((tr, width), lambda j, c: (j, c)),
            pl.BlockSpec((HALO, width), lambda j, c: (jnp.maximum(j * r - 1, 0), c)),
            pl.BlockSpec((HALO, width), lambda j, c: (jnp.minimum((j + 1) * r, nt * r - 1), c))]


def _shifted(u, prev, nxt, j, tps):
    tr = u.shape[0]
    t = j % tps
    has_prev = (t >= 2).astype(F32)
    has_next = jnp.logical_and(t >= 1, t <= tps - 2).astype(F32)
    rows = _iota(u.shape, 0)
    dn = jnp.where(rows == 0, prev[HALO - 1:HALO] * has_prev, pltpu.roll(u, 1, 0))
    up = jnp.where(rows == tr - 1, nxt[0:1] * has_next, pltpu.roll(u, tr - 1, 0))
    return dn, up


def _ffn_act(ucv):
    return _silu(ucv[:, :FF_CHUNK]) * ucv[:, FF_CHUNK:]


def ffn2_fwd(name, u, cw, cb, wd, x1, mod, lng, lnb, tr, tps):
    n = u.shape[0]
    nt = n // tr
    w2 = 2 * FF_CHUNK

    def body(u_ref, up_ref, un_ref, cw_ref, cb_ref, wd_ref, x1_ref, m_ref, g_ref, b_ref, f_ref, x2_ref, acc):
        j, c = pl.program_id(0), pl.program_id(1)
        uu = u_ref[...].astype(F32)
        dn, up = _shifted(uu, up_ref[...].astype(F32), un_ref[...].astype(F32), j, tps)
        cwv = cw_ref[...]
        ucv = cwv[0:1] * dn + cwv[1:2] * uu + cwv[2:3] * up + cb_ref[...]
        part = bdot(_ffn_act(ucv), wd_ref[...])

        @pl.when(c == 0)
        def _():
            acc[...] = part

        @pl.when(c == 1)
        def _():
            f = acc[...] + part
            f_ref[...] = f
            x2_ref[...] = ln2_fn(x1_ref[...], f, m_ref[0], g_ref[...], b_ref[...])[0]

    return pl.pallas_call(
        body, name=name, grid=(nt, 2),
        in_specs=_halo_specs(tr, w2, tps, nt) + [
            pl.BlockSpec((8, w2), lambda j, c: (0, c)), pl.BlockSpec((1, w2), lambda j, c: (0, c)),
            pl.BlockSpec((FF_CHUNK, D), lambda j, c: (c, 0)), pl.BlockSpec((tr, D), lambda j, c: (j, 0)),
            pl.BlockSpec((1, 8, D), lambda j, c: (j, 0, 0)), pl.BlockSpec((1, D), lambda j, c: (0, 0)),
            pl.BlockSpec((1, D), lambda j, c: (0, 0))],
        out_specs=[pl.BlockSpec((tr, D), lambda j, c: (j, 0)), pl.BlockSpec((tr, D), lambda j, c: (j, 0))],
        out_shape=[jax.ShapeDtypeStruct((n, D), F32)] * 2, scratch_shapes=[pltpu.VMEM((tr, D), F32)],
        compiler_params=_cparams(("arbitrary", "arbitrary")),
    )(u, u, u, cw, cb, wd, x1, mod, lng, lnb)


def ffn2_bwd(name, u, cw, cb, wd, df, tr, tps):
    n = u.shape[0]
    nt = n // tr
    w2 = 2 * FF_CHUNK

    def body(u_ref, up_ref, un_ref, cw_ref, cb_ref, wd_ref, df_ref, ducv_ref, dwd_ref):
        c, j = pl.program_id(0), pl.program_id(1)
        uu = u_ref[...].astype(F32)
        dn, up = _shifted(uu, up_ref[...].astype(F32), un_ref[...].astype(F32), j, tps)
        cwv = cw_ref[...]
        ucv = cwv[0:1] * dn + cwv[1:2] * uu + cwv[2:3] * up + cb_ref[...]
        act, act_vjp = jax.vjp(_ffn_act, ucv)
        dfb = df_ref[...].astype(BF16)
        dact = lax.dot_general(dfb, wd_ref[...], (((1,), (1,)), ((), ())), preferred_element_type=F32)
        (ducv,) = act_vjp(dact)
        dwd = lax.dot_general(act.astype(BF16), dfb, (((0,), (0,)), ((), ())), preferred_element_type=F32)
        ducv_ref[...] = ducv.astype(ducv_ref.dtype)

        @pl.when(j == 0)
        def _():
            dwd_ref[...] = jnp.zeros_like(dwd_ref)

        dwd_ref[...] += dwd

    hs = _halo_specs(tr, w2, tps, nt)
    swap = lambda spec: pl.BlockSpec(spec.block_shape, lambda c, j, f=spec.index_map: f(j, c))
    return pl.pallas_call(
        body, name=name, grid=(2, nt),
        in_specs=[swap(s) for s in hs] + [
            pl.BlockSpec((8, w2), lambda c, j: (0, c)), pl.BlockSpec((1, w2), lambda c, j: (0, c)),
            pl.BlockSpec((FF_CHUNK, D), lambda c, j: (c, 0)), pl.BlockSpec((tr, D), lambda c, j: (j, 0))],
        out_specs=[pl.BlockSpec((tr, w2), lambda c, j: (j, c)), pl.BlockSpec((FF_CHUNK, D), lambda c, j: (c, 0))],
        out_shape=[jax.ShapeDtypeStruct((n, 2 * w2), BF16), jax.ShapeDtypeStruct((D_FF, D), F32)],
        compiler_params=_cparams(("parallel", "arbitrary")),
    )(u, u, u, cw, cb, wd, df)


def conv_bwd(name, ducv, u, cw, tr, tps):
    n = u.shape[0]
    nt = n // tr
    w2 = 2 * FF_CHUNK

    def body(g_ref, gp_ref, gn_ref, u_ref, up_ref, un_ref, cw_ref, du_ref, dcw_ref, dcb_ref):
        c, j = pl.program_id(0), pl.program_id(1)
        g = g_ref[...].astype(F32)
        gdn, gup = _shifted(g, gp_ref[...].astype(F32), gn_ref[...].astype(F32), j, tps)
        uu = u_ref[...].astype(F32)
        udn, uup = _shifted(uu, up_ref[...].astype(F32), un_ref[...].astype(F32), j, tps)
        cwv = cw_ref[...]
        du_ref[...] = (cwv[0:1] * gup + cwv[1:2] * g + cwv[2:3] * gdn).astype(du_ref.dtype)
        rows = _iota((8, w2), 0)
        s = lambda z: jnp.sum(z, axis=0, keepdims=True)
        dcw = (jnp.where(rows == 0, s(g * udn), 0.0) + jnp.where(rows == 1, s(g * uu), 0.0)
               + jnp.where(rows == 2, s(g * uup), 0.0))

        @pl.when(j == 0)
        def _():
            dcw_ref[...] = jnp.zeros_like(dcw_ref)
            dcb_ref[...] = jnp.zeros_like(dcb_ref)

        dcw_ref[...] += dcw
        dcb_ref[...] += s(g)

    hs = _halo_specs(tr, w2, tps, nt)
    swap = lambda spec: pl.BlockSpec(spec.block_shape, lambda c, j, f=spec.index_map: f(j, c))
    return pl.pallas_call(
        body, name=name, grid=(2, nt),
        in_specs=[swap(s) for s in hs] * 2 + [pl.BlockSpec((8, w2), lambda c, j: (0, c))],
        out_specs=[pl.BlockSpec((tr, w2), lambda c, j: (j, c)), pl.BlockSpec((8, w2), lambda c, j: (0, c)),
                   pl.BlockSpec((1, w2), lambda c, j: (0, c))],
        out_shape=[jax.ShapeDtypeStruct((n, 2 * w2), BF16), jax.ShapeDtypeStruct((8, 2 * w2), F32),
                   jax.ShapeDtypeStruct((1, 2 * w2), F32)],
        compiler_params=_cparams(("parallel", "arbitrary")),
    )(ducv, ducv, ducv, u, u, u, cw)


def loss_head(name, xf, target, nb, tps, tr):
    n = xf.shape[0]

    def body(x_ref, t_ref, dy_ref, l_ref):
        lat = (pl.program_id(0) % tps > 0).astype(F32)
        err = (x_ref[...] - t_ref[...]) * lat
        dy_ref[...] = err * (1.0 / D)
        l_ref[...] = jnp.zeros_like(l_ref) + 0.5 * jnp.sum(err * err) * (1.0 / D)

    def tmap(j):
        return ((j // tps) * (tps - 1) + jnp.maximum(j % tps - 1, 0), 0)

    return pl.pallas_call(
        body, name=name, grid=(n // tr,),
        in_specs=[pl.BlockSpec((tr, D), lambda j: (j, 0)), pl.BlockSpec((tr, D), tmap)],
        out_specs=[pl.BlockSpec((tr, D), lambda j: (j, 0)), pl.BlockSpec((1, 8, 128), lambda j: (j, 0, 0))],
        out_shape=[jax.ShapeDtypeStruct((n, D), F32), jax.ShapeDtypeStruct((n // tr, 8, 128), F32)],
        compiler_params=_cparams(("arbitrary",)),
    )(xf, target)


ADAM_MAX_ROWS = 512


def adamw(name, w, m, v, g8):
    r, c = w.shape
    k = g8.shape[0]
    rows = max(b for b in range(8, ADAM_MAX_ROWS + 1, 8) if r % b == 0)
    bc1 = 1.0 - ADAM_B1 ** ADAM_STEP
    bc2 = 1.0 - ADAM_B2 ** ADAM_STEP

    def body(w_ref, m_ref, v_ref, g_ref, go_ref, d_ref, mo_ref, vo_ref):
        g = g_ref[0].astype(F32)
        for i in range(1, k):
            g = g + g_ref[i].astype(F32)
        mn = ADAM_B1 * m_ref[...] + (1.0 - ADAM_B1) * g
        vn = ADAM_B2 * v_ref[...] + (1.0 - ADAM_B2) * (g * g)
        go_ref[...] = g
        mo_ref[...] = mn
        vo_ref[...] = vn
        d_ref[...] = -ADAM_LR * ((mn / bc1) / (jnp.sqrt(vn / bc2) + ADAM_EPS) + ADAM_WD * w_ref[...])

    blk = pl.BlockSpec((rows, c), lambda i: (i, 0))
    return pl.pallas_call(
        body, name=name, grid=(r // rows,),
        in_specs=[blk, blk, blk, pl.BlockSpec((k, rows, c), lambda i: (0, i, 0))],
        out_specs=[blk] * 4, out_shape=[jax.ShapeDtypeStruct((r, c), F32)] * 4,
        compiler_params=_cparams(("parallel",)),
    )(w, m, v, g8)


def ada_fwd(name, s, aw, ab):
    nl, _, cw = aw.shape

    def body(s_ref, w_ref, b_ref, o_ref):
        o_ref[0] = hdot(s_ref[...], w_ref[0]) + b_ref[0]

    return pl.pallas_call(
        body, name=name, grid=(nl,),
        in_specs=[pl.BlockSpec(s.shape, lambda l: (0, 0)), pl.BlockSpec((1, D, cw), lambda l: (l, 0, 0)),
                  pl.BlockSpec((1, 1, cw), lambda l: (l, 0, 0))],
        out_specs=pl.BlockSpec((1, s.shape[0], cw), lambda l: (l, 0, 0)),
        out_shape=jax.ShapeDtypeStruct((nl, s.shape[0], cw), F32), compiler_params=_cparams(("arbitrary",)),
    )(s, aw, ab)


def ada_bwd(name, s, aw, dmod):
    nl, _, cw = aw.shape

    def body(s_ref, w_ref, d_ref, dw_ref, ds_ref):
        dw_ref[0] = lax.dot_general(s_ref[...], d_ref[0], (((0,), (0,)), ((), ())), precision=HI,
                                    preferred_element_type=F32)
        ds_ref[0] = lax.dot_general(d_ref[0], w_ref[0], (((1,), (1,)), ((), ())), precision=HI,
                                    preferred_element_type=F32)

    return pl.pallas_call(
        body, name=name, grid=(nl,),
        in_specs=[pl.BlockSpec(s.shape, lambda l: (0, 0)), pl.BlockSpec((1, D, cw), lambda l: (l, 0, 0)),
                  pl.BlockSpec((1, s.shape[0], cw), lambda l: (l, 0, 0))],
        out_specs=[pl.BlockSpec((1, D, cw), lambda l: (l, 0, 0)), pl.BlockSpec((1, s.shape[0], D), lambda l: (l, 0, 0))],
        out_shape=[jax.ShapeDtypeStruct((nl, D, cw), F32), jax.ShapeDtypeStruct((nl, s.shape[0], D), F32)],
        compiler_params=_cparams(("arbitrary",)),
    )(s, aw, dmod)


def _place():
    return lax.axis_index("x"), lax.axis_index("y"), lax.axis_index("c")


def all_gather(name, x, in_vmem):
    r, c = x.shape

    def body(x_ref, out_ref, send_sems, recv_sems, local_sem):
        px, py, pc = _place()
        me, sibling = (px, py, pc), (px, py, 1 - pc)
        chips = [(1 - px, py), (px, 1 - py), (1 - px, 1 - py)]

        def rows(qx, qy, qc):
            return out_ref.at[pl.ds((4 * qx + 2 * qy + qc) * r, r), :]

        def copy(k, block, to, src=None):
            return pltpu.make_async_remote_copy(
                src_ref=rows(*block) if src is None else src, dst_ref=rows(*block),
                send_sem=send_sems.at[k], recv_sem=recv_sems.at[k], device_id=to, device_id_type=MESH)

        mine = pltpu.make_async_copy(x_ref, rows(*me), local_sem)
        mine.start()
        first = [copy(0, me, sibling, src=x_ref)]
        first += [copy(1 + j, me, (*chip, pc), src=x_ref) for j, chip in enumerate(chips)]
        for cp in first:
            cp.start()
        passed = [copy(4 + j, (*chip, pc), sibling) for j, chip in enumerate(chips)]
        for j, chip in enumerate(chips):
            copy(1 + j, (*chip, pc), me).wait_recv()
            passed[j].start()
        copy(0, sibling, me).wait_recv()
        for j, chip in enumerate(chips):
            copy(4 + j, (*chip, 1 - pc), me).wait_recv()
        for cp in first + passed:
            cp.wait_send()
        mine.wait()

    space = pltpu.VMEM if in_vmem else pl.ANY
    return pl.pallas_call(
        body, name=name, out_shape=jax.ShapeDtypeStruct((N_DEV * r, c), x.dtype),
        in_specs=[pl.BlockSpec(memory_space=space)], out_specs=pl.BlockSpec(memory_space=space),
        scratch_shapes=[pltpu.SemaphoreType.DMA((7,)), pltpu.SemaphoreType.DMA((7,)), pltpu.SemaphoreType.DMA],
        compiler_params=pltpu.CompilerParams(vmem_limit_bytes=VMEM_LIMIT_BYTES),
    )(x)


def all_to_all(name, x):
    _, r, c = x.shape

    def body(x_ref, out_ref, send_sems, recv_sems, local_sem):
        px, py, pc = _place()
        my = 4 * px + 2 * py + pc
        mine = pltpu.make_async_copy(x_ref.at[my], out_ref.at[my], local_sem)
        mine.start()
        copies = []
        for k in range(1, N_DEV):
            qx, qy, qc = px ^ (k >> 2 & 1), py ^ (k >> 1 & 1), pc ^ (k & 1)
            copies.append(pltpu.make_async_remote_copy(
                src_ref=x_ref.at[4 * qx + 2 * qy + qc], dst_ref=out_ref.at[my],
                send_sem=send_sems.at[k - 1], recv_sem=recv_sems.at[k - 1],
                device_id=(qx, qy, qc), device_id_type=MESH))
        for cp in copies:
            cp.start()
        for k, cp in enumerate(copies):
            cp.wait_send()
        for k in range(1, N_DEV):
            qx, qy, qc = px ^ (k >> 2 & 1), py ^ (k >> 1 & 1), pc ^ (k & 1)
            q = 4 * qx + 2 * qy + qc
            pltpu.make_async_remote_copy(
                src_ref=x_ref.at[q], dst_ref=out_ref.at[q], send_sem=send_sems.at[k - 1],
                recv_sem=recv_sems.at[k - 1], device_id=(qx, qy, qc), device_id_type=MESH).wait_recv()
        mine.wait()

    return pl.pallas_call(
        body, name=name, out_shape=jax.ShapeDtypeStruct(x.shape, x.dtype),
        in_specs=[pl.BlockSpec(memory_space=pl.ANY)], out_specs=pl.BlockSpec(memory_space=pl.ANY),
        scratch_shapes=[pltpu.SemaphoreType.DMA((7,)), pltpu.SemaphoreType.DMA((7,)), pltpu.SemaphoreType.DMA],
    )(x)


def _tables(seq, nctx_rows):
    f32 = np.float32
    pos = np.arange(seq, dtype=f32)
    ret_inv = (1.0 / (ROPE_BASE ** np.linspace(0.0, 1.0, 16, dtype=f32))).astype(f32)
    ang = pos[:, None] * ret_inv
    rc, rs = np.cos(ang).astype(f32), np.sin(ang).astype(f32)
    rcos = np.tile(np.concatenate([rc, rc], 1), (1, 4))
    rsin = np.tile(np.concatenate([-rs, rs], 1), (1, 4))
    rows = np.repeat(np.arange(seq // 64, dtype=f32), 64)
    cols = np.tile(np.arange(64, dtype=f32), seq // 64)
    ax_inv = (ROPE_BASE ** (-np.arange(8, dtype=f32) / 8)).astype(f32)
    ra, ca = rows[:, None] * ax_inv, cols[:, None] * ax_inv
    one, zero = np.ones((seq, 64), f32), np.zeros((seq, 64), f32)
    mcos = np.concatenate([one, np.cos(ra), np.cos(ra), np.cos(ca), np.cos(ca), one[:, :32]], 1)
    msin = np.concatenate([zero, -np.sin(ra), np.sin(ra), -np.sin(ca), np.sin(ca), zero[:, :32]], 1)
    ident = lambda t, v: np.concatenate([np.full((nctx_rows, 128), v, f32), t.astype(f32)], 0)
    return [jnp.asarray(ident(rcos, 1.0)), jnp.asarray(ident(rsin, 0.0)),
            jnp.asarray(ident(mcos, 1.0)), jnp.asarray(ident(msin, 0.0))]


def _prep_layer(w, l):
    z = lambda *s: jnp.zeros(s, F32)
    p = {}
    win = _rows(w["w_in_t"][l], W_IN_SEGS)
    p["w_in_t"] = jnp.concatenate([win, jnp.zeros((P_PAD - D_IN, D), win.dtype)], axis=0)
    p["w_up_t"] = _rows(w["ffn_up_t"][l], FF_SEGS)
    p["w_down"] = w["ffn_down"][l]
    p["w_out"] = w["w_out"][l]
    p["wuq"] = _pad_heads(w["mla_w_uq_t"][l], 96)
    p["wuk"] = _pad_heads(w["mla_w_uk_t"][l], 64)
    p["wuv"] = w["mla_w_uv_t"][l]
    gw = w["gla_gate_w"][l]
    p["w2f"] = z(128, 128).at[0:16].set(gw[0])
    p["w2b"] = z(128, 128).at[16:32].set(gw[1])
    p["b2f"], p["b2b"] = w["gla_gate_b"][l][0:1], w["gla_gate_b"][l][1:2]
    lg = jax.nn.log_sigmoid(w["ret_decay"][l])
    p["retf"], p["retb"] = jnp.repeat(lg[0], 32)[None], jnp.repeat(lg[1], 32)[None]
    p["qg"], p["kvg"] = w["mla_q_norm_g"][l][None], w["mla_kv_norm_g"][l][None]
    p["gng"] = jnp.tile(w["gla_norm_g"][l], 4)[None]
    p["ln1g"], p["ln1b"] = w["ln1_g"][l][None], w["ln1_b"][l][None]
    p["ln2g"], p["ln2b"] = w["ln2_g"][l][None], w["ln2_b"][l][None]
    p["cw"] = jnp.concatenate([_cols(w["ffn_conv_w"][l], FF_SEGS), z(5, 2 * D_FF)], axis=0)
    p["cb"] = _cols(w["ffn_conv_b"][l], FF_SEGS)[None]
    e2 = np.zeros((128, 1024), np.float32)
    for h in range(8):
        e2[32 + np.arange(32), h * 128 + 64 + np.arange(32)] = 1.0
    p["e2"] = jnp.asarray(e2)
    return p


def _pre_ins(pa, tabs, p):
    row = lambda w, cb: ("row", pa, w, cb)
    return [row(128, 0), row(128, 6), row(128, 7), row(256, 6), row(128, 14), row(128, 15)] + \
           [("pos", t, 128, 0) for t in tabs] + \
           [("par", p[k], 0, 0) for k in ("w2f", "w2b", "b2f", "b2b", "retf", "retb", "qg", "kvg", "wuq", "wuk", "wuv", "e2")]


_PRE_OUTS = [(128, F32)] * 7 + [(1024, BF16), (1024, BF16), (512, BF16)]
_PRE_WANT = [(i, F32) for i in range(6)] + [(i, F32) for i in range(10, 21)]


def _post_ins(ogf, ogb, orf, orb, om, pa, x, mod, p):
    return [("row", ogf, 256, 0), ("row", ogb, 256, 0), ("row", orf, 256, 0), ("row", orb, 256, 0),
            ("row", om, 512, 0), ("row", pa, 256, 2), ("row", pa, 256, 5), ("row", x, D, 0), ("tile", mod, 0, 0),
            ("par", p["gng"], 0, 0), ("par", p["w_out"], 0, 0), ("par", p["ln1g"], 0, 0), ("par", p["ln1b"], 0, 0)]


def layer_fwd(l, x, mod, p, tabs, dims):
    nb, tps, tr, nch, nctx = dims
    n = x.shape[0]
    pa = mm("proj", x, p["w_in_t"], F32, P_PAD, tr, mod=mod, sel=(1, 0))
    gq, af, ab, arf, arb, rq, rk, qa, ka, va = tile_fwd("mix_pre", pre_fn, _pre_ins(pa, tabs, p), _PRE_OUTS, n, tr, tps)
    ogf, ogb, gstf, gstb = scan_fwd("gla_scan", (gq, 128, 0), (pa, 128, 1), (pa, 256, 1), af, ab, nb, nch, nctx)
    orf, orb, rstf, rstb = scan_fwd("ret_scan", (rq, 128, 0), (rk, 128, 0), (pa, 256, 4), arf, arb, nb, nch, nctx)
    om, lse = mla_fwd("mla_attn", qa, ka, va, nb, tps, tr, nctx * CHUNK)
    (x1,) = tile_fwd("mix_post", post_fn, _post_ins(ogf, ogb, orf, orb, om, pa, x, mod, p), [(D, F32)], n, tr, tps)
    u = mm("ffn_up", x1, p["w_up_t"], BF16, FF_CHUNK, tr, mod=mod, sel=(4, 3))
    f, x2 = ffn2_fwd("ffn_down", u, p["cw"], p["cb"], p["w_down"], x1, mod, p["ln2g"], p["ln2b"], tr, tps)
    saved = dict(x=x, pa=pa, gq=gq, af=af, ab=ab, arf=arf, arb=arb, rq=rq, rk=rk, qa=qa, ka=ka, va=va,
                 ogf=ogf, ogb=ogb, gstf=gstf, gstb=gstb, orf=orf, orb=orb, rstf=rstf, rstb=rstb, om=om, lse=lse,
                 x1=x1, u=u, f=f)
    return x2, saved


def layer_bwd(l, dx2, s, mod, p, tabs, dims):
    nb, tps, tr, nch, nctx = dims
    n = dx2.shape[0]
    g = {}
    ln2_ins = [("row", s["x1"], D, 0), ("row", s["f"], D, 0), ("tile", mod, 0, 0),
               ("par", p["ln2g"], 0, 0), ("par", p["ln2b"], 0, 0)]
    dx1a, df, dmod_a, g["ln2g"], g["ln2b"] = tile_bwd(
        "ln2_bwd", ln2_fn, ln2_ins, [dx2], [(0, F32), (1, F32), (2, F32), (3, F32), (4, F32)], n, tr, tps)
    ducv, g["w_down"] = ffn2_bwd("ffn_down_bwd", s["u"], p["cw"], p["cb"], p["w_down"], df, tr, tps)
    du, g["cw"], g["cb"] = conv_bwd("conv_bwd", ducv, s["u"], p["cw"], tr, tps)
    g["w_up_t"] = mm_tn("ffn_up_dw", du, s["x1"], FF_CHUNK, tr, mod=mod, sel=(4, 3))
    dx1, dmod_b = mm_modbwd("ffn_up_dx", du, p["w_up_t"], s["x1"], mod, dx1a, (4, 3), tr)

    post_ins = _post_ins(s["ogf"], s["ogb"], s["orf"], s["orb"], s["om"], s["pa"], s["x"], mod, p)
    want = [(0, F32), (2, F32), (4, F32), (5, F32), (6, F32), (7, F32), (8, F32), (9, F32), (10, F32), (11, F32), (12, F32)]
    dog, dor, dom, dgg, drg, dxa, dmod_c, g["gng"], g["w_out"], g["ln1g"], g["ln1b"] = tile_bwd(
        "mix_post_bwd", post_fn, post_ins, [dx1], want, n, tr, tps)
    dqa, dka, dva = mla_bwd("mla_attn_bwd", s["qa"], s["ka"], s["va"], s["om"], s["lse"], dom, nb, tps, tr,
                            nctx * CHUNK)
    pa = s["pa"]
    gdqf, gdkf, gdvf, gdaf, gdqb, gdkb, gdvb, gdab = scan_bwd(
        "gla_scan_bwd", (s["gq"], 128, 0), (pa, 128, 1), (pa, 256, 1), s["af"], s["ab"], s["gstf"], s["gstb"], dog,
        nb, nch, nctx)
    rdqf, rdkf, rdvf, rdaf, rdqb, rdkb, rdvb, rdab = scan_bwd(
        "ret_scan_bwd", (s["rq"], 128, 0), (s["rk"], 128, 0), (pa, 256, 4), s["arf"], s["arb"], s["rstf"], s["rstb"],
        dor, nb, nch, nctx)

    pre_ins = _pre_ins(pa, tabs, p)
    extra = [gdqf, gdqb, rdqf, rdqb, rdkf, rdkb, gdkf, gdkb, gdvf, gdvb, rdvf, rdvb, dgg, drg]
    kinds = [i[0] for i in pre_ins]
    widx = [w[0] for w in _PRE_WANT]
    npre = len(pre_ins)

    def body(*refs):
        vals = [_load(k, r) for k, r in zip(kinds, refs[:npre])]
        rd = lambda i: refs[npre + i][...].astype(F32)
        cots = (rd(0) + rd(1), rd(14), rd(15), rd(16), rd(17), rd(2) + rd(3), rd(4) + rd(5), rd(18), rd(19), rd(20))

        def f(*dv):
            full = list(vals)
            for i, v in zip(widx, dv):
                full[i] = v
            return tuple(pre_fn(*full))

        _, vjp = jax.vjp(f, *[vals[i] for i in widx])
        grads = vjp(cots)
        dgq, drq, drk, dcq, dckv, dmisc = grads[:6]
        dp = jnp.concatenate([dgq, rd(6) + rd(7), rd(8) + rd(9), rd(12), drq, drk, rd(10) + rd(11), rd(13),
                              dcq, dckv, dmisc], axis=1)
        outs = refs[npre + 21:]
        outs[0][...] = dp.astype(BF16)
        first = pl.program_id(0) == 0
        for r, gr in zip(outs[1:], grads[6:]):
            @pl.when(first)
            def _(r=r):
                r[...] = jnp.zeros_like(r)
            r[...] += gr

    cot_arrays = extra + [gdaf, gdab, rdaf, rdab, dqa, dka, dva]
    par_arrays = [pre_ins[i][1] for i in range(10, 21)]
    res = pl.pallas_call(
        body, name="mix_pre_bwd", grid=(n // tr,),
        in_specs=[_spec(k, a, w, cb, tr, tps) for k, a, w, cb in pre_ins]
        + [pl.BlockSpec((tr, c.shape[1]), lambda j: (j, 0)) for c in cot_arrays],
        out_specs=[pl.BlockSpec((tr, P_PAD), lambda j: (j, 0))] + [pl.BlockSpec(a.shape, lambda j: (0, 0)) for a in par_arrays],
        out_shape=[jax.ShapeDtypeStruct((n, P_PAD), BF16)] + [jax.ShapeDtypeStruct(a.shape, F32) for a in par_arrays],
        compiler_params=_cparams(("arbitrary",)),
    )(*[i[1] for i in pre_ins], *cot_arrays)
    dp = res[0]
    for k, v in zip(("w2f", "w2b", "b2f", "b2b", "retf", "retb", "qg", "kvg", "wuq", "wuk", "wuv"), res[1:]):
        g[k] = v
    g["w_in_t"] = mm_tn("proj_dw", dp, s["x"], P_PAD, tr, mod=mod, sel=(1, 0))
    dx, dmod_d = mm_modbwd("proj_dx", dp, p["w_in_t"], s["x"], mod, dxa, (1, 0), tr)
    return dx, dmod_a + dmod_b + dmod_c + dmod_d, g


def _unprep_grads(g, w, l):
    o = {}
    o["w_in_t"] = _rows(g["w_in_t"], W_IN_INV_SEGS)
    o["ffn_up_t"] = _rows(g["w_up_t"], FF_SEGS)
    o["ffn_down"] = g["w_down"]
    o["w_out"] = g["w_out"]
    o["mla_w_uq_t"] = _unpad_heads(g["wuq"], 96)
    o["mla_w_uk_t"] = _unpad_heads(g["wuk"], 64)
    o["mla_w_uv_t"] = g["wuv"]
    o["gla_gate_w"] = jnp.stack([g["w2f"][0:16], g["w2b"][16:32]])
    o["gla_gate_b"] = jnp.concatenate([g["b2f"], g["b2b"]], axis=0)
    dlg = jnp.stack([g["retf"].reshape(4, 32).sum(-1), g["retb"].reshape(4, 32).sum(-1)])
    o["ret_decay"] = dlg * jax.nn.sigmoid(-w["ret_decay"][l])
    o["mla_q_norm_g"], o["mla_kv_norm_g"] = g["qg"][0], g["kvg"][0]
    o["gla_norm_g"] = g["gng"].reshape(4, 64).sum(0)
    o["ln1_g"], o["ln1_b"], o["ln2_g"], o["ln2_b"] = g["ln1g"][0], g["ln1b"][0], g["ln2g"][0], g["ln2b"][0]
    o["ffn_conv_w"] = _cols(g["cw"][0:3], FF_SEGS)
    o["ffn_conv_b"] = _cols(g["cb"][0], FF_SEGS)
    return o


def local_step(xs, target, modtab, w, dims):
    nb, tps, tr, nch, nctx = dims
    tabs = _tables((tps - 1) * tr, tr)
    x = xs
    saved, preps = [], []
    for l in range(DEPTH):
        p = _prep_layer(w, l)
        x, s = layer_fwd(l, x, modtab[l], p, tabs, dims)
        saved.append(s)
        preps.append(p)
    dy, lpart = loss_head("loss_head", x, target, nb, tps, tr)
    loss = jnp.sum(lpart[:, 0, 0])
    dx = dy
    dmods, grads = [None] * DEPTH, [None] * DEPTH
    for l in reversed(range(DEPTH)):
        dx, dmods[l], g = layer_bwd(l, dx, saved[l], modtab[l], preps[l], tabs, dims)
        grads[l] = _unprep_grads(g, w, l)
    gstack = {k: jnp.stack([grads[l][k] for l in range(DEPTH)]) for k in grads[0]}
    return loss, dx, jnp.stack(dmods), gstack


BIG = [("ffn_up", 2), ("ffn_down", 1), ("w_out", 1), ("w_in", 2), ("mla_w_uq", 2), ("mla_w_uk", 2), ("mla_w_uv", 2)]
SMALL = ["ada_b", "gla_gate_w", "gla_gate_b", "gla_norm_g", "ret_decay", "mla_q_norm_g", "mla_kv_norm_g",
         "ln1_g", "ln1_b", "ffn_conv_b", "ln2_g", "ln2_b"]
PACK_C = 1024


def _big_key(k, axis):
    return k + "_t" if axis == 2 else k


def _shard_rows(a, axis):
    a = jnp.swapaxes(a, 1, 2) if axis == 2 else a
    return a.reshape(-1, PACK_C)


def _shard_from_rows(rows, shape, axis):
    l, r, c = shape
    return jnp.swapaxes(rows.reshape(l, c, r), 1, 2) if axis == 2 else rows.reshape(l, r, c)


def sum8(name, g8):
    k, r, c = g8.shape
    rows = max(b for b in range(16, ADAM_MAX_ROWS + 1, 16) if r % b == 0)

    def body(g_ref, o_ref):
        g = g_ref[0].astype(F32)
        for i in range(1, k):
            g = g + g_ref[i].astype(F32)
        o_ref[...] = g

    return pl.pallas_call(
        body, name=name, grid=(r // rows,), in_specs=[pl.BlockSpec((k, rows, c), lambda i: (0, i, 0))],
        out_specs=pl.BlockSpec((rows, c), lambda i: (i, 0)), out_shape=jax.ShapeDtypeStruct((r, c), F32),
        compiler_params=_cparams(("parallel",)),
    )(g8)


def _pack(arrs, dtype):
    flat = jnp.concatenate([a.reshape(-1).astype(dtype) for a in arrs])
    pad = (-flat.shape[0]) % (8 * PACK_C)
    return jnp.concatenate([flat, jnp.zeros((pad,), dtype)]).reshape(-1, PACK_C)


def _unpack(flat2d, shapes):
    flat = flat2d.reshape(-1)
    out, off = [], 0
    for s in shapes:
        sz = int(np.prod(s))
        out.append(flat[off:off + sz].reshape(s))
        off += sz
    return out


def _row_shape(shape, axis):
    l, r, c = shape
    return (l, c, r) if axis == 2 else (l, r, c)


def _whole_from_blocks(blocks, shape, axis):
    l, r, c = _row_shape(shape, axis)
    return blocks.reshape(N_DEV, l, r, c).transpose(1, 0, 2, 3).reshape(l, N_DEV * r, c)


def _blocks_from_whole(whole, shape, axis):
    l, r, c = _row_shape(shape, axis)
    return whole.reshape(l, N_DEV, r, c).transpose(1, 0, 2, 3).reshape(N_DEV, -1, PACK_C)


def kernel(x, c, ctx, c_ctx, ada_w, ada_b, w_in, gla_gate_w, gla_gate_b, gla_norm_g, ret_decay, mla_q_norm_g, mla_kv_norm_g, mla_w_uq, mla_w_uk, mla_w_uv, w_out, ln1_g, ln1_b, ffn_up, ffn_conv_w, ffn_conv_b, ffn_down, ln2_g, ln2_b, loss_target, m_c_ctx, m_ada_w, m_ada_b, m_w_in, m_gla_gate_w, m_gla_gate_b, m_gla_norm_g, m_ret_decay, m_mla_q_norm_g, m_mla_kv_norm_g, m_mla_w_uq, m_mla_w_uk, m_mla_w_uv, m_w_out, m_ln1_g, m_ln1_b, m_ffn_up, m_ffn_conv_w, m_ffn_conv_b, m_ffn_down, m_ln2_g, m_ln2_b, v_c_ctx, v_ada_w, v_ada_b, v_w_in, v_gla_gate_w, v_gla_gate_b, v_gla_norm_g, v_ret_decay, v_mla_q_norm_g, v_mla_kv_norm_g, v_mla_w_uq, v_mla_w_uk, v_mla_w_uv, v_w_out, v_ln1_g, v_ln1_b, v_ffn_up, v_ffn_conv_w, v_ffn_conv_b, v_ffn_down, v_ln2_g, v_ln2_b):
    names = ["c_ctx", "ada_w", "ada_b", "w_in", "gla_gate_w", "gla_gate_b", "gla_norm_g", "ret_decay", "mla_q_norm_g",
             "mla_kv_norm_g", "mla_w_uq", "mla_w_uk", "mla_w_uv", "w_out", "ln1_g", "ln1_b", "ffn_up", "ffn_conv_w",
             "ffn_conv_b", "ffn_down", "ln2_g", "ln2_b"]
    loc = locals()
    W = {k: loc[k] for k in names}
    M = {k: loc["m_" + k] for k in names}
    V = {k: loc["v_" + k] for k in names}

    nb, seq, _ = x.shape
    tr = ctx.shape[1]
    tps = 1 + seq // tr
    t = tps * tr
    n = nb * t
    nt = nb * tps
    dims = (nb, tps, tr, t // CHUNK, tr // CHUNK)
    px, py, pc = _place()
    me = 4 * px + 2 * py + pc
    ncol = ada_w.shape[2]

    cw_loc = ffn_conv_w.reshape(-1)
    g1 = jnp.concatenate([c.reshape(-1), cw_loc])
    g1 = jnp.concatenate([g1, jnp.zeros(((-g1.shape[0]) % (8 * PACK_C),), F32)]).reshape(-1, PACK_C)
    r1 = g1.shape[0]
    g1a = all_gather("gather_cond", g1, True).reshape(N_DEV, -1)
    c_all = g1a[:, :nb * D].reshape(N_DEV * nb, D)
    cw_all = g1a[:, nb * D:nb * D + cw_loc.shape[0]].reshape(N_DEV, DEPTH, 3, -1).transpose(1, 2, 0, 3).reshape(DEPTH, 3, -1)

    wpack = jnp.concatenate([_shard_rows(W[k].astype(BF16), ax) for k, ax in BIG], axis=0)
    prows = wpack.shape[0]
    wall = all_gather("gather_weights", wpack, False).reshape(N_DEV, prows, PACK_C)
    wl, offs, off = {}, {}, 0
    for k, ax in BIG:
        rows = int(np.prod(W[k].shape)) // PACK_C
        wl[_big_key(k, ax)] = _whole_from_blocks(wall[:, off:off + rows], W[k].shape, ax)
        offs[k] = (off, rows)
        off += rows
    for k in SMALL[1:]:
        wl[k] = W[k]
    wl["ffn_conv_w"] = cw_all

    srows = 40
    s_in = jnp.concatenate([c_all, c_ctx[None], jnp.zeros((srows - N_DEV * nb - 1, D), F32)], axis=0)
    s_act = _silu(s_in)
    ab_loc = lax.dynamic_slice_in_dim(ada_b, me * ncol, ncol, axis=1)[:, None, :]
    mod_part = ada_fwd("ada_fwd", s_act, ada_w, ab_loc)
    mod_all = all_gather("gather_mod", mod_part.reshape(-1, ncol), True).reshape(N_DEV, DEPTH, srows, ncol)
    mod_rows = mod_all.transpose(1, 2, 0, 3).reshape(DEPTH, srows, N_DEV * ncol)
    mod_l = lax.dynamic_slice_in_dim(mod_rows, me * nb, nb, axis=1).reshape(DEPTH, nb, 6, D)
    mod_c = mod_rows[:, N_DEV * nb].reshape(DEPTH, 1, 6, D)
    tile_is_ctx = (jnp.arange(tps) == 0)[None, None, :, None, None]
    modtab = jnp.where(tile_is_ctx, mod_c[:, :, None], mod_l[:, :, None])
    modtab = jnp.concatenate([modtab, jnp.zeros((DEPTH, nb, tps, 2, D), F32)], axis=3).reshape(DEPTH, nt, 8, D)

    xs = jnp.concatenate([ctx, x], axis=1).reshape(n, D)
    loss_loc, dxs, dmodtab, gl = local_step(xs, loss_target.reshape(nb * seq, D), modtab, wl, dims)
    loss = lax.psum(loss_loc, ("x", "y", "c"))
    grad_x = dxs.reshape(nb, t, D)[:, tr:]

    dm = dmodtab.reshape(DEPTH, nb, tps, 8, D)[:, :, :, :6]
    dmod_l = dm[:, :, 1:].sum(2).reshape(DEPTH, nb, 6 * D)
    dmod_c = dm[:, :, 0].sum(1).reshape(DEPTH, 1, 6 * D)
    gl["ada_b"] = dmod_l.sum(1) + dmod_c[:, 0]
    small_list = [gl[k] for k in SMALL] + [gl["ffn_conv_w"]]
    small_shapes = [a.shape for a in small_list]
    spack = _pack(small_list + [jnp.concatenate([dmod_l, dmod_c], axis=1)], F32)
    rs = spack.shape[0]
    sall = all_gather("gather_small_grads", spack, True).reshape(N_DEV, rs, PACK_C)
    nsmall = sum(int(np.prod(s)) for s in small_shapes)
    dmo = sall.reshape(N_DEV, -1)[:, nsmall:nsmall + DEPTH * (nb + 1) * 6 * D].reshape(N_DEV, DEPTH, nb + 1, 6 * D)
    dl_all = dmo[:, :, :nb].transpose(1, 0, 2, 3).reshape(DEPTH, N_DEV * nb, 6 * D)
    dc_all = dmo[:, :, nb].sum(0)[:, None]
    dmod_rows = jnp.concatenate([dl_all, dc_all, jnp.zeros((DEPTH, srows - N_DEV * nb - 1, 6 * D), F32)], axis=1)
    dmod_loc = lax.dynamic_slice_in_dim(dmod_rows.reshape(DEPTH, srows, N_DEV, ncol), me, 1, axis=2)[:, :, 0]
    d_ada_w, d_s = ada_bwd("ada_bwd", s_act, ada_w, dmod_loc)
    sg = jax.nn.sigmoid(c_ctx)
    dcc = d_s[:, N_DEV * nb].sum(0) * (sg * (1.0 + c_ctx * (1.0 - sg)))
    ccp = jnp.concatenate([dcc[None], jnp.zeros((7, D), F32)], axis=0)
    ccall = all_gather("gather_cctx", ccp, True).reshape(N_DEV, 8, D)

    gsend = jnp.concatenate([_blocks_from_whole(gl[_big_key(k, ax)].astype(BF16), W[k].shape, ax) for k, ax in BIG],
                            axis=1)
    gsum = sum8("grad_sum", all_to_all("grad_all_to_all", gsend))
    res = {}

    def update2d(tag, k, g):
        last = W[k].shape[-1]
        outs = adamw(tag, W[k].reshape(-1, last), M[k].reshape(-1, last), V[k].reshape(-1, last),
                     g.reshape(1, -1, last))
        res[k] = [a.reshape(W[k].shape) for a in outs]

    for k, ax in BIG:
        off, rows = offs[k]
        update2d("adamw_" + k, k, _shard_from_rows(gsum[off:off + rows], W[k].shape, ax))

    def update(tag, keys, g8):
        outs = adamw(tag, _pack([W[k] for k in keys], F32), _pack([M[k] for k in keys], F32),
                     _pack([V[k] for k in keys], F32), g8)
        for i, arr in enumerate(outs):
            for k, a in zip(keys, _unpack(arr, [W[k].shape for k in keys])):
                res.setdefault(k, [None] * 4)[i] = a

    nrep = sum(int(np.prod(W[k].shape)) for k in SMALL)
    sflat = sall.reshape(N_DEV, -1)
    def pack8(a):
        a = a.reshape(N_DEV, -1)
        pad = (-a.shape[1]) % (8 * PACK_C)
        return jnp.concatenate([a, jnp.zeros((N_DEV, pad), F32)], axis=1).reshape(N_DEV, -1, PACK_C)

    update("adamw_small", SMALL, pack8(sflat[:, :nrep]))
    ncw = ffn_conv_w.shape[2]
    cw8 = sflat[:, nrep:nsmall].reshape(N_DEV, DEPTH, 3, N_DEV * ncw)
    cw8 = lax.dynamic_slice_in_dim(cw8, me * ncw, ncw, axis=3)
    update("adamw_conv", ["ffn_conv_w"], pack8(cw8))
    update2d("adamw_ada", "ada_w", d_ada_w)
    update("adamw_cctx", ["c_ctx"], ccall)

    out = [loss, grad_x]
    for i in range(4):
        out += [res[k][i] for k in names]
    return tuple(out)
```

```python
import functools
import math

import numpy as np
import jax
import jax.numpy as jnp
from jax import lax
from jax.experimental import pallas as pl
from jax.experimental.pallas import tpu as pltpu

F32 = jnp.float32
BF16 = jnp.bfloat16
HI = lax.Precision.HIGHEST
MESH = pl.DeviceIdType.MESH

N_DEV = 8
D = 1024
DEPTH = 4
CHUNK = 64
EPS = 1e-6
ALPHA = (2 * DEPTH) ** 0.25
GLA_TAU = 16.0
ROPE_BASE = 10000.0
MLA_SCALE = 96 ** -0.5
D_FF = 2816
FF_CHUNK = 1408
P_PAD = 2048
VMEM_LIMIT_BYTES = 56 << 20

ADAM_LR, ADAM_B1, ADAM_B2, ADAM_EPS, ADAM_WD, ADAM_STEP = 0.001, 0.9, 0.999, 1e-08, 0.01, 10

D_IN = 1984
W_IN_SEGS = [(0, 512), (544, 1408), (512, 32), (1952, 32)]
W_IN_INV_SEGS = [(0, 512), (1920, 32), (512, 1408), (1952, 32)]
FF_SEGS = [(0, FF_CHUNK), (D_FF, FF_CHUNK), (FF_CHUNK, FF_CHUNK), (D_FF + FF_CHUNK, FF_CHUNK)]


def _cols(a, segs):
    return jnp.concatenate([a[..., s:s + n] for s, n in segs], axis=-1)


def _rows(a, segs):
    return jnp.concatenate([a[s:s + n] for s, n in segs], axis=0)


def _pad_heads(wt, per_head):
    c = wt.shape[1]
    wt = wt.reshape(8, per_head, c)
    return jnp.concatenate([wt, jnp.zeros((8, 128 - per_head, c), wt.dtype)], axis=1).reshape(1024, c)


def _unpad_heads(g, per_head):
    c = g.shape[1]
    return g.reshape(8, 128, c)[:, :per_head].reshape(8 * per_head, c)


def _cparams(sem=None):
    return pltpu.CompilerParams(vmem_limit_bytes=VMEM_LIMIT_BYTES, dimension_semantics=sem)


@jax.custom_vjp
def bdot(a, w):
    return jnp.dot(a.astype(BF16), w.astype(BF16), preferred_element_type=F32)


def _bdot_fwd(a, w):
    return bdot(a, w), (a, w)


def _bdot_bwd(res, ct):
    a, w = res
    ctb = ct.astype(BF16)
    da = lax.dot_general(ctb, w.astype(BF16), (((1,), (1,)), ((), ())), preferred_element_type=F32)
    dw = lax.dot_general(a.astype(BF16), ctb, (((0,), (0,)), ((), ())), preferred_element_type=F32)
    return da.astype(a.dtype), dw.astype(w.dtype)


bdot.defvjp(_bdot_fwd, _bdot_bwd)


@jax.custom_vjp
def bdot_nt(a, wt):
    return lax.dot_general(a.astype(BF16), wt.astype(BF16), (((1,), (1,)), ((), ())), preferred_element_type=F32)


def _bdot_nt_fwd(a, wt):
    return bdot_nt(a, wt), (a, wt)


def _bdot_nt_bwd(res, ct):
    a, wt = res
    ctb = ct.astype(BF16)
    da = jnp.dot(ctb, wt.astype(BF16), preferred_element_type=F32)
    dwt = lax.dot_general(ctb, a.astype(BF16), (((0,), (0,)), ((), ())), preferred_element_type=F32)
    return da.astype(a.dtype), dwt.astype(wt.dtype)


bdot_nt.defvjp(_bdot_nt_fwd, _bdot_nt_bwd)


@jax.custom_vjp
def bdot_tn(a, b):
    return lax.dot_general(a.astype(BF16), b.astype(BF16), (((0,), (0,)), ((), ())), preferred_element_type=F32)


def _bdot_tn_fwd(a, b):
    return bdot_tn(a, b), (a, b)


def _bdot_tn_bwd(res, ct):
    a, b = res
    ctb = ct.astype(BF16)
    da = lax.dot_general(b.astype(BF16), ctb, (((1,), (1,)), ((), ())), preferred_element_type=F32)
    db = jnp.dot(a.astype(BF16), ctb, preferred_element_type=F32)
    return da.astype(a.dtype), db.astype(b.dtype)


bdot_tn.defvjp(_bdot_tn_fwd, _bdot_tn_bwd)


def _swap_fn(half):
    def swap(x):
        n = x.shape[1]
        first = (_iota(x.shape, 1) // half) % 2 == 0
        return jnp.where(first, pltpu.roll(x, n - half, 1), pltpu.roll(x, half, 1))

    f = jax.custom_vjp(swap)
    f.defvjp(lambda x: (swap(x), None), lambda _, ct: (swap(ct),))
    return f


_swap16 = _swap_fn(16)
_swap8 = _swap_fn(8)


def hdot(a, b):
    return jnp.dot(a, b, precision=HI, preferred_element_type=F32)


def _iota(shape, axis):
    return lax.broadcasted_iota(jnp.int32, shape, axis)


def _group_avg(n, g):
    return (_iota((n, n), 0) // g == _iota((n, n), 1) // g).astype(F32) * (1.0 / g)


def _silu(x):
    return x * jax.nn.sigmoid(x)


def _layer_norm(z, g, b):
    mu = jnp.mean(z, axis=-1, keepdims=True)
    zc = z - mu
    var = jnp.mean(zc * zc, axis=-1, keepdims=True)
    return zc * lax.rsqrt(var + EPS) * g + b


def _rms(x, g):
    return x * lax.rsqrt(jnp.mean(x * x, axis=-1, keepdims=True) + EPS) * g


def mm(name, a, wt, out_dtype, tn, tr, mod=None, sel=None):
    n, k = a.shape
    nw = wt.shape[0]

    def body(*refs):
        if mod is not None:
            a_ref, m_ref, w_ref, o_ref = refs
            m = m_ref[0]
            av = a_ref[...] * (1.0 + m[sel[0]:sel[0] + 1]) + m[sel[1]:sel[1] + 1]
        else:
            a_ref, w_ref, o_ref = refs
            av = a_ref[...]
        o_ref[...] = lax.dot_general(av.astype(BF16), w_ref[...], (((1,), (1,)), ((), ())),
                                     preferred_element_type=F32).astype(o_ref.dtype)

    in_specs = [pl.BlockSpec((tr, k), lambda c, j: (j, 0))]
    args = [a]
    if mod is not None:
        in_specs.append(pl.BlockSpec((1, 8, k), lambda c, j: (j, 0, 0)))
        args.append(mod)
    in_specs.append(pl.BlockSpec((tn, k), lambda c, j: (c, 0)))
    args.append(wt)
    return pl.pallas_call(
        body, name=name, grid=(nw // tn, n // tr), in_specs=in_specs,
        out_specs=pl.BlockSpec((tr, tn), lambda c, j: (j, c)),
        out_shape=jax.ShapeDtypeStruct((n, nw), out_dtype), compiler_params=_cparams(("parallel", "arbitrary")),
    )(*args)


def mm_tn(name, dc, a, tn, tr, mod=None, sel=None):
    n, k = a.shape
    nw = dc.shape[1]

    def body(*refs):
        if mod is not None:
            d_ref, a_ref, m_ref, o_ref = refs
            m = m_ref[0]
            av = a_ref[...] * (1.0 + m[sel[0]:sel[0] + 1]) + m[sel[1]:sel[1] + 1]
        else:
            d_ref, a_ref, o_ref = refs
            av = a_ref[...]

        @pl.when(pl.program_id(1) == 0)
        def _():
            o_ref[...] = jnp.zeros_like(o_ref)

        o_ref[...] += lax.dot_general(d_ref[...].astype(BF16), av.astype(BF16), (((0,), (0,)), ((), ())),
                                      preferred_element_type=F32)

    in_specs = [pl.BlockSpec((tr, tn), lambda c, j: (j, c)), pl.BlockSpec((tr, k), lambda c, j: (j, 0))]
    args = [dc, a]
    if mod is not None:
        in_specs.append(pl.BlockSpec((1, 8, k), lambda c, j: (j, 0, 0)))
        args.append(mod)
    return pl.pallas_call(
        body, name=name, grid=(nw // tn, n // tr), in_specs=in_specs,
        out_specs=pl.BlockSpec((tn, k), lambda c, j: (c, 0)),
        out_shape=jax.ShapeDtypeStruct((nw, k), F32), compiler_params=_cparams(("parallel", "arbitrary")),
    )(*args)


def mm_modbwd(name, dc, wt, x, mod, add, sel, tr):
    n, k = dc.shape
    dm = wt.shape[1]

    def body(dc_ref, wt_ref, x_ref, m_ref, add_ref, dx_ref, dm_ref):
        dh = jnp.dot(dc_ref[...].astype(BF16), wt_ref[...], preferred_element_type=F32)
        m = m_ref[0]
        dx_ref[...] = add_ref[...] + dh * (1.0 + m[sel[0]:sel[0] + 1])
        dsc = jnp.sum(dh * x_ref[...], axis=0, keepdims=True)
        dsh = jnp.sum(dh, axis=0, keepdims=True)
        rows = _iota((8, dm), 0)
        dm_ref[0] = jnp.where(rows == sel[0], dsc, 0.0) + jnp.where(rows == sel[1], dsh, 0.0)

    return pl.pallas_call(
        body, name=name, grid=(n // tr,),
        in_specs=[pl.BlockSpec((tr, k), lambda j: (j, 0)), pl.BlockSpec((k, dm), lambda j: (0, 0)),
                  pl.BlockSpec((tr, dm), lambda j: (j, 0)), pl.BlockSpec((1, 8, dm), lambda j: (j, 0, 0)),
                  pl.BlockSpec((tr, dm), lambda j: (j, 0))],
        out_specs=[pl.BlockSpec((tr, dm), lambda j: (j, 0)), pl.BlockSpec((1, 8, dm), lambda j: (j, 0, 0))],
        out_shape=[jax.ShapeDtypeStruct((n, dm), F32), jax.ShapeDtypeStruct((n // tr, 8, dm), F32)],
        compiler_params=_cparams(("arbitrary",)),
    )(dc, wt, x, mod, add)


def _spec(kind, arr, width, cb, tr, tps):
    if kind == "row":
        return pl.BlockSpec((tr, width), lambda j: (j, cb))
    if kind == "pos":
        return pl.BlockSpec((tr, width), lambda j: (j % tps, cb))
    if kind == "tile":
        return pl.BlockSpec((1,) + arr.shape[1:], lambda j: (j, 0, 0))
    if kind == "par":
        return pl.BlockSpec(arr.shape, lambda j: (0, 0))
    raise ValueError(kind)


def _load(kind, ref):
    v = ref[0] if kind == "tile" else ref[...]
    return v.astype(F32)


def tile_fwd(name, fn, ins, outs, n, tr, tps):
    kinds = [i[0] for i in ins]

    def body(*refs):
        vals = [_load(k, r) for k, r in zip(kinds, refs[:len(ins)])]
        res = fn(*vals)
        for r, o in zip(refs[len(ins):], res):
            r[...] = o.astype(r.dtype)

    return pl.pallas_call(
        body, name=name, grid=(n // tr,),
        in_specs=[_spec(k, a, w, cb, tr, tps) for k, a, w, cb in ins],
        out_specs=[pl.BlockSpec((tr, w), lambda j: (j, 0)) for w, _ in outs],
        out_shape=[jax.ShapeDtypeStruct((n, w), dt) for w, dt in outs],
        compiler_params=_cparams(("arbitrary",)),
    )(*[i[1] for i in ins])


def tile_bwd(name, fn, ins, cots, want, n, tr, tps):
    kinds = [i[0] for i in ins]
    widx = [w[0] for w in want]
    ni, nc = len(ins), len(cots)

    def body(*refs):
        vals = [_load(k, r) for k, r in zip(kinds, refs[:ni])]
        cvals = tuple(r[...].astype(F32) for r in refs[ni:ni + nc])

        def f(*dv):
            full = list(vals)
            for i, v in zip(widx, dv):
                full[i] = v
            return tuple(fn(*full))

        _, vjp = jax.vjp(f, *[vals[i] for i in widx])
        grads = vjp(cvals)
        first = pl.program_id(0) == 0
        for r, g, i in zip(refs[ni + nc:], grads, widx):
            if kinds[i] == "par":
                @pl.when(first)
                def _(r=r):
                    r[...] = jnp.zeros_like(r)
                r[...] += g
            elif kinds[i] == "tile":
                r[0] = g.astype(r.dtype)
            else:
                r[...] = g.astype(r.dtype)

    out_specs, out_shape = [], []
    for i, dt in want:
        k, a, w, cb = ins[i]
        if k == "par":
            out_specs.append(pl.BlockSpec(a.shape, lambda j: (0, 0)))
            out_shape.append(jax.ShapeDtypeStruct(a.shape, F32))
        elif k == "tile":
            out_specs.append(pl.BlockSpec((1,) + a.shape[1:], lambda j: (j, 0, 0)))
            out_shape.append(jax.ShapeDtypeStruct(a.shape, F32))
        else:
            out_specs.append(pl.BlockSpec((tr, w), lambda j: (j, 0)))
            out_shape.append(jax.ShapeDtypeStruct((n, w), dt))
    return pl.pallas_call(
        body, name=name, grid=(n // tr,),
        in_specs=[_spec(k, a, w, cb, tr, tps) for k, a, w, cb in ins]
        + [pl.BlockSpec((tr, c.shape[1]), lambda j: (j, 0)) for c in cots],
        out_specs=out_specs, out_shape=out_shape, compiler_params=_cparams(("arbitrary",)),
    )(*[i[1] for i in ins], *cots)


def pre_fn(p_gq, p_rq, p_rk, p_cq, p_ckv, p_misc, rcos, rsin, mcos, msin,
           w2f, w2b, b2f, b2b, retf, retb, qg, kvg, wuq, wuk, wuv, e2):
    tr = p_gq.shape[0]
    gq = p_gq * (32 ** -0.5)
    af = jax.nn.log_sigmoid(hdot(p_misc, w2f) + b2f) * (1.0 / GLA_TAU)
    ab = jax.nn.log_sigmoid(hdot(p_misc, w2b) + b2b) * (1.0 / GLA_TAU)
    arf = jnp.zeros((tr, 128), F32) + retf
    arb = jnp.zeros((tr, 128), F32) + retb
    rq = p_rq * rcos + _swap16(p_rq) * rsin
    rks = p_rk * (32 ** -0.5)
    rk = rks * rcos + _swap16(rks) * rsin
    qp = bdot_nt(_rms(p_cq, qg), wuq) * MLA_SCALE
    ckvn = _rms(p_ckv, kvg)
    kp = bdot_nt(ckvn, wuk) + hdot(p_misc, e2)
    mc, ms = jnp.tile(mcos, (1, 8)), jnp.tile(msin, (1, 8))
    v = bdot_nt(ckvn, wuv)
    return gq, af, ab, arf, arb, rq, rk, qp * mc + _swap8(qp) * ms, kp * mc + _swap8(kp) * ms, v


def post_fn(ogf, ogb, orf, orb, om, gg, rg, x, mod, gng, wout, lng, lnb):
    avg = _group_avg(256, 64)
    og = ogf + ogb
    mg = og * lax.rsqrt(hdot(og * og, avg) + EPS) * gng * _silu(gg)
    orr = orf + orb
    oc = orr - hdot(orr, avg)
    mr = oc * lax.rsqrt(hdot(oc * oc, avg) + EPS) * _silu(rg)
    m = jnp.concatenate([mg, mr, om], axis=1)
    y = bdot(m, wout)
    return (_layer_norm(ALPHA * x + mod[2:3] * y, lng, lnb),)


def ln2_fn(x1, f, mod, lng, lnb):
    return (_layer_norm(ALPHA * x1 + mod[5:6] * f, lng, lnb),)


def scan_step(q, k, v, a, s, rev):
    ii, jj = _iota((CHUNK, CHUNK), 0), _iota((CHUNK, CHUNK), 1)
    tri = ((jj >= ii) if rev else (jj <= ii)).astype(F32)
    b = hdot(tri, a)
    btot = jnp.sum(a, axis=0, keepdims=True)
    qe = q * jnp.exp(b - btot)
    ke = k * jnp.exp(btot - b)
    lane = _iota((1, 128), 1)
    q4 = jnp.concatenate([qe * (lane // 32 == h).astype(F32) for h in range(4)], axis=0)
    att = bdot_nt(q4, ke)
    att = jnp.where(jnp.concatenate([tri] * 4, axis=0) > 0, att, 0.0)
    r = bdot(att, v)
    col = _iota((1, 256), 1)
    o = bdot(q * jnp.exp(b), s)
    for h in range(4):
        o = o + r[h * CHUNK:(h + 1) * CHUNK] * (col // 64 == h).astype(F32)
    bcol = lax.dot_general(a, jnp.ones((CHUNK, 1), F32), (((0,), (0,)), ((), ())), precision=HI,
                           preferred_element_type=F32)
    kv = bdot_tn(ke, v)
    bd = (_iota((128, 256), 0) // 32 == _iota((128, 256), 1) // 64).astype(F32)
    return o, s * jnp.exp(bcol) + kv * bd


def _chunk_maps(nch, nctx):
    def fwd(b, s):
        return b * nch + s

    def bwd(b, s):
        return b * nch + jnp.where(s < nctx, nctx - 1 - s, nch - 1 - (s - nctx))
    return fwd, bwd


def scan_fwd(name, q, k, v, af, ab, nb, nch, nctx):
    n = af.shape[0]
    fmap, bmap = _chunk_maps(nch, nctx)

    def body(qf, kf, vf, a_f, qb, kb, vb, a_b, of_ref, ob_ref, stf_ref, stb_ref, s_scr):
        @pl.when(pl.program_id(1) == 0)
        def _():
            s_scr[...] = jnp.zeros_like(s_scr)

        stf_ref[0] = s_scr[0]
        stb_ref[0] = s_scr[1]
        o, sn = scan_step(qf[...], kf[...], vf[...], a_f[...], s_scr[0], False)
        of_ref[...] = o
        s_scr[0] = sn
        o, sn = scan_step(qb[...], kb[...], vb[...], a_b[...], s_scr[1], True)
        ob_ref[...] = o
        s_scr[1] = sn

    def specs(m):
        return [pl.BlockSpec((CHUNK, w), lambda b, s, cb=cb: (m(b, s), cb)) for _, w, cb in (q, k, v)] + \
               [pl.BlockSpec((CHUNK, 128), lambda b, s: (m(b, s), 0))]

    return pl.pallas_call(
        body, name=name, grid=(nb, nch), in_specs=specs(fmap) + specs(bmap),
        out_specs=[pl.BlockSpec((CHUNK, 256), lambda b, s: (fmap(b, s), 0)),
                   pl.BlockSpec((CHUNK, 256), lambda b, s: (bmap(b, s), 0)),
                   pl.BlockSpec((1, 128, 256), lambda b, s: (b * nch + s, 0, 0)),
                   pl.BlockSpec((1, 128, 256), lambda b, s: (b * nch + s, 0, 0))],
        out_shape=[jax.ShapeDtypeStruct((n, 256), F32)] * 2 + [jax.ShapeDtypeStruct((nb * nch, 128, 256), F32)] * 2,
        scratch_shapes=[pltpu.VMEM((2, 128, 256), F32)], compiler_params=_cparams(("arbitrary", "arbitrary")),
    )(q[0], k[0], v[0], af, q[0], k[0], v[0], ab)


def scan_bwd(name, q, k, v, af, ab, stf, stb, do, nb, nch, nctx):
    n = af.shape[0]
    fmap0, bmap0 = _chunk_maps(nch, nctx)
    fmap = lambda b, r: fmap0(b, nch - 1 - r)
    bmap = lambda b, r: bmap0(b, nch - 1 - r)

    def body(qf, kf, vf, a_f, sf, dof, qb, kb, vb, a_b, sb, dob,
             dqf, dkf, dvf, daf, dqb, dkb, dvb, dab, ds_scr):
        @pl.when(pl.program_id(1) == 0)
        def _():
            ds_scr[...] = jnp.zeros_like(ds_scr)

        for d, (qr, kr, vr, ar, sr, dor, outs) in enumerate(((qf, kf, vf, a_f, sf, dof, (dqf, dkf, dvf, daf)),
                                                               (qb, kb, vb, a_b, sb, dob, (dqb, dkb, dvb, dab)))):
            _, vjp = jax.vjp(functools.partial(scan_step, rev=bool(d)), qr[...], kr[...], vr[...], ar[...], sr[0])
            dq, dk, dv, da, ds = vjp((dor[...], ds_scr[d]))
            outs[0][...] = dq
            outs[1][...] = dk
            outs[2][...] = dv
            outs[3][...] = da
            ds_scr[d] = ds

    def specs(m):
        return [pl.BlockSpec((CHUNK, w), lambda b, r, cb=cb: (m(b, r), cb)) for _, w, cb in (q, k, v)] + \
               [pl.BlockSpec((CHUNK, 128), lambda b, r: (m(b, r), 0)),
                pl.BlockSpec((1, 128, 256), lambda b, r: (b * nch + nch - 1 - r, 0, 0)),
                pl.BlockSpec((CHUNK, 256), lambda b, r: (m(b, r), 0))]

    def ospecs(m):
        return [pl.BlockSpec((CHUNK, w), lambda b, r: (m(b, r), 0)) for w in (128, 128, 256, 128)]

    oshape = [jax.ShapeDtypeStruct((n, w), F32) for w in (128, 128, 256, 128)]
    return pl.pallas_call(
        body, name=name, grid=(nb, nch), in_specs=specs(fmap) + specs(bmap),
        out_specs=ospecs(fmap) + ospecs(bmap), out_shape=oshape + oshape,
        scratch_shapes=[pltpu.VMEM((2, 128, 256), F32)], compiler_params=_cparams(("arbitrary", "arbitrary")),
    )(q[0], k[0], v[0], af, stf, do, q[0], k[0], v[0], ab, stb, do)


def mla_fwd(name, qa, ka, va, nb, tps, tr, nctx_rows):
    n = qa.shape[0]
    t = tps * tr

    def body(q_ref, k_ref, v_ref, o_ref, lse_ref):
        def attend(nk):
            vv = v_ref[0:nk, :]
            first = _iota(vv.shape, 1) < 64
            one = jnp.ones_like(vv)
            res, lses = [], []
            for h in range(2):
                s = lax.dot_general(q_ref[:, h * 128:(h + 1) * 128], k_ref[0:nk, h * 128:(h + 1) * 128],
                                    (((1,), (1,)), ((), ())), preferred_element_type=F32)
                m = jnp.max(s, axis=-1, keepdims=True)
                e = jnp.exp((s - m).astype(BF16))
                r = jnp.dot(e, jnp.where(first == (h == 0), vv, one), preferred_element_type=F32)
                l = r[:, 64:65] if h == 0 else r[:, 0:1]
                res.append(r / l)
                lses.append(m + jnp.log(l))
            lane = _iota((tr, 128), 1) < 64
            o_ref[...] = jnp.where(lane, res[0], res[1])
            lse_ref[...] = jnp.where(lane, lses[0], lses[1])

        @pl.when(pl.program_id(2) == 0)
        def _():
            attend(nctx_rows)

        @pl.when(pl.program_id(2) > 0)
        def _():
            attend(t)

    return pl.pallas_call(
        body, name=name, grid=(nb, 4, tps),
        in_specs=[pl.BlockSpec((tr, 256), lambda b, h, j: (b * tps + j, h)), pl.BlockSpec((t, 256), lambda b, h, j: (b, h)),
                  pl.BlockSpec((t, 128), lambda b, h, j: (b, h))],
        out_specs=[pl.BlockSpec((tr, 128), lambda b, h, j: (b * tps + j, h))] * 2,
        out_shape=[jax.ShapeDtypeStruct((n, 512), F32)] * 2,
        compiler_params=_cparams(("parallel", "parallel", "arbitrary")),
    )(qa, ka, va)


def mla_bwd(name, qa, ka, va, o, lse, do, nb, tps, tr, nctx_rows):
    n = qa.shape[0]
    t = tps * tr

    def body(q_ref, k_ref, v_ref, o_ref, lse_ref, do_ref, dq_ref, dk_ref, dv_ref):
        @pl.when(pl.program_id(2) == 0)
        def _():
            dk_ref[...] = jnp.zeros_like(dk_ref)
            dv_ref[...] = jnp.zeros_like(dv_ref)

        def attend(nk):
            dov = do_ref[...]
            oo = dov * o_ref[...]
            dob = dov.astype(BF16)
            first = _iota(dob.shape, 1) < 64
            dqs = []
            dv = None
            for h in range(2):
                hs = slice(h * 128, (h + 1) * 128)
                qh, kh = q_ref[:, hs], k_ref[0:nk, hs]
                mine = first == (h == 0)
                delta = jnp.sum(jnp.where(mine, oo, 0.0), axis=-1, keepdims=True)
                doh = jnp.where(mine, dob, jnp.zeros_like(dob))
                s = lax.dot_general(qh, kh, (((1,), (1,)), ((), ())), preferred_element_type=F32)
                p = jnp.exp((s - lse_ref[:, h * 64:h * 64 + 1]).astype(BF16))
                dp = lax.dot_general(doh, v_ref[0:nk, :], (((1,), (1,)), ((), ())), preferred_element_type=F32)
                ds = p * (dp - delta).astype(BF16)
                dqs.append(jnp.dot(ds, kh, preferred_element_type=F32))
                dk_ref[0:nk, hs] += lax.dot_general(ds, qh, (((0,), (0,)), ((), ())), preferred_element_type=F32)
                dvh = lax.dot_general(p, doh, (((0,), (0,)), ((), ())), preferred_element_type=F32)
                dv = dvh if dv is None else dv + dvh
            dq_ref[...] = jnp.concatenate(dqs, axis=1)
            dv_ref[0:nk, :] += dv

        @pl.when(pl.program_id(2) == 0)
        def _():
            attend(nctx_rows)

        @pl.when(pl.program_id(2) > 0)
        def _():
            attend(t)

    qtile = pl.BlockSpec((tr, 128), lambda b, h, j: (b * tps + j, h))
    return pl.pallas_call(
        body, name=name, grid=(nb, 4, tps),
        in_specs=[pl.BlockSpec((tr, 256), lambda b, h, j: (b * tps + j, h)), pl.BlockSpec((t, 256), lambda b, h, j: (b, h)),
                  pl.BlockSpec((t, 128), lambda b, h, j: (b, h)), qtile, qtile, qtile],
        out_specs=[pl.BlockSpec((tr, 256), lambda b, h, j: (b * tps + j, h)), pl.BlockSpec((t, 256), lambda b, h, j: (b, h)),
                   pl.BlockSpec((t, 128), lambda b, h, j: (b, h))],
        out_shape=[jax.ShapeDtypeStruct((n, 1024), F32), jax.ShapeDtypeStruct((n, 1024), F32),
                   jax.ShapeDtypeStruct((n, 512), F32)],
        compiler_params=_cparams(("parallel", "parallel", "arbitrary")),
    )(qa, ka, va, o, lse, do)


HALO = 16


def _halo_specs(tr, width, tps, nt):
    r = tr // HALO
    return [pl.BlockSpec((tr, width), lambda j, c: (j, c)),
            pl.BlockSpec((HALO, width), lambda j, c: (jnp.maximum(j * r - 1, 0), c)),
            pl.BlockSpec((HALO, width), lambda j, c: (jnp.minimum((j + 1) * r, nt * r - 1), c))]


def _shifted(u, prev, nxt, j, tps):
    tr = u.shape[0]
    t = j % tps
    has_prev = (t >= 2).astype(F32)
    has_next = jnp.logical_and(t >= 1, t <= tps - 2).astype(F32)
    rows = _iota(u.shape, 0)
    dn = jnp.where(rows == 0, prev[HALO - 1:HALO] * has_prev, pltpu.roll(u, 1, 0))
    up = jnp.where(rows == tr - 1, nxt[0:1] * has_next, pltpu.roll(u, tr - 1, 0))
    return dn, up


def _ffn_act(ucv):
    return _silu(ucv[:, :FF_CHUNK]) * ucv[:, FF_CHUNK:]


def ffn2_fwd(name, u, cw, cb, wd, x1, mod, lng, lnb, tr, tps):
    n = u.shape[0]
    nt = n // tr
    w2 = 2 * FF_CHUNK

    def body(u_ref, up_ref, un_ref, cw_ref, cb_ref, wd_ref, x1_ref, m_ref, g_ref, b_ref, f_ref, x2_ref, acc):
        j, c = pl.program_id(0), pl.program_id(1)
        uu = u_ref[...].astype(F32)
        dn, up = _shifted(uu, up_ref[...].astype(F32), un_ref[...].astype(F32), j, tps)
        cwv = cw_ref[...]
        ucv = cwv[0:1] * dn + cwv[1:2] * uu + cwv[2:3] * up + cb_ref[...]
        part = bdot(_ffn_act(ucv), wd_ref[...])

        @pl.when(c == 0)
        def _():
            acc[...] = part

        @pl.when(c == 1)
        def _():
            f = acc[...] + part
            f_ref[...] = f
            x2_ref[...] = ln2_fn(x1_ref[...], f, m_ref[0], g_ref[...], b_ref[...])[0]

    return pl.pallas_call(
        body, name=name, grid=(nt, 2),
        in_specs=_halo_specs(tr, w2, tps, nt) + [
            pl.BlockSpec((8, w2), lambda j, c: (0, c)), pl.BlockSpec((1, w2), lambda j, c: (0, c)),
            pl.BlockSpec((FF_CHUNK, D), lambda j, c: (c, 0)), pl.BlockSpec((tr, D), lambda j, c: (j, 0)),
            pl.BlockSpec((1, 8, D), lambda j, c: (j, 0, 0)), pl.BlockSpec((1, D), lambda j, c: (0, 0)),
            pl.BlockSpec((1, D), lambda j, c: (0, 0))],
        out_specs=[pl.BlockSpec((tr, D), lambda j, c: (j, 0)), pl.BlockSpec((tr, D), lambda j, c: (j, 0))],
        out_shape=[jax.ShapeDtypeStruct((n, D), F32)] * 2, scratch_shapes=[pltpu.VMEM((tr, D), F32)],
        compiler_params=_cparams(("arbitrary", "arbitrary")),
    )(u, u, u, cw, cb, wd, x1, mod, lng, lnb)


def ffn2_bwd(name, u, cw, cb, wd, df, tr, tps):
    n = u.shape[0]
    nt = n // tr
    w2 = 2 * FF_CHUNK

    def body(u_ref, up_ref, un_ref, cw_ref, cb_ref, wd_ref, df_ref, ducv_ref, dwd_ref):
        c, j = pl.program_id(0), pl.program_id(1)
        uu = u_ref[...].astype(F32)
        dn, up = _shifted(uu, up_ref[...].astype(F32), un_ref[...].astype(F32), j, tps)
        cwv = cw_ref[...]
        ucv = cwv[0:1] * dn + cwv[1:2] * uu + cwv[2:3] * up + cb_ref[...]
        act, act_vjp = jax.vjp(_ffn_act, ucv)
        dfb = df_ref[...].astype(BF16)
        dact = lax.dot_general(dfb, wd_ref[...], (((1,), (1,)), ((), ())), preferred_element_type=F32)
        (ducv,) = act_vjp(dact)
        dwd = lax.dot_general(act.astype(BF16), dfb, (((0,), (0,)), ((), ())), preferred_element_type=F32)
        ducv_ref[...] = ducv.astype(ducv_ref.dtype)

        @pl.when(j == 0)
        def _():
            dwd_ref[...] = jnp.zeros_like(dwd_ref)

        dwd_ref[...] += dwd

    hs = _halo_specs(tr, w2, tps, nt)
    swap = lambda spec: pl.BlockSpec(spec.block_shape, lambda c, j, f=spec.index_map: f(j, c))
    return pl.pallas_call(
        body, name=name, grid=(2, nt),
        in_specs=[swap(s) for s in hs] + [
            pl.BlockSpec((8, w2), lambda c, j: (0, c)), pl.BlockSpec((1, w2), lambda c, j: (0, c)),
            pl.BlockSpec((FF_CHUNK, D), lambda c, j: (c, 0)), pl.BlockSpec((tr, D), lambda c, j: (j, 0))],
        out_specs=[pl.BlockSpec((tr, w2), lambda c, j: (j, c)), pl.BlockSpec((FF_CHUNK, D), lambda c, j: (c, 0))],
        out_shape=[jax.ShapeDtypeStruct((n, 2 * w2), BF16), jax.ShapeDtypeStruct((D_FF, D), F32)],
        compiler_params=_cparams(("parallel", "arbitrary")),
    )(u, u, u, cw, cb, wd, df)


def conv_bwd(name, ducv, u, cw, tr, tps):
    n = u.shape[0]
    nt = n // tr
    w2 = 2 * FF_CHUNK

    def body(g_ref, gp_ref, gn_ref, u_ref, up_ref, un_ref, cw_ref, du_ref, dcw_ref, dcb_ref):
        c, j = pl.program_id(0), pl.program_id(1)
        g = g_ref[...].astype(F32)
        gdn, gup = _shifted(g, gp_ref[...].astype(F32), gn_ref[...].astype(F32), j, tps)
        uu = u_ref[...].astype(F32)
        udn, uup = _shifted(uu, up_ref[...].astype(F32), un_ref[...].astype(F32), j, tps)
        cwv = cw_ref[...]
        du_ref[...] = (cwv[0:1] * gup + cwv[1:2] * g + cwv[2:3] * gdn).astype(du_ref.dtype)
        rows = _iota((8, w2), 0)
        s = lambda z: jnp.sum(z, axis=0, keepdims=True)
        dcw = (jnp.where(rows == 0, s(g * udn), 0.0) + jnp.where(rows == 1, s(g * uu), 0.0)
               + jnp.where(rows == 2, s(g * uup), 0.0))

        @pl.when(j == 0)
        def _():
            dcw_ref[...] = jnp.zeros_like(dcw_ref)
            dcb_ref[...] = jnp.zeros_like(dcb_ref)

        dcw_ref[...] += dcw
        dcb_ref[...] += s(g)

    hs = _halo_specs(tr, w2, tps, nt)
    swap = lambda spec: pl.BlockSpec(spec.block_shape, lambda c, j, f=spec.index_map: f(j, c))
    return pl.pallas_call(
        body, name=name, grid=(2, nt),
        in_specs=[swap(s) for s in hs] * 2 + [pl.BlockSpec((8, w2), lambda c, j: (0, c))],
        out_specs=[pl.BlockSpec((tr, w2), lambda c, j: (j, c)), pl.BlockSpec((8, w2), lambda c, j: (0, c)),
                   pl.BlockSpec((1, w2), lambda c, j: (0, c))],
        out_shape=[jax.ShapeDtypeStruct((n, 2 * w2), BF16), jax.ShapeDtypeStruct((8, 2 * w2), F32),
                   jax.ShapeDtypeStruct((1, 2 * w2), F32)],
        compiler_params=_cparams(("parallel", "arbitrary")),
    )(ducv, ducv, ducv, u, u, u, cw)


def loss_head(name, xf, target, nb, tps, tr):
    n = xf.shape[0]

    def body(x_ref, t_ref, dy_ref, l_ref):
        lat = (pl.program_id(0) % tps > 0).astype(F32)
        err = (x_ref[...] - t_ref[...]) * lat
        dy_ref[...] = err * (1.0 / D)
        l_ref[...] = jnp.zeros_like(l_ref) + 0.5 * jnp.sum(err * err) * (1.0 / D)

    def tmap(j):
        return ((j // tps) * (tps - 1) + jnp.maximum(j % tps - 1, 0), 0)

    return pl.pallas_call(
        body, name=name, grid=(n // tr,),
        in_specs=[pl.BlockSpec((tr, D), lambda j: (j, 0)), pl.BlockSpec((tr, D), tmap)],
        out_specs=[pl.BlockSpec((tr, D), lambda j: (j, 0)), pl.BlockSpec((1, 8, 128), lambda j: (j, 0, 0))],
        out_shape=[jax.ShapeDtypeStruct((n, D), F32), jax.ShapeDtypeStruct((n // tr, 8, 128), F32)],
        compiler_params=_cparams(("arbitrary",)),
    )(xf, target)


ADAM_MAX_ROWS = 512


def adamw(name, w, m, v, g8):
    r, c = w.shape
    k = g8.shape[0]
    rows = max(b for b in range(8, ADAM_MAX_ROWS + 1, 8) if r % b == 0)
    bc1 = 1.0 - ADAM_B1 ** ADAM_STEP
    bc2 = 1.0 - ADAM_B2 ** ADAM_STEP

    def body(w_ref, m_ref, v_ref, g_ref, go_ref, d_ref, mo_ref, vo_ref):
        g = g_ref[0].astype(F32)
        for i in range(1, k):
            g = g + g_ref[i].astype(F32)
        mn = ADAM_B1 * m_ref[...] + (1.0 - ADAM_B1) * g
        vn = ADAM_B2 * v_ref[...] + (1.0 - ADAM_B2) * (g * g)
        go_ref[...] = g
        mo_ref[...] = mn
        vo_ref[...] = vn
        d_ref[...] = -ADAM_LR * ((mn / bc1) / (jnp.sqrt(vn / bc2) + ADAM_EPS) + ADAM_WD * w_ref[...])

    blk = pl.BlockSpec((rows, c), lambda i: (i, 0))
    return pl.pallas_call(
        body, name=name, grid=(r // rows,),
        in_specs=[blk, blk, blk, pl.BlockSpec((k, rows, c), lambda i: (0, i, 0))],
        out_specs=[blk] * 4, out_shape=[jax.ShapeDtypeStruct((r, c), F32)] * 4,
        compiler_params=_cparams(("parallel",)),
    )(w, m, v, g8)


def ada_fwd(name, s, aw, ab):
    nl, _, cw = aw.shape

    def body(s_ref, w_ref, b_ref, o_ref):
        o_ref[0] = hdot(s_ref[...], w_ref[0]) + b_ref[0]

    return pl.pallas_call(
        body, name=name, grid=(nl,),
        in_specs=[pl.BlockSpec(s.shape, lambda l: (0, 0)), pl.BlockSpec((1, D, cw), lambda l: (l, 0, 0)),
                  pl.BlockSpec((1, 1, cw), lambda l: (l, 0, 0))],
        out_specs=pl.BlockSpec((1, s.shape[0], cw), lambda l: (l, 0, 0)),
        out_shape=jax.ShapeDtypeStruct((nl, s.shape[0], cw), F32), compiler_params=_cparams(("arbitrary",)),
    )(s, aw, ab)


def ada_bwd(name, s, aw, dmod):
    nl, _, cw = aw.shape

    def body(s_ref, w_ref, d_ref, dw_ref, ds_ref):
        dw_ref[0] = lax.dot_general(s_ref[...], d_ref[0], (((0,), (0,)), ((), ())), precision=HI,
                                    preferred_element_type=F32)
        ds_ref[0] = lax.dot_general(d_ref[0], w_ref[0], (((1,), (1,)), ((), ())), precision=HI,
                                    preferred_element_type=F32)

    return pl.pallas_call(
        body, name=name, grid=(nl,),
        in_specs=[pl.BlockSpec(s.shape, lambda l: (0, 0)), pl.BlockSpec((1, D, cw), lambda l: (l, 0, 0)),
                  pl.BlockSpec((1, s.shape[0], cw), lambda l: (l, 0, 0))],
        out_specs=[pl.BlockSpec((1, D, cw), lambda l: (l, 0, 0)), pl.BlockSpec((1, s.shape[0], D), lambda l: (l, 0, 0))],
        out_shape=[jax.ShapeDtypeStruct((nl, D, cw), F32), jax.ShapeDtypeStruct((nl, s.shape[0], D), F32)],
        compiler_params=_cparams(("arbitrary",)),
    )(s, aw, dmod)


def _place():
    return lax.axis_index("x"), lax.axis_index("y"), lax.axis_index("c")


def all_gather(name, x, in_vmem):
    r, c = x.shape

    def body(x_ref, out_ref, send_sems, recv_sems, local_sem):
        px, py, pc = _place()
        me, sibling = (px, py, pc), (px, py, 1 - pc)
        chips = [(1 - px, py), (px, 1 - py), (1 - px, 1 - py)]

        def rows(qx, qy, qc):
            return out_ref.at[pl.ds((4 * qx + 2 * qy + qc) * r, r), :]

        def copy(k, block, to, src=None):
            return pltpu.make_async_remote_copy(
                src_ref=rows(*block) if src is None else src, dst_ref=rows(*block),
                send_sem=send_sems.at[k], recv_sem=recv_sems.at[k], device_id=to, device_id_type=MESH)

        mine = pltpu.make_async_copy(x_ref, rows(*me), local_sem)
        mine.start()
        first = [copy(0, me, sibling, src=x_ref)]
        first += [copy(1 + j, me, (*chip, pc), src=x_ref) for j, chip in enumerate(chips)]
        for cp in first:
            cp.start()
        passed = [copy(4 + j, (*chip, pc), sibling) for j, chip in enumerate(chips)]
        for j, chip in enumerate(chips):
            copy(1 + j, (*chip, pc), me).wait_recv()
            passed[j].start()
        copy(0, sibling, me).wait_recv()
        for j, chip in enumerate(chips):
            copy(4 + j, (*chip, 1 - pc), me).wait_recv()
        for cp in first + passed:
            cp.wait_send()
        mine.wait()

    space = pltpu.VMEM if in_vmem else pl.ANY
    return pl.pallas_call(
        body, name=name, out_shape=jax.ShapeDtypeStruct((N_DEV * r, c), x.dtype),
        in_specs=[pl.BlockSpec(memory_space=space)], out_specs=pl.BlockSpec(memory_space=space),
        scratch_shapes=[pltpu.SemaphoreType.DMA((7,)), pltpu.SemaphoreType.DMA((7,)), pltpu.SemaphoreType.DMA],
        compiler_params=pltpu.CompilerParams(vmem_limit_bytes=VMEM_LIMIT_BYTES),
    )(x)


def all_to_all(name, x):
    _, r, c = x.shape

    def body(x_ref, out_ref, send_sems, recv_sems, local_sem):
        px, py, pc = _place()
        my = 4 * px + 2 * py + pc
        mine = pltpu.make_async_copy(x_ref.at[my], out_ref.at[my], local_sem)
        mine.start()
        copies = []
        for k in range(1, N_DEV):
            qx, qy, qc = px ^ (k >> 2 & 1), py ^ (k >> 1 & 1), pc ^ (k & 1)
            copies.append(pltpu.make_async_remote_copy(
                src_ref=x_ref.at[4 * qx + 2 * qy + qc], dst_ref=out_ref.at[my],
                send_sem=send_sems.at[k - 1], recv_sem=recv_sems.at[k - 1],
                device_id=(qx, qy, qc), device_id_type=MESH))
        for cp in copies:
            cp.start()
        for k, cp in enumerate(copies):
            cp.wait_send()
        for k in range(1, N_DEV):
            qx, qy, qc = px ^ (k >> 2 & 1), py ^ (k >> 1 & 1), pc ^ (k & 1)
            q = 4 * qx + 2 * qy + qc
            pltpu.make_async_remote_copy(
                src_ref=x_ref.at[q], dst_ref=out_ref.at[q], send_sem=send_sems.at[k - 1],
                recv_sem=recv_sems.at[k - 1], device_id=(qx, qy, qc), device_id_type=MESH).wait_recv()
        mine.wait()

    return pl.pallas_call(
        body, name=name, out_shape=jax.ShapeDtypeStruct(x.shape, x.dtype),
        in_specs=[pl.BlockSpec(memory_space=pl.ANY)], out_specs=pl.BlockSpec(memory_space=pl.ANY),
        scratch_shapes=[pltpu.SemaphoreType.DMA((7,)), pltpu.SemaphoreType.DMA((7,)), pltpu.SemaphoreType.DMA],
    )(x)


def _tables(seq, nctx_rows):
    f32 = np.float32
    pos = np.arange(seq, dtype=f32)
    ret_inv = (1.0 / (ROPE_BASE ** np.linspace(0.0, 1.0, 16, dtype=f32))).astype(f32)
    ang = pos[:, None] * ret_inv
    rc, rs = np.cos(ang).astype(f32), np.sin(ang).astype(f32)
    rcos = np.tile(np.concatenate([rc, rc], 1), (1, 4))
    rsin = np.tile(np.concatenate([-rs, rs], 1), (1, 4))
    rows = np.repeat(np.arange(seq // 64, dtype=f32), 64)
    cols = np.tile(np.arange(64, dtype=f32), seq // 64)
    ax_inv = (ROPE_BASE ** (-np.arange(8, dtype=f32) / 8)).astype(f32)
    ra, ca = rows[:, None] * ax_inv, cols[:, None] * ax_inv
    one, zero = np.ones((seq, 64), f32), np.zeros((seq, 64), f32)
    mcos = np.concatenate([one, np.cos(ra), np.cos(ra), np.cos(ca), np.cos(ca), one[:, :32]], 1)
    msin = np.concatenate([zero, -np.sin(ra), np.sin(ra), -np.sin(ca), np.sin(ca), zero[:, :32]], 1)
    ident = lambda t, v: np.concatenate([np.full((nctx_rows, 128), v, f32), t.astype(f32)], 0)
    return [jnp.asarray(ident(rcos, 1.0)), jnp.asarray(ident(rsin, 0.0)),
            jnp.asarray(ident(mcos, 1.0)), jnp.asarray(ident(msin, 0.0))]


def _prep_layer(w, l):
    z = lambda *s: jnp.zeros(s, F32)
    p = {}
    win = _rows(w["w_in_t"][l], W_IN_SEGS)
    p["w_in_t"] = jnp.concatenate([win, jnp.zeros((P_PAD - D_IN, D), win.dtype)], axis=0)
    p["w_up_t"] = _rows(w["ffn_up_t"][l], FF_SEGS)
    p["w_down"] = w["ffn_down"][l]
    p["w_out"] = w["w_out"][l]
    p["wuq"] = _pad_heads(w["mla_w_uq_t"][l], 96)
    p["wuk"] = _pad_heads(w["mla_w_uk_t"][l], 64)
    p["wuv"] = w["mla_w_uv_t"][l]
    gw = w["gla_gate_w"][l]
    p["w2f"] = z(128, 128).at[0:16].set(gw[0])
    p["w2b"] = z(128, 128).at[16:32].set(gw[1])
    p["b2f"], p["b2b"] = w["gla_gate_b"][l][0:1], w["gla_gate_b"][l][1:2]
    lg = jax.nn.log_sigmoid(w["ret_decay"][l])
    p["retf"], p["retb"] = jnp.repeat(lg[0], 32)[None], jnp.repeat(lg[1], 32)[None]
    p["qg"], p["kvg"] = w["mla_q_norm_g"][l][None], w["mla_kv_norm_g"][l][None]
    p["gng"] = jnp.tile(w["gla_norm_g"][l], 4)[None]
    p["ln1g"], p["ln1b"] = w["ln1_g"][l][None], w["ln1_b"][l][None]
    p["ln2g"], p["ln2b"] = w["ln2_g"][l][None], w["ln2_b"][l][None]
    p["cw"] = jnp.concatenate([_cols(w["ffn_conv_w"][l], FF_SEGS), z(5, 2 * D_FF)], axis=0)
    p["cb"] = _cols(w["ffn_conv_b"][l], FF_SEGS)[None]
    e2 = np.zeros((128, 1024), np.float32)
    for h in range(8):
        e2[32 + np.arange(32), h * 128 + 64 + np.arange(32)] = 1.0
    p["e2"] = jnp.asarray(e2)
    return p


def _pre_ins(pa, tabs, p):
    row = lambda w, cb: ("row", pa, w, cb)
    return [row(128, 0), row(128, 6), row(128, 7), row(256, 6), row(128, 14), row(128, 15)] + \
           [("pos", t, 128, 0) for t in tabs] + \
           [("par", p[k], 0, 0) for k in ("w2f", "w2b", "b2f", "b2b", "retf", "retb", "qg", "kvg", "wuq", "wuk", "wuv", "e2")]


_PRE_OUTS = [(128, F32)] * 7 + [(1024, BF16), (1024, BF16), (512, BF16)]
_PRE_WANT = [(i, F32) for i in range(6)] + [(i, F32) for i in range(10, 21)]


def _post_ins(ogf, ogb, orf, orb, om, pa, x, mod, p):
    return [("row", ogf, 256, 0), ("row", ogb, 256, 0), ("row", orf, 256, 0), ("row", orb, 256, 0),
            ("row", om, 512, 0), ("row", pa, 256, 2), ("row", pa, 256, 5), ("row", x, D, 0), ("tile", mod, 0, 0),
            ("par", p["gng"], 0, 0), ("par", p["w_out"], 0, 0), ("par", p["ln1g"], 0, 0), ("par", p["ln1b"], 0, 0)]


def layer_fwd(l, x, mod, p, tabs, dims):
    nb, tps, tr, nch, nctx = dims
    n = x.shape[0]
    pa = mm("proj", x, p["w_in_t"], F32, P_PAD, tr, mod=mod, sel=(1, 0))
    gq, af, ab, arf, arb, rq, rk, qa, ka, va = tile_fwd("mix_pre", pre_fn, _pre_ins(pa, tabs, p), _PRE_OUTS, n, tr, tps)
    ogf, ogb, gstf, gstb = scan_fwd("gla_scan", (gq, 128, 0), (pa, 128, 1), (pa, 256, 1), af, ab, nb, nch, nctx)
    orf, orb, rstf, rstb = scan_fwd("ret_scan", (rq, 128, 0), (rk, 128, 0), (pa, 256, 4), arf, arb, nb, nch, nctx)
    om, lse = mla_fwd("mla_attn", qa, ka, va, nb, tps, tr, nctx * CHUNK)
    (x1,) = tile_fwd("mix_post", post_fn, _post_ins(ogf, ogb, orf, orb, om, pa, x, mod, p), [(D, F32)], n, tr, tps)
    u = mm("ffn_up", x1, p["w_up_t"], BF16, 2 * D_FF, tr, mod=mod, sel=(4, 3))
    f, x2 = ffn2_fwd("ffn_down", u, p["cw"], p["cb"], p["w_down"], x1, mod, p["ln2g"], p["ln2b"], tr, tps)
    saved = dict(x=x, pa=pa, gq=gq, af=af, ab=ab, arf=arf, arb=arb, rq=rq, rk=rk, qa=qa, ka=ka, va=va,
                 ogf=ogf, ogb=ogb, gstf=gstf, gstb=gstb, orf=orf, orb=orb, rstf=rstf, rstb=rstb, om=om, lse=lse,
                 x1=x1, u=u, f=f)
    return x2, saved


def layer_bwd(l, dx2, s, mod, p, tabs, dims):
    nb, tps, tr, nch, nctx = dims
    n = dx2.shape[0]
    g = {}
    ln2_ins = [("row", s["x1"], D, 0), ("row", s["f"], D, 0), ("tile", mod, 0, 0),
               ("par", p["ln2g"], 0, 0), ("par", p["ln2b"], 0, 0)]
    dx1a, df, dmod_a, g["ln2g"], g["ln2b"] = tile_bwd(
        "ln2_bwd", ln2_fn, ln2_ins, [dx2], [(0, F32), (1, F32), (2, F32), (3, F32), (4, F32)], n, tr, tps)
    ducv, g["w_down"] = ffn2_bwd("ffn_down_bwd", s["u"], p["cw"], p["cb"], p["w_down"], df, tr, tps)
    du, g["cw"], g["cb"] = conv_bwd("conv_bwd", ducv, s["u"], p["cw"], tr, tps)
    g["w_up_t"] = mm_tn("ffn_up_dw", du, s["x1"], D_FF, tr, mod=mod, sel=(4, 3))
    dx1, dmod_b = mm_modbwd("ffn_up_dx", du, p["w_up_t"], s["x1"], mod, dx1a, (4, 3), tr)

    post_ins = _post_ins(s["ogf"], s["ogb"], s["orf"], s["orb"], s["om"], s["pa"], s["x"], mod, p)
    want = [(0, F32), (2, F32), (4, F32), (5, F32), (6, F32), (7, F32), (8, F32), (9, F32), (10, F32), (11, F32), (12, F32)]
    dog, dor, dom, dgg, drg, dxa, dmod_c, g["gng"], g["w_out"], g["ln1g"], g["ln1b"] = tile_bwd(
        "mix_post_bwd", post_fn, post_ins, [dx1], want, n, tr, tps)
    dqa, dka, dva = mla_bwd("mla_attn_bwd", s["qa"], s["ka"], s["va"], s["om"], s["lse"], dom, nb, tps, tr,
                            nctx * CHUNK)
    pa = s["pa"]
    gdqf, gdkf, gdvf, gdaf, gdqb, gdkb, gdvb, gdab = scan_bwd(
        "gla_scan_bwd", (s["gq"], 128, 0), (pa, 128, 1), (pa, 256, 1), s["af"], s["ab"], s["gstf"], s["gstb"], dog,
        nb, nch, nctx)
    rdqf, rdkf, rdvf, rdaf, rdqb, rdkb, rdvb, rdab = scan_bwd(
        "ret_scan_bwd", (s["rq"], 128, 0), (s["rk"], 128, 0), (pa, 256, 4), s["arf"], s["arb"], s["rstf"], s["rstb"],
        dor, nb, nch, nctx)

    pre_ins = _pre_ins(pa, tabs, p)
    extra = [gdqf, gdqb, rdqf, rdqb, rdkf, rdkb, gdkf, gdkb, gdvf, gdvb, rdvf, rdvb, dgg, drg]
    kinds = [i[0] for i in pre_ins]
    widx = [w[0] for w in _PRE_WANT]
    npre = len(pre_ins)

    def body(*refs):
        vals = [_load(k, r) for k, r in zip(kinds, refs[:npre])]
        rd = lambda i: refs[npre + i][...].astype(F32)
        cots = (rd(0) + rd(1), rd(14), rd(15), rd(16), rd(17), rd(2) + rd(3), rd(4) + rd(5), rd(18), rd(19), rd(20))

        def f(*dv):
            full = list(vals)
            for i, v in zip(widx, dv):
                full[i] = v
            return tuple(pre_fn(*full))

        _, vjp = jax.vjp(f, *[vals[i] for i in widx])
        grads = vjp(cots)
        dgq, drq, drk, dcq, dckv, dmisc = grads[:6]
        dp = jnp.concatenate([dgq, rd(6) + rd(7), rd(8) + rd(9), rd(12), drq, drk, rd(10) + rd(11), rd(13),
                              dcq, dckv, dmisc], axis=1)
        outs = refs[npre + 21:]
        outs[0][...] = dp.astype(BF16)
        first = pl.program_id(0) == 0
        for r, gr in zip(outs[1:], grads[6:]):
            @pl.when(first)
            def _(r=r):
                r[...] = jnp.zeros_like(r)
            r[...] += gr

    cot_arrays = extra + [gdaf, gdab, rdaf, rdab, dqa, dka, dva]
    par_arrays = [pre_ins[i][1] for i in range(10, 21)]
    res = pl.pallas_call(
        body, name="mix_pre_bwd", grid=(n // tr,),
        in_specs=[_spec(k, a, w, cb, tr, tps) for k, a, w, cb in pre_ins]
        + [pl.BlockSpec((tr, c.shape[1]), lambda j: (j, 0)) for c in cot_arrays],
        out_specs=[pl.BlockSpec((tr, P_PAD), lambda j: (j, 0))] + [pl.BlockSpec(a.shape, lambda j: (0, 0)) for a in par_arrays],
        out_shape=[jax.ShapeDtypeStruct((n, P_PAD), BF16)] + [jax.ShapeDtypeStruct(a.shape, F32) for a in par_arrays],
        compiler_params=_cparams(("arbitrary",)),
    )(*[i[1] for i in pre_ins], *cot_arrays)
    dp = res[0]
    for k, v in zip(("w2f", "w2b", "b2f", "b2b", "retf", "retb", "qg", "kvg", "wuq", "wuk", "wuv"), res[1:]):
        g[k] = v
    g["w_in_t"] = mm_tn("proj_dw", dp, s["x"], P_PAD, tr, mod=mod, sel=(1, 0))
    dx, dmod_d = mm_modbwd("proj_dx", dp, p["w_in_t"], s["x"], mod, dxa, (1, 0), tr)
    return dx, dmod_a + dmod_b + dmod_c + dmod_d, g


def _unprep_grads(g, w, l):
    o = {}
    o["w_in_t"] = _rows(g["w_in_t"], W_IN_INV_SEGS)
    o["ffn_up_t"] = _rows(g["w_up_t"], FF_SEGS)
    o["ffn_down"] = g["w_down"]
    o["w_out"] = g["w_out"]
    o["mla_w_uq_t"] = _unpad_heads(g["wuq"], 96)
    o["mla_w_uk_t"] = _unpad_heads(g["wuk"], 64)
    o["mla_w_uv_t"] = g["wuv"]
    o["gla_gate_w"] = jnp.stack([g["w2f"][0:16], g["w2b"][16:32]])
    o["gla_gate_b"] = jnp.concatenate([g["b2f"], g["b2b"]], axis=0)
    dlg = jnp.stack([g["retf"].reshape(4, 32).sum(-1), g["retb"].reshape(4, 32).sum(-1)])
    o["ret_decay"] = dlg * jax.nn.sigmoid(-w["ret_decay"][l])
    o["mla_q_norm_g"], o["mla_kv_norm_g"] = g["qg"][0], g["kvg"][0]
    o["gla_norm_g"] = g["gng"].reshape(4, 64).sum(0)
    o["ln1_g"], o["ln1_b"], o["ln2_g"], o["ln2_b"] = g["ln1g"][0], g["ln1b"][0], g["ln2g"][0], g["ln2b"][0]
    o["ffn_conv_w"] = _cols(g["cw"][0:3], FF_SEGS)
    o["ffn_conv_b"] = _cols(g["cb"][0], FF_SEGS)
    return o


def local_step(xs, target, modtab, w, dims):
    nb, tps, tr, nch, nctx = dims
    tabs = _tables((tps - 1) * tr, tr)
    x = xs
    saved, preps = [], []
    for l in range(DEPTH):
        p = _prep_layer(w, l)
        x, s = layer_fwd(l, x, modtab[l], p, tabs, dims)
        saved.append(s)
        preps.append(p)
    dy, lpart = loss_head("loss_head", x, target, nb, tps, tr)
    loss = jnp.sum(lpart[:, 0, 0])
    dx = dy
    dmods, grads = [None] * DEPTH, [None] * DEPTH
    for l in reversed(range(DEPTH)):
        dx, dmods[l], g = layer_bwd(l, dx, saved[l], modtab[l], preps[l], tabs, dims)
        grads[l] = _unprep_grads(g, w, l)
    gstack = {k: jnp.stack([grads[l][k] for l in range(DEPTH)]) for k in grads[0]}
    return loss, dx, jnp.stack(dmods), gstack


BIG = [("ffn_up", 2), ("ffn_down", 1), ("w_out", 1), ("w_in", 2), ("mla_w_uq", 2), ("mla_w_uk", 2), ("mla_w_uv", 2)]
SMALL = ["ada_b", "gla_gate_w", "gla_gate_b", "gla_norm_g", "ret_decay", "mla_q_norm_g", "mla_kv_norm_g",
         "ln1_g", "ln1_b", "ffn_conv_b", "ln2_g", "ln2_b"]
PACK_C = 1024


def _big_key(k, axis):
    return k + "_t" if axis == 2 else k


def _shard_rows(a, axis):
    a = jnp.swapaxes(a, 1, 2) if axis == 2 else a
    return a.reshape(-1, PACK_C)


def _shard_from_rows(rows, shape, axis):
    l, r, c = shape
    return jnp.swapaxes(rows.reshape(l, c, r), 1, 2) if axis == 2 else rows.reshape(l, r, c)


def sum8(name, g8):
    k, r, c = g8.shape
    rows = max(b for b in range(16, ADAM_MAX_ROWS + 1, 16) if r % b == 0)

    def body(g_ref, o_ref):
        g = g_ref[0].astype(F32)
        for i in range(1, k):
            g = g + g_ref[i].astype(F32)
        o_ref[...] = g

    return pl.pallas_call(
        body, name=name, grid=(r // rows,), in_specs=[pl.BlockSpec((k, rows, c), lambda i: (0, i, 0))],
        out_specs=pl.BlockSpec((rows, c), lambda i: (i, 0)), out_shape=jax.ShapeDtypeStruct((r, c), F32),
        compiler_params=_cparams(("parallel",)),
    )(g8)


def _pack(arrs, dtype):
    flat = jnp.concatenate([a.reshape(-1).astype(dtype) for a in arrs])
    pad = (-flat.shape[0]) % (8 * PACK_C)
    return jnp.concatenate([flat, jnp.zeros((pad,), dtype)]).reshape(-1, PACK_C)


def _unpack(flat2d, shapes):
    flat = flat2d.reshape(-1)
    out, off = [], 0
    for s in shapes:
        sz = int(np.prod(s))
        out.append(flat[off:off + sz].reshape(s))
        off += sz
    return out


def _row_shape(shape, axis):
    l, r, c = shape
    return (l, c, r) if axis == 2 else (l, r, c)


def _whole_from_blocks(blocks, shape, axis):
    l, r, c = _row_shape(shape, axis)
    return blocks.reshape(N_DEV, l, r, c).transpose(1, 0, 2, 3).reshape(l, N_DEV * r, c)


def _blocks_from_whole(whole, shape, axis):
    l, r, c = _row_shape(shape, axis)
    return whole.reshape(l, N_DEV, r, c).transpose(1, 0, 2, 3).reshape(N_DEV, -1, PACK_C)


def kernel(x, c, ctx, c_ctx, ada_w, ada_b, w_in, gla_gate_w, gla_gate_b, gla_norm_g, ret_decay, mla_q_norm_g, mla_kv_norm_g, mla_w_uq, mla_w_uk, mla_w_uv, w_out, ln1_g, ln1_b, ffn_up, ffn_conv_w, ffn_conv_b, ffn_down, ln2_g, ln2_b, loss_target, m_c_ctx, m_ada_w, m_ada_b, m_w_in, m_gla_gate_w, m_gla_gate_b, m_gla_norm_g, m_ret_decay, m_mla_q_norm_g, m_mla_kv_norm_g, m_mla_w_uq, m_mla_w_uk, m_mla_w_uv, m_w_out, m_ln1_g, m_ln1_b, m_ffn_up, m_ffn_conv_w, m_ffn_conv_b, m_ffn_down, m_ln2_g, m_ln2_b, v_c_ctx, v_ada_w, v_ada_b, v_w_in, v_gla_gate_w, v_gla_gate_b, v_gla_norm_g, v_ret_decay, v_mla_q_norm_g, v_mla_kv_norm_g, v_mla_w_uq, v_mla_w_uk, v_mla_w_uv, v_w_out, v_ln1_g, v_ln1_b, v_ffn_up, v_ffn_conv_w, v_ffn_conv_b, v_ffn_down, v_ln2_g, v_ln2_b):
    names = ["c_ctx", "ada_w", "ada_b", "w_in", "gla_gate_w", "gla_gate_b", "gla_norm_g", "ret_decay", "mla_q_norm_g",
             "mla_kv_norm_g", "mla_w_uq", "mla_w_uk", "mla_w_uv", "w_out", "ln1_g", "ln1_b", "ffn_up", "ffn_conv_w",
             "ffn_conv_b", "ffn_down", "ln2_g", "ln2_b"]
    loc = locals()
    W = {k: loc[k] for k in names}
    M = {k: loc["m_" + k] for k in names}
    V = {k: loc["v_" + k] for k in names}

    nb, seq, _ = x.shape
    tr = ctx.shape[1]
    tps = 1 + seq // tr
    t = tps * tr
    n = nb * t
    nt = nb * tps
    dims = (nb, tps, tr, t // CHUNK, tr // CHUNK)
    px, py, pc = _place()
    me = 4 * px + 2 * py + pc
    ncol = ada_w.shape[2]

    cw_loc = ffn_conv_w.reshape(-1)
    g1 = jnp.concatenate([c.reshape(-1), cw_loc])
    g1 = jnp.concatenate([g1, jnp.zeros(((-g1.shape[0]) % (8 * PACK_C),), F32)]).reshape(-1, PACK_C)
    r1 = g1.shape[0]
    g1a = all_gather("gather_cond", g1, True).reshape(N_DEV, -1)
    c_all = g1a[:, :nb * D].reshape(N_DEV * nb, D)
    cw_all = g1a[:, nb * D:nb * D + cw_loc.shape[0]].reshape(N_DEV, DEPTH, 3, -1).transpose(1, 2, 0, 3).reshape(DEPTH, 3, -1)

    wpack = jnp.concatenate([_shard_rows(W[k].astype(BF16), ax) for k, ax in BIG], axis=0)
    prows = wpack.shape[0]
    wall = all_gather("gather_weights", wpack, False).reshape(N_DEV, prows, PACK_C)
    wl, offs, off = {}, {}, 0
    for k, ax in BIG:
        rows = int(np.prod(W[k].shape)) // PACK_C
        wl[_big_key(k, ax)] = _whole_from_blocks(wall[:, off:off + rows], W[k].shape, ax)
        offs[k] = (off, rows)
        off += rows
    for k in SMALL[1:]:
        wl[k] = W[k]
    wl["ffn_conv_w"] = cw_all

    srows = 40
    s_in = jnp.concatenate([c_all, c_ctx[None], jnp.zeros((srows - N_DEV * nb - 1, D), F32)], axis=0)
    s_act = _silu(s_in)
    ab_loc = lax.dynamic_slice_in_dim(ada_b, me * ncol, ncol, axis=1)[:, None, :]
    mod_part = ada_fwd("ada_fwd", s_act, ada_w, ab_loc)
    mod_all = all_gather("gather_mod", mod_part.reshape(-1, ncol), True).reshape(N_DEV, DEPTH, srows, ncol)
    mod_rows = mod_all.transpose(1, 2, 0, 3).reshape(DEPTH, srows, N_DEV * ncol)
    mod_l = lax.dynamic_slice_in_dim(mod_rows, me * nb, nb, axis=1).reshape(DEPTH, nb, 6, D)
    mod_c = mod_rows[:, N_DEV * nb].reshape(DEPTH, 1, 6, D)
    tile_is_ctx = (jnp.arange(tps) == 0)[None, None, :, None, None]
    modtab = jnp.where(tile_is_ctx, mod_c[:, :, None], mod_l[:, :, None])
    modtab = jnp.concatenate([modtab, jnp.zeros((DEPTH, nb, tps, 2, D), F32)], axis=3).reshape(DEPTH, nt, 8, D)

    xs = jnp.concatenate([ctx, x], axis=1).reshape(n, D)
    loss_loc, dxs, dmodtab, gl = local_step(xs, loss_target.reshape(nb * seq, D), modtab, wl, dims)
    loss = lax.psum(loss_loc, ("x", "y", "c"))
    grad_x = dxs.reshape(nb, t, D)[:, tr:]

    dm = dmodtab.reshape(DEPTH, nb, tps, 8, D)[:, :, :, :6]
    dmod_l = dm[:, :, 1:].sum(2).reshape(DEPTH, nb, 6 * D)
    dmod_c = dm[:, :, 0].sum(1).reshape(DEPTH, 1, 6 * D)
    gl["ada_b"] = dmod_l.sum(1) + dmod_c[:, 0]
    small_list = [gl[k] for k in SMALL] + [gl["ffn_conv_w"]]
    small_shapes = [a.shape for a in small_list]
    spack = _pack(small_list + [jnp.concatenate([dmod_l, dmod_c], axis=1)], F32)
    rs = spack.shape[0]
    sall = all_gather("gather_small_grads", spack, True).reshape(N_DEV, rs, PACK_C)
    nsmall = sum(int(np.prod(s)) for s in small_shapes)
    dmo = sall.reshape(N_DEV, -1)[:, nsmall:nsmall + DEPTH * (nb + 1) * 6 * D].reshape(N_DEV, DEPTH, nb + 1, 6 * D)
    dl_all = dmo[:, :, :nb].transpose(1, 0, 2, 3).reshape(DEPTH, N_DEV * nb, 6 * D)
    dc_all = dmo[:, :, nb].sum(0)[:, None]
    dmod_rows = jnp.concatenate([dl_all, dc_all, jnp.zeros((DEPTH, srows - N_DEV * nb - 1, 6 * D), F32)], axis=1)
    dmod_loc = lax.dynamic_slice_in_dim(dmod_rows.reshape(DEPTH, srows, N_DEV, ncol), me, 1, axis=2)[:, :, 0]
    d_ada_w, d_s = ada_bwd("ada_bwd", s_act, ada_w, dmod_loc)
    sg = jax.nn.sigmoid(c_ctx)
    dcc = d_s[:, N_DEV * nb].sum(0) * (sg * (1.0 + c_ctx * (1.0 - sg)))
    ccp = jnp.concatenate([dcc[None], jnp.zeros((7, D), F32)], axis=0)
    ccall = all_gather("gather_cctx", ccp, True).reshape(N_DEV, 8, D)

    gsend = jnp.concatenate([_blocks_from_whole(gl[_big_key(k, ax)].astype(BF16), W[k].shape, ax) for k, ax in BIG],
                            axis=1)
    gsum = sum8("grad_sum", all_to_all("grad_all_to_all", gsend))
    res = {}

    def update2d(tag, k, g):
        last = W[k].shape[-1]
        outs = adamw(tag, W[k].reshape(-1, last), M[k].reshape(-1, last), V[k].reshape(-1, last),
                     g.reshape(1, -1, last))
        res[k] = [a.reshape(W[k].shape) for a in outs]

    for k, ax in BIG:
        off, rows = offs[k]
        update2d("adamw_" + k, k, _shard_from_rows(gsum[off:off + rows], W[k].shape, ax))

    def update(tag, keys, g8):
        outs = adamw(tag, _pack([W[k] for k in keys], F32), _pack([M[k] for k in keys], F32),
                     _pack([V[k] for k in keys], F32), g8)
        for i, arr in enumerate(outs):
            for k, a in zip(keys, _unpack(arr, [W[k].shape for k in keys])):
                res.setdefault(k, [None] * 4)[i] = a

    nrep = sum(int(np.prod(W[k].shape)) for k in SMALL)
    sflat = sall.reshape(N_DEV, -1)
    def pack8(a):
        a = a.reshape(N_DEV, -1)
        pad = (-a.shape[1]) % (8 * PACK_C)
        return jnp.concatenate([a, jnp.zeros((N_DEV, pad), F32)], axis=1).reshape(N_DEV, -1, PACK_C)

    update("adamw_small", SMALL, pack8(sflat[:, :nrep]))
    ncw = ffn_conv_w.shape[2]
    cw8 = sflat[:, nrep:nsmall].reshape(N_DEV, DEPTH, 3, N_DEV * ncw)
    cw8 = lax.dynamic_slice_in_dim(cw8, me * ncw, ncw, axis=3)
    update("adamw_conv", ["ffn_conv_w"], pack8(cw8))
    update2d("adamw_ada", "ada_w", d_ada_w)
    update("adamw_cctx", ["c_ctx"], ccall)

    out = [loss, grad_x]
    for i in range(4):
        out += [res[k][i] for k in names]
    return tuple(out)
```

```python
import functools
import math

import numpy as np
import jax
import jax.numpy as jnp
from jax import lax
from jax.experimental import pallas as pl
from jax.experimental.pallas import tpu as pltpu

F32 = jnp.float32
BF16 = jnp.bfloat16
HI = lax.Precision.HIGHEST
MESH = pl.DeviceIdType.MESH

N_DEV = 8
D = 1024
DEPTH = 4
CHUNK = 64
EPS = 1e-6
ALPHA = (2 * DEPTH) ** 0.25
GLA_TAU = 16.0
ROPE_BASE = 10000.0
MLA_SCALE = 96 ** -0.5
D_FF = 2816
FF_CHUNK = 1408
P_PAD = 2048
VMEM_LIMIT_BYTES = 56 << 20

ADAM_LR, ADAM_B1, ADAM_B2, ADAM_EPS, ADAM_WD, ADAM_STEP = 0.001, 0.9, 0.999, 1e-08, 0.01, 10

D_IN = 1984
W_IN_SEGS = [(0, 512), (544, 1408), (512, 32), (1952, 32)]
W_IN_INV_SEGS = [(0, 512), (1920, 32), (512, 1408), (1952, 32)]
FF_SEGS = [(0, FF_CHUNK), (D_FF, FF_CHUNK), (FF_CHUNK, FF_CHUNK), (D_FF + FF_CHUNK, FF_CHUNK)]


def _cols(a, segs):
    return jnp.concatenate([a[..., s:s + n] for s, n in segs], axis=-1)


def _rows(a, segs):
    return jnp.concatenate([a[s:s + n] for s, n in segs], axis=0)


def _pad_heads(wt, per_head):
    c = wt.shape[1]
    wt = wt.reshape(8, per_head, c)
    return jnp.concatenate([wt, jnp.zeros((8, 128 - per_head, c), wt.dtype)], axis=1).reshape(1024, c)


def _unpad_heads(g, per_head):
    c = g.shape[1]
    return g.reshape(8, 128, c)[:, :per_head].reshape(8 * per_head, c)


def _cparams(sem=None):
    return pltpu.CompilerParams(vmem_limit_bytes=VMEM_LIMIT_BYTES, dimension_semantics=sem)


@jax.custom_vjp
def bdot(a, w):
    return jnp.dot(a.astype(BF16), w.astype(BF16), preferred_element_type=F32)


def _bdot_fwd(a, w):
    return bdot(a, w), (a, w)


def _bdot_bwd(res, ct):
    a, w = res
    ctb = ct.astype(BF16)
    da = lax.dot_general(ctb, w.astype(BF16), (((1,), (1,)), ((), ())), preferred_element_type=F32)
    dw = lax.dot_general(a.astype(BF16), ctb, (((0,), (0,)), ((), ())), preferred_element_type=F32)
    return da.astype(a.dtype), dw.astype(w.dtype)


bdot.defvjp(_bdot_fwd, _bdot_bwd)


@jax.custom_vjp
def bdot_nt(a, wt):
    return lax.dot_general(a.astype(BF16), wt.astype(BF16), (((1,), (1,)), ((), ())), preferred_element_type=F32)


def _bdot_nt_fwd(a, wt):
    return bdot_nt(a, wt), (a, wt)


def _bdot_nt_bwd(res, ct):
    a, wt = res
    ctb = ct.astype(BF16)
    da = jnp.dot(ctb, wt.astype(BF16), preferred_element_type=F32)
    dwt = lax.dot_general(ctb, a.astype(BF16), (((0,), (0,)), ((), ())), preferred_element_type=F32)
    return da.astype(a.dtype), dwt.astype(wt.dtype)


bdot_nt.defvjp(_bdot_nt_fwd, _bdot_nt_bwd)


@jax.custom_vjp
def bdot_tn(a, b):
    return lax.dot_general(a.astype(BF16), b.astype(BF16), (((0,), (0,)), ((), ())), preferred_element_type=F32)


def _bdot_tn_fwd(a, b):
    return bdot_tn(a, b), (a, b)


def _bdot_tn_bwd(res, ct):
    a, b = res
    ctb = ct.astype(BF16)
    da = lax.dot_general(b.astype(BF16), ctb, (((1,), (1,)), ((), ())), preferred_element_type=F32)
    db = jnp.dot(a.astype(BF16), ctb, preferred_element_type=F32)
    return da.astype(a.dtype), db.astype(b.dtype)


bdot_tn.defvjp(_bdot_tn_fwd, _bdot_tn_bwd)


def _split3(x):
    x1 = x.astype(BF16)
    r1 = x - x1.astype(F32)
    x2 = r1.astype(BF16)
    return x1, x2, (r1 - x2.astype(F32)).astype(BF16)


@jax.custom_vjp
def xdot(x, m):
    mb = m.astype(BF16)
    return sum(jnp.dot(xi, mb, preferred_element_type=F32) for xi in _split3(x))


def _xdot_bwd(m, ct):
    mb = m.astype(BF16)
    dx = sum(lax.dot_general(ci, mb, (((1,), (1,)), ((), ())), preferred_element_type=F32) for ci in _split3(ct))
    return dx, jnp.zeros_like(m)


xdot.defvjp(lambda x, m: (xdot(x, m), m), _xdot_bwd)


@jax.custom_vjp
def xdot_l(m, x):
    mb = m.astype(BF16)
    return sum(jnp.dot(mb, xi, preferred_element_type=F32) for xi in _split3(x))


def _xdot_l_bwd(m, ct):
    mb = m.astype(BF16)
    dx = sum(lax.dot_general(mb, ci, (((0,), (0,)), ((), ())), preferred_element_type=F32) for ci in _split3(ct))
    return jnp.zeros_like(m), dx


xdot_l.defvjp(lambda m, x: (xdot_l(m, x), m), _xdot_l_bwd)


def _swap_fn(half):
    def swap(x):
        n = x.shape[1]
        first = (_iota(x.shape, 1) // half) % 2 == 0
        return jnp.where(first, pltpu.roll(x, n - half, 1), pltpu.roll(x, half, 1))

    f = jax.custom_vjp(swap)
    f.defvjp(lambda x: (swap(x), None), lambda _, ct: (swap(ct),))
    return f


_swap16 = _swap_fn(16)
_swap8 = _swap_fn(8)


def hdot(a, b):
    return jnp.dot(a, b, precision=HI, preferred_element_type=F32)


def _iota(shape, axis):
    return lax.broadcasted_iota(jnp.int32, shape, axis)


def _group_avg(n, g):
    return (_iota((n, n), 0) // g == _iota((n, n), 1) // g).astype(F32) * (1.0 / g)


def _silu(x):
    return x * jax.nn.sigmoid(x)


def _layer_norm(z, g, b):
    mu = jnp.mean(z, axis=-1, keepdims=True)
    zc = z - mu
    var = jnp.mean(zc * zc, axis=-1, keepdims=True)
    return zc * lax.rsqrt(var + EPS) * g + b


def _rms(x, g):
    return x * lax.rsqrt(jnp.mean(x * x, axis=-1, keepdims=True) + EPS) * g


def mm(name, a, wt, out_dtype, tn, tr, mod=None, sel=None):
    n, k = a.shape
    nw = wt.shape[0]

    def body(*refs):
        if mod is not None:
            a_ref, m_ref, w_ref, o_ref = refs
            m = m_ref[0]
            av = a_ref[...] * (1.0 + m[sel[0]:sel[0] + 1]) + m[sel[1]:sel[1] + 1]
        else:
            a_ref, w_ref, o_ref = refs
            av = a_ref[...]
        o_ref[...] = lax.dot_general(av.astype(BF16), w_ref[...], (((1,), (1,)), ((), ())),
                                     preferred_element_type=F32).astype(o_ref.dtype)

    in_specs = [pl.BlockSpec((tr, k), lambda c, j: (j, 0))]
    args = [a]
    if mod is not None:
        in_specs.append(pl.BlockSpec((1, 8, k), lambda c, j: (j, 0, 0)))
        args.append(mod)
    in_specs.append(pl.BlockSpec((tn, k), lambda c, j: (c, 0)))
    args.append(wt)
    return pl.pallas_call(
        body, name=name, grid=(nw // tn, n // tr), in_specs=in_specs,
        out_specs=pl.BlockSpec((tr, tn), lambda c, j: (j, c)),
        out_shape=jax.ShapeDtypeStruct((n, nw), out_dtype), compiler_params=_cparams(("parallel", "arbitrary")),
    )(*args)


def mm_tn(name, dc, a, tn, tr, mod=None, sel=None):
    n, k = a.shape
    nw = dc.shape[1]

    def body(*refs):
        if mod is not None:
            d_ref, a_ref, m_ref, o_ref = refs
            m = m_ref[0]
            av = a_ref[...] * (1.0 + m[sel[0]:sel[0] + 1]) + m[sel[1]:sel[1] + 1]
        else:
            d_ref, a_ref, o_ref = refs
            av = a_ref[...]

        @pl.when(pl.program_id(1) == 0)
        def _():
            o_ref[...] = jnp.zeros_like(o_ref)

        o_ref[...] += lax.dot_general(d_ref[...].astype(BF16), av.astype(BF16), (((0,), (0,)), ((), ())),
                                      preferred_element_type=F32)

    in_specs = [pl.BlockSpec((tr, tn), lambda c, j: (j, c)), pl.BlockSpec((tr, k), lambda c, j: (j, 0))]
    args = [dc, a]
    if mod is not None:
        in_specs.append(pl.BlockSpec((1, 8, k), lambda c, j: (j, 0, 0)))
        args.append(mod)
    return pl.pallas_call(
        body, name=name, grid=(nw // tn, n // tr), in_specs=in_specs,
        out_specs=pl.BlockSpec((tn, k), lambda c, j: (c, 0)),
        out_shape=jax.ShapeDtypeStruct((nw, k), F32), compiler_params=_cparams(("parallel", "arbitrary")),
    )(*args)


def mm_modbwd(name, dc, wt, x, mod, add, sel, tr):
    n, k = dc.shape
    dm = wt.shape[1]

    def body(dc_ref, wt_ref, x_ref, m_ref, add_ref, dx_ref, dm_ref):
        dh = jnp.dot(dc_ref[...].astype(BF16), wt_ref[...], preferred_element_type=F32)
        m = m_ref[0]
        dx_ref[...] = add_ref[...] + dh * (1.0 + m[sel[0]:sel[0] + 1])
        dsc = jnp.sum(dh * x_ref[...], axis=0, keepdims=True)
        dsh = jnp.sum(dh, axis=0, keepdims=True)
        rows = _iota((8, dm), 0)
        dm_ref[0] = jnp.where(rows == sel[0], dsc, 0.0) + jnp.where(rows == sel[1], dsh, 0.0)

    return pl.pallas_call(
        body, name=name, grid=(n // tr,),
        in_specs=[pl.BlockSpec((tr, k), lambda j: (j, 0)), pl.BlockSpec((k, dm), lambda j: (0, 0)),
                  pl.BlockSpec((tr, dm), lambda j: (j, 0)), pl.BlockSpec((1, 8, dm), lambda j: (j, 0, 0)),
                  pl.BlockSpec((tr, dm), lambda j: (j, 0))],
        out_specs=[pl.BlockSpec((tr, dm), lambda j: (j, 0)), pl.BlockSpec((1, 8, dm), lambda j: (j, 0, 0))],
        out_shape=[jax.ShapeDtypeStruct((n, dm), F32), jax.ShapeDtypeStruct((n // tr, 8, dm), F32)],
        compiler_params=_cparams(("arbitrary",)),
    )(dc, wt, x, mod, add)


def _spec(kind, arr, width, cb, tr, tps):
    if kind == "row":
        return pl.BlockSpec((tr, width), lambda j: (j, cb))
    if kind == "pos":
        return pl.BlockSpec((tr, width), lambda j: (j % tps, cb))
    if kind == "tile":
        return pl.BlockSpec((1,) + arr.shape[1:], lambda j: (j, 0, 0))
    if kind == "par":
        return pl.BlockSpec(arr.shape, lambda j: (0, 0))
    raise ValueError(kind)


def _load(kind, ref):
    v = ref[0] if kind == "tile" else ref[...]
    return v.astype(F32)


def tile_fwd(name, fn, ins, outs, n, tr, tps):
    kinds = [i[0] for i in ins]

    def body(*refs):
        vals = [_load(k, r) for k, r in zip(kinds, refs[:len(ins)])]
        res = fn(*vals)
        for r, o in zip(refs[len(ins):], res):
            r[...] = o.astype(r.dtype)

    return pl.pallas_call(
        body, name=name, grid=(n // tr,),
        in_specs=[_spec(k, a, w, cb, tr, tps) for k, a, w, cb in ins],
        out_specs=[pl.BlockSpec((tr, w), lambda j: (j, 0)) for w, _ in outs],
        out_shape=[jax.ShapeDtypeStruct((n, w), dt) for w, dt in outs],
        compiler_params=_cparams(("arbitrary",)),
    )(*[i[1] for i in ins])


def tile_bwd(name, fn, ins, cots, want, n, tr, tps):
    kinds = [i[0] for i in ins]
    widx = [w[0] for w in want]
    ni, nc = len(ins), len(cots)

    def body(*refs):
        vals = [_load(k, r) for k, r in zip(kinds, refs[:ni])]
        cvals = tuple(r[...].astype(F32) for r in refs[ni:ni + nc])

        def f(*dv):
            full = list(vals)
            for i, v in zip(widx, dv):
                full[i] = v
            return tuple(fn(*full))

        _, vjp = jax.vjp(f, *[vals[i] for i in widx])
        grads = vjp(cvals)
        first = pl.program_id(0) == 0
        for r, g, i in zip(refs[ni + nc:], grads, widx):
            if kinds[i] == "par":
                @pl.when(first)
                def _(r=r):
                    r[...] = jnp.zeros_like(r)
                r[...] += g
            elif kinds[i] == "tile":
                r[0] = g.astype(r.dtype)
            else:
                r[...] = g.astype(r.dtype)

    out_specs, out_shape = [], []
    for i, dt in want:
        k, a, w, cb = ins[i]
        if k == "par":
            out_specs.append(pl.BlockSpec(a.shape, lambda j: (0, 0)))
            out_shape.append(jax.ShapeDtypeStruct(a.shape, F32))
        elif k == "tile":
            out_specs.append(pl.BlockSpec((1,) + a.shape[1:], lambda j: (j, 0, 0)))
            out_shape.append(jax.ShapeDtypeStruct(a.shape, F32))
        else:
            out_specs.append(pl.BlockSpec((tr, w), lambda j: (j, 0)))
            out_shape.append(jax.ShapeDtypeStruct((n, w), dt))
    return pl.pallas_call(
        body, name=name, grid=(n // tr,),
        in_specs=[_spec(k, a, w, cb, tr, tps) for k, a, w, cb in ins]
        + [pl.BlockSpec((tr, c.shape[1]), lambda j: (j, 0)) for c in cots],
        out_specs=out_specs, out_shape=out_shape, compiler_params=_cparams(("arbitrary",)),
    )(*[i[1] for i in ins], *cots)


def pre_fn(p_gq, p_rq, p_rk, p_cq, p_ckv, p_misc, rcos, rsin, mcos, msin,
           w2f, w2b, b2f, b2b, retf, retb, qg, kvg, wuq, wuk, wuv, e2):
    tr = p_gq.shape[0]
    gq = p_gq * (32 ** -0.5)
    af = jax.nn.log_sigmoid(hdot(p_misc, w2f) + b2f) * (1.0 / GLA_TAU)
    ab = jax.nn.log_sigmoid(hdot(p_misc, w2b) + b2b) * (1.0 / GLA_TAU)
    arf = jnp.zeros((tr, 128), F32) + retf
    arb = jnp.zeros((tr, 128), F32) + retb
    rq = p_rq * rcos + _swap16(p_rq) * rsin
    rks = p_rk * (32 ** -0.5)
    rk = rks * rcos + _swap16(rks) * rsin
    qp = bdot_nt(_rms(p_cq, qg), wuq) * MLA_SCALE
    ckvn = _rms(p_ckv, kvg)
    kp = bdot_nt(ckvn, wuk) + xdot(p_misc, e2)
    mc, ms = jnp.tile(mcos, (1, 8)), jnp.tile(msin, (1, 8))
    v = bdot_nt(ckvn, wuv)
    return gq, af, ab, arf, arb, rq, rk, qp * mc + _swap8(qp) * ms, kp * mc + _swap8(kp) * ms, v


def post_fn(ogf, ogb, orf, orb, om, gg, rg, x, mod, gng, wout, lng, lnb):
    avg = _group_avg(256, 64)
    og = ogf + ogb
    mg = og * lax.rsqrt(xdot(og * og, avg) + EPS) * gng * _silu(gg)
    orr = orf + orb
    oc = orr - xdot(orr, avg)
    mr = oc * lax.rsqrt(xdot(oc * oc, avg) + EPS) * _silu(rg)
    m = jnp.concatenate([mg, mr, om], axis=1)
    y = bdot(m, wout)
    return (_layer_norm(ALPHA * x + mod[2:3] * y, lng, lnb),)


def ln2_fn(x1, f, mod, lng, lnb):
    return (_layer_norm(ALPHA * x1 + mod[5:6] * f, lng, lnb),)


def scan_step(q, k, v, a, st, rev):
    ii, jj = _iota((CHUNK, CHUNK), 0), _iota((CHUNK, CHUNK), 1)
    tri = ((jj >= ii) if rev else (jj <= ii)).astype(F32)
    b = xdot_l(tri, a)
    btot = jnp.sum(a, axis=0, keepdims=True)
    qe = q * jnp.exp(b - btot)
    ke = k * jnp.exp(btot - b)
    lane = _iota((1, 128), 1)
    q4 = jnp.concatenate([qe * (lane // 32 == h).astype(F32) for h in range(4)], axis=0)
    att = bdot_nt(q4, ke)
    att = jnp.where(jnp.concatenate([tri] * 4, axis=0) > 0, att, 0.0)
    r = bdot(att, v)
    col = _iota((1, 256), 1)
    o = bdot_nt(q * jnp.exp(b), st)
    for h in range(4):
        o = o + r[h * CHUNK:(h + 1) * CHUNK] * (col // 64 == h).astype(F32)
    vk = bdot_tn(v, ke)
    bd = (_iota((256, 128), 0) // 64 == _iota((256, 128), 1) // 32).astype(F32)
    return o, st * jnp.exp(btot) + vk * bd


def _chunk_maps(nch, nctx):
    def fwd(s):
        return s

    def bwd(s):
        return jnp.where(s < nctx, nctx - 1 - s, nch - 1 - (s - nctx))
    return fwd, bwd


def _per_sample(arr, nb):
    return arr.reshape(nb, arr.shape[0] // nb, arr.shape[1])


def scan_fwd(name, q, k, v, af, ab, nb, nch, nctx):
    n = af.shape[0]
    fmap, bmap = _chunk_maps(nch, nctx)

    def body(qf, kf, vf, a_f, qb, kb, vb, a_b, of_ref, ob_ref, stf_ref, stb_ref, s_scr):
        @pl.when(pl.program_id(0) == 0)
        def _():
            s_scr[...] = jnp.zeros_like(s_scr)

        for i in range(nb):
            stf_ref[0, i] = s_scr[2 * i]
            stb_ref[0, i] = s_scr[2 * i + 1]
            o, sn = scan_step(qf[i], kf[i], vf[i], a_f[i], s_scr[2 * i], False)
            of_ref[i] = o
            s_scr[2 * i] = sn
            o, sn = scan_step(qb[i], kb[i], vb[i], a_b[i], s_scr[2 * i + 1], True)
            ob_ref[i] = o
            s_scr[2 * i + 1] = sn

    def specs(m):
        return [pl.BlockSpec((nb, CHUNK, w), lambda s, cb=cb: (0, m(s), cb)) for _, w, cb in (q, k, v)] + \
               [pl.BlockSpec((nb, CHUNK, 128), lambda s: (0, m(s), 0))]

    ps = lambda a: _per_sample(a, nb)
    of, ob, stf, stb = pl.pallas_call(
        body, name=name, grid=(nch,), in_specs=specs(fmap) + specs(bmap),
        out_specs=[pl.BlockSpec((nb, CHUNK, 256), lambda s: (0, fmap(s), 0)),
                   pl.BlockSpec((nb, CHUNK, 256), lambda s: (0, bmap(s), 0)),
                   pl.BlockSpec((1, nb, 256, 128), lambda s: (s, 0, 0, 0)),
                   pl.BlockSpec((1, nb, 256, 128), lambda s: (s, 0, 0, 0))],
        out_shape=[jax.ShapeDtypeStruct((nb, n // nb, 256), F32)] * 2
        + [jax.ShapeDtypeStruct((nch, nb, 256, 128), F32)] * 2,
        scratch_shapes=[pltpu.VMEM((2 * nb, 256, 128), F32)], compiler_params=_cparams(("arbitrary",)),
    )(ps(q[0]), ps(k[0]), ps(v[0]), ps(af), ps(q[0]), ps(k[0]), ps(v[0]), ps(ab))
    return of.reshape(n, 256), ob.reshape(n, 256), stf, stb


def scan_bwd(name, q, k, v, af, ab, stf, stb, do, nb, nch, nctx):
    n = af.shape[0]
    fmap0, bmap0 = _chunk_maps(nch, nctx)
    fmap = lambda r: fmap0(nch - 1 - r)
    bmap = lambda r: bmap0(nch - 1 - r)

    def body(qf, kf, vf, a_f, sf, dof, qb, kb, vb, a_b, sb, dob,
             dqf, dkf, dvf, daf, dqb, dkb, dvb, dab, ds_scr):
        @pl.when(pl.program_id(0) == 0)
        def _():
            ds_scr[...] = jnp.zeros_like(ds_scr)

        for i in range(nb):
            for d, (qr, kr, vr, ar, sr, dor, outs) in enumerate(((qf, kf, vf, a_f, sf, dof, (dqf, dkf, dvf, daf)),
                                                                   (qb, kb, vb, a_b, sb, dob, (dqb, dkb, dvb, dab)))):
                _, vjp = jax.vjp(functools.partial(scan_step, rev=bool(d)), qr[i], kr[i], vr[i], ar[i], sr[0, i])
                dq, dk, dv, da, ds = vjp((dor[i], ds_scr[2 * i + d]))
                outs[0][i] = dq
                outs[1][i] = dk
                outs[2][i] = dv
                outs[3][i] = da
                ds_scr[2 * i + d] = ds

    def specs(m):
        return [pl.BlockSpec((nb, CHUNK, w), lambda r, cb=cb: (0, m(r), cb)) for _, w, cb in (q, k, v)] + \
               [pl.BlockSpec((nb, CHUNK, 128), lambda r: (0, m(r), 0)),
                pl.BlockSpec((1, nb, 256, 128), lambda r: (nch - 1 - r, 0, 0, 0)),
                pl.BlockSpec((nb, CHUNK, 256), lambda r: (0, m(r), 0))]

    def ospecs(m):
        return [pl.BlockSpec((nb, CHUNK, w), lambda r: (0, m(r), 0)) for w in (128, 128, 256, 128)]

    ps = lambda a: _per_sample(a, nb)
    oshape = [jax.ShapeDtypeStruct((nb, n // nb, w), F32) for w in (128, 128, 256, 128)]
    outs = pl.pallas_call(
        body, name=name, grid=(nch,), in_specs=specs(fmap) + specs(bmap),
        out_specs=ospecs(fmap) + ospecs(bmap), out_shape=oshape + oshape,
        scratch_shapes=[pltpu.VMEM((2 * nb, 256, 128), F32)], compiler_params=_cparams(("arbitrary",)),
    )(ps(q[0]), ps(k[0]), ps(v[0]), ps(af), stf, ps(do), ps(q[0]), ps(k[0]), ps(v[0]), ps(ab), stb, ps(do))
    return [o.reshape(n, o.shape[2]) for o in outs]


def mla_fwd(name, qa, ka, va, nb, tps, tr, nctx_rows):
    n = qa.shape[0]
    t = tps * tr

    def body(q_ref, k_ref, v_ref, o_ref, lse_ref):
        def attend(nk):
            vv = v_ref[0:nk, :]
            first = _iota(vv.shape, 1) < 64
            one = jnp.ones_like(vv)
            res, lses = [], []
            for h in range(2):
                s = lax.dot_general(q_ref[:, h * 128:(h + 1) * 128], k_ref[0:nk, h * 128:(h + 1) * 128],
                                    (((1,), (1,)), ((), ())), preferred_element_type=F32)
                m = jnp.max(s, axis=-1, keepdims=True)
                e = jnp.exp((s - m).astype(BF16))
                r = jnp.dot(e, jnp.where(first == (h == 0), vv, one), preferred_element_type=F32)
                l = r[:, 64:65] if h == 0 else r[:, 0:1]
                res.append(r / l)
                lses.append(m + jnp.log(l))
            lane = _iota((tr, 128), 1) < 64
            o_ref[...] = jnp.where(lane, res[0], res[1])
            lse_ref[...] = jnp.where(lane, lses[0], lses[1])

        @pl.when(pl.program_id(2) == 0)
        def _():
            attend(nctx_rows)

        @pl.when(pl.program_id(2) > 0)
        def _():
            attend(t)

    return pl.pallas_call(
        body, name=name, grid=(nb, 4, tps),
        in_specs=[pl.BlockSpec((tr, 256), lambda b, h, j: (b * tps + j, h)), pl.BlockSpec((t, 256), lambda b, h, j: (b, h)),
                  pl.BlockSpec((t, 128), lambda b, h, j: (b, h))],
        out_specs=[pl.BlockSpec((tr, 128), lambda b, h, j: (b * tps + j, h))] * 2,
        out_shape=[jax.ShapeDtypeStruct((n, 512), F32)] * 2,
        compiler_params=_cparams(("parallel", "parallel", "arbitrary")),
    )(qa, ka, va)


def mla_bwd(name, qa, ka, va, o, lse, do, nb, tps, tr, nctx_rows):
    n = qa.shape[0]
    t = tps * tr

    def body(q_ref, k_ref, v_ref, o_ref, lse_ref, do_ref, dq_ref, dk_ref, dv_ref):
        @pl.when(pl.program_id(2) == 0)
        def _():
            dk_ref[...] = jnp.zeros_like(dk_ref)
            dv_ref[...] = jnp.zeros_like(dv_ref)

        def attend(nk):
            dov = do_ref[...]
            oo = dov * o_ref[...]
            dob = dov.astype(BF16)
            first = _iota(dob.shape, 1) < 64
            dqs = []
            dv = None
            for h in range(2):
                hs = slice(h * 128, (h + 1) * 128)
                qh, kh = q_ref[:, hs], k_ref[0:nk, hs]
                mine = first == (h == 0)
                delta = jnp.sum(jnp.where(mine, oo, 0.0), axis=-1, keepdims=True)
                doh = jnp.where(mine, dob, jnp.zeros_like(dob))
                s = lax.dot_general(qh, kh, (((1,), (1,)), ((), ())), preferred_element_type=F32)
                p = jnp.exp((s - lse_ref[:, h * 64:h * 64 + 1]).astype(BF16))
                dp = lax.dot_general(doh, v_ref[0:nk, :], (((1,), (1,)), ((), ())), preferred_element_type=F32)
                ds = p * (dp - delta).astype(BF16)
                dqs.append(jnp.dot(ds, kh, preferred_element_type=F32))
                dk_ref[0:nk, hs] += lax.dot_general(ds, qh, (((0,), (0,)), ((), ())), preferred_element_type=F32)
                dvh = lax.dot_general(p, doh, (((0,), (0,)), ((), ())), preferred_element_type=F32)
                dv = dvh if dv is None else dv + dvh
            dq_ref[...] = jnp.concatenate(dqs, axis=1)
            dv_ref[0:nk, :] += dv

        @pl.when(pl.program_id(2) == 0)
        def _():
            attend(nctx_rows)

        @pl.when(pl.program_id(2) > 0)
        def _():
            attend(t)

    qtile = pl.BlockSpec((tr, 128), lambda b, h, j: (b * tps + j, h))
    return pl.pallas_call(
        body, name=name, grid=(nb, 4, tps),
        in_specs=[pl.BlockSpec((tr, 256), lambda b, h, j: (b * tps + j, h)), pl.BlockSpec((t, 256), lambda b, h, j: (b, h)),
                  pl.BlockSpec((t, 128), lambda b, h, j: (b, h)), qtile, qtile, qtile],
        out_specs=[pl.BlockSpec((tr, 256), lambda b, h, j: (b * tps + j, h)), pl.BlockSpec((t, 256), lambda b, h, j: (b, h)),
                   pl.BlockSpec((t, 128), lambda b, h, j: (b, h))],
        out_shape=[jax.ShapeDtypeStruct((n, 1024), F32), jax.ShapeDtypeStruct((n, 1024), F32),
                   jax.ShapeDtypeStruct((n, 512), F32)],
        compiler_params=_cparams(("parallel", "parallel", "arbitrary")),
    )(qa, ka, va, o, lse, do)


HALO = 16


def _halo_specs(tr, width, tps, nt):
    r = tr // HALO
    return [pl.BlockSpec((tr, width), lambda j, c: (j, c)),
            pl.BlockSpec((HALO, width), lambda j, c: (jnp.maximum(j * r - 1, 0), c)),
            pl.BlockSpec((HALO, width), lambda j, c: (jnp.minimum((j + 1) * r, nt * r - 1), c))]


def _shifted(u, prev, nxt, j, tps):
    tr = u.shape[0]
    t = j % tps
    has_prev = (t >= 2).astype(F32)
    has_next = jnp.logical_and(t >= 1, t <= tps - 2).astype(F32)
    rows = _iota(u.shape, 0)
    dn = jnp.where(rows == 0, prev[HALO - 1:HALO] * has_prev, pltpu.roll(u, 1, 0))
    up = jnp.where(rows == tr - 1, nxt[0:1] * has_next, pltpu.roll(u, tr - 1, 0))
    return dn, up


def _ffn_act(ucv):
    return _silu(ucv[:, :FF_CHUNK]) * ucv[:, FF_CHUNK:]


def ffn2_fwd(name, u, cw, cb, wd, x1, mod, lng, lnb, tr, tps):
    n = u.shape[0]
    nt = n // tr
    w2 = 2 * FF_CHUNK

    def body(u_ref, up_ref, un_ref, cw_ref, cb_ref, wd_ref, x1_ref, m_ref, g_ref, b_ref, f_ref, x2_ref, acc):
        j, c = pl.program_id(0), pl.program_id(1)
        uu = u_ref[...].astype(F32)
        dn, up = _shifted(uu, up_ref[...].astype(F32), un_ref[...].astype(F32), j, tps)
        cwv = cw_ref[...]
        ucv = cwv[0:1] * dn + cwv[1:2] * uu + cwv[2:3] * up + cb_ref[...]
        part = bdot(_ffn_act(ucv), wd_ref[...])

        @pl.when(c == 0)
        def _():
            acc[...] = part

        @pl.when(c == 1)
        def _():
            f = acc[...] + part
            f_ref[...] = f
            x2_ref[...] = ln2_fn(x1_ref[...], f, m_ref[0], g_ref[...], b_ref[...])[0]

    return pl.pallas_call(
        body, name=name, grid=(nt, 2),
        in_specs=_halo_specs(tr, w2, tps, nt) + [
            pl.BlockSpec((8, w2), lambda j, c: (0, c)), pl.BlockSpec((1, w2), lambda j, c: (0, c)),
            pl.BlockSpec((FF_CHUNK, D), lambda j, c: (c, 0)), pl.BlockSpec((tr, D), lambda j, c: (j, 0)),
            pl.BlockSpec((1, 8, D), lambda j, c: (j, 0, 0)), pl.BlockSpec((1, D), lambda j, c: (0, 0)),
            pl.BlockSpec((1, D), lambda j, c: (0, 0))],
        out_specs=[pl.BlockSpec((tr, D), lambda j, c: (j, 0)), pl.BlockSpec((tr, D), lambda j, c: (j, 0))],
        out_shape=[jax.ShapeDtypeStruct((n, D), F32)] * 2, scratch_shapes=[pltpu.VMEM((tr, D), F32)],
        compiler_params=_cparams(("arbitrary", "arbitrary")),
    )(u, u, u, cw, cb, wd, x1, mod, lng, lnb)


def ffn2_bwd(name, u, cw, cb, wd, df, tr, tps):
    n = u.shape[0]
    nt = n // tr
    w2 = 2 * FF_CHUNK

    def body(u_ref, up_ref, un_ref, cw_ref, cb_ref, wd_ref, df_ref, ducv_ref, dwd_ref):
        c, j = pl.program_id(0), pl.program_id(1)
        uu = u_ref[...].astype(F32)
        dn, up = _shifted(uu, up_ref[...].astype(F32), un_ref[...].astype(F32), j, tps)
        cwv = cw_ref[...]
        ucv = cwv[0:1] * dn + cwv[1:2] * uu + cwv[2:3] * up + cb_ref[...]
        act, act_vjp = jax.vjp(_ffn_act, ucv)
        dfb = df_ref[...].astype(BF16)
        dact = lax.dot_general(dfb, wd_ref[...], (((1,), (1,)), ((), ())), preferred_element_type=F32)
        (ducv,) = act_vjp(dact)
        dwd = lax.dot_general(act.astype(BF16), dfb, (((0,), (0,)), ((), ())), preferred_element_type=F32)
        ducv_ref[...] = ducv.astype(ducv_ref.dtype)

        @pl.when(j == 0)
        def _():
            dwd_ref[...] = jnp.zeros_like(dwd_ref)

        dwd_ref[...] += dwd

    hs = _halo_specs(tr, w2, tps, nt)
    swap = lambda spec: pl.BlockSpec(spec.block_shape, lambda c, j, f=spec.index_map: f(j, c))
    return pl.pallas_call(
        body, name=name, grid=(2, nt),
        in_specs=[swap(s) for s in hs] + [
            pl.BlockSpec((8, w2), lambda c, j: (0, c)), pl.BlockSpec((1, w2), lambda c, j: (0, c)),
            pl.BlockSpec((FF_CHUNK, D), lambda c, j: (c, 0)), pl.BlockSpec((tr, D), lambda c, j: (j, 0))],
        out_specs=[pl.BlockSpec((tr, w2), lambda c, j: (j, c)), pl.BlockSpec((FF_CHUNK, D), lambda c, j: (c, 0))],
        out_shape=[jax.ShapeDtypeStruct((n, 2 * w2), BF16), jax.ShapeDtypeStruct((D_FF, D), F32)],
        compiler_params=_cparams(("parallel", "arbitrary")),
    )(u, u, u, cw, cb, wd, df)


def conv_bwd(name, ducv, u, cw, tr, tps):
    n = u.shape[0]
    nt = n // tr
    w2 = 2 * FF_CHUNK

    def body(g_ref, gp_ref, gn_ref, u_ref, up_ref, un_ref, cw_ref, du_ref, dcw_ref, dcb_ref):
        c, j = pl.program_id(0), pl.program_id(1)
        g = g_ref[...].astype(F32)
        gdn, gup = _shifted(g, gp_ref[...].astype(F32), gn_ref[...].astype(F32), j, tps)
        uu = u_ref[...].astype(F32)
        udn, uup = _shifted(uu, up_ref[...].astype(F32), un_ref[...].astype(F32), j, tps)
        cwv = cw_ref[...]
        du_ref[...] = (cwv[0:1] * gup + cwv[1:2] * g + cwv[2:3] * gdn).astype(du_ref.dtype)
        rows = _iota((8, w2), 0)
        s = lambda z: jnp.sum(z, axis=0, keepdims=True)
        dcw = (jnp.where(rows == 0, s(g * udn), 0.0) + jnp.where(rows == 1, s(g * uu), 0.0)
               + jnp.where(rows == 2, s(g * uup), 0.0))

        @pl.when(j == 0)
        def _():
            dcw_ref[...] = jnp.zeros_like(dcw_ref)
            dcb_ref[...] = jnp.zeros_like(dcb_ref)

        dcw_ref[...] += dcw
        dcb_ref[...] += s(g)

    hs = _halo_specs(tr, w2, tps, nt)
    swap = lambda spec: pl.BlockSpec(spec.block_shape, lambda c, j, f=spec.index_map: f(j, c))
    return pl.pallas_call(
        body, name=name, grid=(2, nt),
        in_specs=[swap(s) for s in hs] * 2 + [pl.BlockSpec((8, w2), lambda c, j: (0, c))],
        out_specs=[pl.BlockSpec((tr, w2), lambda c, j: (j, c)), pl.BlockSpec((8, w2), lambda c, j: (0, c)),
                   pl.BlockSpec((1, w2), lambda c, j: (0, c))],
        out_shape=[jax.ShapeDtypeStruct((n, 2 * w2), BF16), jax.ShapeDtypeStruct((8, 2 * w2), F32),
                   jax.ShapeDtypeStruct((1, 2 * w2), F32)],
        compiler_params=_cparams(("parallel", "arbitrary")),
    )(ducv, ducv, ducv, u, u, u, cw)


def loss_head(name, xf, target, nb, tps, tr):
    n = xf.shape[0]

    def body(x_ref, t_ref, dy_ref, l_ref):
        lat = (pl.program_id(0) % tps > 0).astype(F32)
        err = (x_ref[...] - t_ref[...]) * lat
        dy_ref[...] = err * (1.0 / D)
        l_ref[...] = jnp.zeros_like(l_ref) + 0.5 * jnp.sum(err * err) * (1.0 / D)

    def tmap(j):
        return ((j // tps) * (tps - 1) + jnp.maximum(j % tps - 1, 0), 0)

    return pl.pallas_call(
        body, name=name, grid=(n // tr,),
        in_specs=[pl.BlockSpec((tr, D), lambda j: (j, 0)), pl.BlockSpec((tr, D), tmap)],
        out_specs=[pl.BlockSpec((tr, D), lambda j: (j, 0)), pl.BlockSpec((1, 8, 128), lambda j: (j, 0, 0))],
        out_shape=[jax.ShapeDtypeStruct((n, D), F32), jax.ShapeDtypeStruct((n // tr, 8, 128), F32)],
        compiler_params=_cparams(("arbitrary",)),
    )(xf, target)


ADAM_MAX_ROWS = 512


def adamw(name, w, m, v, g8):
    r, c = w.shape
    k = g8.shape[0]
    rows = max(b for b in range(8, ADAM_MAX_ROWS + 1, 8) if r % b == 0)
    bc1 = 1.0 - ADAM_B1 ** ADAM_STEP
    bc2 = 1.0 - ADAM_B2 ** ADAM_STEP

    def body(w_ref, m_ref, v_ref, g_ref, go_ref, d_ref, mo_ref, vo_ref):
        g = g_ref[0].astype(F32)
        for i in range(1, k):
            g = g + g_ref[i].astype(F32)
        mn = ADAM_B1 * m_ref[...] + (1.0 - ADAM_B1) * g
        vn = ADAM_B2 * v_ref[...] + (1.0 - ADAM_B2) * (g * g)
        go_ref[...] = g
        mo_ref[...] = mn
        vo_ref[...] = vn
        d_ref[...] = -ADAM_LR * ((mn / bc1) / (jnp.sqrt(vn / bc2) + ADAM_EPS) + ADAM_WD * w_ref[...])

    blk = pl.BlockSpec((rows, c), lambda i: (i, 0))
    return pl.pallas_call(
        body, name=name, grid=(r // rows,),
        in_specs=[blk, blk, blk, pl.BlockSpec((k, rows, c), lambda i: (0, i, 0))],
        out_specs=[blk] * 4, out_shape=[jax.ShapeDtypeStruct((r, c), F32)] * 4,
        compiler_params=_cparams(("parallel",)),
    )(w, m, v, g8)


def ada_fwd(name, s, aw, ab):
    nl, _, cw = aw.shape

    def body(s_ref, w_ref, b_ref, o_ref):
        o_ref[0] = hdot(s_ref[...], w_ref[0]) + b_ref[0]

    return pl.pallas_call(
        body, name=name, grid=(nl,),
        in_specs=[pl.BlockSpec(s.shape, lambda l: (0, 0)), pl.BlockSpec((1, D, cw), lambda l: (l, 0, 0)),
                  pl.BlockSpec((1, 1, cw), lambda l: (l, 0, 0))],
        out_specs=pl.BlockSpec((1, s.shape[0], cw), lambda l: (l, 0, 0)),
        out_shape=jax.ShapeDtypeStruct((nl, s.shape[0], cw), F32), compiler_params=_cparams(("arbitrary",)),
    )(s, aw, ab)


def ada_bwd(name, s, aw, dmod):
    nl, _, cw = aw.shape

    def body(s_ref, w_ref, d_ref, dw_ref, ds_ref):
        dw_ref[0] = lax.dot_general(s_ref[...], d_ref[0], (((0,), (0,)), ((), ())), precision=HI,
                                    preferred_element_type=F32)
        ds_ref[0] = lax.dot_general(d_ref[0], w_ref[0], (((1,), (1,)), ((), ())), precision=HI,
                                    preferred_element_type=F32)

    return pl.pallas_call(
        body, name=name, grid=(nl,),
        in_specs=[pl.BlockSpec(s.shape, lambda l: (0, 0)), pl.BlockSpec((1, D, cw), lambda l: (l, 0, 0)),
                  pl.BlockSpec((1, s.shape[0], cw), lambda l: (l, 0, 0))],
        out_specs=[pl.BlockSpec((1, D, cw), lambda l: (l, 0, 0)), pl.BlockSpec((1, s.shape[0], D), lambda l: (l, 0, 0))],
        out_shape=[jax.ShapeDtypeStruct((nl, D, cw), F32), jax.ShapeDtypeStruct((nl, s.shape[0], D), F32)],
        compiler_params=_cparams(("arbitrary",)),
    )(s, aw, dmod)


def _place():
    return lax.axis_index("x"), lax.axis_index("y"), lax.axis_index("c")


def all_gather(name, x, in_vmem):
    r, c = x.shape

    def body(x_ref, out_ref, send_sems, recv_sems, local_sem):
        px, py, pc = _place()
        me, sibling = (px, py, pc), (px, py, 1 - pc)
        chips = [(1 - px, py), (px, 1 - py), (1 - px, 1 - py)]

        def rows(qx, qy, qc):
            return out_ref.at[pl.ds((4 * qx + 2 * qy + qc) * r, r), :]

        def copy(k, block, to, src=None):
            return pltpu.make_async_remote_copy(
                src_ref=rows(*block) if src is None else src, dst_ref=rows(*block),
                send_sem=send_sems.at[k], recv_sem=recv_sems.at[k], device_id=to, device_id_type=MESH)

        mine = pltpu.make_async_copy(x_ref, rows(*me), local_sem)
        mine.start()
        first = [copy(0, me, sibling, src=x_ref)]
        first += [copy(1 + j, me, (*chip, pc), src=x_ref) for j, chip in enumerate(chips)]
        for cp in first:
            cp.start()
        passed = [copy(4 + j, (*chip, pc), sibling) for j, chip in enumerate(chips)]
        for j, chip in enumerate(chips):
            copy(1 + j, (*chip, pc), me).wait_recv()
            passed[j].start()
        copy(0, sibling, me).wait_recv()
        for j, chip in enumerate(chips):
            copy(4 + j, (*chip, 1 - pc), me).wait_recv()
        for cp in first + passed:
            cp.wait_send()
        mine.wait()

    space = pltpu.VMEM if in_vmem else pl.ANY
    return pl.pallas_call(
        body, name=name, out_shape=jax.ShapeDtypeStruct((N_DEV * r, c), x.dtype),
        in_specs=[pl.BlockSpec(memory_space=space)], out_specs=pl.BlockSpec(memory_space=space),
        scratch_shapes=[pltpu.SemaphoreType.DMA((7,)), pltpu.SemaphoreType.DMA((7,)), pltpu.SemaphoreType.DMA],
        compiler_params=pltpu.CompilerParams(vmem_limit_bytes=VMEM_LIMIT_BYTES),
    )(x)


def all_to_all(name, x):
    _, r, c = x.shape

    def body(x_ref, out_ref, send_sems, recv_sems, local_sem):
        px, py, pc = _place()
        my = 4 * px + 2 * py + pc
        mine = pltpu.make_async_copy(x_ref.at[my], out_ref.at[my], local_sem)
        mine.start()
        copies = []
        for k in range(1, N_DEV):
            qx, qy, qc = px ^ (k >> 2 & 1), py ^ (k >> 1 & 1), pc ^ (k & 1)
            copies.append(pltpu.make_async_remote_copy(
                src_ref=x_ref.at[4 * qx + 2 * qy + qc], dst_ref=out_ref.at[my],
                send_sem=send_sems.at[k - 1], recv_sem=recv_sems.at[k - 1],
                device_id=(qx, qy, qc), device_id_type=MESH))
        for cp in copies:
            cp.start()
        for k, cp in enumerate(copies):
            cp.wait_send()
        for k in range(1, N_DEV):
            qx, qy, qc = px ^ (k >> 2 & 1), py ^ (k >> 1 & 1), pc ^ (k & 1)
            q = 4 * qx + 2 * qy + qc
            pltpu.make_async_remote_copy(
                src_ref=x_ref.at[q], dst_ref=out_ref.at[q], send_sem=send_sems.at[k - 1],
                recv_sem=recv_sems.at[k - 1], device_id=(qx, qy, qc), device_id_type=MESH).wait_recv()
        mine.wait()

    return pl.pallas_call(
        body, name=name, out_shape=jax.ShapeDtypeStruct(x.shape, x.dtype),
        in_specs=[pl.BlockSpec(memory_space=pl.ANY)], out_specs=pl.BlockSpec(memory_space=pl.ANY),
        scratch_shapes=[pltpu.SemaphoreType.DMA((7,)), pltpu.SemaphoreType.DMA((7,)), pltpu.SemaphoreType.DMA],
    )(x)


def _tables(seq, nctx_rows):
    f32 = np.float32
    pos = np.arange(seq, dtype=f32)
    ret_inv = (1.0 / (ROPE_BASE ** np.linspace(0.0, 1.0, 16, dtype=f32))).astype(f32)
    ang = pos[:, None] * ret_inv
    rc, rs = np.cos(ang).astype(f32), np.sin(ang).astype(f32)
    rcos = np.tile(np.concatenate([rc, rc], 1), (1, 4))
    rsin = np.tile(np.concatenate([-rs, rs], 1), (1, 4))
    rows = np.repeat(np.arange(seq // 64, dtype=f32), 64)
    cols = np.tile(np.arange(64, dtype=f32), seq // 64)
    ax_inv = (ROPE_BASE ** (-np.arange(8, dtype=f32) / 8)).astype(f32)
    ra, ca = rows[:, None] * ax_inv, cols[:, None] * ax_inv
    one, zero = np.ones((seq, 64), f32), np.zeros((seq, 64), f32)
    mcos = np.concatenate([one, np.cos(ra), np.cos(ra), np.cos(ca), np.cos(ca), one[:, :32]], 1)
    msin = np.concatenate([zero, -np.sin(ra), np.sin(ra), -np.sin(ca), np.sin(ca), zero[:, :32]], 1)
    ident = lambda t, v: np.concatenate([np.full((nctx_rows, 128), v, f32), t.astype(f32)], 0)
    return [jnp.asarray(ident(rcos, 1.0)), jnp.asarray(ident(rsin, 0.0)),
            jnp.asarray(ident(mcos, 1.0)), jnp.asarray(ident(msin, 0.0))]


def _prep_layer(w, l):
    z = lambda *s: jnp.zeros(s, F32)
    p = {}
    win = _rows(w["w_in_t"][l], W_IN_SEGS)
    p["w_in_t"] = jnp.concatenate([win, jnp.zeros((P_PAD - D_IN, D), win.dtype)], axis=0)
    p["w_up_t"] = _rows(w["ffn_up_t"][l], FF_SEGS)
    p["w_down"] = w["ffn_down"][l]
    p["w_out"] = w["w_out"][l]
    p["wuq"] = _pad_heads(w["mla_w_uq_t"][l], 96)
    p["wuk"] = _pad_heads(w["mla_w_uk_t"][l], 64)
    p["wuv"] = w["mla_w_uv_t"][l]
    gw = w["gla_gate_w"][l]
    p["w2f"] = z(128, 128).at[0:16].set(gw[0])
    p["w2b"] = z(128, 128).at[16:32].set(gw[1])
    p["b2f"], p["b2b"] = w["gla_gate_b"][l][0:1], w["gla_gate_b"][l][1:2]
    lg = jax.nn.log_sigmoid(w["ret_decay"][l])
    p["retf"], p["retb"] = jnp.repeat(lg[0], 32)[None], jnp.repeat(lg[1], 32)[None]
    p["qg"], p["kvg"] = w["mla_q_norm_g"][l][None], w["mla_kv_norm_g"][l][None]
    p["gng"] = jnp.tile(w["gla_norm_g"][l], 4)[None]
    p["ln1g"], p["ln1b"] = w["ln1_g"][l][None], w["ln1_b"][l][None]
    p["ln2g"], p["ln2b"] = w["ln2_g"][l][None], w["ln2_b"][l][None]
    p["cw"] = jnp.concatenate([_cols(w["ffn_conv_w"][l], FF_SEGS), z(5, 2 * D_FF)], axis=0)
    p["cb"] = _cols(w["ffn_conv_b"][l], FF_SEGS)[None]
    e2 = np.zeros((128, 1024), np.float32)
    for h in range(8):
        e2[32 + np.arange(32), h * 128 + 64 + np.arange(32)] = 1.0
    p["e2"] = jnp.asarray(e2)
    return p


def _pre_ins(pa, tabs, p):
    row = lambda w, cb: ("row", pa, w, cb)
    return [row(128, 0), row(128, 6), row(128, 7), row(256, 6), row(128, 14), row(128, 15)] + \
           [("pos", t, 128, 0) for t in tabs] + \
           [("par", p[k], 0, 0) for k in ("w2f", "w2b", "b2f", "b2b", "retf", "retb", "qg", "kvg", "wuq", "wuk", "wuv", "e2")]


_PRE_OUTS = [(128, F32)] * 7 + [(1024, BF16), (1024, BF16), (512, BF16)]
_PRE_WANT = [(i, F32) for i in range(6)] + [(i, F32) for i in range(10, 21)]


def _post_ins(ogf, ogb, orf, orb, om, pa, x, mod, p):
    return [("row", ogf, 256, 0), ("row", ogb, 256, 0), ("row", orf, 256, 0), ("row", orb, 256, 0),
            ("row", om, 512, 0), ("row", pa, 256, 2), ("row", pa, 256, 5), ("row", x, D, 0), ("tile", mod, 0, 0),
            ("par", p["gng"], 0, 0), ("par", p["w_out"], 0, 0), ("par", p["ln1g"], 0, 0), ("par", p["ln1b"], 0, 0)]


def layer_fwd(l, x, mod, p, tabs, dims):
    nb, tps, tr, nch, nctx = dims
    n = x.shape[0]
    pa = mm("proj", x, p["w_in_t"], F32, P_PAD, tr, mod=mod, sel=(1, 0))
    gq, af, ab, arf, arb, rq, rk, qa, ka, va = tile_fwd("mix_pre", pre_fn, _pre_ins(pa, tabs, p), _PRE_OUTS, n, tr, tps)
    ogf, ogb, gstf, gstb = scan_fwd("gla_scan", (gq, 128, 0), (pa, 128, 1), (pa, 256, 1), af, ab, nb, nch, nctx)
    orf, orb, rstf, rstb = scan_fwd("ret_scan", (rq, 128, 0), (rk, 128, 0), (pa, 256, 4), arf, arb, nb, nch, nctx)
    om, lse = mla_fwd("mla_attn", qa, ka, va, nb, tps, tr, nctx * CHUNK)
    (x1,) = tile_fwd("mix_post", post_fn, _post_ins(ogf, ogb, orf, orb, om, pa, x, mod, p), [(D, F32)], n, tr, tps)
    u = mm("ffn_up", x1, p["w_up_t"], BF16, 2 * D_FF, tr, mod=mod, sel=(4, 3))
    f, x2 = ffn2_fwd("ffn_down", u, p["cw"], p["cb"], p["w_down"], x1, mod, p["ln2g"], p["ln2b"], tr, tps)
    saved = dict(x=x, pa=pa, gq=gq, af=af, ab=ab, arf=arf, arb=arb, rq=rq, rk=rk, qa=qa, ka=ka, va=va,
                 ogf=ogf, ogb=ogb, gstf=gstf, gstb=gstb, orf=orf, orb=orb, rstf=rstf, rstb=rstb, om=om, lse=lse,
                 x1=x1, u=u, f=f)
    return x2, saved


def layer_bwd(l, dx2, s, mod, p, tabs, dims):
    nb, tps, tr, nch, nctx = dims
    n = dx2.shape[0]
    g = {}
    ln2_ins = [("row", s["x1"], D, 0), ("row", s["f"], D, 0), ("tile", mod, 0, 0),
               ("par", p["ln2g"], 0, 0), ("par", p["ln2b"], 0, 0)]
    dx1a, df, dmod_a, g["ln2g"], g["ln2b"] = tile_bwd(
        "ln2_bwd", ln2_fn, ln2_ins, [dx2], [(0, F32), (1, F32), (2, F32), (3, F32), (4, F32)], n, tr, tps)
    ducv, g["w_down"] = ffn2_bwd("ffn_down_bwd", s["u"], p["cw"], p["cb"], p["w_down"], df, tr, tps)
    du, g["cw"], g["cb"] = conv_bwd("conv_bwd", ducv, s["u"], p["cw"], tr, tps)
    g["w_up_t"] = mm_tn("ffn_up_dw", du, s["x1"], D_FF, tr, mod=mod, sel=(4, 3))
    dx1, dmod_b = mm_modbwd("ffn_up_dx", du, p["w_up_t"], s["x1"], mod, dx1a, (4, 3), tr)

    post_ins = _post_ins(s["ogf"], s["ogb"], s["orf"], s["orb"], s["om"], s["pa"], s["x"], mod, p)
    want = [(0, F32), (2, F32), (4, F32), (5, F32), (6, F32), (7, F32), (8, F32), (9, F32), (10, F32), (11, F32), (12, F32)]
    dog, dor, dom, dgg, drg, dxa, dmod_c, g["gng"], g["w_out"], g["ln1g"], g["ln1b"] = tile_bwd(
        "mix_post_bwd", post_fn, post_ins, [dx1], want, n, tr, tps)
    dqa, dka, dva = mla_bwd("mla_attn_bwd", s["qa"], s["ka"], s["va"], s["om"], s["lse"], dom, nb, tps, tr,
                            nctx * CHUNK)
    pa = s["pa"]
    gdqf, gdkf, gdvf, gdaf, gdqb, gdkb, gdvb, gdab = scan_bwd(
        "gla_scan_bwd", (s["gq"], 128, 0), (pa, 128, 1), (pa, 256, 1), s["af"], s["ab"], s["gstf"], s["gstb"], dog,
        nb, nch, nctx)
    rdqf, rdkf, rdvf, rdaf, rdqb, rdkb, rdvb, rdab = scan_bwd(
        "ret_scan_bwd", (s["rq"], 128, 0), (s["rk"], 128, 0), (pa, 256, 4), s["arf"], s["arb"], s["rstf"], s["rstb"],
        dor, nb, nch, nctx)

    pre_ins = _pre_ins(pa, tabs, p)
    extra = [gdqf, gdqb, rdqf, rdqb, rdkf, rdkb, gdkf, gdkb, gdvf, gdvb, rdvf, rdvb, dgg, drg]
    kinds = [i[0] for i in pre_ins]
    widx = [w[0] for w in _PRE_WANT]
    npre = len(pre_ins)

    def body(*refs):
        vals = [_load(k, r) for k, r in zip(kinds, refs[:npre])]
        rd = lambda i: refs[npre + i][...].astype(F32)
        cots = (rd(0) + rd(1), rd(14), rd(15), rd(16), rd(17), rd(2) + rd(3), rd(4) + rd(5), rd(18), rd(19), rd(20))

        def f(*dv):
            full = list(vals)
            for i, v in zip(widx, dv):
                full[i] = v
            return tuple(pre_fn(*full))

        _, vjp = jax.vjp(f, *[vals[i] for i in widx])
        grads = vjp(cots)
        dgq, drq, drk, dcq, dckv, dmisc = grads[:6]
        dp = jnp.concatenate([dgq, rd(6) + rd(7), rd(8) + rd(9), rd(12), drq, drk, rd(10) + rd(11), rd(13),
                              dcq, dckv, dmisc], axis=1)
        outs = refs[npre + 21:]
        outs[0][...] = dp.astype(BF16)
        first = pl.program_id(0) == 0
        for r, gr in zip(outs[1:], grads[6:]):
            @pl.when(first)
            def _(r=r):
                r[...] = jnp.zeros_like(r)
            r[...] += gr

    cot_arrays = extra + [gdaf, gdab, rdaf, rdab, dqa, dka, dva]
    par_arrays = [pre_ins[i][1] for i in range(10, 21)]
    res = pl.pallas_call(
        body, name="mix_pre_bwd", grid=(n // tr,),
        in_specs=[_spec(k, a, w, cb, tr, tps) for k, a, w, cb in pre_ins]
        + [pl.BlockSpec((tr, c.shape[1]), lambda j: (j, 0)) for c in cot_arrays],
        out_specs=[pl.BlockSpec((tr, P_PAD), lambda j: (j, 0))] + [pl.BlockSpec(a.shape, lambda j: (0, 0)) for a in par_arrays],
        out_shape=[jax.ShapeDtypeStruct((n, P_PAD), BF16)] + [jax.ShapeDtypeStruct(a.shape, F32) for a in par_arrays],
        compiler_params=_cparams(("arbitrary",)),
    )(*[i[1] for i in pre_ins], *cot_arrays)
    dp = res[0]
    for k, v in zip(("w2f", "w2b", "b2f", "b2b", "retf", "retb", "qg", "kvg", "wuq", "wuk", "wuv"), res[1:]):
        g[k] = v
    g["w_in_t"] = mm_tn("proj_dw", dp, s["x"], P_PAD, tr, mod=mod, sel=(1, 0))
    dx, dmod_d = mm_modbwd("proj_dx", dp, p["w_in_t"], s["x"], mod, dxa, (1, 0), tr)
    return dx, dmod_a + dmod_b + dmod_c + dmod_d, g


def _unprep_grads(g, w, l):
    o = {}
    o["w_in_t"] = _rows(g["w_in_t"], W_IN_INV_SEGS)
    o["ffn_up_t"] = _rows(g["w_up_t"], FF_SEGS)
    o["ffn_down"] = g["w_down"]
    o["w_out"] = g["w_out"]
    o["mla_w_uq_t"] = _unpad_heads(g["wuq"], 96)
    o["mla_w_uk_t"] = _unpad_heads(g["wuk"], 64)
    o["mla_w_uv_t"] = g["wuv"]
    o["gla_gate_w"] = jnp.stack([g["w2f"][0:16], g["w2b"][16:32]])
    o["gla_gate_b"] = jnp.concatenate([g["b2f"], g["b2b"]], axis=0)
    dlg = jnp.stack([g["retf"].reshape(4, 32).sum(-1), g["retb"].reshape(4, 32).sum(-1)])
    o["ret_decay"] = dlg * jax.nn.sigmoid(-w["ret_decay"][l])
    o["mla_q_norm_g"], o["mla_kv_norm_g"] = g["qg"][0], g["kvg"][0]
    o["gla_norm_g"] = g["gng"].reshape(4, 64).sum(0)
    o["ln1_g"], o["ln1_b"], o["ln2_g"], o["ln2_b"] = g["ln1g"][0], g["ln1b"][0], g["ln2g"][0], g["ln2b"][0]
    o["ffn_conv_w"] = _cols(g["cw"][0:3], FF_SEGS)
    o["ffn_conv_b"] = _cols(g["cb"][0], FF_SEGS)
    return o


def local_step(xs, target, modtab, w, dims):
    nb, tps, tr, nch, nctx = dims
    tabs = _tables((tps - 1) * tr, tr)
    x = xs
    saved, preps = [], []
    for l in range(DEPTH):
        p = _prep_layer(w, l)
        x, s = layer_fwd(l, x, modtab[l], p, tabs, dims)
        saved.append(s)
        preps.append(p)
    dy, lpart = loss_head("loss_head", x, target, nb, tps, tr)
    loss = jnp.sum(lpart[:, 0, 0])
    dx = dy
    dmods, grads = [None] * DEPTH, [None] * DEPTH
    for l in reversed(range(DEPTH)):
        dx, dmods[l], g = layer_bwd(l, dx, saved[l], modtab[l], preps[l], tabs, dims)
        grads[l] = _unprep_grads(g, w, l)
    gstack = {k: jnp.stack([grads[l][k] for l in range(DEPTH)]) for k in grads[0]}
    return loss, dx, jnp.stack(dmods), gstack


BIG = [("ffn_up", 2), ("ffn_down", 1), ("w_out", 1), ("w_in", 2), ("mla_w_uq", 2), ("mla_w_uk", 2), ("mla_w_uv", 2)]
SMALL = ["ada_b", "gla_gate_w", "gla_gate_b", "gla_norm_g", "ret_decay", "mla_q_norm_g", "mla_kv_norm_g",
         "ln1_g", "ln1_b", "ffn_conv_b", "ln2_g", "ln2_b"]
PACK_C = 1024


def _big_key(k, axis):
    return k + "_t" if axis == 2 else k


def _shard_rows(a, axis):
    a = jnp.swapaxes(a, 1, 2) if axis == 2 else a
    return a.reshape(-1, PACK_C)


def _shard_from_rows(rows, shape, axis):
    l, r, c = shape
    return jnp.swapaxes(rows.reshape(l, c, r), 1, 2) if axis == 2 else rows.reshape(l, r, c)


def sum8(name, g8):
    k, r, c = g8.shape
    rows = max(b for b in range(16, ADAM_MAX_ROWS + 1, 16) if r % b == 0)

    def body(g_ref, o_ref):
        g = g_ref[0].astype(F32)
        for i in range(1, k):
            g = g + g_ref[i].astype(F32)
        o_ref[...] = g

    return pl.pallas_call(
        body, name=name, grid=(r // rows,), in_specs=[pl.BlockSpec((k, rows, c), lambda i: (0, i, 0))],
        out_specs=pl.BlockSpec((rows, c), lambda i: (i, 0)), out_shape=jax.ShapeDtypeStruct((r, c), F32),
        compiler_params=_cparams(("parallel",)),
    )(g8)


def _pack(arrs, dtype):
    flat = jnp.concatenate([a.reshape(-1).astype(dtype) for a in arrs])
    pad = (-flat.shape[0]) % (8 * PACK_C)
    return jnp.concatenate([flat, jnp.zeros((pad,), dtype)]).reshape(-1, PACK_C)


def _unpack(flat2d, shapes):
    flat = flat2d.reshape(-1)
    out, off = [], 0
    for s in shapes:
        sz = int(np.prod(s))
        out.append(flat[off:off + sz].reshape(s))
        off += sz
    return out


def _row_shape(shape, axis):
    l, r, c = shape
    return (l, c, r) if axis == 2 else (l, r, c)


def _whole_from_blocks(blocks, shape, axis):
    l, r, c = _row_shape(shape, axis)
    return blocks.reshape(N_DEV, l, r, c).transpose(1, 0, 2, 3).reshape(l, N_DEV * r, c)


def _blocks_from_whole(whole, shape, axis):
    l, r, c = _row_shape(shape, axis)
    return whole.reshape(l, N_DEV, r, c).transpose(1, 0, 2, 3).reshape(N_DEV, -1, PACK_C)


def kernel(x, c, ctx, c_ctx, ada_w, ada_b, w_in, gla_gate_w, gla_gate_b, gla_norm_g, ret_decay, mla_q_norm_g, mla_kv_norm_g, mla_w_uq, mla_w_uk, mla_w_uv, w_out, ln1_g, ln1_b, ffn_up, ffn_conv_w, ffn_conv_b, ffn_down, ln2_g, ln2_b, loss_target, m_c_ctx, m_ada_w, m_ada_b, m_w_in, m_gla_gate_w, m_gla_gate_b, m_gla_norm_g, m_ret_decay, m_mla_q_norm_g, m_mla_kv_norm_g, m_mla_w_uq, m_mla_w_uk, m_mla_w_uv, m_w_out, m_ln1_g, m_ln1_b, m_ffn_up, m_ffn_conv_w, m_ffn_conv_b, m_ffn_down, m_ln2_g, m_ln2_b, v_c_ctx, v_ada_w, v_ada_b, v_w_in, v_gla_gate_w, v_gla_gate_b, v_gla_norm_g, v_ret_decay, v_mla_q_norm_g, v_mla_kv_norm_g, v_mla_w_uq, v_mla_w_uk, v_mla_w_uv, v_w_out, v_ln1_g, v_ln1_b, v_ffn_up, v_ffn_conv_w, v_ffn_conv_b, v_ffn_down, v_ln2_g, v_ln2_b):
    names = ["c_ctx", "ada_w", "ada_b", "w_in", "gla_gate_w", "gla_gate_b", "gla_norm_g", "ret_decay", "mla_q_norm_g",
             "mla_kv_norm_g", "mla_w_uq", "mla_w_uk", "mla_w_uv", "w_out", "ln1_g", "ln1_b", "ffn_up", "ffn_conv_w",
             "ffn_conv_b", "ffn_down", "ln2_g", "ln2_b"]
    loc = locals()
    W = {k: loc[k] for k in names}
    M = {k: loc["m_" + k] for k in names}
    V = {k: loc["v_" + k] for k in names}

    nb, seq, _ = x.shape
    tr = ctx.shape[1]
    tps = 1 + seq // tr
    t = tps * tr
    n = nb * t
    nt = nb * tps
    dims = (nb, tps, tr, t // CHUNK, tr // CHUNK)
    px, py, pc = _place()
    me = 4 * px + 2 * py + pc
    ncol = ada_w.shape[2]

    cw_loc = ffn_conv_w.reshape(-1)
    g1 = jnp.concatenate([c.reshape(-1), cw_loc])
    g1 = jnp.concatenate([g1, jnp.zeros(((-g1.shape[0]) % (8 * PACK_C),), F32)]).reshape(-1, PACK_C)
    r1 = g1.shape[0]
    g1a = all_gather("gather_cond", g1, True).reshape(N_DEV, -1)
    c_all = g1a[:, :nb * D].reshape(N_DEV * nb, D)
    cw_all = g1a[:, nb * D:nb * D + cw_loc.shape[0]].reshape(N_DEV, DEPTH, 3, -1).transpose(1, 2, 0, 3).reshape(DEPTH, 3, -1)

    wpack = jnp.concatenate([_shard_rows(W[k].astype(BF16), ax) for k, ax in BIG], axis=0)
    prows = wpack.shape[0]
    wall = all_gather("gather_weights", wpack, False).reshape(N_DEV, prows, PACK_C)
    wl, offs, off = {}, {}, 0
    for k, ax in BIG:
        rows = int(np.prod(W[k].shape)) // PACK_C
        wl[_big_key(k, ax)] = _whole_from_blocks(wall[:, off:off + rows], W[k].shape, ax)
        offs[k] = (off, rows)
        off += rows
    for k in SMALL[1:]:
        wl[k] = W[k]
    wl["ffn_conv_w"] = cw_all

    srows = 40
    s_in = jnp.concatenate([c_all, c_ctx[None], jnp.zeros((srows - N_DEV * nb - 1, D), F32)], axis=0)
    s_act = _silu(s_in)
    ab_loc = lax.dynamic_slice_in_dim(ada_b, me * ncol, ncol, axis=1)[:, None, :]
    mod_part = ada_fwd("ada_fwd", s_act, ada_w, ab_loc)
    mod_all = all_gather("gather_mod", mod_part.reshape(-1, ncol), True).reshape(N_DEV, DEPTH, srows, ncol)
    mod_rows = mod_all.transpose(1, 2, 0, 3).reshape(DEPTH, srows, N_DEV * ncol)
    mod_l = lax.dynamic_slice_in_dim(mod_rows, me * nb, nb, axis=1).reshape(DEPTH, nb, 6, D)
    mod_c = mod_rows[:, N_DEV * nb].reshape(DEPTH, 1, 6, D)
    tile_is_ctx = (jnp.arange(tps) == 0)[None, None, :, None, None]
    modtab = jnp.where(tile_is_ctx, mod_c[:, :, None], mod_l[:, :, None])
    modtab = jnp.concatenate([modtab, jnp.zeros((DEPTH, nb, tps, 2, D), F32)], axis=3).reshape(DEPTH, nt, 8, D)

    xs = jnp.concatenate([ctx, x], axis=1).reshape(n, D)
    loss_loc, dxs, dmodtab, gl = local_step(xs, loss_target.reshape(nb * seq, D), modtab, wl, dims)
    loss = lax.psum(loss_loc, ("x", "y", "c"))
    grad_x = dxs.reshape(nb, t, D)[:, tr:]

    dm = dmodtab.reshape(DEPTH, nb, tps, 8, D)[:, :, :, :6]
    dmod_l = dm[:, :, 1:].sum(2).reshape(DEPTH, nb, 6 * D)
    dmod_c = dm[:, :, 0].sum(1).reshape(DEPTH, 1, 6 * D)
    gl["ada_b"] = dmod_l.sum(1) + dmod_c[:, 0]
    small_list = [gl[k] for k in SMALL] + [gl["ffn_conv_w"]]
    small_shapes = [a.shape for a in small_list]
    spack = _pack(small_list + [jnp.concatenate([dmod_l, dmod_c], axis=1)], F32)
    rs = spack.shape[0]
    sall = all_gather("gather_small_grads", spack, True).reshape(N_DEV, rs, PACK_C)
    nsmall = sum(int(np.prod(s)) for s in small_shapes)
    dmo = sall.reshape(N_DEV, -1)[:, nsmall:nsmall + DEPTH * (nb + 1) * 6 * D].reshape(N_DEV, DEPTH, nb + 1, 6 * D)
    dl_all = dmo[:, :, :nb].transpose(1, 0, 2, 3).reshape(DEPTH, N_DEV * nb, 6 * D)
    dc_all = dmo[:, :, nb].sum(0)[:, None]
    dmod_rows = jnp.concatenate([dl_all, dc_all, jnp.zeros((DEPTH, srows - N_DEV * nb - 1, 6 * D), F32)], axis=1)
    dmod_loc = lax.dynamic_slice_in_dim(dmod_rows.reshape(DEPTH, srows, N_DEV, ncol), me, 1, axis=2)[:, :, 0]
    d_ada_w, d_s = ada_bwd("ada_bwd", s_act, ada_w, dmod_loc)
    sg = jax.nn.sigmoid(c_ctx)
    dcc = d_s[:, N_DEV * nb].sum(0) * (sg * (1.0 + c_ctx * (1.0 - sg)))
    ccp = jnp.concatenate([dcc[None], jnp.zeros((7, D), F32)], axis=0)
    ccall = all_gather("gather_cctx", ccp, True).reshape(N_DEV, 8, D)

    gsend = jnp.concatenate([_blocks_from_whole(gl[_big_key(k, ax)].astype(BF16), W[k].shape, ax) for k, ax in BIG],
                            axis=1)
    gsum = sum8("grad_sum", all_to_all("grad_all_to_all", gsend))
    res = {}

    def update2d(tag, k, g):
        last = W[k].shape[-1]
        outs = adamw(tag, W[k].reshape(-1, last), M[k].reshape(-1, last), V[k].reshape(-1, last),
                     g.reshape(1, -1, last))
        res[k] = [a.reshape(W[k].shape) for a in outs]

    for k, ax in BIG:
        off, rows = offs[k]
        update2d("adamw_" + k, k, _shard_from_rows(gsum[off:off + rows], W[k].shape, ax))

    def update(tag, keys, g8):
        outs = adamw(tag, _pack([W[k] for k in keys], F32), _pack([M[k] for k in keys], F32),
                     _pack([V[k] for k in keys], F32), g8)
        for i, arr in enumerate(outs):
            for k, a in zip(keys, _unpack(arr, [W[k].shape for k in keys])):
                res.setdefault(k, [None] * 4)[i] = a

    nrep = sum(int(np.prod(W[k].shape)) for k in SMALL)
    sflat = sall.reshape(N_DEV, -1)
    def pack8(a):
        a = a.reshape(N_DEV, -1)
        pad = (-a.shape[1]) % (8 * PACK_C)
        return jnp.concatenate([a, jnp.zeros((N_DEV, pad), F32)], axis=1).reshape(N_DEV, -1, PACK_C)

    update("adamw_small", SMALL, pack8(sflat[:, :nrep]))
    ncw = ffn_conv_w.shape[2]
    cw8 = sflat[:, nrep:nsmall].reshape(N_DEV, DEPTH, 3, N_DEV * ncw)
    cw8 = lax.dynamic_slice_in_dim(cw8, me * ncw, ncw, axis=3)
    update("adamw_conv", ["ffn_conv_w"], pack8(cw8))
    update2d("adamw_ada", "ada_w", d_ada_w)
    update("adamw_cctx", ["c_ctx"], ccall)

    out = [loss, grad_x]
    for i in range(4):
        out += [res[k][i] for k in names]
    return tuple(out)
```

```python
import functools
import math

import numpy as np
import jax
import jax.numpy as jnp
from jax import lax
from jax.experimental import pallas as pl
from jax.experimental.pallas import tpu as pltpu

F32 = jnp.float32
BF16 = jnp.bfloat16
HI = lax.Precision.HIGHEST
MESH = pl.DeviceIdType.MESH

N_DEV = 8
D = 1024
DEPTH = 4
CHUNK = 64
EPS = 1e-6
ALPHA = (2 * DEPTH) ** 0.25
GLA_TAU = 16.0
ROPE_BASE = 10000.0
MLA_SCALE = 96 ** -0.5
D_FF = 2816
FF_CHUNK = 1408
P_PAD = 2048
VMEM_LIMIT_BYTES = 56 << 20

ADAM_LR, ADAM_B1, ADAM_B2, ADAM_EPS, ADAM_WD, ADAM_STEP = 0.001, 0.9, 0.999, 1e-08, 0.01, 10

D_IN = 1984
W_IN_SEGS = [(0, 512), (544, 1408), (512, 32), (1952, 32)]
W_IN_INV_SEGS = [(0, 512), (1920, 32), (512, 1408), (1952, 32)]
FF_SEGS = [(0, FF_CHUNK), (D_FF, FF_CHUNK), (FF_CHUNK, FF_CHUNK), (D_FF + FF_CHUNK, FF_CHUNK)]


def _cols(a, segs):
    return jnp.concatenate([a[..., s:s + n] for s, n in segs], axis=-1)


def _rows(a, segs):
    return jnp.concatenate([a[s:s + n] for s, n in segs], axis=0)


def _pad_heads(wt, per_head):
    c = wt.shape[1]
    wt = wt.reshape(8, per_head, c)
    return jnp.concatenate([wt, jnp.zeros((8, 128 - per_head, c), wt.dtype)], axis=1).reshape(1024, c)


def _unpad_heads(g, per_head):
    c = g.shape[1]
    return g.reshape(8, 128, c)[:, :per_head].reshape(8 * per_head, c)


def _cparams(sem=None):
    return pltpu.CompilerParams(vmem_limit_bytes=VMEM_LIMIT_BYTES, dimension_semantics=sem)


@jax.custom_vjp
def bdot(a, w):
    return jnp.dot(a.astype(BF16), w.astype(BF16), preferred_element_type=F32)


def _bdot_fwd(a, w):
    return bdot(a, w), (a, w)


def _bdot_bwd(res, ct):
    a, w = res
    ctb = ct.astype(BF16)
    da = lax.dot_general(ctb, w.astype(BF16), (((1,), (1,)), ((), ())), preferred_element_type=F32)
    dw = lax.dot_general(a.astype(BF16), ctb, (((0,), (0,)), ((), ())), preferred_element_type=F32)
    return da.astype(a.dtype), dw.astype(w.dtype)


bdot.defvjp(_bdot_fwd, _bdot_bwd)


@jax.custom_vjp
def bdot_nt(a, wt):
    return lax.dot_general(a.astype(BF16), wt.astype(BF16), (((1,), (1,)), ((), ())), preferred_element_type=F32)


def _bdot_nt_fwd(a, wt):
    return bdot_nt(a, wt), (a, wt)


def _bdot_nt_bwd(res, ct):
    a, wt = res
    ctb = ct.astype(BF16)
    da = jnp.dot(ctb, wt.astype(BF16), preferred_element_type=F32)
    dwt = lax.dot_general(ctb, a.astype(BF16), (((0,), (0,)), ((), ())), preferred_element_type=F32)
    return da.astype(a.dtype), dwt.astype(wt.dtype)


bdot_nt.defvjp(_bdot_nt_fwd, _bdot_nt_bwd)


@jax.custom_vjp
def bdot_tn(a, b):
    return lax.dot_general(a.astype(BF16), b.astype(BF16), (((0,), (0,)), ((), ())), preferred_element_type=F32)


def _bdot_tn_fwd(a, b):
    return bdot_tn(a, b), (a, b)


def _bdot_tn_bwd(res, ct):
    a, b = res
    ctb = ct.astype(BF16)
    da = lax.dot_general(b.astype(BF16), ctb, (((1,), (1,)), ((), ())), preferred_element_type=F32)
    db = jnp.dot(a.astype(BF16), ctb, preferred_element_type=F32)
    return da.astype(a.dtype), db.astype(b.dtype)


bdot_tn.defvjp(_bdot_tn_fwd, _bdot_tn_bwd)


def _split3(x):
    x1 = x.astype(BF16)
    r1 = x - x1.astype(F32)
    x2 = r1.astype(BF16)
    return x1, x2, (r1 - x2.astype(F32)).astype(BF16)


@jax.custom_vjp
def xdot(x, m):
    mb = m.astype(BF16)
    return sum(jnp.dot(xi, mb, preferred_element_type=F32) for xi in _split3(x))


def _xdot_bwd(m, ct):
    mb = m.astype(BF16)
    dx = sum(lax.dot_general(ci, mb, (((1,), (1,)), ((), ())), preferred_element_type=F32) for ci in _split3(ct))
    return dx, jnp.zeros_like(m)


xdot.defvjp(lambda x, m: (xdot(x, m), m), _xdot_bwd)


@jax.custom_vjp
def xdot_l(m, x):
    mb = m.astype(BF16)
    return sum(jnp.dot(mb, xi, preferred_element_type=F32) for xi in _split3(x))


def _xdot_l_bwd(m, ct):
    mb = m.astype(BF16)
    dx = sum(lax.dot_general(mb, ci, (((0,), (0,)), ((), ())), preferred_element_type=F32) for ci in _split3(ct))
    return jnp.zeros_like(m), dx


xdot_l.defvjp(lambda m, x: (xdot_l(m, x), m), _xdot_l_bwd)


def _swap_fn(half):
    def swap(x):
        n = x.shape[1]
        first = (_iota(x.shape, 1) // half) % 2 == 0
        return jnp.where(first, pltpu.roll(x, n - half, 1), pltpu.roll(x, half, 1))

    f = jax.custom_vjp(swap)
    f.defvjp(lambda x: (swap(x), None), lambda _, ct: (swap(ct),))
    return f


_swap16 = _swap_fn(16)
_swap8 = _swap_fn(8)


def hdot(a, b):
    return jnp.dot(a, b, precision=HI, preferred_element_type=F32)


def _iota(shape, axis):
    return lax.broadcasted_iota(jnp.int32, shape, axis)


def _group_avg(n, g):
    return (_iota((n, n), 0) // g == _iota((n, n), 1) // g).astype(F32) * (1.0 / g)


def _silu(x):
    return x * jax.nn.sigmoid(x)


def _layer_norm(z, g, b):
    mu = jnp.mean(z, axis=-1, keepdims=True)
    zc = z - mu
    var = jnp.mean(zc * zc, axis=-1, keepdims=True)
    return zc * lax.rsqrt(var + EPS) * g + b


def _rms(x, g):
    return x * lax.rsqrt(jnp.mean(x * x, axis=-1, keepdims=True) + EPS) * g


def mm(name, a, wt, out_dtype, tn, tr, mod=None, sel=None):
    n, k = a.shape
    nw = wt.shape[0]

    def body(*refs):
        if mod is not None:
            a_ref, m_ref, w_ref, o_ref = refs
            m = m_ref[0]
            av = a_ref[...] * (1.0 + m[sel[0]:sel[0] + 1]) + m[sel[1]:sel[1] + 1]
        else:
            a_ref, w_ref, o_ref = refs
            av = a_ref[...]
        o_ref[...] = lax.dot_general(av.astype(BF16), w_ref[...], (((1,), (1,)), ((), ())),
                                     preferred_element_type=F32).astype(o_ref.dtype)

    in_specs = [pl.BlockSpec((tr, k), lambda c, j: (j, 0))]
    args = [a]
    if mod is not None:
        in_specs.append(pl.BlockSpec((1, 8, k), lambda c, j: (j, 0, 0)))
        args.append(mod)
    in_specs.append(pl.BlockSpec((tn, k), lambda c, j: (c, 0)))
    args.append(wt)
    return pl.pallas_call(
        body, name=name, grid=(nw // tn, n // tr), in_specs=in_specs,
        out_specs=pl.BlockSpec((tr, tn), lambda c, j: (j, c)),
        out_shape=jax.ShapeDtypeStruct((n, nw), out_dtype), compiler_params=_cparams(("parallel", "arbitrary")),
    )(*args)


def mm_tn(name, dc, a, tn, tr, mod=None, sel=None):
    n, k = a.shape
    nw = dc.shape[1]

    def body(*refs):
        if mod is not None:
            d_ref, a_ref, m_ref, o_ref = refs
            m = m_ref[0]
            av = a_ref[...] * (1.0 + m[sel[0]:sel[0] + 1]) + m[sel[1]:sel[1] + 1]
        else:
            d_ref, a_ref, o_ref = refs
            av = a_ref[...]

        @pl.when(pl.program_id(1) == 0)
        def _():
            o_ref[...] = jnp.zeros_like(o_ref)

        o_ref[...] += lax.dot_general(d_ref[...].astype(BF16), av.astype(BF16), (((0,), (0,)), ((), ())),
                                      preferred_element_type=F32)

    in_specs = [pl.BlockSpec((tr, tn), lambda c, j: (j, c)), pl.BlockSpec((tr, k), lambda c, j: (j, 0))]
    args = [dc, a]
    if mod is not None:
        in_specs.append(pl.BlockSpec((1, 8, k), lambda c, j: (j, 0, 0)))
        args.append(mod)
    return pl.pallas_call(
        body, name=name, grid=(nw // tn, n // tr), in_specs=in_specs,
        out_specs=pl.BlockSpec((tn, k), lambda c, j: (c, 0)),
        out_shape=jax.ShapeDtypeStruct((nw, k), F32), compiler_params=_cparams(("parallel", "arbitrary")),
    )(*args)


def mm_modbwd(name, dc, wt, x, mod, add, sel, tr):
    n, k = dc.shape
    dm = wt.shape[1]

    def body(dc_ref, wt_ref, x_ref, m_ref, add_ref, dx_ref, dm_ref):
        dh = jnp.dot(dc_ref[...].astype(BF16), wt_ref[...], preferred_element_type=F32)
        m = m_ref[0]
        dx_ref[...] = add_ref[...] + dh * (1.0 + m[sel[0]:sel[0] + 1])
        dsc = jnp.sum(dh * x_ref[...], axis=0, keepdims=True)
        dsh = jnp.sum(dh, axis=0, keepdims=True)
        rows = _iota((8, dm), 0)
        dm_ref[0] = jnp.where(rows == sel[0], dsc, 0.0) + jnp.where(rows == sel[1], dsh, 0.0)

    return pl.pallas_call(
        body, name=name, grid=(n // tr,),
        in_specs=[pl.BlockSpec((tr, k), lambda j: (j, 0)), pl.BlockSpec((k, dm), lambda j: (0, 0)),
                  pl.BlockSpec((tr, dm), lambda j: (j, 0)), pl.BlockSpec((1, 8, dm), lambda j: (j, 0, 0)),
                  pl.BlockSpec((tr, dm), lambda j: (j, 0))],
        out_specs=[pl.BlockSpec((tr, dm), lambda j: (j, 0)), pl.BlockSpec((1, 8, dm), lambda j: (j, 0, 0))],
        out_shape=[jax.ShapeDtypeStruct((n, dm), F32), jax.ShapeDtypeStruct((n // tr, 8, dm), F32)],
        compiler_params=_cparams(("arbitrary",)),
    )(dc, wt, x, mod, add)


def _spec(kind, arr, width, cb, tr, tps):
    if kind == "row":
        return pl.BlockSpec((tr, width), lambda j: (j, cb))
    if kind == "pos":
        return pl.BlockSpec((tr, width), lambda j: (j % tps, cb))
    if kind == "tile":
        return pl.BlockSpec((1,) + arr.shape[1:], lambda j: (j, 0, 0))
    if kind == "par":
        return pl.BlockSpec(arr.shape, lambda j: (0, 0))
    raise ValueError(kind)


def _load(kind, ref):
    v = ref[0] if kind == "tile" else ref[...]
    return v.astype(F32)


def tile_fwd(name, fn, ins, outs, n, tr, tps):
    kinds = [i[0] for i in ins]

    def body(*refs):
        vals = [_load(k, r) for k, r in zip(kinds, refs[:len(ins)])]
        res = fn(*vals)
        for r, o in zip(refs[len(ins):], res):
            r[...] = o.astype(r.dtype)

    return pl.pallas_call(
        body, name=name, grid=(n // tr,),
        in_specs=[_spec(k, a, w, cb, tr, tps) for k, a, w, cb in ins],
        out_specs=[pl.BlockSpec((tr, w), lambda j: (j, 0)) for w, _ in outs],
        out_shape=[jax.ShapeDtypeStruct((n, w), dt) for w, dt in outs],
        compiler_params=_cparams(("arbitrary",)),
    )(*[i[1] for i in ins])


def tile_bwd(name, fn, ins, cots, want, n, tr, tps):
    kinds = [i[0] for i in ins]
    widx = [w[0] for w in want]
    ni, nc = len(ins), len(cots)

    def body(*refs):
        vals = [_load(k, r) for k, r in zip(kinds, refs[:ni])]
        cvals = tuple(r[...].astype(F32) for r in refs[ni:ni + nc])

        def f(*dv):
            full = list(vals)
            for i, v in zip(widx, dv):
                full[i] = v
            return tuple(fn(*full))

        _, vjp = jax.vjp(f, *[vals[i] for i in widx])
        grads = vjp(cvals)
        first = pl.program_id(0) == 0
        for r, g, i in zip(refs[ni + nc:], grads, widx):
            if kinds[i] == "par":
                @pl.when(first)
                def _(r=r):
                    r[...] = jnp.zeros_like(r)
                r[...] += g
            elif kinds[i] == "tile":
                r[0] = g.astype(r.dtype)
            else:
                r[...] = g.astype(r.dtype)

    out_specs, out_shape = [], []
    for i, dt in want:
        k, a, w, cb = ins[i]
        if k == "par":
            out_specs.append(pl.BlockSpec(a.shape, lambda j: (0, 0)))
            out_shape.append(jax.ShapeDtypeStruct(a.shape, F32))
        elif k == "tile":
            out_specs.append(pl.BlockSpec((1,) + a.shape[1:], lambda j: (j, 0, 0)))
            out_shape.append(jax.ShapeDtypeStruct(a.shape, F32))
        else:
            out_specs.append(pl.BlockSpec((tr, w), lambda j: (j, 0)))
            out_shape.append(jax.ShapeDtypeStruct((n, w), dt))
    return pl.pallas_call(
        body, name=name, grid=(n // tr,),
        in_specs=[_spec(k, a, w, cb, tr, tps) for k, a, w, cb in ins]
        + [pl.BlockSpec((tr, c.shape[1]), lambda j: (j, 0)) for c in cots],
        out_specs=out_specs, out_shape=out_shape, compiler_params=_cparams(("arbitrary",)),
    )(*[i[1] for i in ins], *cots)


def pre_fn(p_gq, p_rq, p_rk, p_cq, p_ckv, p_misc, rcos, rsin, mcos, msin,
           w2f, w2b, b2f, b2b, retf, retb, qg, kvg, wuq, wuk, wuv, e2):
    tr = p_gq.shape[0]
    gq = p_gq * (32 ** -0.5)
    af = jax.nn.log_sigmoid(hdot(p_misc, w2f) + b2f) * (1.0 / GLA_TAU)
    ab = jax.nn.log_sigmoid(hdot(p_misc, w2b) + b2b) * (1.0 / GLA_TAU)
    arf = jnp.zeros((tr, 128), F32) + retf
    arb = jnp.zeros((tr, 128), F32) + retb
    rq = p_rq * rcos + _swap16(p_rq) * rsin
    rks = p_rk * (32 ** -0.5)
    rk = rks * rcos + _swap16(rks) * rsin
    qp = bdot_nt(_rms(p_cq, qg), wuq) * MLA_SCALE
    ckvn = _rms(p_ckv, kvg)
    kp = bdot_nt(ckvn, wuk) + xdot(p_misc, e2)
    mc, ms = jnp.tile(mcos, (1, 8)), jnp.tile(msin, (1, 8))
    v = bdot_nt(ckvn, wuv)
    return gq, af, ab, arf, arb, rq, rk, qp * mc + _swap8(qp) * ms, kp * mc + _swap8(kp) * ms, v


def post_fn(ogf, ogb, orf, orb, om, gg, rg, x, mod, gng, wout, lng, lnb):
    avg = _group_avg(256, 64)
    og = ogf + ogb
    mg = og * lax.rsqrt(xdot(og * og, avg) + EPS) * gng * _silu(gg)
    orr = orf + orb
    oc = orr - xdot(orr, avg)
    mr = oc * lax.rsqrt(xdot(oc * oc, avg) + EPS) * _silu(rg)
    m = jnp.concatenate([mg, mr, om], axis=1)
    y = bdot(m, wout)
    return (_layer_norm(ALPHA * x + mod[2:3] * y, lng, lnb),)


def ln2_fn(x1, f, mod, lng, lnb):
    return (_layer_norm(ALPHA * x1 + mod[5:6] * f, lng, lnb),)


def scan_step(q, k, v, a, st, rev):
    ii, jj = _iota((CHUNK, CHUNK), 0), _iota((CHUNK, CHUNK), 1)
    tri = ((jj >= ii) if rev else (jj <= ii)).astype(F32)
    b = xdot_l(tri, a)
    btot = jnp.sum(a, axis=0, keepdims=True)
    qe = q * jnp.exp(b - btot)
    ke = k * jnp.exp(btot - b)
    lane = _iota((1, 128), 1)
    q4 = jnp.concatenate([qe * (lane // 32 == h).astype(F32) for h in range(4)], axis=0)
    att = bdot_nt(q4, ke)
    att = jnp.where(jnp.concatenate([tri] * 4, axis=0) > 0, att, 0.0)
    r = bdot(att, v)
    col = _iota((1, 256), 1)
    o = bdot_nt(q * jnp.exp(b), st)
    for h in range(4):
        o = o + r[h * CHUNK:(h + 1) * CHUNK] * (col // 64 == h).astype(F32)
    vk = bdot_tn(v, ke)
    bd = (_iota((256, 128), 0) // 64 == _iota((256, 128), 1) // 32).astype(F32)
    return o, st * jnp.exp(btot) + vk * bd


def _chunk_maps(nch, nctx):
    def fwd(s):
        return s

    def bwd(s):
        return jnp.where(s < nctx, nctx - 1 - s, nch - 1 - (s - nctx))
    return fwd, bwd


def _per_sample(arr, nb):
    return arr.reshape(nb, arr.shape[0] // nb, arr.shape[1])


def scan_fwd(name, q, k, v, af, ab, nb, nch, nctx):
    n = af.shape[0]
    fmap, bmap = _chunk_maps(nch, nctx)

    def body(qf, kf, vf, a_f, qb, kb, vb, a_b, of_ref, ob_ref, stf_ref, stb_ref, s_scr):
        @pl.when(pl.program_id(0) == 0)
        def _():
            s_scr[...] = jnp.zeros_like(s_scr)

        for i in range(nb):
            stf_ref[0, i] = s_scr[2 * i]
            stb_ref[0, i] = s_scr[2 * i + 1]
            o, sn = scan_step(qf[i], kf[i], vf[i], a_f[i], s_scr[2 * i], False)
            of_ref[i] = o
            s_scr[2 * i] = sn
            o, sn = scan_step(qb[i], kb[i], vb[i], a_b[i], s_scr[2 * i + 1], True)
            ob_ref[i] = o
            s_scr[2 * i + 1] = sn

    def specs(m):
        return [pl.BlockSpec((nb, CHUNK, w), lambda s, cb=cb: (0, m(s), cb)) for _, w, cb in (q, k, v)] + \
               [pl.BlockSpec((nb, CHUNK, 128), lambda s: (0, m(s), 0))]

    ps = lambda a: _per_sample(a, nb)
    of, ob, stf, stb = pl.pallas_call(
        body, name=name, grid=(nch,), in_specs=specs(fmap) + specs(bmap),
        out_specs=[pl.BlockSpec((nb, CHUNK, 256), lambda s: (0, fmap(s), 0)),
                   pl.BlockSpec((nb, CHUNK, 256), lambda s: (0, bmap(s), 0)),
                   pl.BlockSpec((1, nb, 256, 128), lambda s: (s, 0, 0, 0)),
                   pl.BlockSpec((1, nb, 256, 128), lambda s: (s, 0, 0, 0))],
        out_shape=[jax.ShapeDtypeStruct((nb, n // nb, 256), F32)] * 2
        + [jax.ShapeDtypeStruct((nch, nb, 256, 128), F32)] * 2,
        scratch_shapes=[pltpu.VMEM((2 * nb, 256, 128), F32)], compiler_params=_cparams(("arbitrary",)),
    )(ps(q[0]), ps(k[0]), ps(v[0]), ps(af), ps(q[0]), ps(k[0]), ps(v[0]), ps(ab))
    return of.reshape(n, 256), ob.reshape(n, 256), stf, stb


def scan_bwd(name, q, k, v, af, ab, stf, stb, do, nb, nch, nctx):
    n = af.shape[0]
    fmap0, bmap0 = _chunk_maps(nch, nctx)
    fmap = lambda r: fmap0(nch - 1 - r)
    bmap = lambda r: bmap0(nch - 1 - r)

    def body(qf, kf, vf, a_f, sf, dof, qb, kb, vb, a_b, sb, dob,
             dqf, dkf, dvf, daf, dqb, dkb, dvb, dab, ds_scr):
        @pl.when(pl.program_id(0) == 0)
        def _():
            ds_scr[...] = jnp.zeros_like(ds_scr)

        for i in range(nb):
            for d, (qr, kr, vr, ar, sr, dor, outs) in enumerate(((qf, kf, vf, a_f, sf, dof, (dqf, dkf, dvf, daf)),
                                                                   (qb, kb, vb, a_b, sb, dob, (dqb, dkb, dvb, dab)))):
                _, vjp = jax.vjp(functools.partial(scan_step, rev=bool(d)), qr[i], kr[i], vr[i], ar[i], sr[0, i])
                dq, dk, dv, da, ds = vjp((dor[i], ds_scr[2 * i + d]))
                outs[0][i] = dq
                outs[1][i] = dk
                outs[2][i] = dv
                outs[3][i] = da
                ds_scr[2 * i + d] = ds

    def specs(m):
        return [pl.BlockSpec((nb, CHUNK, w), lambda r, cb=cb: (0, m(r), cb)) for _, w, cb in (q, k, v)] + \
               [pl.BlockSpec((nb, CHUNK, 128), lambda r: (0, m(r), 0)),
                pl.BlockSpec((1, nb, 256, 128), lambda r: (nch - 1 - r, 0, 0, 0)),
                pl.BlockSpec((nb, CHUNK, 256), lambda r: (0, m(r), 0))]

    def ospecs(m):
        return [pl.BlockSpec((nb, CHUNK, w), lambda r: (0, m(r), 0)) for w in (128, 128, 256, 128)]

    ps = lambda a: _per_sample(a, nb)
    oshape = [jax.ShapeDtypeStruct((nb, n // nb, w), F32) for w in (128, 128, 256, 128)]
    outs = pl.pallas_call(
        body, name=name, grid=(nch,), in_specs=specs(fmap) + specs(bmap),
        out_specs=ospecs(fmap) + ospecs(bmap), out_shape=oshape + oshape,
        scratch_shapes=[pltpu.VMEM((2 * nb, 256, 128), F32)], compiler_params=_cparams(("arbitrary",)),
    )(ps(q[0]), ps(k[0]), ps(v[0]), ps(af), stf, ps(do), ps(q[0]), ps(k[0]), ps(v[0]), ps(ab), stb, ps(do))
    return [o.reshape(n, o.shape[2]) for o in outs]


def mla_fwd(name, qa, ka, va, nb, tps, tr, nctx_rows):
    n = qa.shape[0]
    t = tps * tr

    def body(q_ref, k_ref, v_ref, o_ref, lse_ref):
        def attend(nk):
            vv = v_ref[0:nk, :]
            first = _iota(vv.shape, 1) < 64
            one = jnp.ones_like(vv)
            res, lses = [], []
            for h in range(2):
                s = lax.dot_general(q_ref[:, h * 128:(h + 1) * 128], k_ref[0:nk, h * 128:(h + 1) * 128],
                                    (((1,), (1,)), ((), ())), preferred_element_type=F32)
                m = jnp.max(s, axis=-1, keepdims=True)
                e = jnp.exp((s - m).astype(BF16))
                r = jnp.dot(e, jnp.where(first == (h == 0), vv, one), preferred_element_type=F32)
                l = r[:, 64:65] if h == 0 else r[:, 0:1]
                res.append(r / l)
                lses.append(m + jnp.log(l))
            lane = _iota((tr, 128), 1) < 64
            o_ref[...] = jnp.where(lane, res[0], res[1])
            lse_ref[...] = jnp.where(lane, lses[0], lses[1])

        @pl.when(pl.program_id(2) == 0)
        def _():
            attend(nctx_rows)

        @pl.when(pl.program_id(2) > 0)
        def _():
            attend(t)

    return pl.pallas_call(
        body, name=name, grid=(nb, 4, tps),
        in_specs=[pl.BlockSpec((tr, 256), lambda b, h, j: (b * tps + j, h)), pl.BlockSpec((t, 256), lambda b, h, j: (b, h)),
                  pl.BlockSpec((t, 128), lambda b, h, j: (b, h))],
        out_specs=[pl.BlockSpec((tr, 128), lambda b, h, j: (b * tps + j, h))] * 2,
        out_shape=[jax.ShapeDtypeStruct((n, 512), F32)] * 2,
        compiler_params=_cparams(("parallel", "parallel", "arbitrary")),
    )(qa, ka, va)


def mla_bwd(name, qa, ka, va, o, lse, do, nb, tps, tr, nctx_rows):
    n = qa.shape[0]
    t = tps * tr

    def body(q_ref, k_ref, v_ref, o_ref, lse_ref, do_ref, dq_ref, dk_ref, dv_ref):
        @pl.when(pl.program_id(2) == 0)
        def _():
            dk_ref[...] = jnp.zeros_like(dk_ref)
            dv_ref[...] = jnp.zeros_like(dv_ref)

        def attend(nk):
            dov = do_ref[...]
            oo = dov * o_ref[...]
            dob = dov.astype(BF16)
            first = _iota(dob.shape, 1) < 64
            dqs = []
            dv = None
            for h in range(2):
                hs = slice(h * 128, (h + 1) * 128)
                qh, kh = q_ref[:, hs], k_ref[0:nk, hs]
                mine = first == (h == 0)
                delta = jnp.sum(jnp.where(mine, oo, 0.0), axis=-1, keepdims=True)
                doh = jnp.where(mine, dob, jnp.zeros_like(dob))
                s = lax.dot_general(qh, kh, (((1,), (1,)), ((), ())), preferred_element_type=F32)
                p = jnp.exp((s - lse_ref[:, h * 64:h * 64 + 1]).astype(BF16))
                dp = lax.dot_general(doh, v_ref[0:nk, :], (((1,), (1,)), ((), ())), preferred_element_type=F32)
                ds = p * (dp - delta).astype(BF16)
                dqs.append(jnp.dot(ds, kh, preferred_element_type=F32))
                dk_ref[0:nk, hs] += lax.dot_general(ds, qh, (((0,), (0,)), ((), ())), preferred_element_type=F32)
                dvh = lax.dot_general(p, doh, (((0,), (0,)), ((), ())), preferred_element_type=F32)
                dv = dvh if dv is None else dv + dvh
            dq_ref[...] = jnp.concatenate(dqs, axis=1)
            dv_ref[0:nk, :] += dv

        @pl.when(pl.program_id(2) == 0)
        def _():
            attend(nctx_rows)

        @pl.when(pl.program_id(2) > 0)
        def _():
            attend(t)

    qtile = pl.BlockSpec((tr, 128), lambda b, h, j: (b * tps + j, h))
    return pl.pallas_call(
        body, name=name, grid=(nb, 4, tps),
        in_specs=[pl.BlockSpec((tr, 256), lambda b, h, j: (b * tps + j, h)), pl.BlockSpec((t, 256), lambda b, h, j: (b, h)),
                  pl.BlockSpec((t, 128), lambda b, h, j: (b, h)), qtile, qtile, qtile],
        out_specs=[pl.BlockSpec((tr, 256), lambda b, h, j: (b * tps + j, h)), pl.BlockSpec((t, 256), lambda b, h, j: (b, h)),
                   pl.BlockSpec((t, 128), lambda b, h, j: (b, h))],
        out_shape=[jax.ShapeDtypeStruct((n, 1024), F32), jax.ShapeDtypeStruct((n, 1024), F32),
                   jax.ShapeDtypeStruct((n, 512), F32)],
        compiler_params=_cparams(("parallel", "parallel", "arbitrary")),
    )(qa, ka, va, o, lse, do)


HALO = 16


def _halo_specs(tr, width, tps, nt):
    r = tr // HALO
    return [pl.BlockSpec((tr, width), lambda j, c: (j, c)),
            pl.BlockSpec((HALO, width), lambda j, c: (jnp.maximum(j * r - 1, 0), c)),
            pl.BlockSpec((HALO, width), lambda j, c: (jnp.minimum((j + 1) * r, nt * r - 1), c))]


def _shifted(u, prev, nxt, j, tps):
    tr = u.shape[0]
    t = j % tps
    has_prev = (t >= 2).astype(F32)
    has_next = jnp.logical_and(t >= 1, t <= tps - 2).astype(F32)
    rows = _iota(u.shape, 0)
    dn = jnp.where(rows == 0, prev[HALO - 1:HALO] * has_prev, pltpu.roll(u, 1, 0))
    up = jnp.where(rows == tr - 1, nxt[0:1] * has_next, pltpu.roll(u, tr - 1, 0))
    return dn, up


def _ffn_act(ucv):
    return _silu(ucv[:, :FF_CHUNK]) * ucv[:, FF_CHUNK:]


def ffn2_fwd(name, u, cw, cb, wd, x1, mod, lng, lnb, tr, tps):
    n = u.shape[0]
    nt = n // tr
    w2 = 2 * FF_CHUNK

    def body(u_ref, up_ref, un_ref, cw_ref, cb_ref, wd_ref, x1_ref, m_ref, g_ref, b_ref, f_ref, x2_ref, acc):
        j, c = pl.program_id(0), pl.program_id(1)
        uu = u_ref[...].astype(F32)
        dn, up = _shifted(uu, up_ref[...].astype(F32), un_ref[...].astype(F32), j, tps)
        cwv = cw_ref[...]
        ucv = cwv[0:1] * dn + cwv[1:2] * uu + cwv[2:3] * up + cb_ref[...]
        part = bdot(_ffn_act(ucv), wd_ref[...])

        @pl.when(c == 0)
        def _():
            acc[...] = part

        @pl.when(c == 1)
        def _():
            f = acc[...] + part
            f_ref[...] = f
            x2_ref[...] = ln2_fn(x1_ref[...], f, m_ref[0], g_ref[...], b_ref[...])[0]

    return pl.pallas_call(
        body, name=name, grid=(nt, 2),
        in_specs=_halo_specs(tr, w2, tps, nt) + [
            pl.BlockSpec((8, w2), lambda j, c: (0, c)), pl.BlockSpec((1, w2), lambda j, c: (0, c)),
            pl.BlockSpec((FF_CHUNK, D), lambda j, c: (c, 0)), pl.BlockSpec((tr, D), lambda j, c: (j, 0)),
            pl.BlockSpec((1, 8, D), lambda j, c: (j, 0, 0)), pl.BlockSpec((1, D), lambda j, c: (0, 0)),
            pl.BlockSpec((1, D), lambda j, c: (0, 0))],
        out_specs=[pl.BlockSpec((tr, D), lambda j, c: (j, 0)), pl.BlockSpec((tr, D), lambda j, c: (j, 0))],
        out_shape=[jax.ShapeDtypeStruct((n, D), F32)] * 2, scratch_shapes=[pltpu.VMEM((tr, D), F32)],
        compiler_params=_cparams(("arbitrary", "arbitrary")),
    )(u, u, u, cw, cb, wd, x1, mod, lng, lnb)


def ffn2_bwd(name, u, cw, cb, wd, df, tr, tps):
    n = u.shape[0]
    nt = n // tr
    w2 = 2 * FF_CHUNK

    def body(u_ref, up_ref, un_ref, cw_ref, cb_ref, wd_ref, df_ref, ducv_ref, dwd_ref):
        c, j = pl.program_id(0), pl.program_id(1)
        uu = u_ref[...].astype(F32)
        dn, up = _shifted(uu, up_ref[...].astype(F32), un_ref[...].astype(F32), j, tps)
        cwv = cw_ref[...]
        ucv = cwv[0:1] * dn + cwv[1:2] * uu + cwv[2:3] * up + cb_ref[...]
        a, g = ucv[:, :FF_CHUNK], ucv[:, FF_CHUNK:]
        sg = jax.nn.sigmoid(a)
        sa = a * sg
        dfb = df_ref[...].astype(BF16)
        dact = lax.dot_general(dfb, wd_ref[...], (((1,), (1,)), ((), ())), preferred_element_type=F32)
        ducv_ref[:, :FF_CHUNK] = (dact * g * (sg + sa * (1.0 - sg))).astype(ducv_ref.dtype)
        ducv_ref[:, FF_CHUNK:] = (dact * sa).astype(ducv_ref.dtype)
        dwd = lax.dot_general((sa * g).astype(BF16), dfb, (((0,), (0,)), ((), ())), preferred_element_type=F32)

        @pl.when(j == 0)
        def _():
            dwd_ref[...] = jnp.zeros_like(dwd_ref)

        dwd_ref[...] += dwd

    hs = _halo_specs(tr, w2, tps, nt)
    swap = lambda spec: pl.BlockSpec(spec.block_shape, lambda c, j, f=spec.index_map: f(j, c))
    return pl.pallas_call(
        body, name=name, grid=(2, nt),
        in_specs=[swap(s) for s in hs] + [
            pl.BlockSpec((8, w2), lambda c, j: (0, c)), pl.BlockSpec((1, w2), lambda c, j: (0, c)),
            pl.BlockSpec((FF_CHUNK, D), lambda c, j: (c, 0)), pl.BlockSpec((tr, D), lambda c, j: (j, 0))],
        out_specs=[pl.BlockSpec((tr, w2), lambda c, j: (j, c)), pl.BlockSpec((FF_CHUNK, D), lambda c, j: (c, 0))],
        out_shape=[jax.ShapeDtypeStruct((n, 2 * w2), BF16), jax.ShapeDtypeStruct((D_FF, D), F32)],
        compiler_params=_cparams(("parallel", "arbitrary")),
    )(u, u, u, cw, cb, wd, df)


def conv_bwd(name, ducv, u, cw, tr, tps):
    n = u.shape[0]
    nt = n // tr
    w2 = 2 * FF_CHUNK

    def body(g_ref, gp_ref, gn_ref, u_ref, up_ref, un_ref, cw_ref, du_ref, dcw_ref, dcb_ref):
        c, j = pl.program_id(0), pl.program_id(1)
        g = g_ref[...].astype(F32)
        gdn, gup = _shifted(g, gp_ref[...].astype(F32), gn_ref[...].astype(F32), j, tps)
        uu = u_ref[...].astype(F32)
        udn, uup = _shifted(uu, up_ref[...].astype(F32), un_ref[...].astype(F32), j, tps)
        cwv = cw_ref[...]
        du_ref[...] = (cwv[0:1] * gup + cwv[1:2] * g + cwv[2:3] * gdn).astype(du_ref.dtype)
        rows = _iota((8, w2), 0)
        s = lambda z: jnp.sum(z, axis=0, keepdims=True)
        dcw = (jnp.where(rows == 0, s(g * udn), 0.0) + jnp.where(rows == 1, s(g * uu), 0.0)
               + jnp.where(rows == 2, s(g * uup), 0.0))

        @pl.when(j == 0)
        def _():
            dcw_ref[...] = jnp.zeros_like(dcw_ref)
            dcb_ref[...] = jnp.zeros_like(dcb_ref)

        dcw_ref[...] += dcw
        dcb_ref[...] += s(g)

    hs = _halo_specs(tr, w2, tps, nt)
    swap = lambda spec: pl.BlockSpec(spec.block_shape, lambda c, j, f=spec.index_map: f(j, c))
    return pl.pallas_call(
        body, name=name, grid=(2, nt),
        in_specs=[swap(s) for s in hs] * 2 + [pl.BlockSpec((8, w2), lambda c, j: (0, c))],
        out_specs=[pl.BlockSpec((tr, w2), lambda c, j: (j, c)), pl.BlockSpec((8, w2), lambda c, j: (0, c)),
                   pl.BlockSpec((1, w2), lambda c, j: (0, c))],
        out_shape=[jax.ShapeDtypeStruct((n, 2 * w2), BF16), jax.ShapeDtypeStruct((8, 2 * w2), F32),
                   jax.ShapeDtypeStruct((1, 2 * w2), F32)],
        compiler_params=_cparams(("parallel", "arbitrary")),
    )(ducv, ducv, ducv, u, u, u, cw)


def loss_head(name, xf, target, nb, tps, tr):
    n = xf.shape[0]

    def body(x_ref, t_ref, dy_ref, l_ref):
        lat = (pl.program_id(0) % tps > 0).astype(F32)
        err = (x_ref[...] - t_ref[...]) * lat
        dy_ref[...] = err * (1.0 / D)
        l_ref[...] = jnp.zeros_like(l_ref) + 0.5 * jnp.sum(err * err) * (1.0 / D)

    def tmap(j):
        return ((j // tps) * (tps - 1) + jnp.maximum(j % tps - 1, 0), 0)

    return pl.pallas_call(
        body, name=name, grid=(n // tr,),
        in_specs=[pl.BlockSpec((tr, D), lambda j: (j, 0)), pl.BlockSpec((tr, D), tmap)],
        out_specs=[pl.BlockSpec((tr, D), lambda j: (j, 0)), pl.BlockSpec((1, 8, 128), lambda j: (j, 0, 0))],
        out_shape=[jax.ShapeDtypeStruct((n, D), F32), jax.ShapeDtypeStruct((n // tr, 8, 128), F32)],
        compiler_params=_cparams(("arbitrary",)),
    )(xf, target)


ADAM_MAX_ROWS = 512


def adamw(name, w, m, v, g8):
    r, c = w.shape
    k = g8.shape[0]
    rows = max(b for b in range(8, ADAM_MAX_ROWS + 1, 8) if r % b == 0)
    bc1 = 1.0 - ADAM_B1 ** ADAM_STEP
    bc2 = 1.0 - ADAM_B2 ** ADAM_STEP

    def body(w_ref, m_ref, v_ref, g_ref, go_ref, d_ref, mo_ref, vo_ref):
        g = g_ref[0].astype(F32)
        for i in range(1, k):
            g = g + g_ref[i].astype(F32)
        mn = ADAM_B1 * m_ref[...] + (1.0 - ADAM_B1) * g
        vn = ADAM_B2 * v_ref[...] + (1.0 - ADAM_B2) * (g * g)
        go_ref[...] = g
        mo_ref[...] = mn
        vo_ref[...] = vn
        d_ref[...] = -ADAM_LR * ((mn / bc1) / (jnp.sqrt(vn / bc2) + ADAM_EPS) + ADAM_WD * w_ref[...])

    blk = pl.BlockSpec((rows, c), lambda i: (i, 0))
    return pl.pallas_call(
        body, name=name, grid=(r // rows,),
        in_specs=[blk, blk, blk, pl.BlockSpec((k, rows, c), lambda i: (0, i, 0))],
        out_specs=[blk] * 4, out_shape=[jax.ShapeDtypeStruct((r, c), F32)] * 4,
        compiler_params=_cparams(("parallel",)),
    )(w, m, v, g8)


def ada_fwd(name, s, aw, ab):
    nl, _, cw = aw.shape

    def body(s_ref, w_ref, b_ref, o_ref):
        o_ref[0] = hdot(s_ref[...], w_ref[0]) + b_ref[0]

    return pl.pallas_call(
        body, name=name, grid=(nl,),
        in_specs=[pl.BlockSpec(s.shape, lambda l: (0, 0)), pl.BlockSpec((1, D, cw), lambda l: (l, 0, 0)),
                  pl.BlockSpec((1, 1, cw), lambda l: (l, 0, 0))],
        out_specs=pl.BlockSpec((1, s.shape[0], cw), lambda l: (l, 0, 0)),
        out_shape=jax.ShapeDtypeStruct((nl, s.shape[0], cw), F32), compiler_params=_cparams(("arbitrary",)),
    )(s, aw, ab)


def ada_bwd(name, s, aw, dmod):
    nl, _, cw = aw.shape

    def body(s_ref, w_ref, d_ref, dw_ref, ds_ref):
        dw_ref[0] = lax.dot_general(s_ref[...], d_ref[0], (((0,), (0,)), ((), ())), precision=HI,
                                    preferred_element_type=F32)
        ds_ref[0] = lax.dot_general(d_ref[0], w_ref[0], (((1,), (1,)), ((), ())), precision=HI,
                                    preferred_element_type=F32)

    return pl.pallas_call(
        body, name=name, grid=(nl,),
        in_specs=[pl.BlockSpec(s.shape, lambda l: (0, 0)), pl.BlockSpec((1, D, cw), lambda l: (l, 0, 0)),
                  pl.BlockSpec((1, s.shape[0], cw), lambda l: (l, 0, 0))],
        out_specs=[pl.BlockSpec((1, D, cw), lambda l: (l, 0, 0)), pl.BlockSpec((1, s.shape[0], D), lambda l: (l, 0, 0))],
        out_shape=[jax.ShapeDtypeStruct((nl, D, cw), F32), jax.ShapeDtypeStruct((nl, s.shape[0], D), F32)],
        compiler_params=_cparams(("arbitrary",)),
    )(s, aw, dmod)


def _place():
    return lax.axis_index("x"), lax.axis_index("y"), lax.axis_index("c")


def all_gather(name, x, in_vmem):
    r, c = x.shape

    def body(x_ref, out_ref, send_sems, recv_sems, local_sem):
        px, py, pc = _place()
        me, sibling = (px, py, pc), (px, py, 1 - pc)
        chips = [(1 - px, py), (px, 1 - py), (1 - px, 1 - py)]

        def rows(qx, qy, qc):
            return out_ref.at[pl.ds((4 * qx + 2 * qy + qc) * r, r), :]

        def copy(k, block, to, src=None):
            return pltpu.make_async_remote_copy(
                src_ref=rows(*block) if src is None else src, dst_ref=rows(*block),
                send_sem=send_sems.at[k], recv_sem=recv_sems.at[k], device_id=to, device_id_type=MESH)

        mine = pltpu.make_async_copy(x_ref, rows(*me), local_sem)
        mine.start()
        first = [copy(0, me, sibling, src=x_ref)]
        first += [copy(1 + j, me, (*chip, pc), src=x_ref) for j, chip in enumerate(chips)]
        for cp in first:
            cp.start()
        passed = [copy(4 + j, (*chip, pc), sibling) for j, chip in enumerate(chips)]
        for j, chip in enumerate(chips):
            copy(1 + j, (*chip, pc), me).wait_recv()
            passed[j].start()
        copy(0, sibling, me).wait_recv()
        for j, chip in enumerate(chips):
            copy(4 + j, (*chip, 1 - pc), me).wait_recv()
        for cp in first + passed:
            cp.wait_send()
        mine.wait()

    space = pltpu.VMEM if in_vmem else pl.ANY
    return pl.pallas_call(
        body, name=name, out_shape=jax.ShapeDtypeStruct((N_DEV * r, c), x.dtype),
        in_specs=[pl.BlockSpec(memory_space=space)], out_specs=pl.BlockSpec(memory_space=space),
        scratch_shapes=[pltpu.SemaphoreType.DMA((7,)), pltpu.SemaphoreType.DMA((7,)), pltpu.SemaphoreType.DMA],
        compiler_params=pltpu.CompilerParams(vmem_limit_bytes=VMEM_LIMIT_BYTES),
    )(x)


def all_to_all(name, x):
    _, r, c = x.shape

    def body(x_ref, out_ref, send_sems, recv_sems, local_sem):
        px, py, pc = _place()
        my = 4 * px + 2 * py + pc
        mine = pltpu.make_async_copy(x_ref.at[my], out_ref.at[my], local_sem)
        mine.start()
        copies = []
        for k in range(1, N_DEV):
            qx, qy, qc = px ^ (k >> 2 & 1), py ^ (k >> 1 & 1), pc ^ (k & 1)
            copies.append(pltpu.make_async_remote_copy(
                src_ref=x_ref.at[4 * qx + 2 * qy + qc], dst_ref=out_ref.at[my],
                send_sem=send_sems.at[k - 1], recv_sem=recv_sems.at[k - 1],
                device_id=(qx, qy, qc), device_id_type=MESH))
        for cp in copies:
            cp.start()
        for k, cp in enumerate(copies):
            cp.wait_send()
        for k in range(1, N_DEV):
            qx, qy, qc = px ^ (k >> 2 & 1), py ^ (k >> 1 & 1), pc ^ (k & 1)
            q = 4 * qx + 2 * qy + qc
            pltpu.make_async_remote_copy(
                src_ref=x_ref.at[q], dst_ref=out_ref.at[q], send_sem=send_sems.at[k - 1],
                recv_sem=recv_sems.at[k - 1], device_id=(qx, qy, qc), device_id_type=MESH).wait_recv()
        mine.wait()

    return pl.pallas_call(
        body, name=name, out_shape=jax.ShapeDtypeStruct(x.shape, x.dtype),
        in_specs=[pl.BlockSpec(memory_space=pl.ANY)], out_specs=pl.BlockSpec(memory_space=pl.ANY),
        scratch_shapes=[pltpu.SemaphoreType.DMA((7,)), pltpu.SemaphoreType.DMA((7,)), pltpu.SemaphoreType.DMA],
    )(x)


_HBM = pl.BlockSpec(memory_space=pltpu.HBM)
_SEM = pl.BlockSpec(memory_space=pltpu.SEMAPHORE)
_EFFECT = pltpu.SideEffectType.DATAFLOW_SIDE_EFFECTING


def _partner(k):
    px, py, pc = _place()
    q = (px ^ (k >> 2 & 1), py ^ (k >> 1 & 1), pc ^ (k & 1))
    return q, 4 * q[0] + 2 * q[1] + q[2]


def xchg_start(name, x, per_peer):
    r, c = x.shape[-2:]

    def body(x_ref, land_ref, send_sems, recv_sems, x_thru, land_thru, token):
        px, py, pc = _place()
        my = 4 * px + 2 * py + pc
        for k in range(1, N_DEV):
            q, qi = _partner(k)
            pltpu.make_async_remote_copy(
                src_ref=x_ref.at[qi] if per_peer else x_ref, dst_ref=land_ref.at[my],
                send_sem=send_sems.at[k - 1], recv_sem=recv_sems.at[k - 1], device_id=q, device_id_type=MESH).start()
        token[...] = jnp.zeros_like(token)

    land = lax.empty((N_DEV, r, c), x.dtype)
    return pl.pallas_call(
        body, name=name,
        out_shape=(pltpu.SemaphoreType.DMA((N_DEV - 1,)), pltpu.SemaphoreType.DMA((N_DEV - 1,)),
                   pltpu.HBM(x.shape, x.dtype), pltpu.HBM(land.shape, land.dtype), jax.ShapeDtypeStruct((8, 128), F32)),
        in_specs=(_HBM, _HBM), out_specs=(_SEM, _SEM, _HBM, _HBM, pl.BlockSpec(memory_space=pltpu.VMEM)),
        input_output_aliases={0: 2, 1: 3}, compiler_params=pltpu.CompilerParams(has_side_effects=_EFFECT),
    )(pltpu.with_memory_space_constraint(x, pltpu.HBM), pltpu.with_memory_space_constraint(land, pltpu.HBM))


def xchg_wait(name, send_sems, recv_sems, x_thru, land_thru, after, per_peer):
    def body(x_ref, land_ref, send_sems, recv_sems, after_ref, x_out, land_out):
        for k in range(1, N_DEV):
            q, qi = _partner(k)
            cp = pltpu.make_async_remote_copy(
                src_ref=x_ref.at[qi] if per_peer else x_ref, dst_ref=land_ref.at[qi],
                send_sem=send_sems.at[k - 1], recv_sem=recv_sems.at[k - 1], device_id=q, device_id_type=MESH)
            cp.wait_send()
            cp.wait_recv()

    return pl.pallas_call(
        body, name=name,
        out_shape=(pltpu.HBM(x_thru.shape, x_thru.dtype), pltpu.HBM(land_thru.shape, land_thru.dtype)),
        in_specs=(_HBM, _HBM, _SEM, _SEM, pl.BlockSpec(memory_space=pl.ANY)), out_specs=(_HBM, _HBM),
        input_output_aliases={0: 0, 1: 1}, compiler_params=pltpu.CompilerParams(has_side_effects=_EFFECT),
    )(x_thru, land_thru, send_sems, recv_sems, after)


def _tables(seq, nctx_rows):
    f32 = np.float32
    pos = np.arange(seq, dtype=f32)
    ret_inv = (1.0 / (ROPE_BASE ** np.linspace(0.0, 1.0, 16, dtype=f32))).astype(f32)
    ang = pos[:, None] * ret_inv
    rc, rs = np.cos(ang).astype(f32), np.sin(ang).astype(f32)
    rcos = np.tile(np.concatenate([rc, rc], 1), (1, 4))
    rsin = np.tile(np.concatenate([-rs, rs], 1), (1, 4))
    rows = np.repeat(np.arange(seq // 64, dtype=f32), 64)
    cols = np.tile(np.arange(64, dtype=f32), seq // 64)
    ax_inv = (ROPE_BASE ** (-np.arange(8, dtype=f32) / 8)).astype(f32)
    ra, ca = rows[:, None] * ax_inv, cols[:, None] * ax_inv
    one, zero = np.ones((seq, 64), f32), np.zeros((seq, 64), f32)
    mcos = np.concatenate([one, np.cos(ra), np.cos(ra), np.cos(ca), np.cos(ca), one[:, :32]], 1)
    msin = np.concatenate([zero, -np.sin(ra), np.sin(ra), -np.sin(ca), np.sin(ca), zero[:, :32]], 1)
    ident = lambda t, v: np.concatenate([np.full((nctx_rows, 128), v, f32), t.astype(f32)], 0)
    return [jnp.asarray(ident(rcos, 1.0)), jnp.asarray(ident(rsin, 0.0)),
            jnp.asarray(ident(mcos, 1.0)), jnp.asarray(ident(msin, 0.0))]


def _prep_layer(w, l):
    z = lambda *s: jnp.zeros(s, F32)
    p = {}
    win = _rows(w["w_in_t"][l], W_IN_SEGS)
    p["w_in_t"] = jnp.concatenate([win, jnp.zeros((P_PAD - D_IN, D), win.dtype)], axis=0)
    p["w_up_t"] = _rows(w["ffn_up_t"][l], FF_SEGS)
    p["w_down"] = w["ffn_down"][l]
    p["w_out"] = w["w_out"][l]
    p["wuq"] = _pad_heads(w["mla_w_uq_t"][l], 96)
    p["wuk"] = _pad_heads(w["mla_w_uk_t"][l], 64)
    p["wuv"] = w["mla_w_uv_t"][l]
    gw = w["gla_gate_w"][l]
    p["w2f"] = z(128, 128).at[0:16].set(gw[0])
    p["w2b"] = z(128, 128).at[16:32].set(gw[1])
    p["b2f"], p["b2b"] = w["gla_gate_b"][l][0:1], w["gla_gate_b"][l][1:2]
    lg = jax.nn.log_sigmoid(w["ret_decay"][l])
    p["retf"], p["retb"] = jnp.repeat(lg[0], 32)[None], jnp.repeat(lg[1], 32)[None]
    p["qg"], p["kvg"] = w["mla_q_norm_g"][l][None], w["mla_kv_norm_g"][l][None]
    p["gng"] = jnp.tile(w["gla_norm_g"][l], 4)[None]
    p["ln1g"], p["ln1b"] = w["ln1_g"][l][None], w["ln1_b"][l][None]
    p["ln2g"], p["ln2b"] = w["ln2_g"][l][None], w["ln2_b"][l][None]
    p["cw"] = jnp.concatenate([_cols(w["ffn_conv_w"][l], FF_SEGS), z(5, 2 * D_FF)], axis=0)
    p["cb"] = _cols(w["ffn_conv_b"][l], FF_SEGS)[None]
    e2 = np.zeros((128, 1024), np.float32)
    for h in range(8):
        e2[32 + np.arange(32), h * 128 + 64 + np.arange(32)] = 1.0
    p["e2"] = jnp.asarray(e2)
    return p


def _pre_ins(pa, tabs, p):
    row = lambda w, cb: ("row", pa, w, cb)
    return [row(128, 0), row(128, 6), row(128, 7), row(256, 6), row(128, 14), row(128, 15)] + \
           [("pos", t, 128, 0) for t in tabs] + \
           [("par", p[k], 0, 0) for k in ("w2f", "w2b", "b2f", "b2b", "retf", "retb", "qg", "kvg", "wuq", "wuk", "wuv", "e2")]


_PRE_OUTS = [(128, F32)] * 7 + [(1024, BF16), (1024, BF16), (512, BF16)]
_PRE_WANT = [(i, F32) for i in range(6)] + [(i, F32) for i in range(10, 21)]


def _post_ins(ogf, ogb, orf, orb, om, pa, x, mod, p):
    return [("row", ogf, 256, 0), ("row", ogb, 256, 0), ("row", orf, 256, 0), ("row", orb, 256, 0),
            ("row", om, 512, 0), ("row", pa, 256, 2), ("row", pa, 256, 5), ("row", x, D, 0), ("tile", mod, 0, 0),
            ("par", p["gng"], 0, 0), ("par", p["w_out"], 0, 0), ("par", p["ln1g"], 0, 0), ("par", p["ln1b"], 0, 0)]


def layer_fwd(l, x, mod, p, tabs, dims):
    nb, tps, tr, nch, nctx = dims
    n = x.shape[0]
    pa = mm("proj", x, p["w_in_t"], F32, P_PAD, tr, mod=mod, sel=(1, 0))
    gq, af, ab, arf, arb, rq, rk, qa, ka, va = tile_fwd("mix_pre", pre_fn, _pre_ins(pa, tabs, p), _PRE_OUTS, n, tr, tps)
    ogf, ogb, gstf, gstb = scan_fwd("gla_scan", (gq, 128, 0), (pa, 128, 1), (pa, 256, 1), af, ab, nb, nch, nctx)
    orf, orb, rstf, rstb = scan_fwd("ret_scan", (rq, 128, 0), (rk, 128, 0), (pa, 256, 4), arf, arb, nb, nch, nctx)
    om, lse = mla_fwd("mla_attn", qa, ka, va, nb, tps, tr, nctx * CHUNK)
    (x1,) = tile_fwd("mix_post", post_fn, _post_ins(ogf, ogb, orf, orb, om, pa, x, mod, p), [(D, F32)], n, tr, tps)
    u = mm("ffn_up", x1, p["w_up_t"], BF16, 2 * D_FF, tr, mod=mod, sel=(4, 3))
    f, x2 = ffn2_fwd("ffn_down", u, p["cw"], p["cb"], p["w_down"], x1, mod, p["ln2g"], p["ln2b"], tr, tps)
    saved = dict(x=x, pa=pa, gq=gq, af=af, ab=ab, arf=arf, arb=arb, rq=rq, rk=rk, qa=qa, ka=ka, va=va,
                 ogf=ogf, ogb=ogb, gstf=gstf, gstb=gstb, orf=orf, orb=orb, rstf=rstf, rstb=rstb, om=om, lse=lse,
                 x1=x1, u=u, f=f)
    return x2, saved


def layer_bwd(l, dx2, s, mod, p, tabs, dims):
    nb, tps, tr, nch, nctx = dims
    n = dx2.shape[0]
    g = {}
    ln2_ins = [("row", s["x1"], D, 0), ("row", s["f"], D, 0), ("tile", mod, 0, 0),
               ("par", p["ln2g"], 0, 0), ("par", p["ln2b"], 0, 0)]
    dx1a, df, dmod_a, g["ln2g"], g["ln2b"] = tile_bwd(
        "ln2_bwd", ln2_fn, ln2_ins, [dx2], [(0, F32), (1, F32), (2, F32), (3, F32), (4, F32)], n, tr, tps)
    ducv, g["w_down"] = ffn2_bwd("ffn_down_bwd", s["u"], p["cw"], p["cb"], p["w_down"], df, tr, tps)
    du, g["cw"], g["cb"] = conv_bwd("conv_bwd", ducv, s["u"], p["cw"], tr, tps)
    g["w_up_t"] = mm_tn("ffn_up_dw", du, s["x1"], D_FF, tr, mod=mod, sel=(4, 3))
    dx1, dmod_b = mm_modbwd("ffn_up_dx", du, p["w_up_t"], s["x1"], mod, dx1a, (4, 3), tr)

    post_ins = _post_ins(s["ogf"], s["ogb"], s["orf"], s["orb"], s["om"], s["pa"], s["x"], mod, p)
    want = [(0, F32), (2, F32), (4, F32), (5, F32), (6, F32), (7, F32), (8, F32), (9, F32), (10, F32), (11, F32), (12, F32)]
    dog, dor, dom, dgg, drg, dxa, dmod_c, g["gng"], g["w_out"], g["ln1g"], g["ln1b"] = tile_bwd(
        "mix_post_bwd", post_fn, post_ins, [dx1], want, n, tr, tps)
    dqa, dka, dva = mla_bwd("mla_attn_bwd", s["qa"], s["ka"], s["va"], s["om"], s["lse"], dom, nb, tps, tr,
                            nctx * CHUNK)
    pa = s["pa"]
    gdqf, gdkf, gdvf, gdaf, gdqb, gdkb, gdvb, gdab = scan_bwd(
        "gla_scan_bwd", (s["gq"], 128, 0), (pa, 128, 1), (pa, 256, 1), s["af"], s["ab"], s["gstf"], s["gstb"], dog,
        nb, nch, nctx)
    rdqf, rdkf, rdvf, rdaf, rdqb, rdkb, rdvb, rdab = scan_bwd(
        "ret_scan_bwd", (s["rq"], 128, 0), (s["rk"], 128, 0), (pa, 256, 4), s["arf"], s["arb"], s["rstf"], s["rstb"],
        dor, nb, nch, nctx)

    pre_ins = _pre_ins(pa, tabs, p)
    extra = [gdqf, gdqb, rdqf, rdqb, rdkf, rdkb, gdkf, gdkb, gdvf, gdvb, rdvf, rdvb, dgg, drg]
    kinds = [i[0] for i in pre_ins]
    widx = [w[0] for w in _PRE_WANT]
    npre = len(pre_ins)

    def body(*refs):
        vals = [_load(k, r) for k, r in zip(kinds, refs[:npre])]
        rd = lambda i: refs[npre + i][...].astype(F32)
        cots = (rd(0) + rd(1), rd(14), rd(15), rd(16), rd(17), rd(2) + rd(3), rd(4) + rd(5), rd(18), rd(19), rd(20))

        def f(*dv):
            full = list(vals)
            for i, v in zip(widx, dv):
                full[i] = v
            return tuple(pre_fn(*full))

        _, vjp = jax.vjp(f, *[vals[i] for i in widx])
        grads = vjp(cots)
        dgq, drq, drk, dcq, dckv, dmisc = grads[:6]
        dp = jnp.concatenate([dgq, rd(6) + rd(7), rd(8) + rd(9), rd(12), drq, drk, rd(10) + rd(11), rd(13),
                              dcq, dckv, dmisc], axis=1)
        outs = refs[npre + 21:]
        outs[0][...] = dp.astype(BF16)
        first = pl.program_id(0) == 0
        for r, gr in zip(outs[1:], grads[6:]):
            @pl.when(first)
            def _(r=r):
                r[...] = jnp.zeros_like(r)
            r[...] += gr

    cot_arrays = extra + [gdaf, gdab, rdaf, rdab, dqa, dka, dva]
    par_arrays = [pre_ins[i][1] for i in range(10, 21)]
    res = pl.pallas_call(
        body, name="mix_pre_bwd", grid=(n // tr,),
        in_specs=[_spec(k, a, w, cb, tr, tps) for k, a, w, cb in pre_ins]
        + [pl.BlockSpec((tr, c.shape[1]), lambda j: (j, 0)) for c in cot_arrays],
        out_specs=[pl.BlockSpec((tr, P_PAD), lambda j: (j, 0))] + [pl.BlockSpec(a.shape, lambda j: (0, 0)) for a in par_arrays],
        out_shape=[jax.ShapeDtypeStruct((n, P_PAD), BF16)] + [jax.ShapeDtypeStruct(a.shape, F32) for a in par_arrays],
        compiler_params=_cparams(("arbitrary",)),
    )(*[i[1] for i in pre_ins], *cot_arrays)
    dp = res[0]
    for k, v in zip(("w2f", "w2b", "b2f", "b2b", "retf", "retb", "qg", "kvg", "wuq", "wuk", "wuv"), res[1:]):
        g[k] = v
    g["w_in_t"] = mm_tn("proj_dw", dp, s["x"], P_PAD, tr, mod=mod, sel=(1, 0))
    dx, dmod_d = mm_modbwd("proj_dx", dp, p["w_in_t"], s["x"], mod, dxa, (1, 0), tr)
    return dx, dmod_a + dmod_b + dmod_c + dmod_d, g


def _unprep_grads(g, w, l):
    o = {}
    o["w_in_t"] = _rows(g["w_in_t"], W_IN_INV_SEGS)
    o["ffn_up_t"] = _rows(g["w_up_t"], FF_SEGS)
    o["ffn_down"] = g["w_down"]
    o["w_out"] = g["w_out"]
    o["mla_w_uq_t"] = _unpad_heads(g["wuq"], 96)
    o["mla_w_uk_t"] = _unpad_heads(g["wuk"], 64)
    o["mla_w_uv_t"] = g["wuv"]
    o["gla_gate_w"] = jnp.stack([g["w2f"][0:16], g["w2b"][16:32]])
    o["gla_gate_b"] = jnp.concatenate([g["b2f"], g["b2b"]], axis=0)
    dlg = jnp.stack([g["retf"].reshape(4, 32).sum(-1), g["retb"].reshape(4, 32).sum(-1)])
    o["ret_decay"] = dlg * jax.nn.sigmoid(-w["ret_decay"][l])
    o["mla_q_norm_g"], o["mla_kv_norm_g"] = g["qg"][0], g["kvg"][0]
    o["gla_norm_g"] = g["gng"].reshape(4, 64).sum(0)
    o["ln1_g"], o["ln1_b"], o["ln2_g"], o["ln2_b"] = g["ln1g"][0], g["ln1b"][0], g["ln2g"][0], g["ln2b"][0]
    o["ffn_conv_w"] = _cols(g["cw"][0:3], FF_SEGS)
    o["ffn_conv_b"] = _cols(g["cb"][0], FF_SEGS)
    return o


def local_step(xs, target, modtab, layer_weights, dims, grads_ready=None):
    nb, tps, tr, nch, nctx = dims
    tabs = _tables((tps - 1) * tr, tr)
    x = xs
    saved, preps, ws = [], [], []
    for l in range(DEPTH):
        w = layer_weights(l, x)
        p = _prep_layer(w, 0)
        x, s = layer_fwd(l, x, modtab[l], p, tabs, dims)
        saved.append(s)
        preps.append(p)
        ws.append(w)
    dy, lpart = loss_head("loss_head", x, target, nb, tps, tr)
    loss = jnp.sum(lpart[:, 0, 0])
    dx = dy
    dmods, grads = [None] * DEPTH, [None] * DEPTH
    tok = None
    for l in reversed(range(DEPTH)):
        mod = modtab[l] if tok is None else modtab[l] + tok
        dx, dmods[l], g = layer_bwd(l, dx, saved[l], mod, preps[l], tabs, dims)
        grads[l] = _unprep_grads(g, ws[l], 0)
        tok = grads_ready(l, grads) if grads_ready is not None else None
    return loss, dx, jnp.stack(dmods), grads


BIG = [("ffn_up", 2), ("ffn_down", 1), ("w_out", 1), ("w_in", 2), ("mla_w_uq", 2), ("mla_w_uk", 2), ("mla_w_uv", 2)]
SMALL = ["ada_b", "gla_gate_w", "gla_gate_b", "gla_norm_g", "ret_decay", "mla_q_norm_g", "mla_kv_norm_g",
         "ln1_g", "ln1_b", "ffn_conv_b", "ln2_g", "ln2_b"]
PACK_C = 1024


def _big_key(k, axis):
    return k + "_t" if axis == 2 else k


def sum8(name, g8):
    k, r, c = g8.shape
    rows = max(b for b in range(16, ADAM_MAX_ROWS + 1, 16) if r % b == 0)

    def body(g_ref, o_ref):
        g = g_ref[0].astype(F32)
        for i in range(1, k):
            g = g + g_ref[i].astype(F32)
        o_ref[...] = g

    return pl.pallas_call(
        body, name=name, grid=(r // rows,), in_specs=[pl.BlockSpec((k, rows, c), lambda i: (0, i, 0))],
        out_specs=pl.BlockSpec((rows, c), lambda i: (i, 0)), out_shape=jax.ShapeDtypeStruct((r, c), F32),
        compiler_params=_cparams(("parallel",)),
    )(g8)


def _pack(arrs, dtype):
    flat = jnp.concatenate([a.reshape(-1).astype(dtype) for a in arrs])
    pad = (-flat.shape[0]) % (8 * PACK_C)
    return jnp.concatenate([flat, jnp.zeros((pad,), dtype)]).reshape(-1, PACK_C)


def _unpack(flat2d, shapes):
    flat = flat2d.reshape(-1)
    out, off = [], 0
    for s in shapes:
        sz = int(np.prod(s))
        out.append(flat[off:off + sz].reshape(s))
        off += sz
    return out


def _tile_pad(a):
    pad = (-a.shape[-2]) % HALO
    return a if pad == 0 else jnp.concatenate([a, jnp.zeros(a.shape[:-2] + (pad, a.shape[-1]), a.dtype)], axis=-2)


def kernel(x, c, ctx, c_ctx, ada_w, ada_b, w_in, gla_gate_w, gla_gate_b, gla_norm_g, ret_decay, mla_q_norm_g, mla_kv_norm_g, mla_w_uq, mla_w_uk, mla_w_uv, w_out, ln1_g, ln1_b, ffn_up, ffn_conv_w, ffn_conv_b, ffn_down, ln2_g, ln2_b, loss_target, m_c_ctx, m_ada_w, m_ada_b, m_w_in, m_gla_gate_w, m_gla_gate_b, m_gla_norm_g, m_ret_decay, m_mla_q_norm_g, m_mla_kv_norm_g, m_mla_w_uq, m_mla_w_uk, m_mla_w_uv, m_w_out, m_ln1_g, m_ln1_b, m_ffn_up, m_ffn_conv_w, m_ffn_conv_b, m_ffn_down, m_ln2_g, m_ln2_b, v_c_ctx, v_ada_w, v_ada_b, v_w_in, v_gla_gate_w, v_gla_gate_b, v_gla_norm_g, v_ret_decay, v_mla_q_norm_g, v_mla_kv_norm_g, v_mla_w_uq, v_mla_w_uk, v_mla_w_uv, v_w_out, v_ln1_g, v_ln1_b, v_ffn_up, v_ffn_conv_w, v_ffn_conv_b, v_ffn_down, v_ln2_g, v_ln2_b):
    names = ["c_ctx", "ada_w", "ada_b", "w_in", "gla_gate_w", "gla_gate_b", "gla_norm_g", "ret_decay", "mla_q_norm_g",
             "mla_kv_norm_g", "mla_w_uq", "mla_w_uk", "mla_w_uv", "w_out", "ln1_g", "ln1_b", "ffn_up", "ffn_conv_w",
             "ffn_conv_b", "ffn_down", "ln2_g", "ln2_b"]
    loc = locals()
    W = {k: loc[k] for k in names}
    M = {k: loc["m_" + k] for k in names}
    V = {k: loc["v_" + k] for k in names}

    nb, seq, _ = x.shape
    tr = ctx.shape[1]
    tps = 1 + seq // tr
    t = tps * tr
    n = nb * t
    nt = nb * tps
    dims = (nb, tps, tr, t // CHUNK, tr // CHUNK)
    px, py, pc = _place()
    me = 4 * px + 2 * py + pc
    ncol = ada_w.shape[2]

    cw_loc = ffn_conv_w.reshape(-1)
    g1 = jnp.concatenate([c.reshape(-1), cw_loc])
    g1 = jnp.concatenate([g1, jnp.zeros(((-g1.shape[0]) % (8 * PACK_C),), F32)]).reshape(-1, PACK_C)
    r1 = g1.shape[0]
    g1a = all_gather("gather_cond", g1, True).reshape(N_DEV, -1)
    c_all = g1a[:, :nb * D].reshape(N_DEV * nb, D)
    cw_all = g1a[:, nb * D:nb * D + cw_loc.shape[0]].reshape(N_DEV, DEPTH, 3, -1).transpose(1, 2, 0, 3).reshape(DEPTH, 3, -1)

    first, rest = [0], list(range(1, DEPTH))
    row_form = {k: (jnp.swapaxes(W[k], 1, 2) if ax == 2 else W[k]).astype(BF16) for k, ax in BIG}

    def part_rows(k):
        rows = int(np.prod(W[k].shape[1:])) // PACK_C
        return rows, -(-rows // HALO) * HALO

    def pack_rows(ls):
        return jnp.concatenate([_tile_pad(row_form[k][l].reshape(-1, PACK_C)) for k, _ in BIG for l in ls], axis=0)

    def whole_weights(wall, ls):
        out, off = {}, 0
        for k, ax in BIG:
            rows, padded = part_rows(k)
            _, r, c = row_form[k].shape
            out[_big_key(k, ax)] = [wall[:, off + i * padded:off + i * padded + rows].reshape(1, N_DEV * r, c)
                                    for i in range(len(ls))]
            off += len(ls) * padded
        return out

    pack0 = pack_rows(first)
    whole0 = whole_weights(all_gather("gather_weights0", pack0, False).reshape(N_DEV, -1, PACK_C), first)
    pack_rest = pack_rows(rest)
    wsend, wrecv, wsrc, wland, wtok = xchg_start("gather_weights_start", pack_rest, False)
    small_w = {k: W[k] for k in SMALL[1:]}
    small_w["ffn_conv_w"] = cw_all
    later = {}

    def layer_weights(l, xin):
        if l >= 1 and not later:
            src, land = xchg_wait("gather_weights_wait", wsend, wrecv, wsrc, wland, xin, False)
            later.update(whole_weights(lax.dynamic_update_slice(land, src[None], (me, 0, 0)), rest))
        big = {k: v[0] for k, v in whole0.items()} if l == 0 else {k: v[l - 1] for k, v in later.items()}
        return {**big, **{k: v[l:l + 1] for k, v in small_w.items()}}

    srows = 40
    s_in = jnp.concatenate([c_all, c_ctx[None], jnp.zeros((srows - N_DEV * nb - 1, D), F32)], axis=0)
    s_act = _silu(s_in)
    ab_loc = lax.dynamic_slice_in_dim(ada_b, me * ncol, ncol, axis=1)[:, None, :]
    mod_part = ada_fwd("ada_fwd", s_act, ada_w, ab_loc)
    mod_all = all_gather("gather_mod", mod_part.reshape(-1, ncol), True).reshape(N_DEV, DEPTH, srows, ncol)
    mod_rows = mod_all.transpose(1, 2, 0, 3).reshape(DEPTH, srows, N_DEV * ncol)
    mod_l = lax.dynamic_slice_in_dim(mod_rows, me * nb, nb, axis=1).reshape(DEPTH, nb, 6, D)
    mod_c = mod_rows[:, N_DEV * nb].reshape(DEPTH, 1, 6, D)
    tile_is_ctx = (jnp.arange(tps) == 0)[None, None, :, None, None]
    modtab = jnp.where(tile_is_ctx, mod_c[:, :, None], mod_l[:, :, None])
    modtab = jnp.concatenate([modtab, jnp.zeros((DEPTH, nb, tps, 2, D), F32)], axis=3).reshape(DEPTH, nt, 8, D)
    modtab = modtab + wtok[0, 0]

    def grad_blocks(grads, ls):
        return jnp.concatenate(
            [_tile_pad(grads[l][_big_key(k, ax)].astype(BF16).reshape(N_DEV, -1, PACK_C)) for k, ax in BIG for l in ls],
            axis=1)

    early = {}

    def grads_ready(l, grads):
        if l != 1:
            return None
        early["sems"] = xchg_start("grad_exchange_start", grad_blocks(grads, rest), True)
        return early["sems"][4][0, 0]

    xs = jnp.concatenate([ctx, x], axis=1).reshape(n, D)
    loss_loc, dxs, dmodtab, grads = local_step(xs, loss_target.reshape(nb * seq, D), modtab, layer_weights, dims,
                                               grads_ready)
    loss = lax.psum(loss_loc, ("x", "y", "c"))
    grad_x = dxs.reshape(nb, t, D)[:, tr:]
    gl = {k: jnp.stack([g[k] for g in grads]) for k in grads[0] if k in SMALL or k == "ffn_conv_w"}

    dm = dmodtab.reshape(DEPTH, nb, tps, 8, D)[:, :, :, :6]
    dmod_l = dm[:, :, 1:].sum(2).reshape(DEPTH, nb, 6 * D)
    dmod_c = dm[:, :, 0].sum(1).reshape(DEPTH, 1, 6 * D)
    gl["ada_b"] = dmod_l.sum(1) + dmod_c[:, 0]
    small_list = [gl[k] for k in SMALL] + [gl["ffn_conv_w"]]
    small_shapes = [a.shape for a in small_list]
    spack = _pack(small_list + [jnp.concatenate([dmod_l, dmod_c], axis=1)], F32)
    rs = spack.shape[0]
    sall = all_gather("gather_small_grads", spack, True).reshape(N_DEV, rs, PACK_C)
    nsmall = sum(int(np.prod(s)) for s in small_shapes)
    dmo = sall.reshape(N_DEV, -1)[:, nsmall:nsmall + DEPTH * (nb + 1) * 6 * D].reshape(N_DEV, DEPTH, nb + 1, 6 * D)
    dl_all = dmo[:, :, :nb].transpose(1, 0, 2, 3).reshape(DEPTH, N_DEV * nb, 6 * D)
    dc_all = dmo[:, :, nb].sum(0)[:, None]
    dmod_rows = jnp.concatenate([dl_all, dc_all, jnp.zeros((DEPTH, srows - N_DEV * nb - 1, 6 * D), F32)], axis=1)
    dmod_loc = lax.dynamic_slice_in_dim(dmod_rows.reshape(DEPTH, srows, N_DEV, ncol), me, 1, axis=2)[:, :, 0]
    d_ada_w, d_s = ada_bwd("ada_bwd", s_act, ada_w, dmod_loc)
    sg = jax.nn.sigmoid(c_ctx)
    dcc = d_s[:, N_DEV * nb].sum(0) * (sg * (1.0 + c_ctx * (1.0 - sg)))
    ccp = jnp.concatenate([dcc[None], jnp.zeros((7, D), F32)], axis=0)
    ccall = all_gather("gather_cctx", ccp, True).reshape(N_DEV, 8, D)

    esend, erecv, esrc, eland, _ = early["sems"]
    esrc, eland = xchg_wait("grad_exchange_wait", esend, erecv, esrc, eland, dxs, True)
    mine = lax.dynamic_slice_in_dim(esrc, me, 1, axis=0)
    gsum_rest = sum8("grad_sum_rest", lax.dynamic_update_slice(eland, mine, (me, 0, 0)))
    gsum_first = sum8("grad_sum_first", all_to_all("grad_all_to_all", grad_blocks(grads, first)))
    res = {}

    def update2d(tag, k, g):
        last = W[k].shape[-1]
        outs = adamw(tag, W[k].reshape(-1, last), M[k].reshape(-1, last), V[k].reshape(-1, last),
                     g.reshape(1, -1, last))
        res[k] = [a.reshape(W[k].shape) for a in outs]

    off0, off1 = 0, 0
    for k, ax in BIG:
        rows, padded = part_rows(k)
        _, r, c = row_form[k].shape
        parts = [gsum_first[off0:off0 + rows]] + [gsum_rest[off1 + i * padded:off1 + i * padded + rows]
                                                  for i in range(len(rest))]
        g = jnp.stack([p.reshape(r, c) for p in parts])
        update2d("adamw_" + k, k, jnp.swapaxes(g, 1, 2) if ax == 2 else g)
        off0 += padded
        off1 += len(rest) * padded

    def update(tag, keys, g8):
        outs = adamw(tag, _pack([W[k] for k in keys], F32), _pack([M[k] for k in keys], F32),
                     _pack([V[k] for k in keys], F32), g8)
        for i, arr in enumerate(outs):
            for k, a in zip(keys, _unpack(arr, [W[k].shape for k in keys])):
                res.setdefault(k, [None] * 4)[i] = a

    nrep = sum(int(np.prod(W[k].shape)) for k in SMALL)
    sflat = sall.reshape(N_DEV, -1)
    def pack8(a):
        a = a.reshape(N_DEV, -1)
        pad = (-a.shape[1]) % (8 * PACK_C)
        return jnp.concatenate([a, jnp.zeros((N_DEV, pad), F32)], axis=1).reshape(N_DEV, -1, PACK_C)

    update("adamw_small", SMALL, pack8(sflat[:, :nrep]))
    ncw = ffn_conv_w.shape[2]
    cw8 = sflat[:, nrep:nsmall].reshape(N_DEV, DEPTH, 3, N_DEV * ncw)
    cw8 = lax.dynamic_slice_in_dim(cw8, me * ncw, ncw, axis=3)
    update("adamw_conv", ["ffn_conv_w"], pack8(cw8))
    update2d("adamw_ada", "ada_w", d_ada_w)
    update("adamw_cctx", ["c_ctx"], ccall)

    out = [loss, grad_x]
    for i in range(4):
        out += [res[k][i] for k in names]
    return tuple(out)
```

```python
import functools
import math

import numpy as np
import jax
import jax.numpy as jnp
from jax import lax
from jax.experimental import pallas as pl
from jax.experimental.pallas import tpu as pltpu

F32 = jnp.float32
BF16 = jnp.bfloat16
HI = lax.Precision.HIGHEST
MESH = pl.DeviceIdType.MESH

N_DEV = 8
D = 1024
DEPTH = 4
CHUNK = 64
EPS = 1e-6
ALPHA = (2 * DEPTH) ** 0.25
GLA_TAU = 16.0
ROPE_BASE = 10000.0
MLA_SCALE = 96 ** -0.5
D_FF = 2816
FF_CHUNK = 1408
P_PAD = 2048
VMEM_LIMIT_BYTES = 56 << 20

ADAM_LR, ADAM_B1, ADAM_B2, ADAM_EPS, ADAM_WD, ADAM_STEP = 0.001, 0.9, 0.999, 1e-08, 0.01, 10

D_IN = 1984
W_IN_SEGS = [(0, 512), (544, 1408), (512, 32), (1952, 32)]
W_IN_INV_SEGS = [(0, 512), (1920, 32), (512, 1408), (1952, 32)]
FF_SEGS = [(0, FF_CHUNK), (D_FF, FF_CHUNK), (FF_CHUNK, FF_CHUNK), (D_FF + FF_CHUNK, FF_CHUNK)]


def _cols(a, segs):
    return jnp.concatenate([a[..., s:s + n] for s, n in segs], axis=-1)


def _rows(a, segs):
    return jnp.concatenate([a[s:s + n] for s, n in segs], axis=0)


def _pad_heads(wt, per_head):
    c = wt.shape[1]
    wt = wt.reshape(8, per_head, c)
    return jnp.concatenate([wt, jnp.zeros((8, 128 - per_head, c), wt.dtype)], axis=1).reshape(1024, c)


def _unpad_heads(g, per_head):
    c = g.shape[1]
    return g.reshape(8, 128, c)[:, :per_head].reshape(8 * per_head, c)


def _cparams(sem=None):
    return pltpu.CompilerParams(vmem_limit_bytes=VMEM_LIMIT_BYTES, dimension_semantics=sem)


@jax.custom_vjp
def bdot(a, w):
    return jnp.dot(a.astype(BF16), w.astype(BF16), preferred_element_type=F32)


def _bdot_fwd(a, w):
    return bdot(a, w), (a, w)


def _bdot_bwd(res, ct):
    a, w = res
    ctb = ct.astype(BF16)
    da = lax.dot_general(ctb, w.astype(BF16), (((1,), (1,)), ((), ())), preferred_element_type=F32)
    dw = lax.dot_general(a.astype(BF16), ctb, (((0,), (0,)), ((), ())), preferred_element_type=F32)
    return da.astype(a.dtype), dw.astype(w.dtype)


bdot.defvjp(_bdot_fwd, _bdot_bwd)


@jax.custom_vjp
def bdot_nt(a, wt):
    return lax.dot_general(a.astype(BF16), wt.astype(BF16), (((1,), (1,)), ((), ())), preferred_element_type=F32)


def _bdot_nt_fwd(a, wt):
    return bdot_nt(a, wt), (a, wt)


def _bdot_nt_bwd(res, ct):
    a, wt = res
    ctb = ct.astype(BF16)
    da = jnp.dot(ctb, wt.astype(BF16), preferred_element_type=F32)
    dwt = lax.dot_general(ctb, a.astype(BF16), (((0,), (0,)), ((), ())), preferred_element_type=F32)
    return da.astype(a.dtype), dwt.astype(wt.dtype)


bdot_nt.defvjp(_bdot_nt_fwd, _bdot_nt_bwd)


@jax.custom_vjp
def bdot_tn(a, b):
    return lax.dot_general(a.astype(BF16), b.astype(BF16), (((0,), (0,)), ((), ())), preferred_element_type=F32)


def _bdot_tn_fwd(a, b):
    return bdot_tn(a, b), (a, b)


def _bdot_tn_bwd(res, ct):
    a, b = res
    ctb = ct.astype(BF16)
    da = lax.dot_general(b.astype(BF16), ctb, (((1,), (1,)), ((), ())), preferred_element_type=F32)
    db = jnp.dot(a.astype(BF16), ctb, preferred_element_type=F32)
    return da.astype(a.dtype), db.astype(b.dtype)


bdot_tn.defvjp(_bdot_tn_fwd, _bdot_tn_bwd)


def _split3(x):
    x1 = x.astype(BF16)
    r1 = x - x1.astype(F32)
    x2 = r1.astype(BF16)
    return x1, x2, (r1 - x2.astype(F32)).astype(BF16)


@jax.custom_vjp
def xdot(x, m):
    mb = m.astype(BF16)
    return sum(jnp.dot(xi, mb, preferred_element_type=F32) for xi in _split3(x))


def _xdot_bwd(m, ct):
    mb = m.astype(BF16)
    dx = sum(lax.dot_general(ci, mb, (((1,), (1,)), ((), ())), preferred_element_type=F32) for ci in _split3(ct))
    return dx, jnp.zeros_like(m)


xdot.defvjp(lambda x, m: (xdot(x, m), m), _xdot_bwd)


@jax.custom_vjp
def xdot_l(m, x):
    mb = m.astype(BF16)
    return sum(jnp.dot(mb, xi, preferred_element_type=F32) for xi in _split3(x))


def _xdot_l_bwd(m, ct):
    mb = m.astype(BF16)
    dx = sum(lax.dot_general(mb, ci, (((0,), (0,)), ((), ())), preferred_element_type=F32) for ci in _split3(ct))
    return jnp.zeros_like(m), dx


xdot_l.defvjp(lambda m, x: (xdot_l(m, x), m), _xdot_l_bwd)


def _swap_fn(half):
    def swap(x):
        n = x.shape[1]
        first = (_iota(x.shape, 1) // half) % 2 == 0
        return jnp.where(first, pltpu.roll(x, n - half, 1), pltpu.roll(x, half, 1))

    f = jax.custom_vjp(swap)
    f.defvjp(lambda x: (swap(x), None), lambda _, ct: (swap(ct),))
    return f


_swap16 = _swap_fn(16)
_swap8 = _swap_fn(8)


def hdot(a, b):
    return jnp.dot(a, b, precision=HI, preferred_element_type=F32)


def _iota(shape, axis):
    return lax.broadcasted_iota(jnp.int32, shape, axis)


def _group_avg(n, g):
    return (_iota((n, n), 0) // g == _iota((n, n), 1) // g).astype(F32) * (1.0 / g)


def _silu(x):
    return x * jax.nn.sigmoid(x)


def _layer_norm(z, g, b):
    mu = jnp.mean(z, axis=-1, keepdims=True)
    zc = z - mu
    var = jnp.mean(zc * zc, axis=-1, keepdims=True)
    return zc * lax.rsqrt(var + EPS) * g + b


def _rms(x, g):
    return x * lax.rsqrt(jnp.mean(x * x, axis=-1, keepdims=True) + EPS) * g


def mm(name, a, wt, out_dtype, tn, tr, mod=None, sel=None):
    n, k = a.shape
    nw = wt.shape[0]

    def body(*refs):
        if mod is not None:
            a_ref, m_ref, w_ref, o_ref = refs
            m = m_ref[0]
            av = a_ref[...] * (1.0 + m[sel[0]:sel[0] + 1]) + m[sel[1]:sel[1] + 1]
        else:
            a_ref, w_ref, o_ref = refs
            av = a_ref[...]
        o_ref[...] = lax.dot_general(av.astype(BF16), w_ref[...], (((1,), (1,)), ((), ())),
                                     preferred_element_type=F32).astype(o_ref.dtype)

    in_specs = [pl.BlockSpec((tr, k), lambda c, j: (j, 0))]
    args = [a]
    if mod is not None:
        in_specs.append(pl.BlockSpec((1, 8, k), lambda c, j: (j, 0, 0)))
        args.append(mod)
    in_specs.append(pl.BlockSpec((tn, k), lambda c, j: (c, 0)))
    args.append(wt)
    return pl.pallas_call(
        body, name=name, grid=(nw // tn, n // tr), in_specs=in_specs,
        out_specs=pl.BlockSpec((tr, tn), lambda c, j: (j, c)),
        out_shape=jax.ShapeDtypeStruct((n, nw), out_dtype), compiler_params=_cparams(("parallel", "arbitrary")),
    )(*args)


def mm_tn(name, dc, a, tn, tr, mod=None, sel=None):
    n, k = a.shape
    nw = dc.shape[1]

    def body(*refs):
        if mod is not None:
            d_ref, a_ref, m_ref, o_ref = refs
            m = m_ref[0]
            av = a_ref[...] * (1.0 + m[sel[0]:sel[0] + 1]) + m[sel[1]:sel[1] + 1]
        else:
            d_ref, a_ref, o_ref = refs
            av = a_ref[...]

        @pl.when(pl.program_id(1) == 0)
        def _():
            o_ref[...] = jnp.zeros_like(o_ref)

        o_ref[...] += lax.dot_general(d_ref[...].astype(BF16), av.astype(BF16), (((0,), (0,)), ((), ())),
                                      preferred_element_type=F32)

    in_specs = [pl.BlockSpec((tr, tn), lambda c, j: (j, c)), pl.BlockSpec((tr, k), lambda c, j: (j, 0))]
    args = [dc, a]
    if mod is not None:
        in_specs.append(pl.BlockSpec((1, 8, k), lambda c, j: (j, 0, 0)))
        args.append(mod)
    return pl.pallas_call(
        body, name=name, grid=(nw // tn, n // tr), in_specs=in_specs,
        out_specs=pl.BlockSpec((tn, k), lambda c, j: (c, 0)),
        out_shape=jax.ShapeDtypeStruct((nw, k), F32), compiler_params=_cparams(("parallel", "arbitrary")),
    )(*args)


def mm_modbwd(name, dc, wt, x, mod, add, sel, tr):
    n, k = dc.shape
    dm = wt.shape[1]

    def body(dc_ref, wt_ref, x_ref, m_ref, add_ref, dx_ref, dm_ref):
        dh = jnp.dot(dc_ref[...].astype(BF16), wt_ref[...], preferred_element_type=F32)
        m = m_ref[0]
        dx_ref[...] = add_ref[...] + dh * (1.0 + m[sel[0]:sel[0] + 1])
        dsc = jnp.sum(dh * x_ref[...], axis=0, keepdims=True)
        dsh = jnp.sum(dh, axis=0, keepdims=True)
        rows = _iota((8, dm), 0)
        dm_ref[0] = jnp.where(rows == sel[0], dsc, 0.0) + jnp.where(rows == sel[1], dsh, 0.0)

    return pl.pallas_call(
        body, name=name, grid=(n // tr,),
        in_specs=[pl.BlockSpec((tr, k), lambda j: (j, 0)), pl.BlockSpec((k, dm), lambda j: (0, 0)),
                  pl.BlockSpec((tr, dm), lambda j: (j, 0)), pl.BlockSpec((1, 8, dm), lambda j: (j, 0, 0)),
                  pl.BlockSpec((tr, dm), lambda j: (j, 0))],
        out_specs=[pl.BlockSpec((tr, dm), lambda j: (j, 0)), pl.BlockSpec((1, 8, dm), lambda j: (j, 0, 0))],
        out_shape=[jax.ShapeDtypeStruct((n, dm), F32), jax.ShapeDtypeStruct((n // tr, 8, dm), F32)],
        compiler_params=_cparams(("arbitrary",)),
    )(dc, wt, x, mod, add)


def _spec(kind, arr, width, cb, tr, tps):
    if kind == "row":
        return pl.BlockSpec((tr, width), lambda j: (j, cb))
    if kind == "pos":
        return pl.BlockSpec((tr, width), lambda j: (j % tps, cb))
    if kind == "tile":
        return pl.BlockSpec((1,) + arr.shape[1:], lambda j: (j, 0, 0))
    if kind == "par":
        return pl.BlockSpec(arr.shape, lambda j: (0, 0))
    raise ValueError(kind)


def _load(kind, ref):
    v = ref[0] if kind == "tile" else ref[...]
    return v.astype(F32)


def tile_fwd(name, fn, ins, outs, n, tr, tps):
    kinds = [i[0] for i in ins]

    def body(*refs):
        vals = [_load(k, r) for k, r in zip(kinds, refs[:len(ins)])]
        res = fn(*vals)
        for r, o in zip(refs[len(ins):], res):
            r[...] = o.astype(r.dtype)

    return pl.pallas_call(
        body, name=name, grid=(n // tr,),
        in_specs=[_spec(k, a, w, cb, tr, tps) for k, a, w, cb in ins],
        out_specs=[pl.BlockSpec((tr, w), lambda j: (j, 0)) for w, _ in outs],
        out_shape=[jax.ShapeDtypeStruct((n, w), dt) for w, dt in outs],
        compiler_params=_cparams(("arbitrary",)),
    )(*[i[1] for i in ins])


def tile_bwd(name, fn, ins, cots, want, n, tr, tps):
    kinds = [i[0] for i in ins]
    widx = [w[0] for w in want]
    ni, nc = len(ins), len(cots)

    def body(*refs):
        vals = [_load(k, r) for k, r in zip(kinds, refs[:ni])]
        cvals = tuple(r[...].astype(F32) for r in refs[ni:ni + nc])

        def f(*dv):
            full = list(vals)
            for i, v in zip(widx, dv):
                full[i] = v
            return tuple(fn(*full))

        _, vjp = jax.vjp(f, *[vals[i] for i in widx])
        grads = vjp(cvals)
        first = pl.program_id(0) == 0
        for r, g, i in zip(refs[ni + nc:], grads, widx):
            if kinds[i] == "par":
                @pl.when(first)
                def _(r=r):
                    r[...] = jnp.zeros_like(r)
                r[...] += g
            elif kinds[i] == "tile":
                r[0] = g.astype(r.dtype)
            else:
                r[...] = g.astype(r.dtype)

    out_specs, out_shape = [], []
    for i, dt in want:
        k, a, w, cb = ins[i]
        if k == "par":
            out_specs.append(pl.BlockSpec(a.shape, lambda j: (0, 0)))
            out_shape.append(jax.ShapeDtypeStruct(a.shape, F32))
        elif k == "tile":
            out_specs.append(pl.BlockSpec((1,) + a.shape[1:], lambda j: (j, 0, 0)))
            out_shape.append(jax.ShapeDtypeStruct(a.shape, F32))
        else:
            out_specs.append(pl.BlockSpec((tr, w), lambda j: (j, 0)))
            out_shape.append(jax.ShapeDtypeStruct((n, w), dt))
    return pl.pallas_call(
        body, name=name, grid=(n // tr,),
        in_specs=[_spec(k, a, w, cb, tr, tps) for k, a, w, cb in ins]
        + [pl.BlockSpec((tr, c.shape[1]), lambda j: (j, 0)) for c in cots],
        out_specs=out_specs, out_shape=out_shape, compiler_params=_cparams(("arbitrary",)),
    )(*[i[1] for i in ins], *cots)


def pre_fn(p_gq, p_rq, p_rk, p_cq, p_ckv, p_misc, rcos, rsin, mcos, msin,
           w2f, w2b, b2f, b2b, retf, retb, qg, kvg, wuq, wuk, wuv, e2):
    tr = p_gq.shape[0]
    gq = p_gq * (32 ** -0.5)
    af = jax.nn.log_sigmoid(hdot(p_misc, w2f) + b2f) * (1.0 / GLA_TAU)
    ab = jax.nn.log_sigmoid(hdot(p_misc, w2b) + b2b) * (1.0 / GLA_TAU)
    arf = jnp.zeros((tr, 128), F32) + retf
    arb = jnp.zeros((tr, 128), F32) + retb
    rq = p_rq * rcos + _swap16(p_rq) * rsin
    rks = p_rk * (32 ** -0.5)
    rk = rks * rcos + _swap16(rks) * rsin
    qp = bdot_nt(_rms(p_cq, qg), wuq) * MLA_SCALE
    ckvn = _rms(p_ckv, kvg)
    kp = bdot_nt(ckvn, wuk) + xdot(p_misc, e2)
    mc, ms = jnp.tile(mcos, (1, 8)), jnp.tile(msin, (1, 8))
    v = bdot_nt(ckvn, wuv)
    return gq, af, ab, arf, arb, rq, rk, qp * mc + _swap8(qp) * ms, kp * mc + _swap8(kp) * ms, v


def post_fn(ogf, ogb, orf, orb, om, gg, rg, x, mod, gng, wout, lng, lnb):
    avg = _group_avg(256, 64)
    og = ogf + ogb
    mg = og * lax.rsqrt(xdot(og * og, avg) + EPS) * gng * _silu(gg)
    orr = orf + orb
    oc = orr - xdot(orr, avg)
    mr = oc * lax.rsqrt(xdot(oc * oc, avg) + EPS) * _silu(rg)
    m = jnp.concatenate([mg, mr, om], axis=1)
    y = bdot(m, wout)
    return (_layer_norm(ALPHA * x + mod[2:3] * y, lng, lnb),)


def ln2_fn(x1, f, mod, lng, lnb):
    return (_layer_norm(ALPHA * x1 + mod[5:6] * f, lng, lnb),)


def scan_step(q, k, v, a, st, rev):
    ii, jj = _iota((CHUNK, CHUNK), 0), _iota((CHUNK, CHUNK), 1)
    tri = ((jj >= ii) if rev else (jj <= ii)).astype(F32)
    b = xdot_l(tri, a)
    btot = jnp.sum(a, axis=0, keepdims=True)
    qe = q * jnp.exp(b - btot)
    ke = k * jnp.exp(btot - b)
    lane = _iota((1, 128), 1)
    q4 = jnp.concatenate([qe * (lane // 32 == h).astype(F32) for h in range(4)], axis=0)
    att = bdot_nt(q4, ke)
    att = jnp.where(jnp.concatenate([tri] * 4, axis=0) > 0, att, 0.0)
    r = bdot(att, v)
    col = _iota((1, 256), 1)
    o = bdot_nt(q * jnp.exp(b), st)
    for h in range(4):
        o = o + r[h * CHUNK:(h + 1) * CHUNK] * (col // 64 == h).astype(F32)
    vk = bdot_tn(v, ke)
    bd = (_iota((256, 128), 0) // 64 == _iota((256, 128), 1) // 32).astype(F32)
    return o, st * jnp.exp(btot) + vk * bd


def _chunk_maps(nch, nctx):
    def fwd(s):
        return s

    def bwd(s):
        return jnp.where(s < nctx, nctx - 1 - s, nch - 1 - (s - nctx))
    return fwd, bwd


def _per_sample(arr, nb):
    return arr.reshape(nb, arr.shape[0] // nb, arr.shape[1])


def scan_fwd(name, q, k, v, af, ab, nb, nch, nctx):
    n = af.shape[0]
    fmap, bmap = _chunk_maps(nch, nctx)

    def body(qf, kf, vf, a_f, qb, kb, vb, a_b, of_ref, ob_ref, stf_ref, stb_ref, s_scr):
        @pl.when(pl.program_id(0) == 0)
        def _():
            s_scr[...] = jnp.zeros_like(s_scr)

        for i in range(nb):
            stf_ref[0, i] = s_scr[2 * i]
            stb_ref[0, i] = s_scr[2 * i + 1]
            o, sn = scan_step(qf[i], kf[i], vf[i], a_f[i], s_scr[2 * i], False)
            of_ref[i] = o
            s_scr[2 * i] = sn
            o, sn = scan_step(qb[i], kb[i], vb[i], a_b[i], s_scr[2 * i + 1], True)
            ob_ref[i] = o
            s_scr[2 * i + 1] = sn

    def specs(m):
        return [pl.BlockSpec((nb, CHUNK, w), lambda s, cb=cb: (0, m(s), cb)) for _, w, cb in (q, k, v)] + \
               [pl.BlockSpec((nb, CHUNK, 128), lambda s: (0, m(s), 0))]

    ps = lambda a: _per_sample(a, nb)
    of, ob, stf, stb = pl.pallas_call(
        body, name=name, grid=(nch,), in_specs=specs(fmap) + specs(bmap),
        out_specs=[pl.BlockSpec((nb, CHUNK, 256), lambda s: (0, fmap(s), 0)),
                   pl.BlockSpec((nb, CHUNK, 256), lambda s: (0, bmap(s), 0)),
                   pl.BlockSpec((1, nb, 256, 128), lambda s: (s, 0, 0, 0)),
                   pl.BlockSpec((1, nb, 256, 128), lambda s: (s, 0, 0, 0))],
        out_shape=[jax.ShapeDtypeStruct((nb, n // nb, 256), F32)] * 2
        + [jax.ShapeDtypeStruct((nch, nb, 256, 128), F32)] * 2,
        scratch_shapes=[pltpu.VMEM((2 * nb, 256, 128), F32)], compiler_params=_cparams(("arbitrary",)),
    )(ps(q[0]), ps(k[0]), ps(v[0]), ps(af), ps(q[0]), ps(k[0]), ps(v[0]), ps(ab))
    return of.reshape(n, 256), ob.reshape(n, 256), stf, stb


def scan_bwd(name, q, k, v, af, ab, stf, stb, do, nb, nch, nctx):
    n = af.shape[0]
    fmap0, bmap0 = _chunk_maps(nch, nctx)
    fmap = lambda r: fmap0(nch - 1 - r)
    bmap = lambda r: bmap0(nch - 1 - r)

    def body(qf, kf, vf, a_f, sf, dof, qb, kb, vb, a_b, sb, dob,
             dqf, dkf, dvf, daf, dqb, dkb, dvb, dab, ds_scr):
        @pl.when(pl.program_id(0) == 0)
        def _():
            ds_scr[...] = jnp.zeros_like(ds_scr)

        for i in range(nb):
            for d, (qr, kr, vr, ar, sr, dor, outs) in enumerate(((qf, kf, vf, a_f, sf, dof, (dqf, dkf, dvf, daf)),
                                                                   (qb, kb, vb, a_b, sb, dob, (dqb, dkb, dvb, dab)))):
                _, vjp = jax.vjp(functools.partial(scan_step, rev=bool(d)), qr[i], kr[i], vr[i], ar[i], sr[0, i])
                dq, dk, dv, da, ds = vjp((dor[i], ds_scr[2 * i + d]))
                outs[0][i] = dq
                outs[1][i] = dk
                outs[2][i] = dv
                outs[3][i] = da
                ds_scr[2 * i + d] = ds

    def specs(m):
        return [pl.BlockSpec((nb, CHUNK, w), lambda r, cb=cb: (0, m(r), cb)) for _, w, cb in (q, k, v)] + \
               [pl.BlockSpec((nb, CHUNK, 128), lambda r: (0, m(r), 0)),
                pl.BlockSpec((1, nb, 256, 128), lambda r: (nch - 1 - r, 0, 0, 0)),
                pl.BlockSpec((nb, CHUNK, 256), lambda r: (0, m(r), 0))]

    def ospecs(m):
        return [pl.BlockSpec((nb, CHUNK, w), lambda r: (0, m(r), 0)) for w in (128, 128, 256, 128)]

    ps = lambda a: _per_sample(a, nb)
    oshape = [jax.ShapeDtypeStruct((nb, n // nb, w), F32) for w in (128, 128, 256, 128)]
    outs = pl.pallas_call(
        body, name=name, grid=(nch,), in_specs=specs(fmap) + specs(bmap),
        out_specs=ospecs(fmap) + ospecs(bmap), out_shape=oshape + oshape,
        scratch_shapes=[pltpu.VMEM((2 * nb, 256, 128), F32)], compiler_params=_cparams(("arbitrary",)),
    )(ps(q[0]), ps(k[0]), ps(v[0]), ps(af), stf, ps(do), ps(q[0]), ps(k[0]), ps(v[0]), ps(ab), stb, ps(do))
    return [o.reshape(n, o.shape[2]) for o in outs]


def mla_fwd(name, qa, ka, va, nb, tps, tr, nctx_rows):
    n = qa.shape[0]
    t = tps * tr

    def body(q_ref, k_ref, v_ref, o_ref, lse_ref):
        def attend(nk):
            vv = v_ref[0:nk, :]
            first = _iota(vv.shape, 1) < 64
            one = jnp.ones_like(vv)
            res, lses = [], []
            for h in range(2):
                s = lax.dot_general(q_ref[:, h * 128:(h + 1) * 128], k_ref[0:nk, h * 128:(h + 1) * 128],
                                    (((1,), (1,)), ((), ())), preferred_element_type=F32)
                m = jnp.max(s, axis=-1, keepdims=True)
                e = jnp.exp((s - m).astype(BF16))
                r = jnp.dot(e, jnp.where(first == (h == 0), vv, one), preferred_element_type=F32)
                l = r[:, 64:65] if h == 0 else r[:, 0:1]
                res.append(r / l)
                lses.append(m + jnp.log(l))
            lane = _iota((tr, 128), 1) < 64
            o_ref[...] = jnp.where(lane, res[0], res[1])
            lse_ref[...] = jnp.where(lane, lses[0], lses[1])

        @pl.when(pl.program_id(2) == 0)
        def _():
            attend(nctx_rows)

        @pl.when(pl.program_id(2) > 0)
        def _():
            attend(t)

    return pl.pallas_call(
        body, name=name, grid=(nb, 4, tps),
        in_specs=[pl.BlockSpec((tr, 256), lambda b, h, j: (b * tps + j, h)), pl.BlockSpec((t, 256), lambda b, h, j: (b, h)),
                  pl.BlockSpec((t, 128), lambda b, h, j: (b, h))],
        out_specs=[pl.BlockSpec((tr, 128), lambda b, h, j: (b * tps + j, h))] * 2,
        out_shape=[jax.ShapeDtypeStruct((n, 512), F32)] * 2,
        compiler_params=_cparams(("parallel", "parallel", "arbitrary")),
    )(qa, ka, va)


def mla_bwd(name, qa, ka, va, o, lse, do, nb, tps, tr, nctx_rows):
    n = qa.shape[0]
    t = tps * tr

    def body(q_ref, k_ref, v_ref, o_ref, lse_ref, do_ref, dq_ref, dk_ref, dv_ref, dkt, dvt):
        @pl.when(pl.program_id(2) == 0)
        def _():
            dkt[...] = jnp.zeros_like(dkt)
            dvt[...] = jnp.zeros_like(dvt)

        def attend(nk):
            dov = do_ref[...]
            oo = dov * o_ref[...]
            dob = dov.astype(BF16)
            first = _iota(dob.shape, 1) < 64
            dqs = []
            for h in range(2):
                hs = slice(h * 128, (h + 1) * 128)
                qh, kh = q_ref[:, hs], k_ref[0:nk, hs]
                mine = first == (h == 0)
                delta = jnp.sum(jnp.where(mine, oo, 0.0), axis=-1, keepdims=True)
                doh = jnp.where(mine, dob, jnp.zeros_like(dob))
                s = lax.dot_general(qh, kh, (((1,), (1,)), ((), ())), preferred_element_type=F32)
                p = jnp.exp((s - lse_ref[:, h * 64:h * 64 + 1]).astype(BF16))
                dp = lax.dot_general(doh, v_ref[0:nk, :], (((1,), (1,)), ((), ())), preferred_element_type=F32)
                ds = p * (dp - delta).astype(BF16)
                dqs.append(jnp.dot(ds, kh, preferred_element_type=F32))
                dkt[hs, 0:nk] += lax.dot_general(qh, ds, (((0,), (0,)), ((), ())), preferred_element_type=F32)
                dvt[:, 0:nk] += lax.dot_general(doh, p, (((0,), (0,)), ((), ())), preferred_element_type=F32)
            dq_ref[...] = jnp.concatenate(dqs, axis=1)

        @pl.when(pl.program_id(2) == 0)
        def _():
            attend(nctx_rows)

        @pl.when(pl.program_id(2) > 0)
        def _():
            attend(t)

        @pl.when(pl.program_id(2) == tps - 1)
        def _():
            dk_ref[...] = dkt[...].T
            dv_ref[...] = dvt[...].T

    qtile = pl.BlockSpec((tr, 128), lambda b, h, j: (b * tps + j, h))
    return pl.pallas_call(
        body, name=name, grid=(nb, 4, tps),
        in_specs=[pl.BlockSpec((tr, 256), lambda b, h, j: (b * tps + j, h)), pl.BlockSpec((t, 256), lambda b, h, j: (b, h)),
                  pl.BlockSpec((t, 128), lambda b, h, j: (b, h)), qtile, qtile, qtile],
        out_specs=[pl.BlockSpec((tr, 256), lambda b, h, j: (b * tps + j, h)), pl.BlockSpec((t, 256), lambda b, h, j: (b, h)),
                   pl.BlockSpec((t, 128), lambda b, h, j: (b, h))],
        out_shape=[jax.ShapeDtypeStruct((n, 1024), F32), jax.ShapeDtypeStruct((n, 1024), F32),
                   jax.ShapeDtypeStruct((n, 512), F32)],
        scratch_shapes=[pltpu.VMEM((256, t), F32), pltpu.VMEM((128, t), F32)],
        compiler_params=_cparams(("parallel", "parallel", "arbitrary")),
    )(qa, ka, va, o, lse, do)


HALO = 16


def _halo_specs(tr, width, tps, nt):
    r = tr // HALO
    return [pl.BlockSpec((tr, width), lambda j, c: (j, c)),
            pl.BlockSpec((HALO, width), lambda j, c: (jnp.maximum(j * r - 1, 0), c)),
            pl.BlockSpec((HALO, width), lambda j, c: (jnp.minimum((j + 1) * r, nt * r - 1), c))]


def _shifted(u, prev, nxt, j, tps):
    tr = u.shape[0]
    t = j % tps
    has_prev = (t >= 2).astype(F32)
    has_next = jnp.logical_and(t >= 1, t <= tps - 2).astype(F32)
    rows = _iota(u.shape, 0)
    dn = jnp.where(rows == 0, prev[HALO - 1:HALO] * has_prev, pltpu.roll(u, 1, 0))
    up = jnp.where(rows == tr - 1, nxt[0:1] * has_next, pltpu.roll(u, tr - 1, 0))
    return dn, up


def _ffn_act(ucv):
    return _silu(ucv[:, :FF_CHUNK]) * ucv[:, FF_CHUNK:]


def ffn2_fwd(name, u, cw, cb, wd, x1, mod, lng, lnb, tr, tps):
    n = u.shape[0]
    nt = n // tr
    w2 = 2 * FF_CHUNK

    def body(u_ref, up_ref, un_ref, cw_ref, cb_ref, wd_ref, x1_ref, m_ref, g_ref, b_ref, f_ref, x2_ref, acc):
        j, c = pl.program_id(0), pl.program_id(1)
        uu = u_ref[...].astype(F32)
        dn, up = _shifted(uu, up_ref[...].astype(F32), un_ref[...].astype(F32), j, tps)
        cwv = cw_ref[...]
        ucv = cwv[0:1] * dn + cwv[1:2] * uu + cwv[2:3] * up + cb_ref[...]
        part = bdot(_ffn_act(ucv), wd_ref[...])

        @pl.when(c == 0)
        def _():
            acc[...] = part

        @pl.when(c == 1)
        def _():
            f = acc[...] + part
            f_ref[...] = f
            x2_ref[...] = ln2_fn(x1_ref[...], f, m_ref[0], g_ref[...], b_ref[...])[0]

    return pl.pallas_call(
        body, name=name, grid=(nt, 2),
        in_specs=_halo_specs(tr, w2, tps, nt) + [
            pl.BlockSpec((8, w2), lambda j, c: (0, c)), pl.BlockSpec((1, w2), lambda j, c: (0, c)),
            pl.BlockSpec((FF_CHUNK, D), lambda j, c: (c, 0)), pl.BlockSpec((tr, D), lambda j, c: (j, 0)),
            pl.BlockSpec((1, 8, D), lambda j, c: (j, 0, 0)), pl.BlockSpec((1, D), lambda j, c: (0, 0)),
            pl.BlockSpec((1, D), lambda j, c: (0, 0))],
        out_specs=[pl.BlockSpec((tr, D), lambda j, c: (j, 0)), pl.BlockSpec((tr, D), lambda j, c: (j, 0))],
        out_shape=[jax.ShapeDtypeStruct((n, D), F32)] * 2, scratch_shapes=[pltpu.VMEM((tr, D), F32)],
        compiler_params=_cparams(("arbitrary", "arbitrary")),
    )(u, u, u, cw, cb, wd, x1, mod, lng, lnb)


def ffn2_bwd(name, u, cw, cb, wd, df, tr, tps):
    n = u.shape[0]
    nt = n // tr
    w2 = 2 * FF_CHUNK

    def body(u_ref, up_ref, un_ref, cw_ref, cb_ref, wd_ref, df_ref, ducv_ref, dwd_ref):
        c, j = pl.program_id(0), pl.program_id(1)
        uu = u_ref[...].astype(F32)
        dn, up = _shifted(uu, up_ref[...].astype(F32), un_ref[...].astype(F32), j, tps)
        cwv = cw_ref[...]
        ucv = cwv[0:1] * dn + cwv[1:2] * uu + cwv[2:3] * up + cb_ref[...]
        a, g = ucv[:, :FF_CHUNK], ucv[:, FF_CHUNK:]
        sg = jax.nn.sigmoid(a)
        sa = a * sg
        dfb = df_ref[...].astype(BF16)
        dact = lax.dot_general(dfb, wd_ref[...], (((1,), (1,)), ((), ())), preferred_element_type=F32)
        ducv_ref[:, :FF_CHUNK] = (dact * g * (sg + sa * (1.0 - sg))).astype(ducv_ref.dtype)
        ducv_ref[:, FF_CHUNK:] = (dact * sa).astype(ducv_ref.dtype)
        dwd = lax.dot_general((sa * g).astype(BF16), dfb, (((0,), (0,)), ((), ())), preferred_element_type=F32)

        @pl.when(j == 0)
        def _():
            dwd_ref[...] = jnp.zeros_like(dwd_ref)

        dwd_ref[...] += dwd

    hs = _halo_specs(tr, w2, tps, nt)
    swap = lambda spec: pl.BlockSpec(spec.block_shape, lambda c, j, f=spec.index_map: f(j, c))
    return pl.pallas_call(
        body, name=name, grid=(2, nt),
        in_specs=[swap(s) for s in hs] + [
            pl.BlockSpec((8, w2), lambda c, j: (0, c)), pl.BlockSpec((1, w2), lambda c, j: (0, c)),
            pl.BlockSpec((FF_CHUNK, D), lambda c, j: (c, 0)), pl.BlockSpec((tr, D), lambda c, j: (j, 0))],
        out_specs=[pl.BlockSpec((tr, w2), lambda c, j: (j, c)), pl.BlockSpec((FF_CHUNK, D), lambda c, j: (c, 0))],
        out_shape=[jax.ShapeDtypeStruct((n, 2 * w2), BF16), jax.ShapeDtypeStruct((D_FF, D), F32)],
        compiler_params=_cparams(("parallel", "arbitrary")),
    )(u, u, u, cw, cb, wd, df)


def conv_bwd(name, ducv, u, cw, tr, tps):
    n = u.shape[0]
    nt = n // tr
    w2 = 2 * FF_CHUNK

    def body(g_ref, gp_ref, gn_ref, u_ref, up_ref, un_ref, cw_ref, du_ref, dcw_ref, dcb_ref):
        c, j = pl.program_id(0), pl.program_id(1)
        g = g_ref[...].astype(F32)
        gdn, gup = _shifted(g, gp_ref[...].astype(F32), gn_ref[...].astype(F32), j, tps)
        uu = u_ref[...].astype(F32)
        udn, uup = _shifted(uu, up_ref[...].astype(F32), un_ref[...].astype(F32), j, tps)
        cwv = cw_ref[...]
        du_ref[...] = (cwv[0:1] * gup + cwv[1:2] * g + cwv[2:3] * gdn).astype(du_ref.dtype)
        rows = _iota((8, w2), 0)
        s = lambda z: jnp.sum(z, axis=0, keepdims=True)
        dcw = (jnp.where(rows == 0, s(g * udn), 0.0) + jnp.where(rows == 1, s(g * uu), 0.0)
               + jnp.where(rows == 2, s(g * uup), 0.0))

        @pl.when(j == 0)
        def _():
            dcw_ref[...] = jnp.zeros_like(dcw_ref)
            dcb_ref[...] = jnp.zeros_like(dcb_ref)

        dcw_ref[...] += dcw
        dcb_ref[...] += s(g)

    hs = _halo_specs(tr, w2, tps, nt)
    swap = lambda spec: pl.BlockSpec(spec.block_shape, lambda c, j, f=spec.index_map: f(j, c))
    return pl.pallas_call(
        body, name=name, grid=(2, nt),
        in_specs=[swap(s) for s in hs] * 2 + [pl.BlockSpec((8, w2), lambda c, j: (0, c))],
        out_specs=[pl.BlockSpec((tr, w2), lambda c, j: (j, c)), pl.BlockSpec((8, w2), lambda c, j: (0, c)),
                   pl.BlockSpec((1, w2), lambda c, j: (0, c))],
        out_shape=[jax.ShapeDtypeStruct((n, 2 * w2), BF16), jax.ShapeDtypeStruct((8, 2 * w2), F32),
                   jax.ShapeDtypeStruct((1, 2 * w2), F32)],
        compiler_params=_cparams(("parallel", "arbitrary")),
    )(ducv, ducv, ducv, u, u, u, cw)


def loss_head(name, xf, target, nb, tps, tr):
    n = xf.shape[0]

    def body(x_ref, t_ref, dy_ref, l_ref):
        lat = (pl.program_id(0) % tps > 0).astype(F32)
        err = (x_ref[...] - t_ref[...]) * lat
        dy_ref[...] = err * (1.0 / D)
        l_ref[...] = jnp.zeros_like(l_ref) + 0.5 * jnp.sum(err * err) * (1.0 / D)

    def tmap(j):
        return ((j // tps) * (tps - 1) + jnp.maximum(j % tps - 1, 0), 0)

    return pl.pallas_call(
        body, name=name, grid=(n // tr,),
        in_specs=[pl.BlockSpec((tr, D), lambda j: (j, 0)), pl.BlockSpec((tr, D), tmap)],
        out_specs=[pl.BlockSpec((tr, D), lambda j: (j, 0)), pl.BlockSpec((1, 8, 128), lambda j: (j, 0, 0))],
        out_shape=[jax.ShapeDtypeStruct((n, D), F32), jax.ShapeDtypeStruct((n // tr, 8, 128), F32)],
        compiler_params=_cparams(("arbitrary",)),
    )(xf, target)


ADAM_MAX_ROWS = 512


def adamw(name, w, m, v, g8):
    r, c = w.shape
    k = g8.shape[0]
    rows = max(b for b in range(8, ADAM_MAX_ROWS + 1, 8) if r % b == 0)
    bc1 = 1.0 - ADAM_B1 ** ADAM_STEP
    bc2 = 1.0 - ADAM_B2 ** ADAM_STEP

    def body(w_ref, m_ref, v_ref, g_ref, go_ref, d_ref, mo_ref, vo_ref):
        g = g_ref[0].astype(F32)
        for i in range(1, k):
            g = g + g_ref[i].astype(F32)
        mn = ADAM_B1 * m_ref[...] + (1.0 - ADAM_B1) * g
        vn = ADAM_B2 * v_ref[...] + (1.0 - ADAM_B2) * (g * g)
        go_ref[...] = g
        mo_ref[...] = mn
        vo_ref[...] = vn
        d_ref[...] = -ADAM_LR * ((mn / bc1) / (jnp.sqrt(vn / bc2) + ADAM_EPS) + ADAM_WD * w_ref[...])

    blk = pl.BlockSpec((rows, c), lambda i: (i, 0))
    return pl.pallas_call(
        body, name=name, grid=(r // rows,),
        in_specs=[blk, blk, blk, pl.BlockSpec((k, rows, c), lambda i: (0, i, 0))],
        out_specs=[blk] * 4, out_shape=[jax.ShapeDtypeStruct((r, c), F32)] * 4,
        compiler_params=_cparams(("parallel",)),
    )(w, m, v, g8)


def ada_fwd(name, s, aw, ab):
    nl, _, cw = aw.shape

    def body(s_ref, w_ref, b_ref, o_ref):
        o_ref[0] = hdot(s_ref[...], w_ref[0]) + b_ref[0]

    return pl.pallas_call(
        body, name=name, grid=(nl,),
        in_specs=[pl.BlockSpec(s.shape, lambda l: (0, 0)), pl.BlockSpec((1, D, cw), lambda l: (l, 0, 0)),
                  pl.BlockSpec((1, 1, cw), lambda l: (l, 0, 0))],
        out_specs=pl.BlockSpec((1, s.shape[0], cw), lambda l: (l, 0, 0)),
        out_shape=jax.ShapeDtypeStruct((nl, s.shape[0], cw), F32), compiler_params=_cparams(("arbitrary",)),
    )(s, aw, ab)


def ada_bwd(name, s, aw, dmod):
    nl, _, cw = aw.shape

    def body(s_ref, w_ref, d_ref, dw_ref, ds_ref):
        dw_ref[0] = lax.dot_general(s_ref[...], d_ref[0], (((0,), (0,)), ((), ())), precision=HI,
                                    preferred_element_type=F32)
        ds_ref[0] = lax.dot_general(d_ref[0], w_ref[0], (((1,), (1,)), ((), ())), precision=HI,
                                    preferred_element_type=F32)

    return pl.pallas_call(
        body, name=name, grid=(nl,),
        in_specs=[pl.BlockSpec(s.shape, lambda l: (0, 0)), pl.BlockSpec((1, D, cw), lambda l: (l, 0, 0)),
                  pl.BlockSpec((1, s.shape[0], cw), lambda l: (l, 0, 0))],
        out_specs=[pl.BlockSpec((1, D, cw), lambda l: (l, 0, 0)), pl.BlockSpec((1, s.shape[0], D), lambda l: (l, 0, 0))],
        out_shape=[jax.ShapeDtypeStruct((nl, D, cw), F32), jax.ShapeDtypeStruct((nl, s.shape[0], D), F32)],
        compiler_params=_cparams(("arbitrary",)),
    )(s, aw, dmod)


def _place():
    return lax.axis_index("x"), lax.axis_index("y"), lax.axis_index("c")


def all_gather(name, x, in_vmem):
    r, c = x.shape

    def body(x_ref, out_ref, send_sems, recv_sems, local_sem):
        px, py, pc = _place()
        me, sibling = (px, py, pc), (px, py, 1 - pc)
        chips = [(1 - px, py), (px, 1 - py), (1 - px, 1 - py)]

        def rows(qx, qy, qc):
            return out_ref.at[pl.ds((4 * qx + 2 * qy + qc) * r, r), :]

        def copy(k, block, to, src=None):
            return pltpu.make_async_remote_copy(
                src_ref=rows(*block) if src is None else src, dst_ref=rows(*block),
                send_sem=send_sems.at[k], recv_sem=recv_sems.at[k], device_id=to, device_id_type=MESH)

        mine = pltpu.make_async_copy(x_ref, rows(*me), local_sem)
        mine.start()
        first = [copy(0, me, sibling, src=x_ref)]
        first += [copy(1 + j, me, (*chip, pc), src=x_ref) for j, chip in enumerate(chips)]
        for cp in first:
            cp.start()
        passed = [copy(4 + j, (*chip, pc), sibling) for j, chip in enumerate(chips)]
        for j, chip in enumerate(chips):
            copy(1 + j, (*chip, pc), me).wait_recv()
            passed[j].start()
        copy(0, sibling, me).wait_recv()
        for j, chip in enumerate(chips):
            copy(4 + j, (*chip, 1 - pc), me).wait_recv()
        for cp in first + passed:
            cp.wait_send()
        mine.wait()

    space = pltpu.VMEM if in_vmem else pl.ANY
    return pl.pallas_call(
        body, name=name, out_shape=jax.ShapeDtypeStruct((N_DEV * r, c), x.dtype),
        in_specs=[pl.BlockSpec(memory_space=space)], out_specs=pl.BlockSpec(memory_space=space),
        scratch_shapes=[pltpu.SemaphoreType.DMA((7,)), pltpu.SemaphoreType.DMA((7,)), pltpu.SemaphoreType.DMA],
        compiler_params=pltpu.CompilerParams(vmem_limit_bytes=VMEM_LIMIT_BYTES),
    )(x)


_HBM = pl.BlockSpec(memory_space=pltpu.HBM)
_SEM = pl.BlockSpec(memory_space=pltpu.SEMAPHORE)
_EFFECT = pltpu.SideEffectType.DATAFLOW_SIDE_EFFECTING


def _partner(k):
    px, py, pc = _place()
    q = (px ^ (k >> 2 & 1), py ^ (k >> 1 & 1), pc ^ (k & 1))
    return q, 4 * q[0] + 2 * q[1] + q[2]


def xchg_start(name, x, per_peer):
    r, c = x.shape[-2:]

    def body(x_ref, land_ref, send_sems, recv_sems, x_thru, land_thru, token):
        px, py, pc = _place()
        my = 4 * px + 2 * py + pc
        for k in range(1, N_DEV):
            q, qi = _partner(k)
            pltpu.make_async_remote_copy(
                src_ref=x_ref.at[qi] if per_peer else x_ref, dst_ref=land_ref.at[my],
                send_sem=send_sems.at[k - 1], recv_sem=recv_sems.at[k - 1], device_id=q, device_id_type=MESH).start()
        token[...] = jnp.zeros_like(token)

    land = lax.empty((N_DEV, r, c), x.dtype)
    return pl.pallas_call(
        body, name=name,
        out_shape=(pltpu.SemaphoreType.DMA((N_DEV - 1,)), pltpu.SemaphoreType.DMA((N_DEV - 1,)),
                   pltpu.HBM(x.shape, x.dtype), pltpu.HBM(land.shape, land.dtype), jax.ShapeDtypeStruct((8, 128), F32)),
        in_specs=(_HBM, _HBM), out_specs=(_SEM, _SEM, _HBM, _HBM, pl.BlockSpec(memory_space=pltpu.VMEM)),
        input_output_aliases={0: 2, 1: 3}, compiler_params=pltpu.CompilerParams(has_side_effects=_EFFECT),
    )(pltpu.with_memory_space_constraint(x, pltpu.HBM), pltpu.with_memory_space_constraint(land, pltpu.HBM))


def xchg_wait(name, send_sems, recv_sems, x_thru, land_thru, after, per_peer):
    def body(x_ref, land_ref, send_sems, recv_sems, after_ref, x_out, land_out):
        for k in range(1, N_DEV):
            q, qi = _partner(k)
            cp = pltpu.make_async_remote_copy(
                src_ref=x_ref.at[qi] if per_peer else x_ref, dst_ref=land_ref.at[qi],
                send_sem=send_sems.at[k - 1], recv_sem=recv_sems.at[k - 1], device_id=q, device_id_type=MESH)
            cp.wait_send()
            cp.wait_recv()

    return pl.pallas_call(
        body, name=name,
        out_shape=(pltpu.HBM(x_thru.shape, x_thru.dtype), pltpu.HBM(land_thru.shape, land_thru.dtype)),
        in_specs=(_HBM, _HBM, _SEM, _SEM, pl.BlockSpec(memory_space=pl.ANY)), out_specs=(_HBM, _HBM),
        input_output_aliases={0: 0, 1: 1}, compiler_params=pltpu.CompilerParams(has_side_effects=_EFFECT),
    )(x_thru, land_thru, send_sems, recv_sems, after)


def _tables(seq, nctx_rows):
    f32 = np.float32
    pos = np.arange(seq, dtype=f32)
    ret_inv = (1.0 / (ROPE_BASE ** np.linspace(0.0, 1.0, 16, dtype=f32))).astype(f32)
    ang = pos[:, None] * ret_inv
    rc, rs = np.cos(ang).astype(f32), np.sin(ang).astype(f32)
    rcos = np.tile(np.concatenate([rc, rc], 1), (1, 4))
    rsin = np.tile(np.concatenate([-rs, rs], 1), (1, 4))
    rows = np.repeat(np.arange(seq // 64, dtype=f32), 64)
    cols = np.tile(np.arange(64, dtype=f32), seq // 64)
    ax_inv = (ROPE_BASE ** (-np.arange(8, dtype=f32) / 8)).astype(f32)
    ra, ca = rows[:, None] * ax_inv, cols[:, None] * ax_inv
    one, zero = np.ones((seq, 64), f32), np.zeros((seq, 64), f32)
    mcos = np.concatenate([one, np.cos(ra), np.cos(ra), np.cos(ca), np.cos(ca), one[:, :32]], 1)
    msin = np.concatenate([zero, -np.sin(ra), np.sin(ra), -np.sin(ca), np.sin(ca), zero[:, :32]], 1)
    ident = lambda t, v: np.concatenate([np.full((nctx_rows, 128), v, f32), t.astype(f32)], 0)
    return [jnp.asarray(ident(rcos, 1.0)), jnp.asarray(ident(rsin, 0.0)),
            jnp.asarray(ident(mcos, 1.0)), jnp.asarray(ident(msin, 0.0))]


def _prep_layer(w, l):
    z = lambda *s: jnp.zeros(s, F32)
    p = {}
    win = _rows(w["w_in_t"][l], W_IN_SEGS)
    p["w_in_t"] = jnp.concatenate([win, jnp.zeros((P_PAD - D_IN, D), win.dtype)], axis=0)
    p["w_up_t"] = _rows(w["ffn_up_t"][l], FF_SEGS)
    p["w_down"] = w["ffn_down"][l]
    p["w_out"] = w["w_out"][l]
    p["wuq"] = _pad_heads(w["mla_w_uq_t"][l], 96)
    p["wuk"] = _pad_heads(w["mla_w_uk_t"][l], 64)
    p["wuv"] = w["mla_w_uv_t"][l]
    gw = w["gla_gate_w"][l]
    p["w2f"] = z(128, 128).at[0:16].set(gw[0])
    p["w2b"] = z(128, 128).at[16:32].set(gw[1])
    p["b2f"], p["b2b"] = w["gla_gate_b"][l][0:1], w["gla_gate_b"][l][1:2]
    lg = jax.nn.log_sigmoid(w["ret_decay"][l])
    p["retf"], p["retb"] = jnp.repeat(lg[0], 32)[None], jnp.repeat(lg[1], 32)[None]
    p["qg"], p["kvg"] = w["mla_q_norm_g"][l][None], w["mla_kv_norm_g"][l][None]
    p["gng"] = jnp.tile(w["gla_norm_g"][l], 4)[None]
    p["ln1g"], p["ln1b"] = w["ln1_g"][l][None], w["ln1_b"][l][None]
    p["ln2g"], p["ln2b"] = w["ln2_g"][l][None], w["ln2_b"][l][None]
    p["cw"] = jnp.concatenate([_cols(w["ffn_conv_w"][l], FF_SEGS), z(5, 2 * D_FF)], axis=0)
    p["cb"] = _cols(w["ffn_conv_b"][l], FF_SEGS)[None]
    e2 = np.zeros((128, 1024), np.float32)
    for h in range(8):
        e2[32 + np.arange(32), h * 128 + 64 + np.arange(32)] = 1.0
    p["e2"] = jnp.asarray(e2)
    return p


def _pre_ins(pa, tabs, p):
    row = lambda w, cb: ("row", pa, w, cb)
    return [row(128, 0), row(128, 6), row(128, 7), row(256, 6), row(128, 14), row(128, 15)] + \
           [("pos", t, 128, 0) for t in tabs] + \
           [("par", p[k], 0, 0) for k in ("w2f", "w2b", "b2f", "b2b", "retf", "retb", "qg", "kvg", "wuq", "wuk", "wuv", "e2")]


_PRE_OUTS = [(128, F32)] * 7 + [(1024, BF16), (1024, BF16), (512, BF16)]
_PRE_WANT = [(i, F32) for i in range(6)] + [(i, F32) for i in range(10, 21)]


def _post_ins(ogf, ogb, orf, orb, om, pa, x, mod, p):
    return [("row", ogf, 256, 0), ("row", ogb, 256, 0), ("row", orf, 256, 0), ("row", orb, 256, 0),
            ("row", om, 512, 0), ("row", pa, 256, 2), ("row", pa, 256, 5), ("row", x, D, 0), ("tile", mod, 0, 0),
            ("par", p["gng"], 0, 0), ("par", p["w_out"], 0, 0), ("par", p["ln1g"], 0, 0), ("par", p["ln1b"], 0, 0)]


def layer_fwd(l, x, mod, p, tabs, dims):
    nb, tps, tr, nch, nctx = dims
    n = x.shape[0]
    pa = mm("proj", x, p["w_in_t"], F32, P_PAD, tr, mod=mod, sel=(1, 0))
    gq, af, ab, arf, arb, rq, rk, qa, ka, va = tile_fwd("mix_pre", pre_fn, _pre_ins(pa, tabs, p), _PRE_OUTS, n, tr, tps)
    ogf, ogb, gstf, gstb = scan_fwd("gla_scan", (gq, 128, 0), (pa, 128, 1), (pa, 256, 1), af, ab, nb, nch, nctx)
    orf, orb, rstf, rstb = scan_fwd("ret_scan", (rq, 128, 0), (rk, 128, 0), (pa, 256, 4), arf, arb, nb, nch, nctx)
    om, lse = mla_fwd("mla_attn", qa, ka, va, nb, tps, tr, nctx * CHUNK)
    (x1,) = tile_fwd("mix_post", post_fn, _post_ins(ogf, ogb, orf, orb, om, pa, x, mod, p), [(D, F32)], n, tr, tps)
    u = mm("ffn_up", x1, p["w_up_t"], BF16, 2 * D_FF, tr, mod=mod, sel=(4, 3))
    f, x2 = ffn2_fwd("ffn_down", u, p["cw"], p["cb"], p["w_down"], x1, mod, p["ln2g"], p["ln2b"], tr, tps)
    saved = dict(x=x, pa=pa, gq=gq, af=af, ab=ab, arf=arf, arb=arb, rq=rq, rk=rk, qa=qa, ka=ka, va=va,
                 ogf=ogf, ogb=ogb, gstf=gstf, gstb=gstb, orf=orf, orb=orb, rstf=rstf, rstb=rstb, om=om, lse=lse,
                 x1=x1, u=u, f=f)
    return x2, saved


def layer_bwd(l, dx2, s, mod, p, tabs, dims):
    nb, tps, tr, nch, nctx = dims
    n = dx2.shape[0]
    g = {}
    ln2_ins = [("row", s["x1"], D, 0), ("row", s["f"], D, 0), ("tile", mod, 0, 0),
               ("par", p["ln2g"], 0, 0), ("par", p["ln2b"], 0, 0)]
    dx1a, df, dmod_a, g["ln2g"], g["ln2b"] = tile_bwd(
        "ln2_bwd", ln2_fn, ln2_ins, [dx2], [(0, F32), (1, F32), (2, F32), (3, F32), (4, F32)], n, tr, tps)
    ducv, g["w_down"] = ffn2_bwd("ffn_down_bwd", s["u"], p["cw"], p["cb"], p["w_down"], df, tr, tps)
    du, g["cw"], g["cb"] = conv_bwd("conv_bwd", ducv, s["u"], p["cw"], tr, tps)
    g["w_up_t"] = mm_tn("ffn_up_dw", du, s["x1"], D_FF, tr, mod=mod, sel=(4, 3))
    dx1, dmod_b = mm_modbwd("ffn_up_dx", du, p["w_up_t"], s["x1"], mod, dx1a, (4, 3), tr)

    post_ins = _post_ins(s["ogf"], s["ogb"], s["orf"], s["orb"], s["om"], s["pa"], s["x"], mod, p)
    want = [(0, F32), (2, F32), (4, F32), (5, F32), (6, F32), (7, F32), (8, F32), (9, F32), (10, F32), (11, F32), (12, F32)]
    dog, dor, dom, dgg, drg, dxa, dmod_c, g["gng"], g["w_out"], g["ln1g"], g["ln1b"] = tile_bwd(
        "mix_post_bwd", post_fn, post_ins, [dx1], want, n, tr, tps)
    dqa, dka, dva = mla_bwd("mla_attn_bwd", s["qa"], s["ka"], s["va"], s["om"], s["lse"], dom, nb, tps, tr,
                            nctx * CHUNK)
    pa = s["pa"]
    gdqf, gdkf, gdvf, gdaf, gdqb, gdkb, gdvb, gdab = scan_bwd(
        "gla_scan_bwd", (s["gq"], 128, 0), (pa, 128, 1), (pa, 256, 1), s["af"], s["ab"], s["gstf"], s["gstb"], dog,
        nb, nch, nctx)
    rdqf, rdkf, rdvf, rdaf, rdqb, rdkb, rdvb, rdab = scan_bwd(
        "ret_scan_bwd", (s["rq"], 128, 0), (s["rk"], 128, 0), (pa, 256, 4), s["arf"], s["arb"], s["rstf"], s["rstb"],
        dor, nb, nch, nctx)

    pre_ins = _pre_ins(pa, tabs, p)
    extra = [gdqf, gdqb, rdqf, rdqb, rdkf, rdkb, gdkf, gdkb, gdvf, gdvb, rdvf, rdvb, dgg, drg]
    kinds = [i[0] for i in pre_ins]
    widx = [w[0] for w in _PRE_WANT]
    npre = len(pre_ins)

    def body(*refs):
        vals = [_load(k, r) for k, r in zip(kinds, refs[:npre])]
        rd = lambda i: refs[npre + i][...].astype(F32)
        cots = (rd(0) + rd(1), rd(14), rd(15), rd(16), rd(17), rd(2) + rd(3), rd(4) + rd(5), rd(18), rd(19), rd(20))

        def f(*dv):
            full = list(vals)
            for i, v in zip(widx, dv):
                full[i] = v
            return tuple(pre_fn(*full))

        _, vjp = jax.vjp(f, *[vals[i] for i in widx])
        grads = vjp(cots)
        dgq, drq, drk, dcq, dckv, dmisc = grads[:6]
        dp = jnp.concatenate([dgq, rd(6) + rd(7), rd(8) + rd(9), rd(12), drq, drk, rd(10) + rd(11), rd(13),
                              dcq, dckv, dmisc], axis=1)
        outs = refs[npre + 21:]
        outs[0][...] = dp.astype(BF16)
        first = pl.program_id(0) == 0
        for r, gr in zip(outs[1:], grads[6:]):
            @pl.when(first)
            def _(r=r):
                r[...] = jnp.zeros_like(r)
            r[...] += gr

    cot_arrays = extra + [gdaf, gdab, rdaf, rdab, dqa, dka, dva]
    par_arrays = [pre_ins[i][1] for i in range(10, 21)]
    res = pl.pallas_call(
        body, name="mix_pre_bwd", grid=(n // tr,),
        in_specs=[_spec(k, a, w, cb, tr, tps) for k, a, w, cb in pre_ins]
        + [pl.BlockSpec((tr, c.shape[1]), lambda j: (j, 0)) for c in cot_arrays],
        out_specs=[pl.BlockSpec((tr, P_PAD), lambda j: (j, 0))] + [pl.BlockSpec(a.shape, lambda j: (0, 0)) for a in par_arrays],
        out_shape=[jax.ShapeDtypeStruct((n, P_PAD), BF16)] + [jax.ShapeDtypeStruct(a.shape, F32) for a in par_arrays],
        compiler_params=_cparams(("arbitrary",)),
    )(*[i[1] for i in pre_ins], *cot_arrays)
    dp = res[0]
    for k, v in zip(("w2f", "w2b", "b2f", "b2b", "retf", "retb", "qg", "kvg", "wuq", "wuk", "wuv"), res[1:]):
        g[k] = v
    g["w_in_t"] = mm_tn("proj_dw", dp, s["x"], P_PAD, tr, mod=mod, sel=(1, 0))
    dx, dmod_d = mm_modbwd("proj_dx", dp, p["w_in_t"], s["x"], mod, dxa, (1, 0), tr)
    return dx, dmod_a + dmod_b + dmod_c + dmod_d, g


def _unprep_grads(g, w, l):
    o = {}
    o["w_in_t"] = _rows(g["w_in_t"], W_IN_INV_SEGS)
    o["ffn_up_t"] = _rows(g["w_up_t"], FF_SEGS)
    o["ffn_down"] = g["w_down"]
    o["w_out"] = g["w_out"]
    o["mla_w_uq_t"] = _unpad_heads(g["wuq"], 96)
    o["mla_w_uk_t"] = _unpad_heads(g["wuk"], 64)
    o["mla_w_uv_t"] = g["wuv"]
    o["gla_gate_w"] = jnp.stack([g["w2f"][0:16], g["w2b"][16:32]])
    o["gla_gate_b"] = jnp.concatenate([g["b2f"], g["b2b"]], axis=0)
    dlg = jnp.stack([g["retf"].reshape(4, 32).sum(-1), g["retb"].reshape(4, 32).sum(-1)])
    o["ret_decay"] = dlg * jax.nn.sigmoid(-w["ret_decay"][l])
    o["mla_q_norm_g"], o["mla_kv_norm_g"] = g["qg"][0], g["kvg"][0]
    o["gla_norm_g"] = g["gng"].reshape(4, 64).sum(0)
    o["ln1_g"], o["ln1_b"], o["ln2_g"], o["ln2_b"] = g["ln1g"][0], g["ln1b"][0], g["ln2g"][0], g["ln2b"][0]
    o["ffn_conv_w"] = _cols(g["cw"][0:3], FF_SEGS)
    o["ffn_conv_b"] = _cols(g["cb"][0], FF_SEGS)
    return o


def local_step(xs, target, modtab, layer_weights, dims, grads_ready=None):
    nb, tps, tr, nch, nctx = dims
    tabs = _tables((tps - 1) * tr, tr)
    x = xs
    saved, preps, ws = [], [], []
    for l in range(DEPTH):
        w = layer_weights(l, x)
        p = _prep_layer(w, 0)
        x, s = layer_fwd(l, x, modtab[l], p, tabs, dims)
        saved.append(s)
        preps.append(p)
        ws.append(w)
    dy, lpart = loss_head("loss_head", x, target, nb, tps, tr)
    loss = jnp.sum(lpart[:, 0, 0])
    dx = dy
    dmods, grads = [None] * DEPTH, [None] * DEPTH
    tok = None
    for l in reversed(range(DEPTH)):
        mod = modtab[l] if tok is None else modtab[l] + tok
        dx, dmods[l], g = layer_bwd(l, dx, saved[l], mod, preps[l], tabs, dims)
        grads[l] = _unprep_grads(g, ws[l], 0)
        tok = grads_ready(l, grads) if grads_ready is not None else None
    return loss, dx, jnp.stack(dmods), grads


BIG = [("ffn_up", 2), ("ffn_down", 1), ("w_out", 1), ("w_in", 2), ("mla_w_uq", 2), ("mla_w_uk", 2), ("mla_w_uv", 2)]
SMALL = ["ada_b", "gla_gate_w", "gla_gate_b", "gla_norm_g", "ret_decay", "mla_q_norm_g", "mla_kv_norm_g",
         "ln1_g", "ln1_b", "ffn_conv_b", "ln2_g", "ln2_b"]
PACK_C = 1024


def _big_key(k, axis):
    return k + "_t" if axis == 2 else k


def sum8(name, g8):
    k, r, c = g8.shape
    rows = max(b for b in range(16, ADAM_MAX_ROWS + 1, 16) if r % b == 0)

    def body(g_ref, o_ref):
        g = g_ref[0].astype(F32)
        for i in range(1, k):
            g = g + g_ref[i].astype(F32)
        o_ref[...] = g

    return pl.pallas_call(
        body, name=name, grid=(r // rows,), in_specs=[pl.BlockSpec((k, rows, c), lambda i: (0, i, 0))],
        out_specs=pl.BlockSpec((rows, c), lambda i: (i, 0)), out_shape=jax.ShapeDtypeStruct((r, c), F32),
        compiler_params=_cparams(("parallel",)),
    )(g8)


def _pack(arrs, dtype):
    flat = jnp.concatenate([a.reshape(-1).astype(dtype) for a in arrs])
    pad = (-flat.shape[0]) % (8 * PACK_C)
    return jnp.concatenate([flat, jnp.zeros((pad,), dtype)]).reshape(-1, PACK_C)


def _unpack(flat2d, shapes):
    flat = flat2d.reshape(-1)
    out, off = [], 0
    for s in shapes:
        sz = int(np.prod(s))
        out.append(flat[off:off + sz].reshape(s))
        off += sz
    return out


def _tile_pad(a):
    pad = (-a.shape[-2]) % HALO
    return a if pad == 0 else jnp.concatenate([a, jnp.zeros(a.shape[:-2] + (pad, a.shape[-1]), a.dtype)], axis=-2)


def kernel(x, c, ctx, c_ctx, ada_w, ada_b, w_in, gla_gate_w, gla_gate_b, gla_norm_g, ret_decay, mla_q_norm_g, mla_kv_norm_g, mla_w_uq, mla_w_uk, mla_w_uv, w_out, ln1_g, ln1_b, ffn_up, ffn_conv_w, ffn_conv_b, ffn_down, ln2_g, ln2_b, loss_target, m_c_ctx, m_ada_w, m_ada_b, m_w_in, m_gla_gate_w, m_gla_gate_b, m_gla_norm_g, m_ret_decay, m_mla_q_norm_g, m_mla_kv_norm_g, m_mla_w_uq, m_mla_w_uk, m_mla_w_uv, m_w_out, m_ln1_g, m_ln1_b, m_ffn_up, m_ffn_conv_w, m_ffn_conv_b, m_ffn_down, m_ln2_g, m_ln2_b, v_c_ctx, v_ada_w, v_ada_b, v_w_in, v_gla_gate_w, v_gla_gate_b, v_gla_norm_g, v_ret_decay, v_mla_q_norm_g, v_mla_kv_norm_g, v_mla_w_uq, v_mla_w_uk, v_mla_w_uv, v_w_out, v_ln1_g, v_ln1_b, v_ffn_up, v_ffn_conv_w, v_ffn_conv_b, v_ffn_down, v_ln2_g, v_ln2_b):
    names = ["c_ctx", "ada_w", "ada_b", "w_in", "gla_gate_w", "gla_gate_b", "gla_norm_g", "ret_decay", "mla_q_norm_g",
             "mla_kv_norm_g", "mla_w_uq", "mla_w_uk", "mla_w_uv", "w_out", "ln1_g", "ln1_b", "ffn_up", "ffn_conv_w",
             "ffn_conv_b", "ffn_down", "ln2_g", "ln2_b"]
    loc = locals()
    W = {k: loc[k] for k in names}
    M = {k: loc["m_" + k] for k in names}
    V = {k: loc["v_" + k] for k in names}

    nb, seq, _ = x.shape
    tr = ctx.shape[1]
    tps = 1 + seq // tr
    t = tps * tr
    n = nb * t
    nt = nb * tps
    dims = (nb, tps, tr, t // CHUNK, tr // CHUNK)
    px, py, pc = _place()
    me = 4 * px + 2 * py + pc
    ncol = ada_w.shape[2]

    cw_loc = ffn_conv_w.reshape(-1)
    g1 = jnp.concatenate([c.reshape(-1), cw_loc])
    g1 = jnp.concatenate([g1, jnp.zeros(((-g1.shape[0]) % (8 * PACK_C),), F32)]).reshape(-1, PACK_C)
    r1 = g1.shape[0]
    g1a = all_gather("gather_cond", g1, True).reshape(N_DEV, -1)
    c_all = g1a[:, :nb * D].reshape(N_DEV * nb, D)
    cw_all = g1a[:, nb * D:nb * D + cw_loc.shape[0]].reshape(N_DEV, DEPTH, 3, -1).transpose(1, 2, 0, 3).reshape(DEPTH, 3, -1)

    first, rest = [0], list(range(1, DEPTH))
    row_form = {k: (jnp.swapaxes(W[k], 1, 2) if ax == 2 else W[k]).astype(BF16) for k, ax in BIG}

    def part_rows(k):
        rows = int(np.prod(W[k].shape[1:])) // PACK_C
        return rows, -(-rows // HALO) * HALO

    def pack_rows(ls):
        return jnp.concatenate([_tile_pad(row_form[k][l].reshape(-1, PACK_C)) for k, _ in BIG for l in ls], axis=0)

    def whole_weights(wall, ls):
        out, off = {}, 0
        for k, ax in BIG:
            rows, padded = part_rows(k)
            _, r, c = row_form[k].shape
            out[_big_key(k, ax)] = [wall[:, off + i * padded:off + i * padded + rows].reshape(1, N_DEV * r, c)
                                    for i in range(len(ls))]
            off += len(ls) * padded
        return out

    pack0 = pack_rows(first)
    whole0 = whole_weights(all_gather("gather_weights0", pack0, False).reshape(N_DEV, -1, PACK_C), first)
    pack_rest = pack_rows(rest)
    wsend, wrecv, wsrc, wland, wtok = xchg_start("gather_weights_start", pack_rest, False)
    small_w = {k: W[k] for k in SMALL[1:]}
    small_w["ffn_conv_w"] = cw_all
    later = {}

    def layer_weights(l, xin):
        if l >= 1 and not later:
            src, land = xchg_wait("gather_weights_wait", wsend, wrecv, wsrc, wland, xin, False)
            later.update(whole_weights(lax.dynamic_update_slice(land, src[None], (me, 0, 0)), rest))
        big = {k: v[0] for k, v in whole0.items()} if l == 0 else {k: v[l - 1] for k, v in later.items()}
        return {**big, **{k: v[l:l + 1] for k, v in small_w.items()}}

    srows = 40
    s_in = jnp.concatenate([c_all, c_ctx[None], jnp.zeros((srows - N_DEV * nb - 1, D), F32)], axis=0)
    s_act = _silu(s_in)
    ab_loc = lax.dynamic_slice_in_dim(ada_b, me * ncol, ncol, axis=1)[:, None, :]
    mod_part = ada_fwd("ada_fwd", s_act, ada_w, ab_loc)
    mod_all = all_gather("gather_mod", mod_part.reshape(-1, ncol), True).reshape(N_DEV, DEPTH, srows, ncol)
    mod_rows = mod_all.transpose(1, 2, 0, 3).reshape(DEPTH, srows, N_DEV * ncol)
    mod_l = lax.dynamic_slice_in_dim(mod_rows, me * nb, nb, axis=1).reshape(DEPTH, nb, 6, D)
    mod_c = mod_rows[:, N_DEV * nb].reshape(DEPTH, 1, 6, D)
    tile_is_ctx = (jnp.arange(tps) == 0)[None, None, :, None, None]
    modtab = jnp.where(tile_is_ctx, mod_c[:, :, None], mod_l[:, :, None])
    modtab = jnp.concatenate([modtab, jnp.zeros((DEPTH, nb, tps, 2, D), F32)], axis=3).reshape(DEPTH, nt, 8, D)
    modtab = modtab + wtok[0, 0]

    def grad_blocks(grads, ls):
        return jnp.concatenate(
            [_tile_pad(grads[l][_big_key(k, ax)].astype(BF16).reshape(N_DEV, -1, PACK_C)) for k, ax in BIG for l in ls],
            axis=1)

    early = {}

    def grads_ready(l, grads):
        if l != 1:
            return None
        early["sems"] = xchg_start("grad_exchange_start", grad_blocks(grads, rest), True)
        return early["sems"][4][0, 0]

    xs = jnp.concatenate([ctx, x], axis=1).reshape(n, D)
    loss_loc, dxs, dmodtab, grads = local_step(xs, loss_target.reshape(nb * seq, D), modtab, layer_weights, dims,
                                               grads_ready)
    loss = lax.psum(loss_loc, ("x", "y", "c"))
    grad_x = dxs.reshape(nb, t, D)[:, tr:]
    gl = {k: jnp.stack([g[k] for g in grads]) for k in grads[0] if k in SMALL or k == "ffn_conv_w"}

    dm = dmodtab.reshape(DEPTH, nb, tps, 8, D)[:, :, :, :6]
    dmod_l = dm[:, :, 1:].sum(2).reshape(DEPTH, nb, 6 * D)
    dmod_c = dm[:, :, 0].sum(1).reshape(DEPTH, 1, 6 * D)
    gl["ada_b"] = dmod_l.sum(1) + dmod_c[:, 0]
    small_list = [gl[k] for k in SMALL] + [gl["ffn_conv_w"]]
    small_shapes = [a.shape for a in small_list]
    fsend, frecv, fsrc, fland, ftok = xchg_start("grad_exchange0_start", grad_blocks(grads, first), True)
    spack = _pack(small_list + [jnp.concatenate([dmod_l, dmod_c], axis=1)], F32) + ftok[0, 0]
    rs = spack.shape[0]
    sall = all_gather("gather_small_grads", spack, True).reshape(N_DEV, rs, PACK_C)
    nsmall = sum(int(np.prod(s)) for s in small_shapes)
    dmo = sall.reshape(N_DEV, -1)[:, nsmall:nsmall + DEPTH * (nb + 1) * 6 * D].reshape(N_DEV, DEPTH, nb + 1, 6 * D)
    dl_all = dmo[:, :, :nb].transpose(1, 0, 2, 3).reshape(DEPTH, N_DEV * nb, 6 * D)
    dc_all = dmo[:, :, nb].sum(0)[:, None]
    dmod_rows = jnp.concatenate([dl_all, dc_all, jnp.zeros((DEPTH, srows - N_DEV * nb - 1, 6 * D), F32)], axis=1)
    dmod_loc = lax.dynamic_slice_in_dim(dmod_rows.reshape(DEPTH, srows, N_DEV, ncol), me, 1, axis=2)[:, :, 0]
    d_ada_w, d_s = ada_bwd("ada_bwd", s_act, ada_w, dmod_loc)
    sg = jax.nn.sigmoid(c_ctx)
    dcc = d_s[:, N_DEV * nb].sum(0) * (sg * (1.0 + c_ctx * (1.0 - sg)))
    ccp = jnp.concatenate([dcc[None], jnp.zeros((7, D), F32)], axis=0)
    ccall = all_gather("gather_cctx", ccp, True).reshape(N_DEV, 8, D)

    esend, erecv, esrc, eland, _ = early["sems"]
    def landed(tag, sems, src, land, after):
        src, land = xchg_wait(tag + "_wait", sems[0], sems[1], src, land, after, True)
        mine = lax.dynamic_slice_in_dim(src, me, 1, axis=0)
        return sum8(tag + "_sum", lax.dynamic_update_slice(land, mine, (me, 0, 0)))

    gsum_rest = landed("grad_exchange", (esend, erecv), esrc, eland, ccall)
    gsum_first = landed("grad_exchange0", (fsend, frecv), fsrc, fland, gsum_rest)
    res = {}

    def update2d(tag, k, g):
        last = W[k].shape[-1]
        outs = adamw(tag, W[k].reshape(-1, last), M[k].reshape(-1, last), V[k].reshape(-1, last),
                     g.reshape(1, -1, last))
        res[k] = [a.reshape(W[k].shape) for a in outs]

    off0, off1 = 0, 0
    for k, ax in BIG:
        rows, padded = part_rows(k)
        _, r, c = row_form[k].shape
        parts = [gsum_first[off0:off0 + rows]] + [gsum_rest[off1 + i * padded:off1 + i * padded + rows]
                                                  for i in range(len(rest))]
        g = jnp.stack([p.reshape(r, c) for p in parts])
        update2d("adamw_" + k, k, jnp.swapaxes(g, 1, 2) if ax == 2 else g)
        off0 += padded
        off1 += len(rest) * padded

    def update(tag, keys, g8):
        outs = adamw(tag, _pack([W[k] for k in keys], F32), _pack([M[k] for k in keys], F32),
                     _pack([V[k] for k in keys], F32), g8)
        for i, arr in enumerate(outs):
            for k, a in zip(keys, _unpack(arr, [W[k].shape for k in keys])):
                res.setdefault(k, [None] * 4)[i] = a

    nrep = sum(int(np.prod(W[k].shape)) for k in SMALL)
    sflat = sall.reshape(N_DEV, -1)
    def pack8(a):
        a = a.reshape(N_DEV, -1)
        pad = (-a.shape[1]) % (8 * PACK_C)
        return jnp.concatenate([a, jnp.zeros((N_DEV, pad), F32)], axis=1).reshape(N_DEV, -1, PACK_C)

    update("adamw_small", SMALL, pack8(sflat[:, :nrep]))
    ncw = ffn_conv_w.shape[2]
    cw8 = sflat[:, nrep:nsmall].reshape(N_DEV, DEPTH, 3, N_DEV * ncw)
    cw8 = lax.dynamic_slice_in_dim(cw8, me * ncw, ncw, axis=3)
    update("adamw_conv", ["ffn_conv_w"], pack8(cw8))
    update2d("adamw_ada", "ada_w", d_ada_w)
    update("adamw_cctx", ["c_ctx"], ccall)

    out = [loss, grad_x]
    for i in range(4):
        out += [res[k][i] for k in names]
    return tuple(out)
```

```python
import functools
import math

import numpy as np
import jax
import jax.numpy as jnp
from jax import lax
from jax.experimental import pallas as pl
from jax.experimental.pallas import tpu as pltpu

F32 = jnp.float32
BF16 = jnp.bfloat16
HI = lax.Precision.HIGHEST
MESH = pl.DeviceIdType.MESH

N_DEV = 8
D = 1024
DEPTH = 4
CHUNK = 64
EPS = 1e-6
ALPHA = (2 * DEPTH) ** 0.25
GLA_TAU = 16.0
ROPE_BASE = 10000.0
MLA_SCALE = 96 ** -0.5
D_FF = 2816
FF_CHUNK = 1408
P_PAD = 2048
VMEM_LIMIT_BYTES = 56 << 20

ADAM_LR, ADAM_B1, ADAM_B2, ADAM_EPS, ADAM_WD, ADAM_STEP = 0.001, 0.9, 0.999, 1e-08, 0.01, 10

D_IN = 1984
W_IN_SEGS = [(0, 512), (544, 1408), (512, 32), (1952, 32)]
W_IN_INV_SEGS = [(0, 512), (1920, 32), (512, 1408), (1952, 32)]
FF_SEGS = [(0, FF_CHUNK), (D_FF, FF_CHUNK), (FF_CHUNK, FF_CHUNK), (D_FF + FF_CHUNK, FF_CHUNK)]


def _cols(a, segs):
    return jnp.concatenate([a[..., s:s + n] for s, n in segs], axis=-1)


def _rows(a, segs):
    return jnp.concatenate([a[s:s + n] for s, n in segs], axis=0)


def _pad_heads(wt, per_head):
    c = wt.shape[1]
    wt = wt.reshape(8, per_head, c)
    return jnp.concatenate([wt, jnp.zeros((8, 128 - per_head, c), wt.dtype)], axis=1).reshape(1024, c)


def _unpad_heads(g, per_head):
    c = g.shape[1]
    return g.reshape(8, 128, c)[:, :per_head].reshape(8 * per_head, c)


def _cparams(sem=None):
    return pltpu.CompilerParams(vmem_limit_bytes=VMEM_LIMIT_BYTES, dimension_semantics=sem)


@jax.custom_vjp
def bdot(a, w):
    return jnp.dot(a.astype(BF16), w.astype(BF16), preferred_element_type=F32)


def _bdot_fwd(a, w):
    return bdot(a, w), (a, w)


def _bdot_bwd(res, ct):
    a, w = res
    ctb = ct.astype(BF16)
    da = lax.dot_general(ctb, w.astype(BF16), (((1,), (1,)), ((), ())), preferred_element_type=F32)
    dw = lax.dot_general(a.astype(BF16), ctb, (((0,), (0,)), ((), ())), preferred_element_type=F32)
    return da.astype(a.dtype), dw.astype(w.dtype)


bdot.defvjp(_bdot_fwd, _bdot_bwd)


@jax.custom_vjp
def bdot_nt(a, wt):
    return lax.dot_general(a.astype(BF16), wt.astype(BF16), (((1,), (1,)), ((), ())), preferred_element_type=F32)


def _bdot_nt_fwd(a, wt):
    return bdot_nt(a, wt), (a, wt)


def _bdot_nt_bwd(res, ct):
    a, wt = res
    ctb = ct.astype(BF16)
    da = jnp.dot(ctb, wt.astype(BF16), preferred_element_type=F32)
    dwt = lax.dot_general(ctb, a.astype(BF16), (((0,), (0,)), ((), ())), preferred_element_type=F32)
    return da.astype(a.dtype), dwt.astype(wt.dtype)


bdot_nt.defvjp(_bdot_nt_fwd, _bdot_nt_bwd)


@jax.custom_vjp
def bdot_tn(a, b):
    return lax.dot_general(a.astype(BF16), b.astype(BF16), (((0,), (0,)), ((), ())), preferred_element_type=F32)


def _bdot_tn_fwd(a, b):
    return bdot_tn(a, b), (a, b)


def _bdot_tn_bwd(res, ct):
    a, b = res
    ctb = ct.astype(BF16)
    da = lax.dot_general(b.astype(BF16), ctb, (((1,), (1,)), ((), ())), preferred_element_type=F32)
    db = jnp.dot(a.astype(BF16), ctb, preferred_element_type=F32)
    return da.astype(a.dtype), db.astype(b.dtype)


bdot_tn.defvjp(_bdot_tn_fwd, _bdot_tn_bwd)


def _split3(x):
    x1 = x.astype(BF16)
    r1 = x - x1.astype(F32)
    x2 = r1.astype(BF16)
    return x1, x2, (r1 - x2.astype(F32)).astype(BF16)


@jax.custom_vjp
def xdot(x, m):
    mb = m.astype(BF16)
    return sum(jnp.dot(xi, mb, preferred_element_type=F32) for xi in _split3(x))


def _xdot_bwd(m, ct):
    mb = m.astype(BF16)
    dx = sum(lax.dot_general(ci, mb, (((1,), (1,)), ((), ())), preferred_element_type=F32) for ci in _split3(ct))
    return dx, jnp.zeros_like(m)


xdot.defvjp(lambda x, m: (xdot(x, m), m), _xdot_bwd)


@jax.custom_vjp
def xdot_l(m, x):
    mb = m.astype(BF16)
    return sum(jnp.dot(mb, xi, preferred_element_type=F32) for xi in _split3(x))


def _xdot_l_bwd(m, ct):
    mb = m.astype(BF16)
    dx = sum(lax.dot_general(mb, ci, (((0,), (0,)), ((), ())), preferred_element_type=F32) for ci in _split3(ct))
    return jnp.zeros_like(m), dx


xdot_l.defvjp(lambda m, x: (xdot_l(m, x), m), _xdot_l_bwd)


def _swap_fn(half):
    def swap(x):
        n = x.shape[1]
        first = (_iota(x.shape, 1) // half) % 2 == 0
        return jnp.where(first, pltpu.roll(x, n - half, 1), pltpu.roll(x, half, 1))

    f = jax.custom_vjp(swap)
    f.defvjp(lambda x: (swap(x), None), lambda _, ct: (swap(ct),))
    return f


_swap16 = _swap_fn(16)
_swap8 = _swap_fn(8)


def hdot(a, b):
    return jnp.dot(a, b, precision=HI, preferred_element_type=F32)


def _iota(shape, axis):
    return lax.broadcasted_iota(jnp.int32, shape, axis)


def _group_avg(n, g):
    return (_iota((n, n), 0) // g == _iota((n, n), 1) // g).astype(F32) * (1.0 / g)


def _silu(x):
    return x * jax.nn.sigmoid(x)


def _layer_norm(z, g, b):
    mu = jnp.mean(z, axis=-1, keepdims=True)
    zc = z - mu
    var = jnp.mean(zc * zc, axis=-1, keepdims=True)
    return zc * lax.rsqrt(var + EPS) * g + b


def _rms(x, g):
    return x * lax.rsqrt(jnp.mean(x * x, axis=-1, keepdims=True) + EPS) * g


def mm(name, a, wt, out_dtype, tn, tr, mod=None, sel=None):
    n, k = a.shape
    nw = wt.shape[0]

    def body(*refs):
        if mod is not None:
            a_ref, m_ref, w_ref, o_ref = refs
            m = m_ref[0]
            av = a_ref[...] * (1.0 + m[sel[0]:sel[0] + 1]) + m[sel[1]:sel[1] + 1]
        else:
            a_ref, w_ref, o_ref = refs
            av = a_ref[...]
        o_ref[...] = lax.dot_general(av.astype(BF16), w_ref[...], (((1,), (1,)), ((), ())),
                                     preferred_element_type=F32).astype(o_ref.dtype)

    in_specs = [pl.BlockSpec((tr, k), lambda c, j: (j, 0))]
    args = [a]
    if mod is not None:
        in_specs.append(pl.BlockSpec((1, 8, k), lambda c, j: (j, 0, 0)))
        args.append(mod)
    in_specs.append(pl.BlockSpec((tn, k), lambda c, j: (c, 0)))
    args.append(wt)
    return pl.pallas_call(
        body, name=name, grid=(nw // tn, n // tr), in_specs=in_specs,
        out_specs=pl.BlockSpec((tr, tn), lambda c, j: (j, c)),
        out_shape=jax.ShapeDtypeStruct((n, nw), out_dtype), compiler_params=_cparams(("parallel", "arbitrary")),
    )(*args)


def mm_tn(name, dc, a, tn, tr, mod=None, sel=None):
    n, k = a.shape
    nw = dc.shape[1]

    def body(*refs):
        if mod is not None:
            d_ref, a_ref, m_ref, o_ref = refs
            m = m_ref[0]
            av = a_ref[...] * (1.0 + m[sel[0]:sel[0] + 1]) + m[sel[1]:sel[1] + 1]
        else:
            d_ref, a_ref, o_ref = refs
            av = a_ref[...]

        @pl.when(pl.program_id(1) == 0)
        def _():
            o_ref[...] = jnp.zeros_like(o_ref)

        o_ref[...] += lax.dot_general(d_ref[...].astype(BF16), av.astype(BF16), (((0,), (0,)), ((), ())),
                                      preferred_element_type=F32)

    in_specs = [pl.BlockSpec((tr, tn), lambda c, j: (j, c)), pl.BlockSpec((tr, k), lambda c, j: (j, 0))]
    args = [dc, a]
    if mod is not None:
        in_specs.append(pl.BlockSpec((1, 8, k), lambda c, j: (j, 0, 0)))
        args.append(mod)
    return pl.pallas_call(
        body, name=name, grid=(nw // tn, n // tr), in_specs=in_specs,
        out_specs=pl.BlockSpec((tn, k), lambda c, j: (c, 0)),
        out_shape=jax.ShapeDtypeStruct((nw, k), F32), compiler_params=_cparams(("parallel", "arbitrary")),
    )(*args)


def mm_modbwd(name, dc, wt, x, mod, add, sel, tr):
    n, k = dc.shape
    dm = wt.shape[1]

    def body(dc_ref, wt_ref, x_ref, m_ref, add_ref, dx_ref, dm_ref):
        dh = jnp.dot(dc_ref[...].astype(BF16), wt_ref[...], preferred_element_type=F32)
        m = m_ref[0]
        dx_ref[...] = add_ref[...] + dh * (1.0 + m[sel[0]:sel[0] + 1])
        dsc = jnp.sum(dh * x_ref[...], axis=0, keepdims=True)
        dsh = jnp.sum(dh, axis=0, keepdims=True)
        rows = _iota((8, dm), 0)
        dm_ref[0] = jnp.where(rows == sel[0], dsc, 0.0) + jnp.where(rows == sel[1], dsh, 0.0)

    return pl.pallas_call(
        body, name=name, grid=(n // tr,),
        in_specs=[pl.BlockSpec((tr, k), lambda j: (j, 0)), pl.BlockSpec((k, dm), lambda j: (0, 0)),
                  pl.BlockSpec((tr, dm), lambda j: (j, 0)), pl.BlockSpec((1, 8, dm), lambda j: (j, 0, 0)),
                  pl.BlockSpec((tr, dm), lambda j: (j, 0))],
        out_specs=[pl.BlockSpec((tr, dm), lambda j: (j, 0)), pl.BlockSpec((1, 8, dm), lambda j: (j, 0, 0))],
        out_shape=[jax.ShapeDtypeStruct((n, dm), F32), jax.ShapeDtypeStruct((n // tr, 8, dm), F32)],
        compiler_params=_cparams(("arbitrary",)),
    )(dc, wt, x, mod, add)


def _spec(kind, arr, width, cb, tr, tps):
    if kind == "row":
        return pl.BlockSpec((tr, width), lambda j: (j, cb))
    if kind == "pos":
        return pl.BlockSpec((tr, width), lambda j: (j % tps, cb))
    if kind == "tile":
        return pl.BlockSpec((1,) + arr.shape[1:], lambda j: (j, 0, 0))
    if kind == "par":
        return pl.BlockSpec(arr.shape, lambda j: (0, 0))
    raise ValueError(kind)


def _load(kind, ref):
    v = ref[0] if kind == "tile" else ref[...]
    return v.astype(F32)


def tile_fwd(name, fn, ins, outs, n, tr, tps):
    kinds = [i[0] for i in ins]

    def body(*refs):
        vals = [_load(k, r) for k, r in zip(kinds, refs[:len(ins)])]
        res = fn(*vals)
        for r, o in zip(refs[len(ins):], res):
            r[...] = o.astype(r.dtype)

    return pl.pallas_call(
        body, name=name, grid=(n // tr,),
        in_specs=[_spec(k, a, w, cb, tr, tps) for k, a, w, cb in ins],
        out_specs=[pl.BlockSpec((tr, w), lambda j: (j, 0)) for w, _ in outs],
        out_shape=[jax.ShapeDtypeStruct((n, w), dt) for w, dt in outs],
        compiler_params=_cparams(("arbitrary",)),
    )(*[i[1] for i in ins])


def tile_bwd(name, fn, ins, cots, want, n, tr, tps):
    kinds = [i[0] for i in ins]
    widx = [w[0] for w in want]
    ni, nc = len(ins), len(cots)

    def body(*refs):
        vals = [_load(k, r) for k, r in zip(kinds, refs[:ni])]
        cvals = tuple(r[...].astype(F32) for r in refs[ni:ni + nc])

        def f(*dv):
            full = list(vals)
            for i, v in zip(widx, dv):
                full[i] = v
            return tuple(fn(*full))

        _, vjp = jax.vjp(f, *[vals[i] for i in widx])
        grads = vjp(cvals)
        first = pl.program_id(0) == 0
        for r, g, i in zip(refs[ni + nc:], grads, widx):
            if kinds[i] == "par":
                @pl.when(first)
                def _(r=r):
                    r[...] = jnp.zeros_like(r)
                r[...] += g
            elif kinds[i] == "tile":
                r[0] = g.astype(r.dtype)
            else:
                r[...] = g.astype(r.dtype)

    out_specs, out_shape = [], []
    for i, dt in want:
        k, a, w, cb = ins[i]
        if k == "par":
            out_specs.append(pl.BlockSpec(a.shape, lambda j: (0, 0)))
            out_shape.append(jax.ShapeDtypeStruct(a.shape, F32))
        elif k == "tile":
            out_specs.append(pl.BlockSpec((1,) + a.shape[1:], lambda j: (j, 0, 0)))
            out_shape.append(jax.ShapeDtypeStruct(a.shape, F32))
        else:
            out_specs.append(pl.BlockSpec((tr, w), lambda j: (j, 0)))
            out_shape.append(jax.ShapeDtypeStruct((n, w), dt))
    return pl.pallas_call(
        body, name=name, grid=(n // tr,),
        in_specs=[_spec(k, a, w, cb, tr, tps) for k, a, w, cb in ins]
        + [pl.BlockSpec((tr, c.shape[1]), lambda j: (j, 0)) for c in cots],
        out_specs=out_specs, out_shape=out_shape, compiler_params=_cparams(("arbitrary",)),
    )(*[i[1] for i in ins], *cots)


def pre_fn(p_gq, p_rq, p_rk, p_cq, p_ckv, p_misc, rcos, rsin, mcos, msin,
           w2f, w2b, b2f, b2b, retf, retb, qg, kvg, wuq, wuk, wuv, e2):
    tr = p_gq.shape[0]
    gq = p_gq * (32 ** -0.5)
    af = jax.nn.log_sigmoid(hdot(p_misc, w2f) + b2f) * (1.0 / GLA_TAU)
    ab = jax.nn.log_sigmoid(hdot(p_misc, w2b) + b2b) * (1.0 / GLA_TAU)
    arf = jnp.zeros((tr, 128), F32) + retf
    arb = jnp.zeros((tr, 128), F32) + retb
    rq = p_rq * rcos + _swap16(p_rq) * rsin
    rks = p_rk * (32 ** -0.5)
    rk = rks * rcos + _swap16(rks) * rsin
    qp = bdot_nt(_rms(p_cq, qg), wuq) * MLA_SCALE
    ckvn = _rms(p_ckv, kvg)
    kp = bdot_nt(ckvn, wuk) + xdot(p_misc, e2)
    mc, ms = jnp.tile(mcos, (1, 8)), jnp.tile(msin, (1, 8))
    v = bdot_nt(ckvn, wuv)
    return gq, af, ab, arf, arb, rq, rk, qp * mc + _swap8(qp) * ms, kp * mc + _swap8(kp) * ms, v


def post_fn(ogf, ogb, orf, orb, om, gg, rg, x, mod, gng, wout, lng, lnb):
    avg = _group_avg(256, 64)
    og = ogf + ogb
    mg = og * lax.rsqrt(xdot(og * og, avg) + EPS) * gng * _silu(gg)
    orr = orf + orb
    oc = orr - xdot(orr, avg)
    mr = oc * lax.rsqrt(xdot(oc * oc, avg) + EPS) * _silu(rg)
    m = jnp.concatenate([mg, mr, om], axis=1)
    y = bdot(m, wout)
    return (_layer_norm(ALPHA * x + mod[2:3] * y, lng, lnb),)


def ln2_fn(x1, f, mod, lng, lnb):
    return (_layer_norm(ALPHA * x1 + mod[5:6] * f, lng, lnb),)


def scan_step(q, k, v, a, st, rev):
    ii, jj = _iota((CHUNK, CHUNK), 0), _iota((CHUNK, CHUNK), 1)
    tri = ((jj >= ii) if rev else (jj <= ii)).astype(F32)
    b = xdot_l(tri, a)
    btot = jnp.sum(a, axis=0, keepdims=True)
    qe = q * jnp.exp(b - btot)
    ke = k * jnp.exp(btot - b)
    lane = _iota((1, 128), 1)
    q4 = jnp.concatenate([qe * (lane // 32 == h).astype(F32) for h in range(4)], axis=0)
    att = bdot_nt(q4, ke)
    att = jnp.where(jnp.concatenate([tri] * 4, axis=0) > 0, att, 0.0)
    r = bdot(att, v)
    col = _iota((1, 256), 1)
    o = bdot_nt(q * jnp.exp(b), st)
    for h in range(4):
        o = o + r[h * CHUNK:(h + 1) * CHUNK] * (col // 64 == h).astype(F32)
    vk = bdot_tn(v, ke)
    bd = (_iota((256, 128), 0) // 64 == _iota((256, 128), 1) // 32).astype(F32)
    return o, st * jnp.exp(btot) + vk * bd


def _chunk_maps(nch, nctx):
    def fwd(s):
        return s

    def bwd(s):
        return jnp.where(s < nctx, nctx - 1 - s, nch - 1 - (s - nctx))
    return fwd, bwd


def _per_sample(arr, nb):
    return arr.reshape(nb, arr.shape[0] // nb, arr.shape[1])


def scan_fwd(name, q, k, v, af, ab, nb, nch, nctx):
    n = af.shape[0]
    fmap, bmap = _chunk_maps(nch, nctx)

    def body(qf, kf, vf, a_f, qb, kb, vb, a_b, of_ref, ob_ref, stf_ref, stb_ref, s_scr):
        @pl.when(pl.program_id(0) == 0)
        def _():
            s_scr[...] = jnp.zeros_like(s_scr)

        for i in range(nb):
            stf_ref[0, i] = s_scr[2 * i]
            stb_ref[0, i] = s_scr[2 * i + 1]
            o, sn = scan_step(qf[i], kf[i], vf[i], a_f[i], s_scr[2 * i], False)
            of_ref[i] = o
            s_scr[2 * i] = sn
            o, sn = scan_step(qb[i], kb[i], vb[i], a_b[i], s_scr[2 * i + 1], True)
            ob_ref[i] = o
            s_scr[2 * i + 1] = sn

    def specs(m):
        return [pl.BlockSpec((nb, CHUNK, w), lambda s, cb=cb: (0, m(s), cb)) for _, w, cb in (q, k, v)] + \
               [pl.BlockSpec((nb, CHUNK, 128), lambda s: (0, m(s), 0))]

    ps = lambda a: _per_sample(a, nb)
    of, ob, stf, stb = pl.pallas_call(
        body, name=name, grid=(nch,), in_specs=specs(fmap) + specs(bmap),
        out_specs=[pl.BlockSpec((nb, CHUNK, 256), lambda s: (0, fmap(s), 0)),
                   pl.BlockSpec((nb, CHUNK, 256), lambda s: (0, bmap(s), 0)),
                   pl.BlockSpec((1, nb, 256, 128), lambda s: (s, 0, 0, 0)),
                   pl.BlockSpec((1, nb, 256, 128), lambda s: (s, 0, 0, 0))],
        out_shape=[jax.ShapeDtypeStruct((nb, n // nb, 256), F32)] * 2
        + [jax.ShapeDtypeStruct((nch, nb, 256, 128), F32)] * 2,
        scratch_shapes=[pltpu.VMEM((2 * nb, 256, 128), F32)], compiler_params=_cparams(("arbitrary",)),
    )(ps(q[0]), ps(k[0]), ps(v[0]), ps(af), ps(q[0]), ps(k[0]), ps(v[0]), ps(ab))
    return of.reshape(n, 256), ob.reshape(n, 256), stf, stb


def scan_bwd(name, q, k, v, af, ab, stf, stb, do, nb, nch, nctx):
    n = af.shape[0]
    fmap0, bmap0 = _chunk_maps(nch, nctx)
    fmap = lambda r: fmap0(nch - 1 - r)
    bmap = lambda r: bmap0(nch - 1 - r)

    def body(qf, kf, vf, a_f, sf, dof, qb, kb, vb, a_b, sb, dob,
             dqf, dkf, dvf, daf, dqb, dkb, dvb, dab, ds_scr):
        @pl.when(pl.program_id(0) == 0)
        def _():
            ds_scr[...] = jnp.zeros_like(ds_scr)

        for i in range(nb):
            for d, (qr, kr, vr, ar, sr, dor, outs) in enumerate(((qf, kf, vf, a_f, sf, dof, (dqf, dkf, dvf, daf)),
                                                                   (qb, kb, vb, a_b, sb, dob, (dqb, dkb, dvb, dab)))):
                _, vjp = jax.vjp(functools.partial(scan_step, rev=bool(d)), qr[i], kr[i], vr[i], ar[i], sr[0, i])
                dq, dk, dv, da, ds = vjp((dor[i], ds_scr[2 * i + d]))
                outs[0][i] = dq
                outs[1][i] = dk
                outs[2][i] = dv
                outs[3][i] = da
                ds_scr[2 * i + d] = ds

    def specs(m):
        return [pl.BlockSpec((nb, CHUNK, w), lambda r, cb=cb: (0, m(r), cb)) for _, w, cb in (q, k, v)] + \
               [pl.BlockSpec((nb, CHUNK, 128), lambda r: (0, m(r), 0)),
                pl.BlockSpec((1, nb, 256, 128), lambda r: (nch - 1 - r, 0, 0, 0)),
                pl.BlockSpec((nb, CHUNK, 256), lambda r: (0, m(r), 0))]

    def ospecs(m):
        return [pl.BlockSpec((nb, CHUNK, w), lambda r: (0, m(r), 0)) for w in (128, 128, 256, 128)]

    ps = lambda a: _per_sample(a, nb)
    oshape = [jax.ShapeDtypeStruct((nb, n // nb, w), F32) for w in (128, 128, 256, 128)]
    outs = pl.pallas_call(
        body, name=name, grid=(nch,), in_specs=specs(fmap) + specs(bmap),
        out_specs=ospecs(fmap) + ospecs(bmap), out_shape=oshape + oshape,
        scratch_shapes=[pltpu.VMEM((2 * nb, 256, 128), F32)], compiler_params=_cparams(("arbitrary",)),
    )(ps(q[0]), ps(k[0]), ps(v[0]), ps(af), stf, ps(do), ps(q[0]), ps(k[0]), ps(v[0]), ps(ab), stb, ps(do))
    return [o.reshape(n, o.shape[2]) for o in outs]


def mla_fwd(name, qa, ka, va, nb, tps, tr, nctx_rows):
    n = qa.shape[0]
    t = tps * tr

    def body(q_ref, k_ref, v_ref, o_ref, lse_ref, vt):
        @pl.when(pl.program_id(2) == 0)
        def _():
            vtr = v_ref[...].astype(F32).T.astype(BF16)
            top = _iota(vtr.shape, 0) < 64
            one = jnp.ones_like(vtr)
            vt[0:128, :] = jnp.where(top, vtr, one)
            vt[128:256, :] = jnp.where(top, one, vtr)

        def attend(nk):
            res, lses = [], []
            for h in range(2):
                hs = slice(h * 128, (h + 1) * 128)
                st = lax.dot_general(k_ref[0:nk, hs], q_ref[:, hs], (((1,), (1,)), ((), ())),
                                     preferred_element_type=F32)
                m = jnp.max(st, axis=0, keepdims=True)
                e = jnp.exp((st - m).astype(BF16))
                r = jnp.dot(vt[hs, 0:nk], e, preferred_element_type=F32)
                l = r[64:65] if h == 0 else r[0:1]
                res.append(r / l)
                lses.append(m + jnp.log(l))
            top = _iota((128, tr), 0) < 64
            o_ref[...] = jnp.where(top, res[0], res[1]).T
            lse_ref[...] = jnp.where(top, lses[0], lses[1]).T

        @pl.when(pl.program_id(2) == 0)
        def _():
            attend(nctx_rows)

        @pl.when(pl.program_id(2) > 0)
        def _():
            attend(t)

    return pl.pallas_call(
        body, name=name, grid=(nb, 4, tps),
        in_specs=[pl.BlockSpec((tr, 256), lambda b, h, j: (b * tps + j, h)), pl.BlockSpec((t, 256), lambda b, h, j: (b, h)),
                  pl.BlockSpec((t, 128), lambda b, h, j: (b, h))],
        out_specs=[pl.BlockSpec((tr, 128), lambda b, h, j: (b * tps + j, h))] * 2,
        out_shape=[jax.ShapeDtypeStruct((n, 512), F32)] * 2, scratch_shapes=[pltpu.VMEM((256, t), BF16)],
        compiler_params=_cparams(("parallel", "parallel", "arbitrary")),
    )(qa, ka, va)


def mla_bwd(name, qa, ka, va, o, lse, do, nb, tps, tr, nctx_rows):
    n = qa.shape[0]
    t = tps * tr

    def body(q_ref, k_ref, v_ref, o_ref, lse_ref, do_ref, dq_ref, dk_ref, dv_ref, dkt, dvt):
        @pl.when(pl.program_id(2) == 0)
        def _():
            dkt[...] = jnp.zeros_like(dkt)
            dvt[...] = jnp.zeros_like(dvt)

        def attend(nk):
            dov = do_ref[...]
            oo = dov * o_ref[...]
            dob = dov.astype(BF16)
            first = _iota(dob.shape, 1) < 64
            dqs = []
            for h in range(2):
                hs = slice(h * 128, (h + 1) * 128)
                qh, kh = q_ref[:, hs], k_ref[0:nk, hs]
                mine = first == (h == 0)
                delta = jnp.sum(jnp.where(mine, oo, 0.0), axis=-1, keepdims=True)
                doh = jnp.where(mine, dob, jnp.zeros_like(dob))
                s = lax.dot_general(qh, kh, (((1,), (1,)), ((), ())), preferred_element_type=F32)
                p = jnp.exp((s - lse_ref[:, h * 64:h * 64 + 1]).astype(BF16))
                dp = lax.dot_general(doh, v_ref[0:nk, :], (((1,), (1,)), ((), ())), preferred_element_type=F32)
                ds = p * (dp - delta).astype(BF16)
                dqs.append(jnp.dot(ds, kh, preferred_element_type=F32))
                dkt[hs, 0:nk] += lax.dot_general(qh, ds, (((0,), (0,)), ((), ())), preferred_element_type=F32)
                dvt[:, 0:nk] += lax.dot_general(doh, p, (((0,), (0,)), ((), ())), preferred_element_type=F32)
            dq_ref[...] = jnp.concatenate(dqs, axis=1)

        @pl.when(pl.program_id(2) == 0)
        def _():
            attend(nctx_rows)

        @pl.when(pl.program_id(2) > 0)
        def _():
            attend(t)

        @pl.when(pl.program_id(2) == tps - 1)
        def _():
            dk_ref[...] = dkt[...].T
            dv_ref[...] = dvt[...].T

    qtile = pl.BlockSpec((tr, 128), lambda b, h, j: (b * tps + j, h))
    return pl.pallas_call(
        body, name=name, grid=(nb, 4, tps),
        in_specs=[pl.BlockSpec((tr, 256), lambda b, h, j: (b * tps + j, h)), pl.BlockSpec((t, 256), lambda b, h, j: (b, h)),
                  pl.BlockSpec((t, 128), lambda b, h, j: (b, h)), qtile, qtile, qtile],
        out_specs=[pl.BlockSpec((tr, 256), lambda b, h, j: (b * tps + j, h)), pl.BlockSpec((t, 256), lambda b, h, j: (b, h)),
                   pl.BlockSpec((t, 128), lambda b, h, j: (b, h))],
        out_shape=[jax.ShapeDtypeStruct((n, 1024), F32), jax.ShapeDtypeStruct((n, 1024), F32),
                   jax.ShapeDtypeStruct((n, 512), F32)],
        scratch_shapes=[pltpu.VMEM((256, t), F32), pltpu.VMEM((128, t), F32)],
        compiler_params=_cparams(("parallel", "parallel", "arbitrary")),
    )(qa, ka, va, o, lse, do)


HALO = 16


def _halo_specs(tr, width, tps, nt):
    r = tr // HALO
    return [pl.BlockSpec((tr, width), lambda j, c: (j, c)),
            pl.BlockSpec((HALO, width), lambda j, c: (jnp.maximum(j * r - 1, 0), c)),
            pl.BlockSpec((HALO, width), lambda j, c: (jnp.minimum((j + 1) * r, nt * r - 1), c))]


def _shifted(u, prev, nxt, j, tps):
    tr = u.shape[0]
    t = j % tps
    has_prev = (t >= 2).astype(F32)
    has_next = jnp.logical_and(t >= 1, t <= tps - 2).astype(F32)
    rows = _iota(u.shape, 0)
    dn = jnp.where(rows == 0, prev[HALO - 1:HALO] * has_prev, pltpu.roll(u, 1, 0))
    up = jnp.where(rows == tr - 1, nxt[0:1] * has_next, pltpu.roll(u, tr - 1, 0))
    return dn, up


def _ffn_act(ucv):
    return _silu(ucv[:, :FF_CHUNK]) * ucv[:, FF_CHUNK:]


def ffn2_fwd(name, u, cw, cb, wd, x1, mod, lng, lnb, tr, tps):
    n = u.shape[0]
    nt = n // tr
    w2 = 2 * FF_CHUNK

    def body(u_ref, up_ref, un_ref, cw_ref, cb_ref, wd_ref, x1_ref, m_ref, g_ref, b_ref, f_ref, x2_ref, ucv_ref, acc):
        j, c = pl.program_id(0), pl.program_id(1)
        uu = u_ref[...].astype(F32)
        dn, up = _shifted(uu, up_ref[...].astype(F32), un_ref[...].astype(F32), j, tps)
        cwv = cw_ref[...]
        ucv = cwv[0:1] * dn + cwv[1:2] * uu + cwv[2:3] * up + cb_ref[...]
        ucv_ref[...] = ucv.astype(ucv_ref.dtype)
        part = bdot(_ffn_act(ucv), wd_ref[...])

        @pl.when(c == 0)
        def _():
            acc[...] = part

        @pl.when(c == 1)
        def _():
            f = acc[...] + part
            f_ref[...] = f
            x2_ref[...] = ln2_fn(x1_ref[...], f, m_ref[0], g_ref[...], b_ref[...])[0]

    return pl.pallas_call(
        body, name=name, grid=(nt, 2),
        in_specs=_halo_specs(tr, w2, tps, nt) + [
            pl.BlockSpec((8, w2), lambda j, c: (0, c)), pl.BlockSpec((1, w2), lambda j, c: (0, c)),
            pl.BlockSpec((FF_CHUNK, D), lambda j, c: (c, 0)), pl.BlockSpec((tr, D), lambda j, c: (j, 0)),
            pl.BlockSpec((1, 8, D), lambda j, c: (j, 0, 0)), pl.BlockSpec((1, D), lambda j, c: (0, 0)),
            pl.BlockSpec((1, D), lambda j, c: (0, 0))],
        out_specs=[pl.BlockSpec((tr, D), lambda j, c: (j, 0)), pl.BlockSpec((tr, D), lambda j, c: (j, 0)),
                   pl.BlockSpec((tr, w2), lambda j, c: (j, c))],
        out_shape=[jax.ShapeDtypeStruct((n, D), F32)] * 2 + [jax.ShapeDtypeStruct((n, 2 * w2), BF16)],
        scratch_shapes=[pltpu.VMEM((tr, D), F32)], compiler_params=_cparams(("arbitrary", "arbitrary")),
    )(u, u, u, cw, cb, wd, x1, mod, lng, lnb)


def ffn2_bwd(name, ucv, wd, df, tr):
    n = ucv.shape[0]
    nt = n // tr
    w2 = 2 * FF_CHUNK

    def body(ucv_ref, wd_ref, df_ref, ducv_ref, dwd_ref):
        j = pl.program_id(1)
        a, g = ucv_ref[:, :FF_CHUNK].astype(F32), ucv_ref[:, FF_CHUNK:].astype(F32)
        sg = jax.nn.sigmoid(a)
        sa = a * sg
        dfb = df_ref[...].astype(BF16)
        dact = lax.dot_general(dfb, wd_ref[...], (((1,), (1,)), ((), ())), preferred_element_type=F32)
        ducv_ref[:, :FF_CHUNK] = (dact * g * (sg + sa * (1.0 - sg))).astype(ducv_ref.dtype)
        ducv_ref[:, FF_CHUNK:] = (dact * sa).astype(ducv_ref.dtype)
        dwd = lax.dot_general((sa * g).astype(BF16), dfb, (((0,), (0,)), ((), ())), preferred_element_type=F32)

        @pl.when(j == 0)
        def _():
            dwd_ref[...] = jnp.zeros_like(dwd_ref)

        dwd_ref[...] += dwd

    return pl.pallas_call(
        body, name=name, grid=(2, nt),
        in_specs=[pl.BlockSpec((tr, w2), lambda c, j: (j, c)), pl.BlockSpec((FF_CHUNK, D), lambda c, j: (c, 0)),
                  pl.BlockSpec((tr, D), lambda c, j: (j, 0))],
        out_specs=[pl.BlockSpec((tr, w2), lambda c, j: (j, c)), pl.BlockSpec((FF_CHUNK, D), lambda c, j: (c, 0))],
        out_shape=[jax.ShapeDtypeStruct((n, 2 * w2), BF16), jax.ShapeDtypeStruct((D_FF, D), F32)],
        compiler_params=_cparams(("parallel", "arbitrary")),
    )(ucv, wd, df)


def conv_bwd(name, ducv, u, cw, tr, tps):
    n = u.shape[0]
    nt = n // tr
    w2 = 2 * FF_CHUNK

    def body(g_ref, gp_ref, gn_ref, u_ref, up_ref, un_ref, cw_ref, du_ref, dcw_ref, dcb_ref):
        c, j = pl.program_id(0), pl.program_id(1)
        g = g_ref[...].astype(F32)
        gdn, gup = _shifted(g, gp_ref[...].astype(F32), gn_ref[...].astype(F32), j, tps)
        uu = u_ref[...].astype(F32)
        udn, uup = _shifted(uu, up_ref[...].astype(F32), un_ref[...].astype(F32), j, tps)
        cwv = cw_ref[...]
        du_ref[...] = (cwv[0:1] * gup + cwv[1:2] * g + cwv[2:3] * gdn).astype(du_ref.dtype)
        rows = _iota((8, w2), 0)
        s = lambda z: jnp.sum(z, axis=0, keepdims=True)
        dcw = (jnp.where(rows == 0, s(g * udn), 0.0) + jnp.where(rows == 1, s(g * uu), 0.0)
               + jnp.where(rows == 2, s(g * uup), 0.0))

        @pl.when(j == 0)
        def _():
            dcw_ref[...] = jnp.zeros_like(dcw_ref)
            dcb_ref[...] = jnp.zeros_like(dcb_ref)

        dcw_ref[...] += dcw
        dcb_ref[...] += s(g)

    hs = _halo_specs(tr, w2, tps, nt)
    swap = lambda spec: pl.BlockSpec(spec.block_shape, lambda c, j, f=spec.index_map: f(j, c))
    return pl.pallas_call(
        body, name=name, grid=(2, nt),
        in_specs=[swap(s) for s in hs] * 2 + [pl.BlockSpec((8, w2), lambda c, j: (0, c))],
        out_specs=[pl.BlockSpec((tr, w2), lambda c, j: (j, c)), pl.BlockSpec((8, w2), lambda c, j: (0, c)),
                   pl.BlockSpec((1, w2), lambda c, j: (0, c))],
        out_shape=[jax.ShapeDtypeStruct((n, 2 * w2), BF16), jax.ShapeDtypeStruct((8, 2 * w2), F32),
                   jax.ShapeDtypeStruct((1, 2 * w2), F32)],
        compiler_params=_cparams(("parallel", "arbitrary")),
    )(ducv, ducv, ducv, u, u, u, cw)


def loss_head(name, xf, target, nb, tps, tr):
    n = xf.shape[0]

    def body(x_ref, t_ref, dy_ref, l_ref):
        lat = (pl.program_id(0) % tps > 0).astype(F32)
        err = (x_ref[...] - t_ref[...]) * lat
        dy_ref[...] = err * (1.0 / D)
        l_ref[...] = jnp.zeros_like(l_ref) + 0.5 * jnp.sum(err * err) * (1.0 / D)

    def tmap(j):
        return ((j // tps) * (tps - 1) + jnp.maximum(j % tps - 1, 0), 0)

    return pl.pallas_call(
        body, name=name, grid=(n // tr,),
        in_specs=[pl.BlockSpec((tr, D), lambda j: (j, 0)), pl.BlockSpec((tr, D), tmap)],
        out_specs=[pl.BlockSpec((tr, D), lambda j: (j, 0)), pl.BlockSpec((1, 8, 128), lambda j: (j, 0, 0))],
        out_shape=[jax.ShapeDtypeStruct((n, D), F32), jax.ShapeDtypeStruct((n // tr, 8, 128), F32)],
        compiler_params=_cparams(("arbitrary",)),
    )(xf, target)


ADAM_MAX_ROWS = 512


def adamw(name, w, m, v, g8):
    r, c = w.shape
    k = g8.shape[0]
    rows = max(b for b in range(8, ADAM_MAX_ROWS + 1, 8) if r % b == 0)
    bc1 = 1.0 - ADAM_B1 ** ADAM_STEP
    bc2 = 1.0 - ADAM_B2 ** ADAM_STEP

    def body(w_ref, m_ref, v_ref, g_ref, go_ref, d_ref, mo_ref, vo_ref):
        g = g_ref[0].astype(F32)
        for i in range(1, k):
            g = g + g_ref[i].astype(F32)
        mn = ADAM_B1 * m_ref[...] + (1.0 - ADAM_B1) * g
        vn = ADAM_B2 * v_ref[...] + (1.0 - ADAM_B2) * (g * g)
        go_ref[...] = g
        mo_ref[...] = mn
        vo_ref[...] = vn
        d_ref[...] = -ADAM_LR * ((mn / bc1) / (jnp.sqrt(vn / bc2) + ADAM_EPS) + ADAM_WD * w_ref[...])

    blk = pl.BlockSpec((rows, c), lambda i: (i, 0))
    return pl.pallas_call(
        body, name=name, grid=(r // rows,),
        in_specs=[blk, blk, blk, pl.BlockSpec((k, rows, c), lambda i: (0, i, 0))],
        out_specs=[blk] * 4, out_shape=[jax.ShapeDtypeStruct((r, c), F32)] * 4,
        compiler_params=_cparams(("parallel",)),
    )(w, m, v, g8)


def ada_fwd(name, s, aw, ab):
    nl, _, cw = aw.shape

    def body(s_ref, w_ref, b_ref, o_ref):
        o_ref[0] = hdot(s_ref[...], w_ref[0]) + b_ref[0]

    return pl.pallas_call(
        body, name=name, grid=(nl,),
        in_specs=[pl.BlockSpec(s.shape, lambda l: (0, 0)), pl.BlockSpec((1, D, cw), lambda l: (l, 0, 0)),
                  pl.BlockSpec((1, 1, cw), lambda l: (l, 0, 0))],
        out_specs=pl.BlockSpec((1, s.shape[0], cw), lambda l: (l, 0, 0)),
        out_shape=jax.ShapeDtypeStruct((nl, s.shape[0], cw), F32), compiler_params=_cparams(("arbitrary",)),
    )(s, aw, ab)


def ada_bwd(name, s, aw, dmod):
    nl, _, cw = aw.shape

    def body(s_ref, w_ref, d_ref, dw_ref, ds_ref):
        dw_ref[0] = lax.dot_general(s_ref[...], d_ref[0], (((0,), (0,)), ((), ())), precision=HI,
                                    preferred_element_type=F32)
        ds_ref[0] = lax.dot_general(d_ref[0], w_ref[0], (((1,), (1,)), ((), ())), precision=HI,
                                    preferred_element_type=F32)

    return pl.pallas_call(
        body, name=name, grid=(nl,),
        in_specs=[pl.BlockSpec(s.shape, lambda l: (0, 0)), pl.BlockSpec((1, D, cw), lambda l: (l, 0, 0)),
                  pl.BlockSpec((1, s.shape[0], cw), lambda l: (l, 0, 0))],
        out_specs=[pl.BlockSpec((1, D, cw), lambda l: (l, 0, 0)), pl.BlockSpec((1, s.shape[0], D), lambda l: (l, 0, 0))],
        out_shape=[jax.ShapeDtypeStruct((nl, D, cw), F32), jax.ShapeDtypeStruct((nl, s.shape[0], D), F32)],
        compiler_params=_cparams(("arbitrary",)),
    )(s, aw, dmod)


def _place():
    return lax.axis_index("x"), lax.axis_index("y"), lax.axis_index("c")


def all_gather(name, x, in_vmem):
    r, c = x.shape

    def body(x_ref, out_ref, send_sems, recv_sems, local_sem):
        px, py, pc = _place()
        me, sibling = (px, py, pc), (px, py, 1 - pc)
        chips = [(1 - px, py), (px, 1 - py), (1 - px, 1 - py)]

        def rows(qx, qy, qc):
            return out_ref.at[pl.ds((4 * qx + 2 * qy + qc) * r, r), :]

        def copy(k, block, to, src=None):
            return pltpu.make_async_remote_copy(
                src_ref=rows(*block) if src is None else src, dst_ref=rows(*block),
                send_sem=send_sems.at[k], recv_sem=recv_sems.at[k], device_id=to, device_id_type=MESH)

        mine = pltpu.make_async_copy(x_ref, rows(*me), local_sem)
        mine.start()
        first = [copy(0, me, sibling, src=x_ref)]
        first += [copy(1 + j, me, (*chip, pc), src=x_ref) for j, chip in enumerate(chips)]
        for cp in first:
            cp.start()
        passed = [copy(4 + j, (*chip, pc), sibling) for j, chip in enumerate(chips)]
        for j, chip in enumerate(chips):
            copy(1 + j, (*chip, pc), me).wait_recv()
            passed[j].start()
        copy(0, sibling, me).wait_recv()
        for j, chip in enumerate(chips):
            copy(4 + j, (*chip, 1 - pc), me).wait_recv()
        for cp in first + passed:
            cp.wait_send()
        mine.wait()

    space = pltpu.VMEM if in_vmem else pl.ANY
    return pl.pallas_call(
        body, name=name, out_shape=jax.ShapeDtypeStruct((N_DEV * r, c), x.dtype),
        in_specs=[pl.BlockSpec(memory_space=space)], out_specs=pl.BlockSpec(memory_space=space),
        scratch_shapes=[pltpu.SemaphoreType.DMA((7,)), pltpu.SemaphoreType.DMA((7,)), pltpu.SemaphoreType.DMA],
        compiler_params=pltpu.CompilerParams(vmem_limit_bytes=VMEM_LIMIT_BYTES),
    )(x)


_HBM = pl.BlockSpec(memory_space=pltpu.HBM)
_SEM = pl.BlockSpec(memory_space=pltpu.SEMAPHORE)
_EFFECT = pltpu.SideEffectType.DATAFLOW_SIDE_EFFECTING


def _partner(k):
    px, py, pc = _place()
    q = (px ^ (k >> 2 & 1), py ^ (k >> 1 & 1), pc ^ (k & 1))
    return q, 4 * q[0] + 2 * q[1] + q[2]


def xchg_start(name, x, per_peer):
    r, c = x.shape[-2:]

    def body(x_ref, land_ref, send_sems, recv_sems, x_thru, land_thru, token):
        px, py, pc = _place()
        my = 4 * px + 2 * py + pc
        for k in range(1, N_DEV):
            q, qi = _partner(k)
            pltpu.make_async_remote_copy(
                src_ref=x_ref.at[qi] if per_peer else x_ref, dst_ref=land_ref.at[my],
                send_sem=send_sems.at[k - 1], recv_sem=recv_sems.at[k - 1], device_id=q, device_id_type=MESH).start()
        token[...] = jnp.zeros_like(token)

    land = lax.empty((N_DEV, r, c), x.dtype)
    return pl.pallas_call(
        body, name=name,
        out_shape=(pltpu.SemaphoreType.DMA((N_DEV - 1,)), pltpu.SemaphoreType.DMA((N_DEV - 1,)),
                   pltpu.HBM(x.shape, x.dtype), pltpu.HBM(land.shape, land.dtype), jax.ShapeDtypeStruct((8, 128), F32)),
        in_specs=(_HBM, _HBM), out_specs=(_SEM, _SEM, _HBM, _HBM, pl.BlockSpec(memory_space=pltpu.VMEM)),
        input_output_aliases={0: 2, 1: 3}, compiler_params=pltpu.CompilerParams(has_side_effects=_EFFECT),
    )(pltpu.with_memory_space_constraint(x, pltpu.HBM), pltpu.with_memory_space_constraint(land, pltpu.HBM))


def xchg_wait(name, send_sems, recv_sems, x_thru, land_thru, after, per_peer):
    def body(x_ref, land_ref, send_sems, recv_sems, after_ref, x_out, land_out):
        for k in range(1, N_DEV):
            q, qi = _partner(k)
            cp = pltpu.make_async_remote_copy(
                src_ref=x_ref.at[qi] if per_peer else x_ref, dst_ref=land_ref.at[qi],
                send_sem=send_sems.at[k - 1], recv_sem=recv_sems.at[k - 1], device_id=q, device_id_type=MESH)
            cp.wait_send()
            cp.wait_recv()

    return pl.pallas_call(
        body, name=name,
        out_shape=(pltpu.HBM(x_thru.shape, x_thru.dtype), pltpu.HBM(land_thru.shape, land_thru.dtype)),
        in_specs=(_HBM, _HBM, _SEM, _SEM, pl.BlockSpec(memory_space=pl.ANY)), out_specs=(_HBM, _HBM),
        input_output_aliases={0: 0, 1: 1}, compiler_params=pltpu.CompilerParams(has_side_effects=_EFFECT),
    )(x_thru, land_thru, send_sems, recv_sems, after)


def _tables(seq, nctx_rows):
    f32 = np.float32
    pos = np.arange(seq, dtype=f32)
    ret_inv = (1.0 / (ROPE_BASE ** np.linspace(0.0, 1.0, 16, dtype=f32))).astype(f32)
    ang = pos[:, None] * ret_inv
    rc, rs = np.cos(ang).astype(f32), np.sin(ang).astype(f32)
    rcos = np.tile(np.concatenate([rc, rc], 1), (1, 4))
    rsin = np.tile(np.concatenate([-rs, rs], 1), (1, 4))
    rows = np.repeat(np.arange(seq // 64, dtype=f32), 64)
    cols = np.tile(np.arange(64, dtype=f32), seq // 64)
    ax_inv = (ROPE_BASE ** (-np.arange(8, dtype=f32) / 8)).astype(f32)
    ra, ca = rows[:, None] * ax_inv, cols[:, None] * ax_inv
    one, zero = np.ones((seq, 64), f32), np.zeros((seq, 64), f32)
    mcos = np.concatenate([one, np.cos(ra), np.cos(ra), np.cos(ca), np.cos(ca), one[:, :32]], 1)
    msin = np.concatenate([zero, -np.sin(ra), np.sin(ra), -np.sin(ca), np.sin(ca), zero[:, :32]], 1)
    ident = lambda t, v: np.concatenate([np.full((nctx_rows, 128), v, f32), t.astype(f32)], 0)
    return [jnp.asarray(ident(rcos, 1.0)), jnp.asarray(ident(rsin, 0.0)),
            jnp.asarray(ident(mcos, 1.0)), jnp.asarray(ident(msin, 0.0))]


def _prep_layer(w, l):
    z = lambda *s: jnp.zeros(s, F32)
    p = {}
    win = _rows(w["w_in_t"][l], W_IN_SEGS)
    p["w_in_t"] = jnp.concatenate([win, jnp.zeros((P_PAD - D_IN, D), win.dtype)], axis=0)
    p["w_up_t"] = _rows(w["ffn_up_t"][l], FF_SEGS)
    p["w_down"] = w["ffn_down"][l]
    p["w_out"] = w["w_out"][l]
    p["wuq"] = _pad_heads(w["mla_w_uq_t"][l], 96)
    p["wuk"] = _pad_heads(w["mla_w_uk_t"][l], 64)
    p["wuv"] = w["mla_w_uv_t"][l]
    gw = w["gla_gate_w"][l]
    p["w2f"] = z(128, 128).at[0:16].set(gw[0])
    p["w2b"] = z(128, 128).at[16:32].set(gw[1])
    p["b2f"], p["b2b"] = w["gla_gate_b"][l][0:1], w["gla_gate_b"][l][1:2]
    lg = jax.nn.log_sigmoid(w["ret_decay"][l])
    p["retf"], p["retb"] = jnp.repeat(lg[0], 32)[None], jnp.repeat(lg[1], 32)[None]
    p["qg"], p["kvg"] = w["mla_q_norm_g"][l][None], w["mla_kv_norm_g"][l][None]
    p["gng"] = jnp.tile(w["gla_norm_g"][l], 4)[None]
    p["ln1g"], p["ln1b"] = w["ln1_g"][l][None], w["ln1_b"][l][None]
    p["ln2g"], p["ln2b"] = w["ln2_g"][l][None], w["ln2_b"][l][None]
    p["cw"] = jnp.concatenate([_cols(w["ffn_conv_w"][l], FF_SEGS), z(5, 2 * D_FF)], axis=0)
    p["cb"] = _cols(w["ffn_conv_b"][l], FF_SEGS)[None]
    e2 = np.zeros((128, 1024), np.float32)
    for h in range(8):
        e2[32 + np.arange(32), h * 128 + 64 + np.arange(32)] = 1.0
    p["e2"] = jnp.asarray(e2)
    return p


def _pre_ins(pa, tabs, p):
    row = lambda w, cb: ("row", pa, w, cb)
    return [row(128, 0), row(128, 6), row(128, 7), row(256, 6), row(128, 14), row(128, 15)] + \
           [("pos", t, 128, 0) for t in tabs] + \
           [("par", p[k], 0, 0) for k in ("w2f", "w2b", "b2f", "b2b", "retf", "retb", "qg", "kvg", "wuq", "wuk", "wuv", "e2")]


_PRE_OUTS = [(128, F32)] * 7 + [(1024, BF16), (1024, BF16), (512, BF16)]
_PRE_WANT = [(i, F32) for i in range(6)] + [(i, F32) for i in range(10, 21)]


def _post_ins(ogf, ogb, orf, orb, om, pa, x, mod, p):
    return [("row", ogf, 256, 0), ("row", ogb, 256, 0), ("row", orf, 256, 0), ("row", orb, 256, 0),
            ("row", om, 512, 0), ("row", pa, 256, 2), ("row", pa, 256, 5), ("row", x, D, 0), ("tile", mod, 0, 0),
            ("par", p["gng"], 0, 0), ("par", p["w_out"], 0, 0), ("par", p["ln1g"], 0, 0), ("par", p["ln1b"], 0, 0)]


def layer_fwd(l, x, mod, p, tabs, dims):
    nb, tps, tr, nch, nctx = dims
    n = x.shape[0]
    pa = mm("proj", x, p["w_in_t"], F32, P_PAD, tr, mod=mod, sel=(1, 0))
    gq, af, ab, arf, arb, rq, rk, qa, ka, va = tile_fwd("mix_pre", pre_fn, _pre_ins(pa, tabs, p), _PRE_OUTS, n, tr, tps)
    ogf, ogb, gstf, gstb = scan_fwd("gla_scan", (gq, 128, 0), (pa, 128, 1), (pa, 256, 1), af, ab, nb, nch, nctx)
    orf, orb, rstf, rstb = scan_fwd("ret_scan", (rq, 128, 0), (rk, 128, 0), (pa, 256, 4), arf, arb, nb, nch, nctx)
    om, lse = mla_fwd("mla_attn", qa, ka, va, nb, tps, tr, nctx * CHUNK)
    (x1,) = tile_fwd("mix_post", post_fn, _post_ins(ogf, ogb, orf, orb, om, pa, x, mod, p), [(D, F32)], n, tr, tps)
    u = mm("ffn_up", x1, p["w_up_t"], BF16, 2 * D_FF, tr, mod=mod, sel=(4, 3))
    f, x2, ucv = ffn2_fwd("ffn_down", u, p["cw"], p["cb"], p["w_down"], x1, mod, p["ln2g"], p["ln2b"], tr, tps)
    saved = dict(x=x, pa=pa, gq=gq, af=af, ab=ab, arf=arf, arb=arb, rq=rq, rk=rk, qa=qa, ka=ka, va=va,
                 ogf=ogf, ogb=ogb, gstf=gstf, gstb=gstb, orf=orf, orb=orb, rstf=rstf, rstb=rstb, om=om, lse=lse,
                 x1=x1, u=u, ucv=ucv, f=f)
    return x2, saved


def layer_bwd(l, dx2, s, mod, p, tabs, dims):
    nb, tps, tr, nch, nctx = dims
    n = dx2.shape[0]
    g = {}
    ln2_ins = [("row", s["x1"], D, 0), ("row", s["f"], D, 0), ("tile", mod, 0, 0),
               ("par", p["ln2g"], 0, 0), ("par", p["ln2b"], 0, 0)]
    dx1a, df, dmod_a, g["ln2g"], g["ln2b"] = tile_bwd(
        "ln2_bwd", ln2_fn, ln2_ins, [dx2], [(0, F32), (1, F32), (2, F32), (3, F32), (4, F32)], n, tr, tps)
    ducv, g["w_down"] = ffn2_bwd("ffn_down_bwd", s["ucv"], p["w_down"], df, tr)
    du, g["cw"], g["cb"] = conv_bwd("conv_bwd", ducv, s["u"], p["cw"], tr, tps)
    g["w_up_t"] = mm_tn("ffn_up_dw", du, s["x1"], D_FF, tr, mod=mod, sel=(4, 3))
    dx1, dmod_b = mm_modbwd("ffn_up_dx", du, p["w_up_t"], s["x1"], mod, dx1a, (4, 3), tr)

    post_ins = _post_ins(s["ogf"], s["ogb"], s["orf"], s["orb"], s["om"], s["pa"], s["x"], mod, p)
    want = [(0, F32), (2, F32), (4, F32), (5, F32), (6, F32), (7, F32), (8, F32), (9, F32), (10, F32), (11, F32), (12, F32)]
    dog, dor, dom, dgg, drg, dxa, dmod_c, g["gng"], g["w_out"], g["ln1g"], g["ln1b"] = tile_bwd(
        "mix_post_bwd", post_fn, post_ins, [dx1], want, n, tr, tps)
    dqa, dka, dva = mla_bwd("mla_attn_bwd", s["qa"], s["ka"], s["va"], s["om"], s["lse"], dom, nb, tps, tr,
                            nctx * CHUNK)
    pa = s["pa"]
    gdqf, gdkf, gdvf, gdaf, gdqb, gdkb, gdvb, gdab = scan_bwd(
        "gla_scan_bwd", (s["gq"], 128, 0), (pa, 128, 1), (pa, 256, 1), s["af"], s["ab"], s["gstf"], s["gstb"], dog,
        nb, nch, nctx)
    rdqf, rdkf, rdvf, rdaf, rdqb, rdkb, rdvb, rdab = scan_bwd(
        "ret_scan_bwd", (s["rq"], 128, 0), (s["rk"], 128, 0), (pa, 256, 4), s["arf"], s["arb"], s["rstf"], s["rstb"],
        dor, nb, nch, nctx)

    pre_ins = _pre_ins(pa, tabs, p)
    extra = [gdqf, gdqb, rdqf, rdqb, rdkf, rdkb, gdkf, gdkb, gdvf, gdvb, rdvf, rdvb, dgg, drg]
    kinds = [i[0] for i in pre_ins]
    widx = [w[0] for w in _PRE_WANT]
    npre = len(pre_ins)

    def body(*refs):
        vals = [_load(k, r) for k, r in zip(kinds, refs[:npre])]
        rd = lambda i: refs[npre + i][...].astype(F32)
        cots = (rd(0) + rd(1), rd(14), rd(15), rd(16), rd(17), rd(2) + rd(3), rd(4) + rd(5), rd(18), rd(19), rd(20))

        def f(*dv):
            full = list(vals)
            for i, v in zip(widx, dv):
                full[i] = v
            return tuple(pre_fn(*full))

        _, vjp = jax.vjp(f, *[vals[i] for i in widx])
        grads = vjp(cots)
        dgq, drq, drk, dcq, dckv, dmisc = grads[:6]
        dp = jnp.concatenate([dgq, rd(6) + rd(7), rd(8) + rd(9), rd(12), drq, drk, rd(10) + rd(11), rd(13),
                              dcq, dckv, dmisc], axis=1)
        outs = refs[npre + 21:]
        outs[0][...] = dp.astype(BF16)
        first = pl.program_id(0) == 0
        for r, gr in zip(outs[1:], grads[6:]):
            @pl.when(first)
            def _(r=r):
                r[...] = jnp.zeros_like(r)
            r[...] += gr

    cot_arrays = extra + [gdaf, gdab, rdaf, rdab, dqa, dka, dva]
    par_arrays = [pre_ins[i][1] for i in range(10, 21)]
    res = pl.pallas_call(
        body, name="mix_pre_bwd", grid=(n // tr,),
        in_specs=[_spec(k, a, w, cb, tr, tps) for k, a, w, cb in pre_ins]
        + [pl.BlockSpec((tr, c.shape[1]), lambda j: (j, 0)) for c in cot_arrays],
        out_specs=[pl.BlockSpec((tr, P_PAD), lambda j: (j, 0))] + [pl.BlockSpec(a.shape, lambda j: (0, 0)) for a in par_arrays],
        out_shape=[jax.ShapeDtypeStruct((n, P_PAD), BF16)] + [jax.ShapeDtypeStruct(a.shape, F32) for a in par_arrays],
        compiler_params=_cparams(("arbitrary",)),
    )(*[i[1] for i in pre_ins], *cot_arrays)
    dp = res[0]
    for k, v in zip(("w2f", "w2b", "b2f", "b2b", "retf", "retb", "qg", "kvg", "wuq", "wuk", "wuv"), res[1:]):
        g[k] = v
    g["w_in_t"] = mm_tn("proj_dw", dp, s["x"], P_PAD, tr, mod=mod, sel=(1, 0))
    dx, dmod_d = mm_modbwd("proj_dx", dp, p["w_in_t"], s["x"], mod, dxa, (1, 0), tr)
    return dx, dmod_a + dmod_b + dmod_c + dmod_d, g


def _unprep_grads(g, w, l):
    o = {}
    o["w_in_t"] = _rows(g["w_in_t"], W_IN_INV_SEGS)
    o["ffn_up_t"] = _rows(g["w_up_t"], FF_SEGS)
    o["ffn_down"] = g["w_down"]
    o["w_out"] = g["w_out"]
    o["mla_w_uq_t"] = _unpad_heads(g["wuq"], 96)
    o["mla_w_uk_t"] = _unpad_heads(g["wuk"], 64)
    o["mla_w_uv_t"] = g["wuv"]
    o["gla_gate_w"] = jnp.stack([g["w2f"][0:16], g["w2b"][16:32]])
    o["gla_gate_b"] = jnp.concatenate([g["b2f"], g["b2b"]], axis=0)
    dlg = jnp.stack([g["retf"].reshape(4, 32).sum(-1), g["retb"].reshape(4, 32).sum(-1)])
    o["ret_decay"] = dlg * jax.nn.sigmoid(-w["ret_decay"][l])
    o["mla_q_norm_g"], o["mla_kv_norm_g"] = g["qg"][0], g["kvg"][0]
    o["gla_norm_g"] = g["gng"].reshape(4, 64).sum(0)
    o["ln1_g"], o["ln1_b"], o["ln2_g"], o["ln2_b"] = g["ln1g"][0], g["ln1b"][0], g["ln2g"][0], g["ln2b"][0]
    o["ffn_conv_w"] = _cols(g["cw"][0:3], FF_SEGS)
    o["ffn_conv_b"] = _cols(g["cb"][0], FF_SEGS)
    return o


def local_step(xs, target, modtab, layer_weights, dims, grads_ready=None):
    nb, tps, tr, nch, nctx = dims
    tabs = _tables((tps - 1) * tr, tr)
    x = xs
    saved, preps, ws = [], [], []
    for l in range(DEPTH):
        w = layer_weights(l, x)
        p = _prep_layer(w, 0)
        x, s = layer_fwd(l, x, modtab[l], p, tabs, dims)
        saved.append(s)
        preps.append(p)
        ws.append(w)
    dy, lpart = loss_head("loss_head", x, target, nb, tps, tr)
    loss = jnp.sum(lpart[:, 0, 0])
    dx = dy
    dmods, grads = [None] * DEPTH, [None] * DEPTH
    tok = None
    for l in reversed(range(DEPTH)):
        mod = modtab[l] if tok is None else modtab[l] + tok
        dx, dmods[l], g = layer_bwd(l, dx, saved[l], mod, preps[l], tabs, dims)
        grads[l] = _unprep_grads(g, ws[l], 0)
        tok = grads_ready(l, grads) if grads_ready is not None else None
    return loss, dx, jnp.stack(dmods), grads


BIG = [("ffn_up", 2), ("ffn_down", 1), ("w_out", 1), ("w_in", 2), ("mla_w_uq", 2), ("mla_w_uk", 2), ("mla_w_uv", 2)]
SMALL = ["ada_b", "gla_gate_w", "gla_gate_b", "gla_norm_g", "ret_decay", "mla_q_norm_g", "mla_kv_norm_g",
         "ln1_g", "ln1_b", "ffn_conv_b", "ln2_g", "ln2_b"]
PACK_C = 1024


def _big_key(k, axis):
    return k + "_t" if axis == 2 else k


def sum8(name, g8):
    k, r, c = g8.shape
    rows = max(b for b in range(16, ADAM_MAX_ROWS + 1, 16) if r % b == 0)

    def body(g_ref, o_ref):
        g = g_ref[0].astype(F32)
        for i in range(1, k):
            g = g + g_ref[i].astype(F32)
        o_ref[...] = g

    return pl.pallas_call(
        body, name=name, grid=(r // rows,), in_specs=[pl.BlockSpec((k, rows, c), lambda i: (0, i, 0))],
        out_specs=pl.BlockSpec((rows, c), lambda i: (i, 0)), out_shape=jax.ShapeDtypeStruct((r, c), F32),
        compiler_params=_cparams(("parallel",)),
    )(g8)


def _pack(arrs, dtype):
    flat = jnp.concatenate([a.reshape(-1).astype(dtype) for a in arrs])
    pad = (-flat.shape[0]) % (8 * PACK_C)
    return jnp.concatenate([flat, jnp.zeros((pad,), dtype)]).reshape(-1, PACK_C)


def _unpack(flat2d, shapes):
    flat = flat2d.reshape(-1)
    out, off = [], 0
    for s in shapes:
        sz = int(np.prod(s))
        out.append(flat[off:off + sz].reshape(s))
        off += sz
    return out


def _tile_pad(a):
    pad = (-a.shape[-2]) % HALO
    return a if pad == 0 else jnp.concatenate([a, jnp.zeros(a.shape[:-2] + (pad, a.shape[-1]), a.dtype)], axis=-2)


def kernel(x, c, ctx, c_ctx, ada_w, ada_b, w_in, gla_gate_w, gla_gate_b, gla_norm_g, ret_decay, mla_q_norm_g, mla_kv_norm_g, mla_w_uq, mla_w_uk, mla_w_uv, w_out, ln1_g, ln1_b, ffn_up, ffn_conv_w, ffn_conv_b, ffn_down, ln2_g, ln2_b, loss_target, m_c_ctx, m_ada_w, m_ada_b, m_w_in, m_gla_gate_w, m_gla_gate_b, m_gla_norm_g, m_ret_decay, m_mla_q_norm_g, m_mla_kv_norm_g, m_mla_w_uq, m_mla_w_uk, m_mla_w_uv, m_w_out, m_ln1_g, m_ln1_b, m_ffn_up, m_ffn_conv_w, m_ffn_conv_b, m_ffn_down, m_ln2_g, m_ln2_b, v_c_ctx, v_ada_w, v_ada_b, v_w_in, v_gla_gate_w, v_gla_gate_b, v_gla_norm_g, v_ret_decay, v_mla_q_norm_g, v_mla_kv_norm_g, v_mla_w_uq, v_mla_w_uk, v_mla_w_uv, v_w_out, v_ln1_g, v_ln1_b, v_ffn_up, v_ffn_conv_w, v_ffn_conv_b, v_ffn_down, v_ln2_g, v_ln2_b):
    names = ["c_ctx", "ada_w", "ada_b", "w_in", "gla_gate_w", "gla_gate_b", "gla_norm_g", "ret_decay", "mla_q_norm_g",
             "mla_kv_norm_g", "mla_w_uq", "mla_w_uk", "mla_w_uv", "w_out", "ln1_g", "ln1_b", "ffn_up", "ffn_conv_w",
             "ffn_conv_b", "ffn_down", "ln2_g", "ln2_b"]
    loc = locals()
    W = {k: loc[k] for k in names}
    M = {k: loc["m_" + k] for k in names}
    V = {k: loc["v_" + k] for k in names}

    nb, seq, _ = x.shape
    tr = ctx.shape[1]
    tps = 1 + seq // tr
    t = tps * tr
    n = nb * t
    nt = nb * tps
    dims = (nb, tps, tr, t // CHUNK, tr // CHUNK)
    px, py, pc = _place()
    me = 4 * px + 2 * py + pc
    ncol = ada_w.shape[2]

    cw_loc = ffn_conv_w.reshape(-1)
    g1 = jnp.concatenate([c.reshape(-1), cw_loc])
    g1 = jnp.concatenate([g1, jnp.zeros(((-g1.shape[0]) % (8 * PACK_C),), F32)]).reshape(-1, PACK_C)
    r1 = g1.shape[0]
    g1a = all_gather("gather_cond", g1, True).reshape(N_DEV, -1)
    c_all = g1a[:, :nb * D].reshape(N_DEV * nb, D)
    cw_all = g1a[:, nb * D:nb * D + cw_loc.shape[0]].reshape(N_DEV, DEPTH, 3, -1).transpose(1, 2, 0, 3).reshape(DEPTH, 3, -1)

    first, rest = [0], list(range(1, DEPTH))
    row_form = {k: (jnp.swapaxes(W[k], 1, 2) if ax == 2 else W[k]).astype(BF16) for k, ax in BIG}

    def part_rows(k):
        rows = int(np.prod(W[k].shape[1:])) // PACK_C
        return rows, -(-rows // HALO) * HALO

    def pack_rows(ls):
        return jnp.concatenate([_tile_pad(row_form[k][l].reshape(-1, PACK_C)) for k, _ in BIG for l in ls], axis=0)

    def whole_weights(wall, ls):
        out, off = {}, 0
        for k, ax in BIG:
            rows, padded = part_rows(k)
            _, r, c = row_form[k].shape
            out[_big_key(k, ax)] = [wall[:, off + i * padded:off + i * padded + rows].reshape(1, N_DEV * r, c)
                                    for i in range(len(ls))]
            off += len(ls) * padded
        return out

    pack0 = pack_rows(first)
    whole0 = whole_weights(all_gather("gather_weights0", pack0, False).reshape(N_DEV, -1, PACK_C), first)
    pack_rest = pack_rows(rest)
    wsend, wrecv, wsrc, wland, wtok = xchg_start("gather_weights_start", pack_rest, False)
    small_w = {k: W[k] for k in SMALL[1:]}
    small_w["ffn_conv_w"] = cw_all
    later = {}

    def layer_weights(l, xin):
        if l >= 1 and not later:
            src, land = xchg_wait("gather_weights_wait", wsend, wrecv, wsrc, wland, xin, False)
            later.update(whole_weights(lax.dynamic_update_slice(land, src[None], (me, 0, 0)), rest))
        big = {k: v[0] for k, v in whole0.items()} if l == 0 else {k: v[l - 1] for k, v in later.items()}
        return {**big, **{k: v[l:l + 1] for k, v in small_w.items()}}

    srows = 40
    s_in = jnp.concatenate([c_all, c_ctx[None], jnp.zeros((srows - N_DEV * nb - 1, D), F32)], axis=0)
    s_act = _silu(s_in)
    ab_loc = lax.dynamic_slice_in_dim(ada_b, me * ncol, ncol, axis=1)[:, None, :]
    mod_part = ada_fwd("ada_fwd", s_act, ada_w, ab_loc)
    mod_all = all_gather("gather_mod", mod_part.reshape(-1, ncol), True).reshape(N_DEV, DEPTH, srows, ncol)
    mod_rows = mod_all.transpose(1, 2, 0, 3).reshape(DEPTH, srows, N_DEV * ncol)
    mod_l = lax.dynamic_slice_in_dim(mod_rows, me * nb, nb, axis=1).reshape(DEPTH, nb, 6, D)
    mod_c = mod_rows[:, N_DEV * nb].reshape(DEPTH, 1, 6, D)
    tile_is_ctx = (jnp.arange(tps) == 0)[None, None, :, None, None]
    modtab = jnp.where(tile_is_ctx, mod_c[:, :, None], mod_l[:, :, None])
    modtab = jnp.concatenate([modtab, jnp.zeros((DEPTH, nb, tps, 2, D), F32)], axis=3).reshape(DEPTH, nt, 8, D)
    modtab = modtab + wtok[0, 0]

    def grad_blocks(grads, ls):
        return jnp.concatenate(
            [_tile_pad(grads[l][_big_key(k, ax)].astype(BF16).reshape(N_DEV, -1, PACK_C)) for k, ax in BIG for l in ls],
            axis=1)

    early = {}

    def grads_ready(l, grads):
        if l != 1:
            return None
        early["sems"] = xchg_start("grad_exchange_start", grad_blocks(grads, rest), True)
        return early["sems"][4][0, 0]

    xs = jnp.concatenate([ctx, x], axis=1).reshape(n, D)
    loss_loc, dxs, dmodtab, grads = local_step(xs, loss_target.reshape(nb * seq, D), modtab, layer_weights, dims,
                                               grads_ready)
    loss = lax.psum(loss_loc, ("x", "y", "c"))
    grad_x = dxs.reshape(nb, t, D)[:, tr:]
    gl = {k: jnp.stack([g[k] for g in grads]) for k in grads[0] if k in SMALL or k == "ffn_conv_w"}

    dm = dmodtab.reshape(DEPTH, nb, tps, 8, D)[:, :, :, :6]
    dmod_l = dm[:, :, 1:].sum(2).reshape(DEPTH, nb, 6 * D)
    dmod_c = dm[:, :, 0].sum(1).reshape(DEPTH, 1, 6 * D)
    gl["ada_b"] = dmod_l.sum(1) + dmod_c[:, 0]
    small_list = [gl[k] for k in SMALL] + [gl["ffn_conv_w"]]
    small_shapes = [a.shape for a in small_list]
    fsend, frecv, fsrc, fland, ftok = xchg_start("grad_exchange0_start", grad_blocks(grads, first), True)
    spack = _pack(small_list + [jnp.concatenate([dmod_l, dmod_c], axis=1)], F32) + ftok[0, 0]
    rs = spack.shape[0]
    sall = all_gather("gather_small_grads", spack, True).reshape(N_DEV, rs, PACK_C)
    nsmall = sum(int(np.prod(s)) for s in small_shapes)
    dmo = sall.reshape(N_DEV, -1)[:, nsmall:nsmall + DEPTH * (nb + 1) * 6 * D].reshape(N_DEV, DEPTH, nb + 1, 6 * D)
    dl_all = dmo[:, :, :nb].transpose(1, 0, 2, 3).reshape(DEPTH, N_DEV * nb, 6 * D)
    dc_all = dmo[:, :, nb].sum(0)[:, None]
    dmod_rows = jnp.concatenate([dl_all, dc_all, jnp.zeros((DEPTH, srows - N_DEV * nb - 1, 6 * D), F32)], axis=1)
    dmod_loc = lax.dynamic_slice_in_dim(dmod_rows.reshape(DEPTH, srows, N_DEV, ncol), me, 1, axis=2)[:, :, 0]
    d_ada_w, d_s = ada_bwd("ada_bwd", s_act, ada_w, dmod_loc)
    sg = jax.nn.sigmoid(c_ctx)
    dcc = d_s[:, N_DEV * nb].sum(0) * (sg * (1.0 + c_ctx * (1.0 - sg)))
    ccp = jnp.concatenate([dcc[None], jnp.zeros((7, D), F32)], axis=0)
    ccall = all_gather("gather_cctx", ccp, True).reshape(N_DEV, 8, D)

    esend, erecv, esrc, eland, _ = early["sems"]
    def landed(tag, sems, src, land, after):
        src, land = xchg_wait(tag + "_wait", sems[0], sems[1], src, land, after, True)
        mine = lax.dynamic_slice_in_dim(src, me, 1, axis=0)
        return sum8(tag + "_sum", lax.dynamic_update_slice(land, mine, (me, 0, 0)))

    gsum_rest = landed("grad_exchange", (esend, erecv), esrc, eland, ccall)
    gsum_first = landed("grad_exchange0", (fsend, frecv), fsrc, fland, gsum_rest)
    res = {}

    def update2d(tag, k, g):
        last = W[k].shape[-1]
        outs = adamw(tag, W[k].reshape(-1, last), M[k].reshape(-1, last), V[k].reshape(-1, last),
                     g.reshape(1, -1, last))
        res[k] = [a.reshape(W[k].shape) for a in outs]

    off0, off1 = 0, 0
    for k, ax in BIG:
        rows, padded = part_rows(k)
        _, r, c = row_form[k].shape
        parts = [gsum_first[off0:off0 + rows]] + [gsum_rest[off1 + i * padded:off1 + i * padded + rows]
                                                  for i in range(len(rest))]
        g = jnp.stack([p.reshape(r, c) for p in parts])
        update2d("adamw_" + k, k, jnp.swapaxes(g, 1, 2) if ax == 2 else g)
        off0 += padded
        off1 += len(rest) * padded

    def update(tag, keys, g8):
        outs = adamw(tag, _pack([W[k] for k in keys], F32), _pack([M[k] for k in keys], F32),
                     _pack([V[k] for k in keys], F32), g8)
        for i, arr in enumerate(outs):
            for k, a in zip(keys, _unpack(arr, [W[k].shape for k in keys])):
                res.setdefault(k, [None] * 4)[i] = a

    nrep = sum(int(np.prod(W[k].shape)) for k in SMALL)
    sflat = sall.reshape(N_DEV, -1)
    def pack8(a):
        a = a.reshape(N_DEV, -1)
        pad = (-a.shape[1]) % (8 * PACK_C)
        return jnp.concatenate([a, jnp.zeros((N_DEV, pad), F32)], axis=1).reshape(N_DEV, -1, PACK_C)

    update("adamw_small", SMALL, pack8(sflat[:, :nrep]))
    ncw = ffn_conv_w.shape[2]
    cw8 = sflat[:, nrep:nsmall].reshape(N_DEV, DEPTH, 3, N_DEV * ncw)
    cw8 = lax.dynamic_slice_in_dim(cw8, me * ncw, ncw, axis=3)
    update("adamw_conv", ["ffn_conv_w"], pack8(cw8))
    update2d("adamw_ada", "ada_w", d_ada_w)
    update("adamw_cctx", ["c_ctx"], ccall)

    out = [loss, grad_x]
    for i in range(4):
        out += [res[k][i] for k in names]
    return tuple(out)
```

```python
import functools
import math

import numpy as np
import jax
import jax.numpy as jnp
from jax import lax
from jax.experimental import pallas as pl
from jax.experimental.pallas import tpu as pltpu

F32 = jnp.float32
BF16 = jnp.bfloat16
HI = lax.Precision.HIGHEST
MESH = pl.DeviceIdType.MESH

N_DEV = 8
D = 1024
DEPTH = 4
CHUNK = 64
EPS = 1e-6
ALPHA = (2 * DEPTH) ** 0.25
GLA_TAU = 16.0
ROPE_BASE = 10000.0
MLA_SCALE = 96 ** -0.5
D_FF = 2816
FF_CHUNK = 1408
P_PAD = 2048
VMEM_LIMIT_BYTES = 56 << 20

ADAM_LR, ADAM_B1, ADAM_B2, ADAM_EPS, ADAM_WD, ADAM_STEP = 0.001, 0.9, 0.999, 1e-08, 0.01, 10

D_IN = 1984
W_IN_SEGS = [(0, 512), (544, 1408), (512, 32), (1952, 32)]
W_IN_INV_SEGS = [(0, 512), (1920, 32), (512, 1408), (1952, 32)]
FF_SEGS = [(0, FF_CHUNK), (D_FF, FF_CHUNK), (FF_CHUNK, FF_CHUNK), (D_FF + FF_CHUNK, FF_CHUNK)]


def _cols(a, segs):
    return jnp.concatenate([a[..., s:s + n] for s, n in segs], axis=-1)


def _rows(a, segs):
    return jnp.concatenate([a[s:s + n] for s, n in segs], axis=0)


def _pad_heads(wt, per_head):
    c = wt.shape[1]
    wt = wt.reshape(8, per_head, c)
    return jnp.concatenate([wt, jnp.zeros((8, 128 - per_head, c), wt.dtype)], axis=1).reshape(1024, c)


def _unpad_heads(g, per_head):
    c = g.shape[1]
    return g.reshape(8, 128, c)[:, :per_head].reshape(8 * per_head, c)


def _cparams(sem=None):
    return pltpu.CompilerParams(vmem_limit_bytes=VMEM_LIMIT_BYTES, dimension_semantics=sem)


@jax.custom_vjp
def bdot(a, w):
    return jnp.dot(a.astype(BF16), w.astype(BF16), preferred_element_type=F32)


def _bdot_fwd(a, w):
    return bdot(a, w), (a, w)


def _bdot_bwd(res, ct):
    a, w = res
    ctb = ct.astype(BF16)
    da = lax.dot_general(ctb, w.astype(BF16), (((1,), (1,)), ((), ())), preferred_element_type=F32)
    dw = lax.dot_general(a.astype(BF16), ctb, (((0,), (0,)), ((), ())), preferred_element_type=F32)
    return da.astype(a.dtype), dw.astype(w.dtype)


bdot.defvjp(_bdot_fwd, _bdot_bwd)


@jax.custom_vjp
def bdot_nt(a, wt):
    return lax.dot_general(a.astype(BF16), wt.astype(BF16), (((1,), (1,)), ((), ())), preferred_element_type=F32)


def _bdot_nt_fwd(a, wt):
    return bdot_nt(a, wt), (a, wt)


def _bdot_nt_bwd(res, ct):
    a, wt = res
    ctb = ct.astype(BF16)
    da = jnp.dot(ctb, wt.astype(BF16), preferred_element_type=F32)
    dwt = lax.dot_general(ctb, a.astype(BF16), (((0,), (0,)), ((), ())), preferred_element_type=F32)
    return da.astype(a.dtype), dwt.astype(wt.dtype)


bdot_nt.defvjp(_bdot_nt_fwd, _bdot_nt_bwd)


@jax.custom_vjp
def bdot_tn(a, b):
    return lax.dot_general(a.astype(BF16), b.astype(BF16), (((0,), (0,)), ((), ())), preferred_element_type=F32)


def _bdot_tn_fwd(a, b):
    return bdot_tn(a, b), (a, b)


def _bdot_tn_bwd(res, ct):
    a, b = res
    ctb = ct.astype(BF16)
    da = lax.dot_general(b.astype(BF16), ctb, (((1,), (1,)), ((), ())), preferred_element_type=F32)
    db = jnp.dot(a.astype(BF16), ctb, preferred_element_type=F32)
    return da.astype(a.dtype), db.astype(b.dtype)


bdot_tn.defvjp(_bdot_tn_fwd, _bdot_tn_bwd)


def _split3(x):
    x1 = x.astype(BF16)
    r1 = x - x1.astype(F32)
    x2 = r1.astype(BF16)
    return x1, x2, (r1 - x2.astype(F32)).astype(BF16)


@jax.custom_vjp
def xdot(x, m):
    mb = m.astype(BF16)
    return sum(jnp.dot(xi, mb, preferred_element_type=F32) for xi in _split3(x))


def _xdot_bwd(m, ct):
    mb = m.astype(BF16)
    dx = sum(lax.dot_general(ci, mb, (((1,), (1,)), ((), ())), preferred_element_type=F32) for ci in _split3(ct))
    return dx, jnp.zeros_like(m)


xdot.defvjp(lambda x, m: (xdot(x, m), m), _xdot_bwd)


@jax.custom_vjp
def xdot_l(m, x):
    mb = m.astype(BF16)
    return sum(jnp.dot(mb, xi, preferred_element_type=F32) for xi in _split3(x))


def _xdot_l_bwd(m, ct):
    mb = m.astype(BF16)
    dx = sum(lax.dot_general(mb, ci, (((0,), (0,)), ((), ())), preferred_element_type=F32) for ci in _split3(ct))
    return jnp.zeros_like(m), dx


xdot_l.defvjp(lambda m, x: (xdot_l(m, x), m), _xdot_l_bwd)


def _swap_fn(half):
    def swap(x):
        n = x.shape[1]
        first = (_iota(x.shape, 1) // half) % 2 == 0
        return jnp.where(first, pltpu.roll(x, n - half, 1), pltpu.roll(x, half, 1))

    f = jax.custom_vjp(swap)
    f.defvjp(lambda x: (swap(x), None), lambda _, ct: (swap(ct),))
    return f


_swap16 = _swap_fn(16)
_swap8 = _swap_fn(8)


def hdot(a, b):
    return jnp.dot(a, b, precision=HI, preferred_element_type=F32)


def _iota(shape, axis):
    return lax.broadcasted_iota(jnp.int32, shape, axis)


def _group_avg(n, g):
    return (_iota((n, n), 0) // g == _iota((n, n), 1) // g).astype(F32) * (1.0 / g)


def _silu(x):
    return x * jax.nn.sigmoid(x)


def _layer_norm(z, g, b):
    mu = jnp.mean(z, axis=-1, keepdims=True)
    zc = z - mu
    var = jnp.mean(zc * zc, axis=-1, keepdims=True)
    return zc * lax.rsqrt(var + EPS) * g + b


def _rms(x, g):
    return x * lax.rsqrt(jnp.mean(x * x, axis=-1, keepdims=True) + EPS) * g


def mm(name, a, wt, out_dtype, tn, tr, mod=None, sel=None):
    n, k = a.shape
    nw = wt.shape[0]

    def body(*refs):
        if mod is not None:
            a_ref, m_ref, w_ref, o_ref = refs
            m = m_ref[0]
            av = a_ref[...] * (1.0 + m[sel[0]:sel[0] + 1]) + m[sel[1]:sel[1] + 1]
        else:
            a_ref, w_ref, o_ref = refs
            av = a_ref[...]
        o_ref[...] = lax.dot_general(av.astype(BF16), w_ref[...], (((1,), (1,)), ((), ())),
                                     preferred_element_type=F32).astype(o_ref.dtype)

    in_specs = [pl.BlockSpec((tr, k), lambda c, j: (j, 0))]
    args = [a]
    if mod is not None:
        in_specs.append(pl.BlockSpec((1, 8, k), lambda c, j: (j, 0, 0)))
        args.append(mod)
    in_specs.append(pl.BlockSpec((tn, k), lambda c, j: (c, 0)))
    args.append(wt)
    return pl.pallas_call(
        body, name=name, grid=(nw // tn, n // tr), in_specs=in_specs,
        out_specs=pl.BlockSpec((tr, tn), lambda c, j: (j, c)),
        out_shape=jax.ShapeDtypeStruct((n, nw), out_dtype), compiler_params=_cparams(("parallel", "arbitrary")),
    )(*args)


def mm_tn(name, dc, a, tn, tr, mod=None, sel=None):
    n, k = a.shape
    nw = dc.shape[1]

    def body(*refs):
        if mod is not None:
            d_ref, a_ref, m_ref, o_ref = refs
            m = m_ref[0]
            av = a_ref[...] * (1.0 + m[sel[0]:sel[0] + 1]) + m[sel[1]:sel[1] + 1]
        else:
            d_ref, a_ref, o_ref = refs
            av = a_ref[...]

        @pl.when(pl.program_id(1) == 0)
        def _():
            o_ref[...] = jnp.zeros_like(o_ref)

        o_ref[...] += lax.dot_general(d_ref[...].astype(BF16), av.astype(BF16), (((0,), (0,)), ((), ())),
                                      preferred_element_type=F32)

    in_specs = [pl.BlockSpec((tr, tn), lambda c, j: (j, c)), pl.BlockSpec((tr, k), lambda c, j: (j, 0))]
    args = [dc, a]
    if mod is not None:
        in_specs.append(pl.BlockSpec((1, 8, k), lambda c, j: (j, 0, 0)))
        args.append(mod)
    return pl.pallas_call(
        body, name=name, grid=(nw // tn, n // tr), in_specs=in_specs,
        out_specs=pl.BlockSpec((tn, k), lambda c, j: (c, 0)),
        out_shape=jax.ShapeDtypeStruct((nw, k), F32), compiler_params=_cparams(("parallel", "arbitrary")),
    )(*args)


def mm_modbwd(name, dc, wt, x, mod, add, sel, tr):
    n, k = dc.shape
    dm = wt.shape[1]

    def body(dc_ref, wt_ref, x_ref, m_ref, add_ref, dx_ref, dm_ref):
        dh = jnp.dot(dc_ref[...].astype(BF16), wt_ref[...], preferred_element_type=F32)
        m = m_ref[0]
        dx_ref[...] = add_ref[...] + dh * (1.0 + m[sel[0]:sel[0] + 1])
        dsc = jnp.sum(dh * x_ref[...], axis=0, keepdims=True)
        dsh = jnp.sum(dh, axis=0, keepdims=True)
        rows = _iota((8, dm), 0)
        dm_ref[0] = jnp.where(rows == sel[0], dsc, 0.0) + jnp.where(rows == sel[1], dsh, 0.0)

    return pl.pallas_call(
        body, name=name, grid=(n // tr,),
        in_specs=[pl.BlockSpec((tr, k), lambda j: (j, 0)), pl.BlockSpec((k, dm), lambda j: (0, 0)),
                  pl.BlockSpec((tr, dm), lambda j: (j, 0)), pl.BlockSpec((1, 8, dm), lambda j: (j, 0, 0)),
                  pl.BlockSpec((tr, dm), lambda j: (j, 0))],
        out_specs=[pl.BlockSpec((tr, dm), lambda j: (j, 0)), pl.BlockSpec((1, 8, dm), lambda j: (j, 0, 0))],
        out_shape=[jax.ShapeDtypeStruct((n, dm), F32), jax.ShapeDtypeStruct((n // tr, 8, dm), F32)],
        compiler_params=_cparams(("arbitrary",)),
    )(dc, wt, x, mod, add)


def _spec(kind, arr, width, cb, tr, tps):
    if kind == "row":
        return pl.BlockSpec((tr, width), lambda j: (j, cb))
    if kind == "pos":
        return pl.BlockSpec((tr, width), lambda j: (j % tps, cb))
    if kind == "tile":
        return pl.BlockSpec((1,) + arr.shape[1:], lambda j: (j, 0, 0))
    if kind == "par":
        return pl.BlockSpec(arr.shape, lambda j: (0, 0))
    raise ValueError(kind)


def _load(kind, ref):
    v = ref[0] if kind == "tile" else ref[...]
    return v.astype(F32)


def tile_fwd(name, fn, ins, outs, n, tr, tps):
    kinds = [i[0] for i in ins]

    def body(*refs):
        vals = [_load(k, r) for k, r in zip(kinds, refs[:len(ins)])]
        res = fn(*vals)
        for r, o in zip(refs[len(ins):], res):
            r[...] = o.astype(r.dtype)

    return pl.pallas_call(
        body, name=name, grid=(n // tr,),
        in_specs=[_spec(k, a, w, cb, tr, tps) for k, a, w, cb in ins],
        out_specs=[pl.BlockSpec((tr, w), lambda j: (j, 0)) for w, _ in outs],
        out_shape=[jax.ShapeDtypeStruct((n, w), dt) for w, dt in outs],
        compiler_params=_cparams(("arbitrary",)),
    )(*[i[1] for i in ins])


def tile_bwd(name, fn, ins, cots, want, n, tr, tps):
    kinds = [i[0] for i in ins]
    widx = [w[0] for w in want]
    ni, nc = len(ins), len(cots)

    def body(*refs):
        vals = [_load(k, r) for k, r in zip(kinds, refs[:ni])]
        cvals = tuple(r[...].astype(F32) for r in refs[ni:ni + nc])

        def f(*dv):
            full = list(vals)
            for i, v in zip(widx, dv):
                full[i] = v
            return tuple(fn(*full))

        _, vjp = jax.vjp(f, *[vals[i] for i in widx])
        grads = vjp(cvals)
        first = pl.program_id(0) == 0
        for r, g, i in zip(refs[ni + nc:], grads, widx):
            if kinds[i] == "par":
                @pl.when(first)
                def _(r=r):
                    r[...] = jnp.zeros_like(r)
                r[...] += g
            elif kinds[i] == "tile":
                r[0] = g.astype(r.dtype)
            else:
                r[...] = g.astype(r.dtype)

    out_specs, out_shape = [], []
    for i, dt in want:
        k, a, w, cb = ins[i]
        if k == "par":
            out_specs.append(pl.BlockSpec(a.shape, lambda j: (0, 0)))
            out_shape.append(jax.ShapeDtypeStruct(a.shape, F32))
        elif k == "tile":
            out_specs.append(pl.BlockSpec((1,) + a.shape[1:], lambda j: (j, 0, 0)))
            out_shape.append(jax.ShapeDtypeStruct(a.shape, F32))
        else:
            out_specs.append(pl.BlockSpec((tr, w), lambda j: (j, 0)))
            out_shape.append(jax.ShapeDtypeStruct((n, w), dt))
    return pl.pallas_call(
        body, name=name, grid=(n // tr,),
        in_specs=[_spec(k, a, w, cb, tr, tps) for k, a, w, cb in ins]
        + [pl.BlockSpec((tr, c.shape[1]), lambda j: (j, 0)) for c in cots],
        out_specs=out_specs, out_shape=out_shape, compiler_params=_cparams(("arbitrary",)),
    )(*[i[1] for i in ins], *cots)


def pre_fn(p_gq, p_rq, p_rk, p_cq, p_ckv, p_misc, rcos, rsin, mcos, msin,
           w2f, w2b, b2f, b2b, retf, retb, qg, kvg, wuq, wuk, wuv, e2):
    tr = p_gq.shape[0]
    gq = p_gq * (32 ** -0.5)
    af = jax.nn.log_sigmoid(hdot(p_misc, w2f) + b2f) * (1.0 / GLA_TAU)
    ab = jax.nn.log_sigmoid(hdot(p_misc, w2b) + b2b) * (1.0 / GLA_TAU)
    arf = jnp.zeros((tr, 128), F32) + retf
    arb = jnp.zeros((tr, 128), F32) + retb
    rq = p_rq * rcos + _swap16(p_rq) * rsin
    rks = p_rk * (32 ** -0.5)
    rk = rks * rcos + _swap16(rks) * rsin
    qp = bdot_nt(_rms(p_cq, qg), wuq) * MLA_SCALE
    ckvn = _rms(p_ckv, kvg)
    kp = bdot_nt(ckvn, wuk) + xdot(p_misc, e2)
    mc, ms = jnp.tile(mcos, (1, 8)), jnp.tile(msin, (1, 8))
    v = bdot_nt(ckvn, wuv)
    return gq, af, ab, arf, arb, rq, rk, qp * mc + _swap8(qp) * ms, kp * mc + _swap8(kp) * ms, v


def post_fn(ogf, ogb, orf, orb, om, gg, rg, x, mod, gng, wout, lng, lnb):
    avg = _group_avg(256, 64)
    og = ogf + ogb
    mg = og * lax.rsqrt(xdot(og * og, avg) + EPS) * gng * _silu(gg)
    orr = orf + orb
    oc = orr - xdot(orr, avg)
    mr = oc * lax.rsqrt(xdot(oc * oc, avg) + EPS) * _silu(rg)
    m = jnp.concatenate([mg, mr, om], axis=1)
    y = bdot(m, wout)
    return (_layer_norm(ALPHA * x + mod[2:3] * y, lng, lnb),)


def ln2_fn(x1, f, mod, lng, lnb):
    return (_layer_norm(ALPHA * x1 + mod[5:6] * f, lng, lnb),)


def scan_step(q, k, v, a, st, rev):
    ii, jj = _iota((CHUNK, CHUNK), 0), _iota((CHUNK, CHUNK), 1)
    tri = ((jj >= ii) if rev else (jj <= ii)).astype(F32)
    b = xdot_l(tri, a)
    btot = jnp.sum(a, axis=0, keepdims=True)
    qe = q * jnp.exp(b - btot)
    ke = k * jnp.exp(btot - b)
    lane = _iota((1, 128), 1)
    q4 = jnp.concatenate([qe * (lane // 32 == h).astype(F32) for h in range(4)], axis=0)
    att = bdot_nt(q4, ke)
    att = jnp.where(jnp.concatenate([tri] * 4, axis=0) > 0, att, 0.0)
    r = bdot(att, v)
    col = _iota((1, 256), 1)
    o = bdot_nt(q * jnp.exp(b), st)
    for h in range(4):
        o = o + r[h * CHUNK:(h + 1) * CHUNK] * (col // 64 == h).astype(F32)
    vk = bdot_tn(v, ke)
    bd = (_iota((256, 128), 0) // 64 == _iota((256, 128), 1) // 32).astype(F32)
    return o, st * jnp.exp(btot) + vk * bd


def _chunk_maps(nch, nctx):
    def fwd(s):
        return s

    def bwd(s):
        return jnp.where(s < nctx, nctx - 1 - s, nch - 1 - (s - nctx))
    return fwd, bwd


def _per_sample(arr, nb):
    return arr.reshape(nb, arr.shape[0] // nb, arr.shape[1])


def scan_fwd(name, q, k, v, af, ab, nb, nch, nctx):
    n = af.shape[0]
    fmap, bmap = _chunk_maps(nch, nctx)

    def body(qf, kf, vf, a_f, qb, kb, vb, a_b, of_ref, ob_ref, stf_ref, stb_ref, s_scr):
        @pl.when(pl.program_id(0) == 0)
        def _():
            s_scr[...] = jnp.zeros_like(s_scr)

        for i in range(nb):
            stf_ref[0, i] = s_scr[2 * i]
            stb_ref[0, i] = s_scr[2 * i + 1]
            o, sn = scan_step(qf[i], kf[i], vf[i], a_f[i], s_scr[2 * i], False)
            of_ref[i] = o
            s_scr[2 * i] = sn
            o, sn = scan_step(qb[i], kb[i], vb[i], a_b[i], s_scr[2 * i + 1], True)
            ob_ref[i] = o
            s_scr[2 * i + 1] = sn

    def specs(m):
        return [pl.BlockSpec((nb, CHUNK, w), lambda s, cb=cb: (0, m(s), cb)) for _, w, cb in (q, k, v)] + \
               [pl.BlockSpec((nb, CHUNK, 128), lambda s: (0, m(s), 0))]

    ps = lambda a: _per_sample(a, nb)
    of, ob, stf, stb = pl.pallas_call(
        body, name=name, grid=(nch,), in_specs=specs(fmap) + specs(bmap),
        out_specs=[pl.BlockSpec((nb, CHUNK, 256), lambda s: (0, fmap(s), 0)),
                   pl.BlockSpec((nb, CHUNK, 256), lambda s: (0, bmap(s), 0)),
                   pl.BlockSpec((1, nb, 256, 128), lambda s: (s, 0, 0, 0)),
                   pl.BlockSpec((1, nb, 256, 128), lambda s: (s, 0, 0, 0))],
        out_shape=[jax.ShapeDtypeStruct((nb, n // nb, 256), F32)] * 2
        + [jax.ShapeDtypeStruct((nch, nb, 256, 128), F32)] * 2,
        scratch_shapes=[pltpu.VMEM((2 * nb, 256, 128), F32)], compiler_params=_cparams(("arbitrary",)),
    )(ps(q[0]), ps(k[0]), ps(v[0]), ps(af), ps(q[0]), ps(k[0]), ps(v[0]), ps(ab))
    return of.reshape(n, 256), ob.reshape(n, 256), stf, stb


def scan_bwd(name, q, k, v, af, ab, stf, stb, do, nb, nch, nctx):
    n = af.shape[0]
    fmap0, bmap0 = _chunk_maps(nch, nctx)
    fmap = lambda r: fmap0(nch - 1 - r)
    bmap = lambda r: bmap0(nch - 1 - r)

    def body(qf, kf, vf, a_f, sf, dof, qb, kb, vb, a_b, sb, dob,
             dqf, dkf, dvf, daf, dqb, dkb, dvb, dab, ds_scr):
        @pl.when(pl.program_id(0) == 0)
        def _():
            ds_scr[...] = jnp.zeros_like(ds_scr)

        for i in range(nb):
            for d, (qr, kr, vr, ar, sr, dor, outs) in enumerate(((qf, kf, vf, a_f, sf, dof, (dqf, dkf, dvf, daf)),
                                                                   (qb, kb, vb, a_b, sb, dob, (dqb, dkb, dvb, dab)))):
                _, vjp = jax.vjp(functools.partial(scan_step, rev=bool(d)), qr[i], kr[i], vr[i], ar[i], sr[0, i])
                dq, dk, dv, da, ds = vjp((dor[i], ds_scr[2 * i + d]))
                outs[0][i] = dq
                outs[1][i] = dk
                outs[2][i] = dv
                outs[3][i] = da
                ds_scr[2 * i + d] = ds

    def specs(m):
        return [pl.BlockSpec((nb, CHUNK, w), lambda r, cb=cb: (0, m(r), cb)) for _, w, cb in (q, k, v)] + \
               [pl.BlockSpec((nb, CHUNK, 128), lambda r: (0, m(r), 0)),
                pl.BlockSpec((1, nb, 256, 128), lambda r: (nch - 1 - r, 0, 0, 0)),
                pl.BlockSpec((nb, CHUNK, 256), lambda r: (0, m(r), 0))]

    def ospecs(m):
        return [pl.BlockSpec((nb, CHUNK, w), lambda r: (0, m(r), 0)) for w in (128, 128, 256, 128)]

    ps = lambda a: _per_sample(a, nb)
    oshape = [jax.ShapeDtypeStruct((nb, n // nb, w), F32) for w in (128, 128, 256, 128)]
    outs = pl.pallas_call(
        body, name=name, grid=(nch,), in_specs=specs(fmap) + specs(bmap),
        out_specs=ospecs(fmap) + ospecs(bmap), out_shape=oshape + oshape,
        scratch_shapes=[pltpu.VMEM((2 * nb, 256, 128), F32)], compiler_params=_cparams(("arbitrary",)),
    )(ps(q[0]), ps(k[0]), ps(v[0]), ps(af), stf, ps(do), ps(q[0]), ps(k[0]), ps(v[0]), ps(ab), stb, ps(do))
    return [o.reshape(n, o.shape[2]) for o in outs]


def mla_fwd(name, qa, ka, va, nb, tps, tr, nctx_rows):
    n = qa.shape[0]
    t = tps * tr

    def body(q_ref, k_ref, v_ref, o_ref, lse_ref):
        def attend(nk):
            vv = v_ref[0:nk, :]
            first = _iota(vv.shape, 1) < 64
            one = jnp.ones_like(vv)
            res, lses = [], []
            for h in range(2):
                s = lax.dot_general(q_ref[:, h * 128:(h + 1) * 128], k_ref[0:nk, h * 128:(h + 1) * 128],
                                    (((1,), (1,)), ((), ())), preferred_element_type=F32)
                m = jnp.max(s, axis=-1, keepdims=True)
                e = jnp.exp((s - m).astype(BF16))
                r = jnp.dot(e, jnp.where(first == (h == 0), vv, one), preferred_element_type=F32)
                l = r[:, 64:65] if h == 0 else r[:, 0:1]
                res.append(r / l)
                lses.append(m + jnp.log(l))
            lane = _iota((tr, 128), 1) < 64
            o_ref[...] = jnp.where(lane, res[0], res[1])
            lse_ref[...] = jnp.where(lane, lses[0], lses[1])

        @pl.when(pl.program_id(2) == 0)
        def _():
            attend(nctx_rows)

        @pl.when(pl.program_id(2) > 0)
        def _():
            attend(t)

    return pl.pallas_call(
        body, name=name, grid=(nb, 4, tps),
        in_specs=[pl.BlockSpec((tr, 256), lambda b, h, j: (b * tps + j, h)), pl.BlockSpec((t, 256), lambda b, h, j: (b, h)),
                  pl.BlockSpec((t, 128), lambda b, h, j: (b, h))],
        out_specs=[pl.BlockSpec((tr, 128), lambda b, h, j: (b * tps + j, h))] * 2,
        out_shape=[jax.ShapeDtypeStruct((n, 512), F32)] * 2,
        compiler_params=_cparams(("parallel", "parallel", "arbitrary")),
    )(qa, ka, va)


def mla_bwd(name, qa, ka, va, o, lse, do, nb, tps, tr, nctx_rows):
    n = qa.shape[0]
    t = tps * tr

    def body(q_ref, k_ref, v_ref, o_ref, lse_ref, do_ref, dq_ref, dk_ref, dv_ref, dkt, dvt):
        @pl.when(pl.program_id(2) == 0)
        def _():
            dkt[...] = jnp.zeros_like(dkt)
            dvt[...] = jnp.zeros_like(dvt)

        def attend(nk):
            dov = do_ref[...]
            oo = dov * o_ref[...]
            dob = dov.astype(BF16)
            first = _iota(dob.shape, 1) < 64
            dqs = []
            for h in range(2):
                hs = slice(h * 128, (h + 1) * 128)
                qh, kh = q_ref[:, hs], k_ref[0:nk, hs]
                mine = first == (h == 0)
                delta = jnp.sum(jnp.where(mine, oo, 0.0), axis=-1, keepdims=True)
                doh = jnp.where(mine, dob, jnp.zeros_like(dob))
                s = lax.dot_general(qh, kh, (((1,), (1,)), ((), ())), preferred_element_type=F32)
                p = jnp.exp((s - lse_ref[:, h * 64:h * 64 + 1]).astype(BF16))
                dp = lax.dot_general(doh, v_ref[0:nk, :], (((1,), (1,)), ((), ())), preferred_element_type=F32)
                ds = p * (dp - delta).astype(BF16)
                dqs.append(jnp.dot(ds, kh, preferred_element_type=F32))
                dkt[hs, 0:nk] += lax.dot_general(qh, ds, (((0,), (0,)), ((), ())), preferred_element_type=F32)
                dvt[:, 0:nk] += lax.dot_general(doh, p, (((0,), (0,)), ((), ())), preferred_element_type=F32)
            dq_ref[...] = jnp.concatenate(dqs, axis=1)

        @pl.when(pl.program_id(2) == 0)
        def _():
            attend(nctx_rows)

        @pl.when(pl.program_id(2) > 0)
        def _():
            attend(t)

        @pl.when(pl.program_id(2) == tps - 1)
        def _():
            dk_ref[...] = dkt[...].T
            dv_ref[...] = dvt[...].T

    qtile = pl.BlockSpec((tr, 128), lambda b, h, j: (b * tps + j, h))
    return pl.pallas_call(
        body, name=name, grid=(nb, 4, tps),
        in_specs=[pl.BlockSpec((tr, 256), lambda b, h, j: (b * tps + j, h)), pl.BlockSpec((t, 256), lambda b, h, j: (b, h)),
                  pl.BlockSpec((t, 128), lambda b, h, j: (b, h)), qtile, qtile, qtile],
        out_specs=[pl.BlockSpec((tr, 256), lambda b, h, j: (b * tps + j, h)), pl.BlockSpec((t, 256), lambda b, h, j: (b, h)),
                   pl.BlockSpec((t, 128), lambda b, h, j: (b, h))],
        out_shape=[jax.ShapeDtypeStruct((n, 1024), F32), jax.ShapeDtypeStruct((n, 1024), F32),
                   jax.ShapeDtypeStruct((n, 512), F32)],
        scratch_shapes=[pltpu.VMEM((256, t), F32), pltpu.VMEM((128, t), F32)],
        compiler_params=_cparams(("parallel", "parallel", "arbitrary")),
    )(qa, ka, va, o, lse, do)


HALO = 16


def _halo_specs(tr, width, tps, nt):
    r = tr // HALO
    return [pl.BlockSpec((tr, width), lambda j, c: (j, c)),
            pl.BlockSpec((HALO, width), lambda j, c: (jnp.maximum(j * r - 1, 0), c)),
            pl.BlockSpec((HALO, width), lambda j, c: (jnp.minimum((j + 1) * r, nt * r - 1), c))]


def _shifted(u, prev, nxt, j, tps):
    tr = u.shape[0]
    t = j % tps
    has_prev = (t >= 2).astype(F32)
    has_next = jnp.logical_and(t >= 1, t <= tps - 2).astype(F32)
    rows = _iota(u.shape, 0)
    dn = jnp.where(rows == 0, prev[HALO - 1:HALO] * has_prev, pltpu.roll(u, 1, 0))
    up = jnp.where(rows == tr - 1, nxt[0:1] * has_next, pltpu.roll(u, tr - 1, 0))
    return dn, up


def _ffn_act(ucv):
    return _silu(ucv[:, :FF_CHUNK]) * ucv[:, FF_CHUNK:]


def ffn2_fwd(name, u, cw, cb, wd, x1, mod, lng, lnb, tr, tps):
    n = u.shape[0]
    nt = n // tr
    w2 = 2 * FF_CHUNK

    def body(u_ref, up_ref, un_ref, cw_ref, cb_ref, wd_ref, x1_ref, m_ref, g_ref, b_ref, f_ref, x2_ref, ucv_ref, acc):
        j, c = pl.program_id(0), pl.program_id(1)
        uu = u_ref[...].astype(F32)
        dn, up = _shifted(uu, up_ref[...].astype(F32), un_ref[...].astype(F32), j, tps)
        cwv = cw_ref[...]
        ucv = cwv[0:1] * dn + cwv[1:2] * uu + cwv[2:3] * up + cb_ref[...]
        ucv_ref[...] = ucv.astype(ucv_ref.dtype)
        part = bdot(_ffn_act(ucv), wd_ref[...])

        @pl.when(c == 0)
        def _():
            acc[...] = part

        @pl.when(c == 1)
        def _():
            f = acc[...] + part
            f_ref[...] = f
            x2_ref[...] = ln2_fn(x1_ref[...], f, m_ref[0], g_ref[...], b_ref[...])[0]

    return pl.pallas_call(
        body, name=name, grid=(nt, 2),
        in_specs=_halo_specs(tr, w2, tps, nt) + [
            pl.BlockSpec((8, w2), lambda j, c: (0, c)), pl.BlockSpec((1, w2), lambda j, c: (0, c)),
            pl.BlockSpec((FF_CHUNK, D), lambda j, c: (c, 0)), pl.BlockSpec((tr, D), lambda j, c: (j, 0)),
            pl.BlockSpec((1, 8, D), lambda j, c: (j, 0, 0)), pl.BlockSpec((1, D), lambda j, c: (0, 0)),
            pl.BlockSpec((1, D), lambda j, c: (0, 0))],
        out_specs=[pl.BlockSpec((tr, D), lambda j, c: (j, 0)), pl.BlockSpec((tr, D), lambda j, c: (j, 0)),
                   pl.BlockSpec((tr, w2), lambda j, c: (j, c))],
        out_shape=[jax.ShapeDtypeStruct((n, D), F32)] * 2 + [jax.ShapeDtypeStruct((n, 2 * w2), BF16)],
        scratch_shapes=[pltpu.VMEM((tr, D), F32)], compiler_params=_cparams(("arbitrary", "arbitrary")),
    )(u, u, u, cw, cb, wd, x1, mod, lng, lnb)


def ffn2_bwd(name, ucv, wd, df, tr):
    n = ucv.shape[0]
    nt = n // tr
    w2 = 2 * FF_CHUNK

    def body(ucv_ref, wd_ref, df_ref, ducv_ref, dwd_ref):
        j = pl.program_id(1)
        a, g = ucv_ref[:, :FF_CHUNK].astype(F32), ucv_ref[:, FF_CHUNK:].astype(F32)
        sg = jax.nn.sigmoid(a)
        sa = a * sg
        dfb = df_ref[...].astype(BF16)
        dact = lax.dot_general(dfb, wd_ref[...], (((1,), (1,)), ((), ())), preferred_element_type=F32)
        ducv_ref[:, :FF_CHUNK] = (dact * g * (sg + sa * (1.0 - sg))).astype(ducv_ref.dtype)
        ducv_ref[:, FF_CHUNK:] = (dact * sa).astype(ducv_ref.dtype)
        dwd = lax.dot_general((sa * g).astype(BF16), dfb, (((0,), (0,)), ((), ())), preferred_element_type=F32)

        @pl.when(j == 0)
        def _():
            dwd_ref[...] = jnp.zeros_like(dwd_ref)

        dwd_ref[...] += dwd

    return pl.pallas_call(
        body, name=name, grid=(2, nt),
        in_specs=[pl.BlockSpec((tr, w2), lambda c, j: (j, c)), pl.BlockSpec((FF_CHUNK, D), lambda c, j: (c, 0)),
                  pl.BlockSpec((tr, D), lambda c, j: (j, 0))],
        out_specs=[pl.BlockSpec((tr, w2), lambda c, j: (j, c)), pl.BlockSpec((FF_CHUNK, D), lambda c, j: (c, 0))],
        out_shape=[jax.ShapeDtypeStruct((n, 2 * w2), BF16), jax.ShapeDtypeStruct((D_FF, D), F32)],
        compiler_params=_cparams(("parallel", "arbitrary")),
    )(ucv, wd, df)


def conv_bwd(name, ducv, u, cw, tr, tps):
    n = u.shape[0]
    nt = n // tr
    w2 = 2 * FF_CHUNK

    def body(g_ref, gp_ref, gn_ref, u_ref, up_ref, un_ref, cw_ref, du_ref, dcw_ref, dcb_ref):
        c, j = pl.program_id(0), pl.program_id(1)
        g = g_ref[...].astype(F32)
        gdn, gup = _shifted(g, gp_ref[...].astype(F32), gn_ref[...].astype(F32), j, tps)
        uu = u_ref[...].astype(F32)
        udn, uup = _shifted(uu, up_ref[...].astype(F32), un_ref[...].astype(F32), j, tps)
        cwv = cw_ref[...]
        du_ref[...] = (cwv[0:1] * gup + cwv[1:2] * g + cwv[2:3] * gdn).astype(du_ref.dtype)
        rows = _iota((8, w2), 0)
        s = lambda z: jnp.sum(z, axis=0, keepdims=True)
        dcw = (jnp.where(rows == 0, s(g * udn), 0.0) + jnp.where(rows == 1, s(g * uu), 0.0)
               + jnp.where(rows == 2, s(g * uup), 0.0))

        @pl.when(j == 0)
        def _():
            dcw_ref[...] = jnp.zeros_like(dcw_ref)
            dcb_ref[...] = jnp.zeros_like(dcb_ref)

        dcw_ref[...] += dcw
        dcb_ref[...] += s(g)

    hs = _halo_specs(tr, w2, tps, nt)
    swap = lambda spec: pl.BlockSpec(spec.block_shape, lambda c, j, f=spec.index_map: f(j, c))
    return pl.pallas_call(
        body, name=name, grid=(2, nt),
        in_specs=[swap(s) for s in hs] * 2 + [pl.BlockSpec((8, w2), lambda c, j: (0, c))],
        out_specs=[pl.BlockSpec((tr, w2), lambda c, j: (j, c)), pl.BlockSpec((8, w2), lambda c, j: (0, c)),
                   pl.BlockSpec((1, w2), lambda c, j: (0, c))],
        out_shape=[jax.ShapeDtypeStruct((n, 2 * w2), BF16), jax.ShapeDtypeStruct((8, 2 * w2), F32),
                   jax.ShapeDtypeStruct((1, 2 * w2), F32)],
        compiler_params=_cparams(("parallel", "arbitrary")),
    )(ducv, ducv, ducv, u, u, u, cw)


def loss_head(name, xf, target, nb, tps, tr):
    n = xf.shape[0]

    def body(x_ref, t_ref, dy_ref, l_ref):
        lat = (pl.program_id(0) % tps > 0).astype(F32)
        err = (x_ref[...] - t_ref[...]) * lat
        dy_ref[...] = err * (1.0 / D)
        l_ref[...] = jnp.zeros_like(l_ref) + 0.5 * jnp.sum(err * err) * (1.0 / D)

    def tmap(j):
        return ((j // tps) * (tps - 1) + jnp.maximum(j % tps - 1, 0), 0)

    return pl.pallas_call(
        body, name=name, grid=(n // tr,),
        in_specs=[pl.BlockSpec((tr, D), lambda j: (j, 0)), pl.BlockSpec((tr, D), tmap)],
        out_specs=[pl.BlockSpec((tr, D), lambda j: (j, 0)), pl.BlockSpec((1, 8, 128), lambda j: (j, 0, 0))],
        out_shape=[jax.ShapeDtypeStruct((n, D), F32), jax.ShapeDtypeStruct((n // tr, 8, 128), F32)],
        compiler_params=_cparams(("arbitrary",)),
    )(xf, target)


ADAM_MAX_ROWS = 512


def adamw(name, w, m, v, g8):
    r, c = w.shape
    k = g8.shape[0]
    rows = max(b for b in range(8, ADAM_MAX_ROWS + 1, 8) if r % b == 0)
    bc1 = 1.0 - ADAM_B1 ** ADAM_STEP
    bc2 = 1.0 - ADAM_B2 ** ADAM_STEP

    def body(w_ref, m_ref, v_ref, g_ref, go_ref, d_ref, mo_ref, vo_ref):
        g = g_ref[0].astype(F32)
        for i in range(1, k):
            g = g + g_ref[i].astype(F32)
        mn = ADAM_B1 * m_ref[...] + (1.0 - ADAM_B1) * g
        vn = ADAM_B2 * v_ref[...] + (1.0 - ADAM_B2) * (g * g)
        go_ref[...] = g
        mo_ref[...] = mn
        vo_ref[...] = vn
        d_ref[...] = -ADAM_LR * ((mn / bc1) / (jnp.sqrt(vn / bc2) + ADAM_EPS) + ADAM_WD * w_ref[...])

    blk = pl.BlockSpec((rows, c), lambda i: (i, 0))
    return pl.pallas_call(
        body, name=name, grid=(r // rows,),
        in_specs=[blk, blk, blk, pl.BlockSpec((k, rows, c), lambda i: (0, i, 0))],
        out_specs=[blk] * 4, out_shape=[jax.ShapeDtypeStruct((r, c), F32)] * 4,
        compiler_params=_cparams(("parallel",)),
    )(w, m, v, g8)


def ada_fwd(name, s, aw, ab):
    nl, _, cw = aw.shape

    def body(s_ref, w_ref, b_ref, o_ref):
        o_ref[0] = hdot(s_ref[...], w_ref[0]) + b_ref[0]

    return pl.pallas_call(
        body, name=name, grid=(nl,),
        in_specs=[pl.BlockSpec(s.shape, lambda l: (0, 0)), pl.BlockSpec((1, D, cw), lambda l: (l, 0, 0)),
                  pl.BlockSpec((1, 1, cw), lambda l: (l, 0, 0))],
        out_specs=pl.BlockSpec((1, s.shape[0], cw), lambda l: (l, 0, 0)),
        out_shape=jax.ShapeDtypeStruct((nl, s.shape[0], cw), F32), compiler_params=_cparams(("arbitrary",)),
    )(s, aw, ab)


def ada_bwd(name, s, aw, dmod):
    nl, _, cw = aw.shape

    def body(s_ref, w_ref, d_ref, dw_ref, ds_ref):
        dw_ref[0] = lax.dot_general(s_ref[...], d_ref[0], (((0,), (0,)), ((), ())), precision=HI,
                                    preferred_element_type=F32)
        ds_ref[0] = lax.dot_general(d_ref[0], w_ref[0], (((1,), (1,)), ((), ())), precision=HI,
                                    preferred_element_type=F32)

    return pl.pallas_call(
        body, name=name, grid=(nl,),
        in_specs=[pl.BlockSpec(s.shape, lambda l: (0, 0)), pl.BlockSpec((1, D, cw), lambda l: (l, 0, 0)),
                  pl.BlockSpec((1, s.shape[0], cw), lambda l: (l, 0, 0))],
        out_specs=[pl.BlockSpec((1, D, cw), lambda l: (l, 0, 0)), pl.BlockSpec((1, s.shape[0], D), lambda l: (l, 0, 0))],
        out_shape=[jax.ShapeDtypeStruct((nl, D, cw), F32), jax.ShapeDtypeStruct((nl, s.shape[0], D), F32)],
        compiler_params=_cparams(("arbitrary",)),
    )(s, aw, dmod)


def _place():
    return lax.axis_index("x"), lax.axis_index("y"), lax.axis_index("c")


def all_gather(name, x, in_vmem):
    r, c = x.shape

    def body(x_ref, out_ref, send_sems, recv_sems, local_sem):
        px, py, pc = _place()
        me, sibling = (px, py, pc), (px, py, 1 - pc)
        chips = [(1 - px, py), (px, 1 - py), (1 - px, 1 - py)]

        def rows(qx, qy, qc):
            return out_ref.at[pl.ds((4 * qx + 2 * qy + qc) * r, r), :]

        def copy(k, block, to, src=None):
            return pltpu.make_async_remote_copy(
                src_ref=rows(*block) if src is None else src, dst_ref=rows(*block),
                send_sem=send_sems.at[k], recv_sem=recv_sems.at[k], device_id=to, device_id_type=MESH)

        mine = pltpu.make_async_copy(x_ref, rows(*me), local_sem)
        mine.start()
        first = [copy(0, me, sibling, src=x_ref)]
        first += [copy(1 + j, me, (*chip, pc), src=x_ref) for j, chip in enumerate(chips)]
        for cp in first:
            cp.start()
        passed = [copy(4 + j, (*chip, pc), sibling) for j, chip in enumerate(chips)]
        for j, chip in enumerate(chips):
            copy(1 + j, (*chip, pc), me).wait_recv()
            passed[j].start()
        copy(0, sibling, me).wait_recv()
        for j, chip in enumerate(chips):
            copy(4 + j, (*chip, 1 - pc), me).wait_recv()
        for cp in first + passed:
            cp.wait_send()
        mine.wait()

    space = pltpu.VMEM if in_vmem else pl.ANY
    return pl.pallas_call(
        body, name=name, out_shape=jax.ShapeDtypeStruct((N_DEV * r, c), x.dtype),
        in_specs=[pl.BlockSpec(memory_space=space)], out_specs=pl.BlockSpec(memory_space=space),
        scratch_shapes=[pltpu.SemaphoreType.DMA((7,)), pltpu.SemaphoreType.DMA((7,)), pltpu.SemaphoreType.DMA],
        compiler_params=pltpu.CompilerParams(vmem_limit_bytes=VMEM_LIMIT_BYTES),
    )(x)


_HBM = pl.BlockSpec(memory_space=pltpu.HBM)
_SEM = pl.BlockSpec(memory_space=pltpu.SEMAPHORE)
_EFFECT = pltpu.SideEffectType.DATAFLOW_SIDE_EFFECTING


def _partner(k):
    px, py, pc = _place()
    q = (px ^ (k >> 2 & 1), py ^ (k >> 1 & 1), pc ^ (k & 1))
    return q, 4 * q[0] + 2 * q[1] + q[2]


def xchg_start(name, x, per_peer):
    r, c = x.shape[-2:]

    def body(x_ref, land_ref, send_sems, recv_sems, x_thru, land_thru, token):
        px, py, pc = _place()
        my = 4 * px + 2 * py + pc
        for k in range(1, N_DEV):
            q, qi = _partner(k)
            pltpu.make_async_remote_copy(
                src_ref=x_ref.at[qi] if per_peer else x_ref, dst_ref=land_ref.at[my],
                send_sem=send_sems.at[k - 1], recv_sem=recv_sems.at[k - 1], device_id=q, device_id_type=MESH).start()
        token[...] = jnp.zeros_like(token)

    land = lax.empty((N_DEV, r, c), x.dtype)
    return pl.pallas_call(
        body, name=name,
        out_shape=(pltpu.SemaphoreType.DMA((N_DEV - 1,)), pltpu.SemaphoreType.DMA((N_DEV - 1,)),
                   pltpu.HBM(x.shape, x.dtype), pltpu.HBM(land.shape, land.dtype), jax.ShapeDtypeStruct((8, 128), F32)),
        in_specs=(_HBM, _HBM), out_specs=(_SEM, _SEM, _HBM, _HBM, pl.BlockSpec(memory_space=pltpu.VMEM)),
        input_output_aliases={0: 2, 1: 3}, compiler_params=pltpu.CompilerParams(has_side_effects=_EFFECT),
    )(pltpu.with_memory_space_constraint(x, pltpu.HBM), pltpu.with_memory_space_constraint(land, pltpu.HBM))


def xchg_wait(name, send_sems, recv_sems, x_thru, land_thru, after, per_peer):
    def body(x_ref, land_ref, send_sems, recv_sems, after_ref, x_out, land_out):
        for k in range(1, N_DEV):
            q, qi = _partner(k)
            cp = pltpu.make_async_remote_copy(
                src_ref=x_ref.at[qi] if per_peer else x_ref, dst_ref=land_ref.at[qi],
                send_sem=send_sems.at[k - 1], recv_sem=recv_sems.at[k - 1], device_id=q, device_id_type=MESH)
            cp.wait_send()
            cp.wait_recv()

    return pl.pallas_call(
        body, name=name,
        out_shape=(pltpu.HBM(x_thru.shape, x_thru.dtype), pltpu.HBM(land_thru.shape, land_thru.dtype)),
        in_specs=(_HBM, _HBM, _SEM, _SEM, pl.BlockSpec(memory_space=pl.ANY)), out_specs=(_HBM, _HBM),
        input_output_aliases={0: 0, 1: 1}, compiler_params=pltpu.CompilerParams(has_side_effects=_EFFECT),
    )(x_thru, land_thru, send_sems, recv_sems, after)


def _tables(seq, nctx_rows):
    f32 = np.float32
    pos = np.arange(seq, dtype=f32)
    ret_inv = (1.0 / (ROPE_BASE ** np.linspace(0.0, 1.0, 16, dtype=f32))).astype(f32)
    ang = pos[:, None] * ret_inv
    rc, rs = np.cos(ang).astype(f32), np.sin(ang).astype(f32)
    rcos = np.tile(np.concatenate([rc, rc], 1), (1, 4))
    rsin = np.tile(np.concatenate([-rs, rs], 1), (1, 4))
    rows = np.repeat(np.arange(seq // 64, dtype=f32), 64)
    cols = np.tile(np.arange(64, dtype=f32), seq // 64)
    ax_inv = (ROPE_BASE ** (-np.arange(8, dtype=f32) / 8)).astype(f32)
    ra, ca = rows[:, None] * ax_inv, cols[:, None] * ax_inv
    one, zero = np.ones((seq, 64), f32), np.zeros((seq, 64), f32)
    mcos = np.concatenate([one, np.cos(ra), np.cos(ra), np.cos(ca), np.cos(ca), one[:, :32]], 1)
    msin = np.concatenate([zero, -np.sin(ra), np.sin(ra), -np.sin(ca), np.sin(ca), zero[:, :32]], 1)
    ident = lambda t, v: np.concatenate([np.full((nctx_rows, 128), v, f32), t.astype(f32)], 0)
    return [jnp.asarray(ident(rcos, 1.0)), jnp.asarray(ident(rsin, 0.0)),
            jnp.asarray(ident(mcos, 1.0)), jnp.asarray(ident(msin, 0.0))]


def _prep_layer(w, l):
    z = lambda *s: jnp.zeros(s, F32)
    p = {}
    win = _rows(w["w_in_t"][l], W_IN_SEGS)
    p["w_in_t"] = jnp.concatenate([win, jnp.zeros((P_PAD - D_IN, D), win.dtype)], axis=0)
    p["w_up_t"] = w["ffn_up_t"][l]
    p["w_down"] = w["ffn_down"][l]
    p["w_out"] = w["w_out"][l]
    p["wuq"] = _pad_heads(w["mla_w_uq_t"][l], 96)
    p["wuk"] = _pad_heads(w["mla_w_uk_t"][l], 64)
    p["wuv"] = w["mla_w_uv_t"][l]
    gw = w["gla_gate_w"][l]
    p["w2f"] = z(128, 128).at[0:16].set(gw[0])
    p["w2b"] = z(128, 128).at[16:32].set(gw[1])
    p["b2f"], p["b2b"] = w["gla_gate_b"][l][0:1], w["gla_gate_b"][l][1:2]
    lg = jax.nn.log_sigmoid(w["ret_decay"][l])
    p["retf"], p["retb"] = jnp.repeat(lg[0], 32)[None], jnp.repeat(lg[1], 32)[None]
    p["qg"], p["kvg"] = w["mla_q_norm_g"][l][None], w["mla_kv_norm_g"][l][None]
    p["gng"] = jnp.tile(w["gla_norm_g"][l], 4)[None]
    p["ln1g"], p["ln1b"] = w["ln1_g"][l][None], w["ln1_b"][l][None]
    p["ln2g"], p["ln2b"] = w["ln2_g"][l][None], w["ln2_b"][l][None]
    p["cw"] = jnp.concatenate([_cols(w["ffn_conv_w"][l], FF_SEGS), z(5, 2 * D_FF)], axis=0)
    p["cb"] = _cols(w["ffn_conv_b"][l], FF_SEGS)[None]
    e2 = np.zeros((128, 1024), np.float32)
    for h in range(8):
        e2[32 + np.arange(32), h * 128 + 64 + np.arange(32)] = 1.0
    p["e2"] = jnp.asarray(e2)
    return p


def _pre_ins(pa, tabs, p):
    row = lambda w, cb: ("row", pa, w, cb)
    return [row(128, 0), row(128, 6), row(128, 7), row(256, 6), row(128, 14), row(128, 15)] + \
           [("pos", t, 128, 0) for t in tabs] + \
           [("par", p[k], 0, 0) for k in ("w2f", "w2b", "b2f", "b2b", "retf", "retb", "qg", "kvg", "wuq", "wuk", "wuv", "e2")]


_PRE_OUTS = [(128, F32)] * 7 + [(1024, BF16), (1024, BF16), (512, BF16)]
_PRE_WANT = [(i, F32) for i in range(6)] + [(i, F32) for i in range(10, 21)]


def _post_ins(ogf, ogb, orf, orb, om, pa, x, mod, p):
    return [("row", ogf, 256, 0), ("row", ogb, 256, 0), ("row", orf, 256, 0), ("row", orb, 256, 0),
            ("row", om, 512, 0), ("row", pa, 256, 2), ("row", pa, 256, 5), ("row", x, D, 0), ("tile", mod, 0, 0),
            ("par", p["gng"], 0, 0), ("par", p["w_out"], 0, 0), ("par", p["ln1g"], 0, 0), ("par", p["ln1b"], 0, 0)]


def layer_fwd(l, x, mod, p, tabs, dims):
    nb, tps, tr, nch, nctx = dims
    n = x.shape[0]
    pa = mm("proj", x, p["w_in_t"], F32, P_PAD, tr, mod=mod, sel=(1, 0))
    gq, af, ab, arf, arb, rq, rk, qa, ka, va = tile_fwd("mix_pre", pre_fn, _pre_ins(pa, tabs, p), _PRE_OUTS, n, tr, tps)
    ogf, ogb, gstf, gstb = scan_fwd("gla_scan", (gq, 128, 0), (pa, 128, 1), (pa, 256, 1), af, ab, nb, nch, nctx)
    orf, orb, rstf, rstb = scan_fwd("ret_scan", (rq, 128, 0), (rk, 128, 0), (pa, 256, 4), arf, arb, nb, nch, nctx)
    om, lse = mla_fwd("mla_attn", qa, ka, va, nb, tps, tr, nctx * CHUNK)
    (x1,) = tile_fwd("mix_post", post_fn, _post_ins(ogf, ogb, orf, orb, om, pa, x, mod, p), [(D, F32)], n, tr, tps)
    u = mm("ffn_up", x1, p["w_up_t"], BF16, 2 * D_FF, tr, mod=mod, sel=(4, 3))
    f, x2, ucv = ffn2_fwd("ffn_down", u, p["cw"], p["cb"], p["w_down"], x1, mod, p["ln2g"], p["ln2b"], tr, tps)
    saved = dict(x=x, pa=pa, gq=gq, af=af, ab=ab, arf=arf, arb=arb, rq=rq, rk=rk, qa=qa, ka=ka, va=va,
                 ogf=ogf, ogb=ogb, gstf=gstf, gstb=gstb, orf=orf, orb=orb, rstf=rstf, rstb=rstb, om=om, lse=lse,
                 x1=x1, u=u, ucv=ucv, f=f)
    return x2, saved


def layer_bwd(l, dx2, s, mod, p, tabs, dims):
    nb, tps, tr, nch, nctx = dims
    n = dx2.shape[0]
    g = {}
    ln2_ins = [("row", s["x1"], D, 0), ("row", s["f"], D, 0), ("tile", mod, 0, 0),
               ("par", p["ln2g"], 0, 0), ("par", p["ln2b"], 0, 0)]
    dx1a, df, dmod_a, g["ln2g"], g["ln2b"] = tile_bwd(
        "ln2_bwd", ln2_fn, ln2_ins, [dx2], [(0, F32), (1, F32), (2, F32), (3, F32), (4, F32)], n, tr, tps)
    ducv, g["w_down"] = ffn2_bwd("ffn_down_bwd", s["ucv"], p["w_down"], df, tr)
    du, g["cw"], g["cb"] = conv_bwd("conv_bwd", ducv, s["u"], p["cw"], tr, tps)
    g["w_up_t"] = mm_tn("ffn_up_dw", du, s["x1"], D_FF, tr, mod=mod, sel=(4, 3))
    dx1, dmod_b = mm_modbwd("ffn_up_dx", du, p["w_up_t"], s["x1"], mod, dx1a, (4, 3), tr)

    post_ins = _post_ins(s["ogf"], s["ogb"], s["orf"], s["orb"], s["om"], s["pa"], s["x"], mod, p)
    want = [(0, F32), (2, F32), (4, F32), (5, F32), (6, F32), (7, F32), (8, F32), (9, F32), (10, F32), (11, F32), (12, F32)]
    dog, dor, dom, dgg, drg, dxa, dmod_c, g["gng"], g["w_out"], g["ln1g"], g["ln1b"] = tile_bwd(
        "mix_post_bwd", post_fn, post_ins, [dx1], want, n, tr, tps)
    dqa, dka, dva = mla_bwd("mla_attn_bwd", s["qa"], s["ka"], s["va"], s["om"], s["lse"], dom, nb, tps, tr,
                            nctx * CHUNK)
    pa = s["pa"]
    gdqf, gdkf, gdvf, gdaf, gdqb, gdkb, gdvb, gdab = scan_bwd(
        "gla_scan_bwd", (s["gq"], 128, 0), (pa, 128, 1), (pa, 256, 1), s["af"], s["ab"], s["gstf"], s["gstb"], dog,
        nb, nch, nctx)
    rdqf, rdkf, rdvf, rdaf, rdqb, rdkb, rdvb, rdab = scan_bwd(
        "ret_scan_bwd", (s["rq"], 128, 0), (s["rk"], 128, 0), (pa, 256, 4), s["arf"], s["arb"], s["rstf"], s["rstb"],
        dor, nb, nch, nctx)

    pre_ins = _pre_ins(pa, tabs, p)
    extra = [gdqf, gdqb, rdqf, rdqb, rdkf, rdkb, gdkf, gdkb, gdvf, gdvb, rdvf, rdvb, dgg, drg]
    kinds = [i[0] for i in pre_ins]
    widx = [w[0] for w in _PRE_WANT]
    npre = len(pre_ins)

    def body(*refs):
        vals = [_load(k, r) for k, r in zip(kinds, refs[:npre])]
        rd = lambda i: refs[npre + i][...].astype(F32)
        cots = (rd(0) + rd(1), rd(14), rd(15), rd(16), rd(17), rd(2) + rd(3), rd(4) + rd(5), rd(18), rd(19), rd(20))

        def f(*dv):
            full = list(vals)
            for i, v in zip(widx, dv):
                full[i] = v
            return tuple(pre_fn(*full))

        _, vjp = jax.vjp(f, *[vals[i] for i in widx])
        grads = vjp(cots)
        dgq, drq, drk, dcq, dckv, dmisc = grads[:6]
        dp = jnp.concatenate([dgq, rd(6) + rd(7), rd(8) + rd(9), rd(12), drq, drk, rd(10) + rd(11), rd(13),
                              dcq, dckv, dmisc], axis=1)
        outs = refs[npre + 21:]
        outs[0][...] = dp.astype(BF16)
        first = pl.program_id(0) == 0
        for r, gr in zip(outs[1:], grads[6:]):
            @pl.when(first)
            def _(r=r):
                r[...] = jnp.zeros_like(r)
            r[...] += gr

    cot_arrays = extra + [gdaf, gdab, rdaf, rdab, dqa, dka, dva]
    par_arrays = [pre_ins[i][1] for i in range(10, 21)]
    res = pl.pallas_call(
        body, name="mix_pre_bwd", grid=(n // tr,),
        in_specs=[_spec(k, a, w, cb, tr, tps) for k, a, w, cb in pre_ins]
        + [pl.BlockSpec((tr, c.shape[1]), lambda j: (j, 0)) for c in cot_arrays],
        out_specs=[pl.BlockSpec((tr, P_PAD), lambda j: (j, 0))] + [pl.BlockSpec(a.shape, lambda j: (0, 0)) for a in par_arrays],
        out_shape=[jax.ShapeDtypeStruct((n, P_PAD), BF16)] + [jax.ShapeDtypeStruct(a.shape, F32) for a in par_arrays],
        compiler_params=_cparams(("arbitrary",)),
    )(*[i[1] for i in pre_ins], *cot_arrays)
    dp = res[0]
    for k, v in zip(("w2f", "w2b", "b2f", "b2b", "retf", "retb", "qg", "kvg", "wuq", "wuk", "wuv"), res[1:]):
        g[k] = v
    g["w_in_t"] = mm_tn("proj_dw", dp, s["x"], P_PAD, tr, mod=mod, sel=(1, 0))
    dx, dmod_d = mm_modbwd("proj_dx", dp, p["w_in_t"], s["x"], mod, dxa, (1, 0), tr)
    return dx, dmod_a + dmod_b + dmod_c + dmod_d, g


def _unprep_grads(g, w, l):
    o = {}
    o["w_in_t"] = _rows(g["w_in_t"], W_IN_INV_SEGS)
    o["ffn_up_t"] = g["w_up_t"]
    o["ffn_down"] = g["w_down"]
    o["w_out"] = g["w_out"]
    o["mla_w_uq_t"] = _unpad_heads(g["wuq"], 96)
    o["mla_w_uk_t"] = _unpad_heads(g["wuk"], 64)
    o["mla_w_uv_t"] = g["wuv"]
    o["gla_gate_w"] = jnp.stack([g["w2f"][0:16], g["w2b"][16:32]])
    o["gla_gate_b"] = jnp.concatenate([g["b2f"], g["b2b"]], axis=0)
    dlg = jnp.stack([g["retf"].reshape(4, 32).sum(-1), g["retb"].reshape(4, 32).sum(-1)])
    o["ret_decay"] = dlg * jax.nn.sigmoid(-w["ret_decay"][l])
    o["mla_q_norm_g"], o["mla_kv_norm_g"] = g["qg"][0], g["kvg"][0]
    o["gla_norm_g"] = g["gng"].reshape(4, 64).sum(0)
    o["ln1_g"], o["ln1_b"], o["ln2_g"], o["ln2_b"] = g["ln1g"][0], g["ln1b"][0], g["ln2g"][0], g["ln2b"][0]
    o["ffn_conv_w"] = _cols(g["cw"][0:3], FF_SEGS)
    o["ffn_conv_b"] = _cols(g["cb"][0], FF_SEGS)
    return o


def local_step(xs, target, modtab, layer_weights, dims, grads_ready=None):
    nb, tps, tr, nch, nctx = dims
    tabs = _tables((tps - 1) * tr, tr)
    x = xs
    saved, preps, ws = [], [], []
    for l in range(DEPTH):
        w = layer_weights(l, x)
        p = _prep_layer(w, 0)
        x, s = layer_fwd(l, x, modtab[l], p, tabs, dims)
        saved.append(s)
        preps.append(p)
        ws.append(w)
    dy, lpart = loss_head("loss_head", x, target, nb, tps, tr)
    loss = jnp.sum(lpart[:, 0, 0])
    dx = dy
    dmods, grads = [None] * DEPTH, [None] * DEPTH
    tok = None
    for l in reversed(range(DEPTH)):
        mod = modtab[l] if tok is None else modtab[l] + tok
        dx, dmods[l], g = layer_bwd(l, dx, saved[l], mod, preps[l], tabs, dims)
        grads[l] = _unprep_grads(g, ws[l], 0)
        tok = grads_ready(l, grads) if grads_ready is not None else None
    return loss, dx, jnp.stack(dmods), grads


BIG = [("ffn_up", 2), ("ffn_down", 1), ("w_out", 1), ("w_in", 2), ("mla_w_uq", 2), ("mla_w_uk", 2), ("mla_w_uv", 2)]
SMALL = ["ada_b", "gla_gate_w", "gla_gate_b", "gla_norm_g", "ret_decay", "mla_q_norm_g", "mla_kv_norm_g",
         "ln1_g", "ln1_b", "ffn_conv_b", "ln2_g", "ln2_b"]
PACK_C = 1024


def _big_key(k, axis):
    return k + "_t" if axis == 2 else k


def sum8(name, g8):
    k, r, c = g8.shape
    rows = max(b for b in range(16, ADAM_MAX_ROWS + 1, 16) if r % b == 0)

    def body(g_ref, o_ref):
        g = g_ref[0].astype(F32)
        for i in range(1, k):
            g = g + g_ref[i].astype(F32)
        o_ref[...] = g

    return pl.pallas_call(
        body, name=name, grid=(r // rows,), in_specs=[pl.BlockSpec((k, rows, c), lambda i: (0, i, 0))],
        out_specs=pl.BlockSpec((rows, c), lambda i: (i, 0)), out_shape=jax.ShapeDtypeStruct((r, c), F32),
        compiler_params=_cparams(("parallel",)),
    )(g8)


def _pack(arrs, dtype):
    flat = jnp.concatenate([a.reshape(-1).astype(dtype) for a in arrs])
    pad = (-flat.shape[0]) % (8 * PACK_C)
    return jnp.concatenate([flat, jnp.zeros((pad,), dtype)]).reshape(-1, PACK_C)


def _unpack(flat2d, shapes):
    flat = flat2d.reshape(-1)
    out, off = [], 0
    for s in shapes:
        sz = int(np.prod(s))
        out.append(flat[off:off + sz].reshape(s))
        off += sz
    return out


def _ff_order(b):
    return jnp.concatenate([b[0:2], b[4:6], b[2:4], b[6:8]], axis=0)


def _tile_pad(a):
    pad = (-a.shape[-2]) % HALO
    return a if pad == 0 else jnp.concatenate([a, jnp.zeros(a.shape[:-2] + (pad, a.shape[-1]), a.dtype)], axis=-2)


def kernel(x, c, ctx, c_ctx, ada_w, ada_b, w_in, gla_gate_w, gla_gate_b, gla_norm_g, ret_decay, mla_q_norm_g, mla_kv_norm_g, mla_w_uq, mla_w_uk, mla_w_uv, w_out, ln1_g, ln1_b, ffn_up, ffn_conv_w, ffn_conv_b, ffn_down, ln2_g, ln2_b, loss_target, m_c_ctx, m_ada_w, m_ada_b, m_w_in, m_gla_gate_w, m_gla_gate_b, m_gla_norm_g, m_ret_decay, m_mla_q_norm_g, m_mla_kv_norm_g, m_mla_w_uq, m_mla_w_uk, m_mla_w_uv, m_w_out, m_ln1_g, m_ln1_b, m_ffn_up, m_ffn_conv_w, m_ffn_conv_b, m_ffn_down, m_ln2_g, m_ln2_b, v_c_ctx, v_ada_w, v_ada_b, v_w_in, v_gla_gate_w, v_gla_gate_b, v_gla_norm_g, v_ret_decay, v_mla_q_norm_g, v_mla_kv_norm_g, v_mla_w_uq, v_mla_w_uk, v_mla_w_uv, v_w_out, v_ln1_g, v_ln1_b, v_ffn_up, v_ffn_conv_w, v_ffn_conv_b, v_ffn_down, v_ln2_g, v_ln2_b):
    names = ["c_ctx", "ada_w", "ada_b", "w_in", "gla_gate_w", "gla_gate_b", "gla_norm_g", "ret_decay", "mla_q_norm_g",
             "mla_kv_norm_g", "mla_w_uq", "mla_w_uk", "mla_w_uv", "w_out", "ln1_g", "ln1_b", "ffn_up", "ffn_conv_w",
             "ffn_conv_b", "ffn_down", "ln2_g", "ln2_b"]
    loc = locals()
    W = {k: loc[k] for k in names}
    M = {k: loc["m_" + k] for k in names}
    V = {k: loc["v_" + k] for k in names}

    nb, seq, _ = x.shape
    tr = ctx.shape[1]
    tps = 1 + seq // tr
    t = tps * tr
    n = nb * t
    nt = nb * tps
    dims = (nb, tps, tr, t // CHUNK, tr // CHUNK)
    px, py, pc = _place()
    me = 4 * px + 2 * py + pc
    ncol = ada_w.shape[2]

    cw_loc = ffn_conv_w.reshape(-1)
    g1 = jnp.concatenate([c.reshape(-1), cw_loc])
    g1 = jnp.concatenate([g1, jnp.zeros(((-g1.shape[0]) % (8 * PACK_C),), F32)]).reshape(-1, PACK_C)
    r1 = g1.shape[0]
    g1a = all_gather("gather_cond", g1, True).reshape(N_DEV, -1)
    c_all = g1a[:, :nb * D].reshape(N_DEV * nb, D)
    cw_all = g1a[:, nb * D:nb * D + cw_loc.shape[0]].reshape(N_DEV, DEPTH, 3, -1).transpose(1, 2, 0, 3).reshape(DEPTH, 3, -1)

    first, rest = [0], list(range(1, DEPTH))
    row_form = {k: (jnp.swapaxes(W[k], 1, 2) if ax == 2 else W[k]).astype(BF16) for k, ax in BIG}

    def part_rows(k):
        rows = int(np.prod(W[k].shape[1:])) // PACK_C
        return rows, -(-rows // HALO) * HALO

    def pack_rows(ls):
        return jnp.concatenate([_tile_pad(row_form[k][l].reshape(-1, PACK_C)) for k, _ in BIG for l in ls], axis=0)

    def whole_weights(wall, ls):
        out, off = {}, 0
        for k, ax in BIG:
            rows, padded = part_rows(k)
            _, r, c = row_form[k].shape
            order = _ff_order if k == "ffn_up" else (lambda b: b)
            out[_big_key(k, ax)] = [order(wall[:, off + i * padded:off + i * padded + rows]).reshape(1, N_DEV * r, c)
                                    for i in range(len(ls))]
            off += len(ls) * padded
        return out

    pack0 = pack_rows(first)
    whole0 = whole_weights(all_gather("gather_weights0", pack0, False).reshape(N_DEV, -1, PACK_C), first)
    pack_rest = pack_rows(rest)
    wsend, wrecv, wsrc, wland, wtok = xchg_start("gather_weights_start", pack_rest, False)
    small_w = {k: W[k] for k in SMALL[1:]}
    small_w["ffn_conv_w"] = cw_all
    later = {}

    def layer_weights(l, xin):
        if l >= 1 and not later:
            src, land = xchg_wait("gather_weights_wait", wsend, wrecv, wsrc, wland, xin, False)
            later.update(whole_weights(lax.dynamic_update_slice(land, src[None], (me, 0, 0)), rest))
        big = {k: v[0] for k, v in whole0.items()} if l == 0 else {k: v[l - 1] for k, v in later.items()}
        return {**big, **{k: v[l:l + 1] for k, v in small_w.items()}}

    srows = 40
    s_in = jnp.concatenate([c_all, c_ctx[None], jnp.zeros((srows - N_DEV * nb - 1, D), F32)], axis=0)
    s_act = _silu(s_in)
    ab_loc = lax.dynamic_slice_in_dim(ada_b, me * ncol, ncol, axis=1)[:, None, :]
    mod_part = ada_fwd("ada_fwd", s_act, ada_w, ab_loc)
    mod_all = all_gather("gather_mod", mod_part.reshape(-1, ncol), True).reshape(N_DEV, DEPTH, srows, ncol)
    mod_rows = mod_all.transpose(1, 2, 0, 3).reshape(DEPTH, srows, N_DEV * ncol)
    mod_l = lax.dynamic_slice_in_dim(mod_rows, me * nb, nb, axis=1).reshape(DEPTH, nb, 6, D)
    mod_c = mod_rows[:, N_DEV * nb].reshape(DEPTH, 1, 6, D)
    tile_is_ctx = (jnp.arange(tps) == 0)[None, None, :, None, None]
    modtab = jnp.where(tile_is_ctx, mod_c[:, :, None], mod_l[:, :, None])
    modtab = jnp.concatenate([modtab, jnp.zeros((DEPTH, nb, tps, 2, D), F32)], axis=3).reshape(DEPTH, nt, 8, D)
    modtab = modtab + wtok[0, 0]

    def grad_blocks(grads, ls):
        def blocks(k, ax, l):
            b = grads[l][_big_key(k, ax)].astype(BF16).reshape(N_DEV, -1, PACK_C)
            return _ff_order(b) if k == "ffn_up" else b

        return jnp.concatenate([_tile_pad(blocks(k, ax, l)) for k, ax in BIG for l in ls], axis=1)

    early = {}

    def grads_ready(l, grads):
        if l != 1:
            return None
        early["sems"] = xchg_start("grad_exchange_start", grad_blocks(grads, rest), True)
        return early["sems"][4][0, 0]

    xs = jnp.concatenate([ctx, x], axis=1).reshape(n, D)
    loss_loc, dxs, dmodtab, grads = local_step(xs, loss_target.reshape(nb * seq, D), modtab, layer_weights, dims,
                                               grads_ready)
    loss = lax.psum(loss_loc, ("x", "y", "c"))
    grad_x = dxs.reshape(nb, t, D)[:, tr:]
    gl = {k: jnp.stack([g[k] for g in grads]) for k in grads[0] if k in SMALL or k == "ffn_conv_w"}

    dm = dmodtab.reshape(DEPTH, nb, tps, 8, D)[:, :, :, :6]
    dmod_l = dm[:, :, 1:].sum(2).reshape(DEPTH, nb, 6 * D)
    dmod_c = dm[:, :, 0].sum(1).reshape(DEPTH, 1, 6 * D)
    gl["ada_b"] = dmod_l.sum(1) + dmod_c[:, 0]
    small_list = [gl[k] for k in SMALL] + [gl["ffn_conv_w"]]
    small_shapes = [a.shape for a in small_list]
    fsend, frecv, fsrc, fland, ftok = xchg_start("grad_exchange0_start", grad_blocks(grads, first), True)
    spack = _pack(small_list + [jnp.concatenate([dmod_l, dmod_c], axis=1)], F32) + ftok[0, 0]
    rs = spack.shape[0]
    sall = all_gather("gather_small_grads", spack, True).reshape(N_DEV, rs, PACK_C)
    nsmall = sum(int(np.prod(s)) for s in small_shapes)
    dmo = sall.reshape(N_DEV, -1)[:, nsmall:nsmall + DEPTH * (nb + 1) * 6 * D].reshape(N_DEV, DEPTH, nb + 1, 6 * D)
    dl_all = dmo[:, :, :nb].transpose(1, 0, 2, 3).reshape(DEPTH, N_DEV * nb, 6 * D)
    dc_all = dmo[:, :, nb].sum(0)[:, None]
    dmod_rows = jnp.concatenate([dl_all, dc_all, jnp.zeros((DEPTH, srows - N_DEV * nb - 1, 6 * D), F32)], axis=1)
    dmod_loc = lax.dynamic_slice_in_dim(dmod_rows.reshape(DEPTH, srows, N_DEV, ncol), me, 1, axis=2)[:, :, 0]
    d_ada_w, d_s = ada_bwd("ada_bwd", s_act, ada_w, dmod_loc)
    sg = jax.nn.sigmoid(c_ctx)
    dcc = d_s[:, N_DEV * nb].sum(0) * (sg * (1.0 + c_ctx * (1.0 - sg)))
    ccp = jnp.concatenate([dcc[None], jnp.zeros((7, D), F32)], axis=0)
    ccall = all_gather("gather_cctx", ccp, True).reshape(N_DEV, 8, D)

    esend, erecv, esrc, eland, _ = early["sems"]
    def landed(tag, sems, src, land, after):
        src, land = xchg_wait(tag + "_wait", sems[0], sems[1], src, land, after, True)
        mine = lax.dynamic_slice_in_dim(src, me, 1, axis=0)
        return sum8(tag + "_sum", lax.dynamic_update_slice(land, mine, (me, 0, 0)))

    gsum_rest = landed("grad_exchange", (esend, erecv), esrc, eland, ccall)
    gsum_first = landed("grad_exchange0", (fsend, frecv), fsrc, fland, gsum_rest)
    res = {}

    def update2d(tag, k, g):
        last = W[k].shape[-1]
        outs = adamw(tag, W[k].reshape(-1, last), M[k].reshape(-1, last), V[k].reshape(-1, last),
                     g.reshape(1, -1, last))
        res[k] = [a.reshape(W[k].shape) for a in outs]

    off0, off1 = 0, 0
    for k, ax in BIG:
        rows, padded = part_rows(k)
        _, r, c = row_form[k].shape
        parts = [gsum_first[off0:off0 + rows]] + [gsum_rest[off1 + i * padded:off1 + i * padded + rows]
                                                  for i in range(len(rest))]
        g = jnp.stack([p.reshape(r, c) for p in parts])
        update2d("adamw_" + k, k, jnp.swapaxes(g, 1, 2) if ax == 2 else g)
        off0 += padded
        off1 += len(rest) * padded

    def update(tag, keys, g8):
        outs = adamw(tag, _pack([W[k] for k in keys], F32), _pack([M[k] for k in keys], F32),
                     _pack([V[k] for k in keys], F32), g8)
        for i, arr in enumerate(outs):
            for k, a in zip(keys, _unpack(arr, [W[k].shape for k in keys])):
                res.setdefault(k, [None] * 4)[i] = a

    nrep = sum(int(np.prod(W[k].shape)) for k in SMALL)
    sflat = sall.reshape(N_DEV, -1)
    def pack8(a):
        a = a.reshape(N_DEV, -1)
        pad = (-a.shape[1]) % (8 * PACK_C)
        return jnp.concatenate([a, jnp.zeros((N_DEV, pad), F32)], axis=1).reshape(N_DEV, -1, PACK_C)

    update("adamw_small", SMALL, pack8(sflat[:, :nrep]))
    ncw = ffn_conv_w.shape[2]
    cw8 = sflat[:, nrep:nsmall].reshape(N_DEV, DEPTH, 3, N_DEV * ncw)
    cw8 = lax.dynamic_slice_in_dim(cw8, me * ncw, ncw, axis=3)
    update("adamw_conv", ["ffn_conv_w"], pack8(cw8))
    update2d("adamw_ada", "ada_w", d_ada_w)
    update("adamw_cctx", ["c_ctx"], ccall)

    out = [loss, grad_x]
    for i in range(4):
        out += [res[k][i] for k in names]
    return tuple(out)
```

```python
import functools
import math

import numpy as np
import jax
import jax.numpy as jnp
from jax import lax
from jax.experimental import pallas as pl
from jax.experimental.pallas import tpu as pltpu

F32 = jnp.float32
BF16 = jnp.bfloat16
HI = lax.Precision.HIGHEST
MESH = pl.DeviceIdType.MESH

N_DEV = 8
D = 1024
DEPTH = 4
CHUNK = 64
EPS = 1e-6
ALPHA = (2 * DEPTH) ** 0.25
GLA_TAU = 16.0
ROPE_BASE = 10000.0
MLA_SCALE = 96 ** -0.5
D_FF = 2816
FF_CHUNK = 1408
P_PAD = 2048
VMEM_LIMIT_BYTES = 56 << 20

ADAM_LR, ADAM_B1, ADAM_B2, ADAM_EPS, ADAM_WD, ADAM_STEP = 0.001, 0.9, 0.999, 1e-08, 0.01, 10

D_IN = 1984
W_IN_SEGS = [(0, 512), (544, 1408), (512, 32), (1952, 32)]
W_IN_INV_SEGS = [(0, 512), (1920, 32), (512, 1408), (1952, 32)]
FF_SEGS = [(0, FF_CHUNK), (D_FF, FF_CHUNK), (FF_CHUNK, FF_CHUNK), (D_FF + FF_CHUNK, FF_CHUNK)]


def _cols(a, segs):
    return jnp.concatenate([a[..., s:s + n] for s, n in segs], axis=-1)


def _rows(a, segs):
    return jnp.concatenate([a[s:s + n] for s, n in segs], axis=0)


def _pad_heads(wt, per_head):
    c = wt.shape[1]
    wt = wt.reshape(8, per_head, c)
    return jnp.concatenate([wt, jnp.zeros((8, 128 - per_head, c), wt.dtype)], axis=1).reshape(1024, c)


def _unpad_heads(g, per_head):
    c = g.shape[1]
    return g.reshape(8, 128, c)[:, :per_head].reshape(8 * per_head, c)


def _cparams(sem=None):
    return pltpu.CompilerParams(vmem_limit_bytes=VMEM_LIMIT_BYTES, dimension_semantics=sem)


@jax.custom_vjp
def bdot(a, w):
    return jnp.dot(a.astype(BF16), w.astype(BF16), preferred_element_type=F32)


def _bdot_fwd(a, w):
    return bdot(a, w), (a, w)


def _bdot_bwd(res, ct):
    a, w = res
    ctb = ct.astype(BF16)
    da = lax.dot_general(ctb, w.astype(BF16), (((1,), (1,)), ((), ())), preferred_element_type=F32)
    dw = lax.dot_general(a.astype(BF16), ctb, (((0,), (0,)), ((), ())), preferred_element_type=F32)
    return da.astype(a.dtype), dw.astype(w.dtype)


bdot.defvjp(_bdot_fwd, _bdot_bwd)


@jax.custom_vjp
def bdot_nt(a, wt):
    return lax.dot_general(a.astype(BF16), wt.astype(BF16), (((1,), (1,)), ((), ())), preferred_element_type=F32)


def _bdot_nt_fwd(a, wt):
    return bdot_nt(a, wt), (a, wt)


def _bdot_nt_bwd(res, ct):
    a, wt = res
    ctb = ct.astype(BF16)
    da = jnp.dot(ctb, wt.astype(BF16), preferred_element_type=F32)
    dwt = lax.dot_general(ctb, a.astype(BF16), (((0,), (0,)), ((), ())), preferred_element_type=F32)
    return da.astype(a.dtype), dwt.astype(wt.dtype)


bdot_nt.defvjp(_bdot_nt_fwd, _bdot_nt_bwd)


@jax.custom_vjp
def bdot_tn(a, b):
    return lax.dot_general(a.astype(BF16), b.astype(BF16), (((0,), (0,)), ((), ())), preferred_element_type=F32)


def _bdot_tn_fwd(a, b):
    return bdot_tn(a, b), (a, b)


def _bdot_tn_bwd(res, ct):
    a, b = res
    ctb = ct.astype(BF16)
    da = lax.dot_general(b.astype(BF16), ctb, (((1,), (1,)), ((), ())), preferred_element_type=F32)
    db = jnp.dot(a.astype(BF16), ctb, preferred_element_type=F32)
    return da.astype(a.dtype), db.astype(b.dtype)


bdot_tn.defvjp(_bdot_tn_fwd, _bdot_tn_bwd)


def _split3(x):
    x1 = x.astype(BF16)
    r1 = x - x1.astype(F32)
    x2 = r1.astype(BF16)
    return x1, x2, (r1 - x2.astype(F32)).astype(BF16)


@jax.custom_vjp
def xdot(x, m):
    mb = m.astype(BF16)
    return sum(jnp.dot(xi, mb, preferred_element_type=F32) for xi in _split3(x))


def _xdot_bwd(m, ct):
    mb = m.astype(BF16)
    dx = sum(lax.dot_general(ci, mb, (((1,), (1,)), ((), ())), preferred_element_type=F32) for ci in _split3(ct))
    return dx, jnp.zeros_like(m)


xdot.defvjp(lambda x, m: (xdot(x, m), m), _xdot_bwd)


@jax.custom_vjp
def xdot_l(m, x):
    mb = m.astype(BF16)
    return sum(jnp.dot(mb, xi, preferred_element_type=F32) for xi in _split3(x))


def _xdot_l_bwd(m, ct):
    mb = m.astype(BF16)
    dx = sum(lax.dot_general(mb, ci, (((0,), (0,)), ((), ())), preferred_element_type=F32) for ci in _split3(ct))
    return jnp.zeros_like(m), dx


xdot_l.defvjp(lambda m, x: (xdot_l(m, x), m), _xdot_l_bwd)


def _swap_fn(half):
    def swap(x):
        n = x.shape[1]
        first = (_iota(x.shape, 1) // half) % 2 == 0
        return jnp.where(first, pltpu.roll(x, n - half, 1), pltpu.roll(x, half, 1))

    f = jax.custom_vjp(swap)
    f.defvjp(lambda x: (swap(x), None), lambda _, ct: (swap(ct),))
    return f


_swap16 = _swap_fn(16)
_swap8 = _swap_fn(8)


def _rope_key_fn():
    def rope_lanes(x):
        lane = _iota(x.shape, 1)
        return jnp.logical_and(lane >= 64, lane < 96)

    def place(misc):
        moved = pltpu.roll(misc, 32, 1)
        return jnp.tile(jnp.where(rope_lanes(moved), moved, 0.0), (1, 8))

    def place_t(ct):
        acc = ct[:, 0:128]
        for h in range(1, 8):
            acc = acc + ct[:, h * 128:(h + 1) * 128]
        return pltpu.roll(jnp.where(rope_lanes(acc), acc, 0.0), 96, 1)

    f = jax.custom_vjp(place)
    f.defvjp(lambda misc: (place(misc), None), lambda _, ct: (place_t(ct),))
    return f


_rope_key = _rope_key_fn()


def hdot(a, b):
    return jnp.dot(a, b, precision=HI, preferred_element_type=F32)


def _iota(shape, axis):
    return lax.broadcasted_iota(jnp.int32, shape, axis)


def _group_avg(n, g):
    return (_iota((n, n), 0) // g == _iota((n, n), 1) // g).astype(F32) * (1.0 / g)


def _silu(x):
    return x * jax.nn.sigmoid(x)


def _layer_norm(z, g, b):
    mu = jnp.mean(z, axis=-1, keepdims=True)
    zc = z - mu
    var = jnp.mean(zc * zc, axis=-1, keepdims=True)
    return zc * lax.rsqrt(var + EPS) * g + b


def _rms(x, g):
    return x * lax.rsqrt(jnp.mean(x * x, axis=-1, keepdims=True) + EPS) * g


def mm(name, a, wt, out_dtype, tn, tr, mod=None, sel=None):
    n, k = a.shape
    nw = wt.shape[0]

    def body(*refs):
        if mod is not None:
            a_ref, m_ref, w_ref, o_ref = refs
            m = m_ref[0]
            av = a_ref[...] * (1.0 + m[sel[0]:sel[0] + 1]) + m[sel[1]:sel[1] + 1]
        else:
            a_ref, w_ref, o_ref = refs
            av = a_ref[...]
        o_ref[...] = lax.dot_general(av.astype(BF16), w_ref[...], (((1,), (1,)), ((), ())),
                                     preferred_element_type=F32).astype(o_ref.dtype)

    in_specs = [pl.BlockSpec((tr, k), lambda c, j: (j, 0))]
    args = [a]
    if mod is not None:
        in_specs.append(pl.BlockSpec((1, 8, k), lambda c, j: (j, 0, 0)))
        args.append(mod)
    in_specs.append(pl.BlockSpec((tn, k), lambda c, j: (c, 0)))
    args.append(wt)
    return pl.pallas_call(
        body, name=name, grid=(nw // tn, n // tr), in_specs=in_specs,
        out_specs=pl.BlockSpec((tr, tn), lambda c, j: (j, c)),
        out_shape=jax.ShapeDtypeStruct((n, nw), out_dtype), compiler_params=_cparams(("parallel", "arbitrary")),
    )(*args)


def mm_tn(name, dc, a, tn, tr, mod=None, sel=None):
    n, k = a.shape
    nw = dc.shape[1]

    def body(*refs):
        if mod is not None:
            d_ref, a_ref, m_ref, o_ref = refs
            m = m_ref[0]
            av = a_ref[...] * (1.0 + m[sel[0]:sel[0] + 1]) + m[sel[1]:sel[1] + 1]
        else:
            d_ref, a_ref, o_ref = refs
            av = a_ref[...]

        @pl.when(pl.program_id(1) == 0)
        def _():
            o_ref[...] = jnp.zeros_like(o_ref)

        o_ref[...] += lax.dot_general(d_ref[...].astype(BF16), av.astype(BF16), (((0,), (0,)), ((), ())),
                                      preferred_element_type=F32)

    in_specs = [pl.BlockSpec((tr, tn), lambda c, j: (j, c)), pl.BlockSpec((tr, k), lambda c, j: (j, 0))]
    args = [dc, a]
    if mod is not None:
        in_specs.append(pl.BlockSpec((1, 8, k), lambda c, j: (j, 0, 0)))
        args.append(mod)
    return pl.pallas_call(
        body, name=name, grid=(nw // tn, n // tr), in_specs=in_specs,
        out_specs=pl.BlockSpec((tn, k), lambda c, j: (c, 0)),
        out_shape=jax.ShapeDtypeStruct((nw, k), F32), compiler_params=_cparams(("parallel", "arbitrary")),
    )(*args)


def mm_modbwd(name, dc, wt, x, mod, add, sel, tr):
    n, k = dc.shape
    dm = wt.shape[1]

    def body(dc_ref, wt_ref, x_ref, m_ref, add_ref, dx_ref, dm_ref):
        dh = jnp.dot(dc_ref[...].astype(BF16), wt_ref[...], preferred_element_type=F32)
        m = m_ref[0]
        dx_ref[...] = add_ref[...] + dh * (1.0 + m[sel[0]:sel[0] + 1])
        dsc = jnp.sum(dh * x_ref[...], axis=0, keepdims=True)
        dsh = jnp.sum(dh, axis=0, keepdims=True)
        rows = _iota((8, dm), 0)
        dm_ref[0] = jnp.where(rows == sel[0], dsc, 0.0) + jnp.where(rows == sel[1], dsh, 0.0)

    return pl.pallas_call(
        body, name=name, grid=(n // tr,),
        in_specs=[pl.BlockSpec((tr, k), lambda j: (j, 0)), pl.BlockSpec((k, dm), lambda j: (0, 0)),
                  pl.BlockSpec((tr, dm), lambda j: (j, 0)), pl.BlockSpec((1, 8, dm), lambda j: (j, 0, 0)),
                  pl.BlockSpec((tr, dm), lambda j: (j, 0))],
        out_specs=[pl.BlockSpec((tr, dm), lambda j: (j, 0)), pl.BlockSpec((1, 8, dm), lambda j: (j, 0, 0))],
        out_shape=[jax.ShapeDtypeStruct((n, dm), F32), jax.ShapeDtypeStruct((n // tr, 8, dm), F32)],
        compiler_params=_cparams(("arbitrary",)),
    )(dc, wt, x, mod, add)


def _spec(kind, arr, width, cb, tr, tps):
    if kind == "row":
        return pl.BlockSpec((tr, width), lambda j: (j, cb))
    if kind == "pos":
        return pl.BlockSpec((tr, width), lambda j: (j % tps, cb))
    if kind == "tile":
        return pl.BlockSpec((1,) + arr.shape[1:], lambda j: (j, 0, 0))
    if kind == "par":
        return pl.BlockSpec(arr.shape, lambda j: (0, 0))
    raise ValueError(kind)


def _load(kind, ref):
    v = ref[0] if kind == "tile" else ref[...]
    return v.astype(F32)


def tile_fwd(name, fn, ins, outs, n, tr, tps):
    kinds = [i[0] for i in ins]

    def body(*refs):
        vals = [_load(k, r) for k, r in zip(kinds, refs[:len(ins)])]
        res = fn(*vals)
        for r, o in zip(refs[len(ins):], res):
            r[...] = o.astype(r.dtype)

    return pl.pallas_call(
        body, name=name, grid=(n // tr,),
        in_specs=[_spec(k, a, w, cb, tr, tps) for k, a, w, cb in ins],
        out_specs=[pl.BlockSpec((tr, w), lambda j: (j, 0)) for w, _ in outs],
        out_shape=[jax.ShapeDtypeStruct((n, w), dt) for w, dt in outs],
        compiler_params=_cparams(("arbitrary",)),
    )(*[i[1] for i in ins])


def tile_bwd(name, fn, ins, cots, want, n, tr, tps):
    kinds = [i[0] for i in ins]
    widx = [w[0] for w in want]
    ni, nc = len(ins), len(cots)

    def body(*refs):
        vals = [_load(k, r) for k, r in zip(kinds, refs[:ni])]
        cvals = tuple(r[...].astype(F32) for r in refs[ni:ni + nc])

        def f(*dv):
            full = list(vals)
            for i, v in zip(widx, dv):
                full[i] = v
            return tuple(fn(*full))

        _, vjp = jax.vjp(f, *[vals[i] for i in widx])
        grads = vjp(cvals)
        first = pl.program_id(0) == 0
        for r, g, i in zip(refs[ni + nc:], grads, widx):
            if kinds[i] == "par":
                @pl.when(first)
                def _(r=r):
                    r[...] = jnp.zeros_like(r)
                r[...] += g
            elif kinds[i] == "tile":
                r[0] = g.astype(r.dtype)
            else:
                r[...] = g.astype(r.dtype)

    out_specs, out_shape = [], []
    for i, dt in want:
        k, a, w, cb = ins[i]
        if k == "par":
            out_specs.append(pl.BlockSpec(a.shape, lambda j: (0, 0)))
            out_shape.append(jax.ShapeDtypeStruct(a.shape, F32))
        elif k == "tile":
            out_specs.append(pl.BlockSpec((1,) + a.shape[1:], lambda j: (j, 0, 0)))
            out_shape.append(jax.ShapeDtypeStruct(a.shape, F32))
        else:
            out_specs.append(pl.BlockSpec((tr, w), lambda j: (j, 0)))
            out_shape.append(jax.ShapeDtypeStruct((n, w), dt))
    return pl.pallas_call(
        body, name=name, grid=(n // tr,),
        in_specs=[_spec(k, a, w, cb, tr, tps) for k, a, w, cb in ins]
        + [pl.BlockSpec((tr, c.shape[1]), lambda j: (j, 0)) for c in cots],
        out_specs=out_specs, out_shape=out_shape, compiler_params=_cparams(("arbitrary",)),
    )(*[i[1] for i in ins], *cots)


def pre_fn(p_gq, p_rq, p_rk, p_cq, p_ckv, p_misc, rcos, rsin, mcos, msin,
           w2f, w2b, b2f, b2b, retf, retb, qg, kvg, wuq, wuk, wuv):
    tr = p_gq.shape[0]
    gq = p_gq * (32 ** -0.5)
    af = jax.nn.log_sigmoid(hdot(p_misc, w2f) + b2f) * (1.0 / GLA_TAU)
    ab = jax.nn.log_sigmoid(hdot(p_misc, w2b) + b2b) * (1.0 / GLA_TAU)
    arf = jnp.zeros((tr, 128), F32) + retf
    arb = jnp.zeros((tr, 128), F32) + retb
    rq = p_rq * rcos + _swap16(p_rq) * rsin
    rks = p_rk * (32 ** -0.5)
    rk = rks * rcos + _swap16(rks) * rsin
    qp = bdot_nt(_rms(p_cq, qg), wuq) * MLA_SCALE
    ckvn = _rms(p_ckv, kvg)
    kp = bdot_nt(ckvn, wuk) + _rope_key(p_misc)
    mc, ms = jnp.tile(mcos, (1, 8)), jnp.tile(msin, (1, 8))
    v = bdot_nt(ckvn, wuv)
    return gq, af, ab, arf, arb, rq, rk, qp * mc + _swap8(qp) * ms, kp * mc + _swap8(kp) * ms, v


def post_fn(ogf, ogb, orf, orb, om, gg, rg, x, mod, gng, wout, lng, lnb):
    avg = _group_avg(256, 64)
    og = ogf + ogb
    mg = og * lax.rsqrt(xdot(og * og, avg) + EPS) * gng * _silu(gg)
    orr = orf + orb
    oc = orr - xdot(orr, avg)
    mr = oc * lax.rsqrt(xdot(oc * oc, avg) + EPS) * _silu(rg)
    m = jnp.concatenate([mg, mr, om], axis=1)
    y = bdot(m, wout)
    return (_layer_norm(ALPHA * x + mod[2:3] * y, lng, lnb),)


def ln2_fn(x1, f, mod, lng, lnb):
    return (_layer_norm(ALPHA * x1 + mod[5:6] * f, lng, lnb),)


def scan_step(q, k, v, a, st, rev):
    ii, jj = _iota((CHUNK, CHUNK), 0), _iota((CHUNK, CHUNK), 1)
    tri = ((jj >= ii) if rev else (jj <= ii)).astype(F32)
    b = xdot_l(tri, a)
    btot = jnp.sum(a, axis=0, keepdims=True)
    qe = q * jnp.exp(b - btot)
    ke = k * jnp.exp(btot - b)
    lane = _iota((1, 128), 1)
    q4 = jnp.concatenate([qe * (lane // 32 == h).astype(F32) for h in range(4)], axis=0)
    att = bdot_nt(q4, ke)
    att = jnp.where(jnp.concatenate([tri] * 4, axis=0) > 0, att, 0.0)
    r = bdot(att, v)
    col = _iota((1, 256), 1)
    o = bdot_nt(q * jnp.exp(b), st)
    for h in range(4):
        o = o + r[h * CHUNK:(h + 1) * CHUNK] * (col // 64 == h).astype(F32)
    vk = bdot_tn(v, ke)
    bd = (_iota((256, 128), 0) // 64 == _iota((256, 128), 1) // 32).astype(F32)
    return o, st * jnp.exp(btot) + vk * bd


def _chunk_maps(nch, nctx):
    def fwd(s):
        return s

    def bwd(s):
        return jnp.where(s < nctx, nctx - 1 - s, nch - 1 - (s - nctx))
    return fwd, bwd


def _per_sample(arr, nb):
    return arr.reshape(nb, arr.shape[0] // nb, arr.shape[1])


def scan_fwd(name, q, k, v, af, ab, nb, nch, nctx):
    n = af.shape[0]
    fmap, bmap = _chunk_maps(nch, nctx)

    def body(qf, kf, vf, a_f, qb, kb, vb, a_b, of_ref, ob_ref, stf_ref, stb_ref, s_scr):
        @pl.when(pl.program_id(0) == 0)
        def _():
            s_scr[...] = jnp.zeros_like(s_scr)

        for i in range(nb):
            stf_ref[0, i] = s_scr[2 * i]
            stb_ref[0, i] = s_scr[2 * i + 1]
            o, sn = scan_step(qf[i], kf[i], vf[i], a_f[i], s_scr[2 * i], False)
            of_ref[i] = o
            s_scr[2 * i] = sn
            o, sn = scan_step(qb[i], kb[i], vb[i], a_b[i], s_scr[2 * i + 1], True)
            ob_ref[i] = o
            s_scr[2 * i + 1] = sn

    def specs(m):
        return [pl.BlockSpec((nb, CHUNK, w), lambda s, cb=cb: (0, m(s), cb)) for _, w, cb in (q, k, v)] + \
               [pl.BlockSpec((nb, CHUNK, 128), lambda s: (0, m(s), 0))]

    ps = lambda a: _per_sample(a, nb)
    of, ob, stf, stb = pl.pallas_call(
        body, name=name, grid=(nch,), in_specs=specs(fmap) + specs(bmap),
        out_specs=[pl.BlockSpec((nb, CHUNK, 256), lambda s: (0, fmap(s), 0)),
                   pl.BlockSpec((nb, CHUNK, 256), lambda s: (0, bmap(s), 0)),
                   pl.BlockSpec((1, nb, 256, 128), lambda s: (s, 0, 0, 0)),
                   pl.BlockSpec((1, nb, 256, 128), lambda s: (s, 0, 0, 0))],
        out_shape=[jax.ShapeDtypeStruct((nb, n // nb, 256), F32)] * 2
        + [jax.ShapeDtypeStruct((nch, nb, 256, 128), F32)] * 2,
        scratch_shapes=[pltpu.VMEM((2 * nb, 256, 128), F32)], compiler_params=_cparams(("arbitrary",)),
    )(ps(q[0]), ps(k[0]), ps(v[0]), ps(af), ps(q[0]), ps(k[0]), ps(v[0]), ps(ab))
    return of.reshape(n, 256), ob.reshape(n, 256), stf, stb


def scan_bwd(name, q, k, v, af, ab, stf, stb, do, nb, nch, nctx):
    n = af.shape[0]
    fmap0, bmap0 = _chunk_maps(nch, nctx)
    fmap = lambda r: fmap0(nch - 1 - r)
    bmap = lambda r: bmap0(nch - 1 - r)

    def body(qf, kf, vf, a_f, sf, dof, qb, kb, vb, a_b, sb, dob,
             dqf, dkf, dvf, daf, dqb, dkb, dvb, dab, ds_scr):
        @pl.when(pl.program_id(0) == 0)
        def _():
            ds_scr[...] = jnp.zeros_like(ds_scr)

        for i in range(nb):
            for d, (qr, kr, vr, ar, sr, dor, outs) in enumerate(((qf, kf, vf, a_f, sf, dof, (dqf, dkf, dvf, daf)),
                                                                   (qb, kb, vb, a_b, sb, dob, (dqb, dkb, dvb, dab)))):
                _, vjp = jax.vjp(functools.partial(scan_step, rev=bool(d)), qr[i], kr[i], vr[i], ar[i], sr[0, i])
                dq, dk, dv, da, ds = vjp((dor[i], ds_scr[2 * i + d]))
                outs[0][i] = dq
                outs[1][i] = dk
                outs[2][i] = dv
                outs[3][i] = da
                ds_scr[2 * i + d] = ds

    def specs(m):
        return [pl.BlockSpec((nb, CHUNK, w), lambda r, cb=cb: (0, m(r), cb)) for _, w, cb in (q, k, v)] + \
               [pl.BlockSpec((nb, CHUNK, 128), lambda r: (0, m(r), 0)),
                pl.BlockSpec((1, nb, 256, 128), lambda r: (nch - 1 - r, 0, 0, 0)),
                pl.BlockSpec((nb, CHUNK, 256), lambda r: (0, m(r), 0))]

    def ospecs(m):
        return [pl.BlockSpec((nb, CHUNK, w), lambda r: (0, m(r), 0)) for w in (128, 128, 256, 128)]

    ps = lambda a: _per_sample(a, nb)
    oshape = [jax.ShapeDtypeStruct((nb, n // nb, w), F32) for w in (128, 128, 256, 128)]
    outs = pl.pallas_call(
        body, name=name, grid=(nch,), in_specs=specs(fmap) + specs(bmap),
        out_specs=ospecs(fmap) + ospecs(bmap), out_shape=oshape + oshape,
        scratch_shapes=[pltpu.VMEM((2 * nb, 256, 128), F32)], compiler_params=_cparams(("arbitrary",)),
    )(ps(q[0]), ps(k[0]), ps(v[0]), ps(af), stf, ps(do), ps(q[0]), ps(k[0]), ps(v[0]), ps(ab), stb, ps(do))
    return [o.reshape(n, o.shape[2]) for o in outs]


def mla_fwd(name, qa, ka, va, nb, tps, tr, nctx_rows):
    n = qa.shape[0]
    t = tps * tr

    def body(q_ref, k_ref, v_ref, o_ref, lse_ref):
        def attend(nk):
            vv = v_ref[0:nk, :]
            first = _iota(vv.shape, 1) < 64
            one = jnp.ones_like(vv)
            res, lses = [], []
            for h in range(2):
                s = lax.dot_general(q_ref[:, h * 128:(h + 1) * 128], k_ref[0:nk, h * 128:(h + 1) * 128],
                                    (((1,), (1,)), ((), ())), preferred_element_type=F32)
                m = jnp.max(s, axis=-1, keepdims=True)
                e = jnp.exp((s - m).astype(BF16))
                r = jnp.dot(e, jnp.where(first == (h == 0), vv, one), preferred_element_type=F32)
                l = r[:, 64:65] if h == 0 else r[:, 0:1]
                res.append(r / l)
                lses.append(m + jnp.log(l))
            lane = _iota((tr, 128), 1) < 64
            o_ref[...] = jnp.where(lane, res[0], res[1])
            lse_ref[...] = jnp.where(lane, lses[0], lses[1])

        @pl.when(pl.program_id(2) == 0)
        def _():
            attend(nctx_rows)

        @pl.when(pl.program_id(2) > 0)
        def _():
            attend(t)

    return pl.pallas_call(
        body, name=name, grid=(nb, 4, tps),
        in_specs=[pl.BlockSpec((tr, 256), lambda b, h, j: (b * tps + j, h)), pl.BlockSpec((t, 256), lambda b, h, j: (b, h)),
                  pl.BlockSpec((t, 128), lambda b, h, j: (b, h))],
        out_specs=[pl.BlockSpec((tr, 128), lambda b, h, j: (b * tps + j, h))] * 2,
        out_shape=[jax.ShapeDtypeStruct((n, 512), F32)] * 2,
        compiler_params=_cparams(("parallel", "parallel", "arbitrary")),
    )(qa, ka, va)


def mla_bwd(name, qa, ka, va, o, lse, do, nb, tps, tr, nctx_rows):
    n = qa.shape[0]
    t = tps * tr

    def body(q_ref, k_ref, v_ref, o_ref, lse_ref, do_ref, dq_ref, dk_ref, dv_ref, dkt, dvt):
        @pl.when(pl.program_id(2) == 0)
        def _():
            dkt[...] = jnp.zeros_like(dkt)
            dvt[...] = jnp.zeros_like(dvt)

        def attend(nk):
            dov = do_ref[...]
            oo = dov * o_ref[...]
            dob = dov.astype(BF16)
            first = _iota(dob.shape, 1) < 64
            dqs = []
            for h in range(2):
                hs = slice(h * 128, (h + 1) * 128)
                qh, kh = q_ref[:, hs], k_ref[0:nk, hs]
                mine = first == (h == 0)
                delta = jnp.sum(jnp.where(mine, oo, 0.0), axis=-1, keepdims=True)
                doh = jnp.where(mine, dob, jnp.zeros_like(dob))
                s = lax.dot_general(qh, kh, (((1,), (1,)), ((), ())), preferred_element_type=F32)
                p = jnp.exp((s - lse_ref[:, h * 64:h * 64 + 1]).astype(BF16))
                dp = lax.dot_general(doh, v_ref[0:nk, :], (((1,), (1,)), ((), ())), preferred_element_type=F32)
                ds = p * (dp - delta).astype(BF16)
                dqs.append(jnp.dot(ds, kh, preferred_element_type=F32))
                dkt[hs, 0:nk] += lax.dot_general(qh, ds, (((0,), (0,)), ((), ())), preferred_element_type=F32)
                dvt[:, 0:nk] += lax.dot_general(doh, p, (((0,), (0,)), ((), ())), preferred_element_type=F32)
            dq_ref[...] = jnp.concatenate(dqs, axis=1)

        @pl.when(pl.program_id(2) == 0)
        def _():
            attend(nctx_rows)

        @pl.when(pl.program_id(2) > 0)
        def _():
            attend(t)

        @pl.when(pl.program_id(2) == tps - 1)
        def _():
            dk_ref[...] = dkt[...].T
            dv_ref[...] = dvt[...].T

    qtile = pl.BlockSpec((tr, 128), lambda b, h, j: (b * tps + j, h))
    return pl.pallas_call(
        body, name=name, grid=(nb, 4, tps),
        in_specs=[pl.BlockSpec((tr, 256), lambda b, h, j: (b * tps + j, h)), pl.BlockSpec((t, 256), lambda b, h, j: (b, h)),
                  pl.BlockSpec((t, 128), lambda b, h, j: (b, h)), qtile, qtile, qtile],
        out_specs=[pl.BlockSpec((tr, 256), lambda b, h, j: (b * tps + j, h)), pl.BlockSpec((t, 256), lambda b, h, j: (b, h)),
                   pl.BlockSpec((t, 128), lambda b, h, j: (b, h))],
        out_shape=[jax.ShapeDtypeStruct((n, 1024), F32), jax.ShapeDtypeStruct((n, 1024), F32),
                   jax.ShapeDtypeStruct((n, 512), F32)],
        scratch_shapes=[pltpu.VMEM((256, t), F32), pltpu.VMEM((128, t), F32)],
        compiler_params=_cparams(("parallel", "parallel", "arbitrary")),
    )(qa, ka, va, o, lse, do)


HALO = 16


def _halo_specs(tr, width, tps, nt):
    r = tr // HALO
    return [pl.BlockSpec((tr, width), lambda j, c: (j, c)),
            pl.BlockSpec((HALO, width), lambda j, c: (jnp.maximum(j * r - 1, 0), c)),
            pl.BlockSpec((HALO, width), lambda j, c: (jnp.minimum((j + 1) * r, nt * r - 1), c))]


def _shifted(u, prev, nxt, j, tps):
    tr = u.shape[0]
    t = j % tps
    has_prev = (t >= 2).astype(F32)
    has_next = jnp.logical_and(t >= 1, t <= tps - 2).astype(F32)
    rows = _iota(u.shape, 0)
    dn = jnp.where(rows == 0, prev[HALO - 1:HALO] * has_prev, pltpu.roll(u, 1, 0))
    up = jnp.where(rows == tr - 1, nxt[0:1] * has_next, pltpu.roll(u, tr - 1, 0))
    return dn, up


def _ffn_act(ucv):
    return _silu(ucv[:, :FF_CHUNK]) * ucv[:, FF_CHUNK:]


def ffn2_fwd(name, u, cw, cb, wd, x1, mod, lng, lnb, tr, tps):
    n = u.shape[0]
    nt = n // tr
    w2 = 2 * FF_CHUNK

    def body(u_ref, up_ref, un_ref, cw_ref, cb_ref, wd_ref, x1_ref, m_ref, g_ref, b_ref, f_ref, x2_ref, ucv_ref, acc):
        j, c = pl.program_id(0), pl.program_id(1)
        uu = u_ref[...].astype(F32)
        dn, up = _shifted(uu, up_ref[...].astype(F32), un_ref[...].astype(F32), j, tps)
        cwv = cw_ref[...]
        ucv = cwv[0:1] * dn + cwv[1:2] * uu + cwv[2:3] * up + cb_ref[...]
        ucv_ref[...] = ucv.astype(ucv_ref.dtype)
        part = bdot(_ffn_act(ucv), wd_ref[...])

        @pl.when(c == 0)
        def _():
            acc[...] = part

        @pl.when(c == 1)
        def _():
            f = acc[...] + part
            f_ref[...] = f
            x2_ref[...] = ln2_fn(x1_ref[...], f, m_ref[0], g_ref[...], b_ref[...])[0]

    return pl.pallas_call(
        body, name=name, grid=(nt, 2),
        in_specs=_halo_specs(tr, w2, tps, nt) + [
            pl.BlockSpec((8, w2), lambda j, c: (0, c)), pl.BlockSpec((1, w2), lambda j, c: (0, c)),
            pl.BlockSpec((FF_CHUNK, D), lambda j, c: (c, 0)), pl.BlockSpec((tr, D), lambda j, c: (j, 0)),
            pl.BlockSpec((1, 8, D), lambda j, c: (j, 0, 0)), pl.BlockSpec((1, D), lambda j, c: (0, 0)),
            pl.BlockSpec((1, D), lambda j, c: (0, 0))],
        out_specs=[pl.BlockSpec((tr, D), lambda j, c: (j, 0)), pl.BlockSpec((tr, D), lambda j, c: (j, 0)),
                   pl.BlockSpec((tr, w2), lambda j, c: (j, c))],
        out_shape=[jax.ShapeDtypeStruct((n, D), F32)] * 2 + [jax.ShapeDtypeStruct((n, 2 * w2), BF16)],
        scratch_shapes=[pltpu.VMEM((tr, D), F32)], compiler_params=_cparams(("arbitrary", "arbitrary")),
    )(u, u, u, cw, cb, wd, x1, mod, lng, lnb)


def ffn2_bwd(name, ucv, wd, df, tr):
    n = ucv.shape[0]
    nt = n // tr
    w2 = 2 * FF_CHUNK

    def body(ucv_ref, wd_ref, df_ref, ducv_ref, dwd_ref):
        j = pl.program_id(1)
        a, g = ucv_ref[:, :FF_CHUNK].astype(F32), ucv_ref[:, FF_CHUNK:].astype(F32)
        sg = jax.nn.sigmoid(a)
        sa = a * sg
        dfb = df_ref[...].astype(BF16)
        dact = lax.dot_general(dfb, wd_ref[...], (((1,), (1,)), ((), ())), preferred_element_type=F32)
        ducv_ref[:, :FF_CHUNK] = (dact * g * (sg + sa * (1.0 - sg))).astype(ducv_ref.dtype)
        ducv_ref[:, FF_CHUNK:] = (dact * sa).astype(ducv_ref.dtype)
        dwd = lax.dot_general((sa * g).astype(BF16), dfb, (((0,), (0,)), ((), ())), preferred_element_type=F32)

        @pl.when(j == 0)
        def _():
            dwd_ref[...] = jnp.zeros_like(dwd_ref)

        dwd_ref[...] += dwd

    return pl.pallas_call(
        body, name=name, grid=(2, nt),
        in_specs=[pl.BlockSpec((tr, w2), lambda c, j: (j, c)), pl.BlockSpec((FF_CHUNK, D), lambda c, j: (c, 0)),
                  pl.BlockSpec((tr, D), lambda c, j: (j, 0))],
        out_specs=[pl.BlockSpec((tr, w2), lambda c, j: (j, c)), pl.BlockSpec((FF_CHUNK, D), lambda c, j: (c, 0))],
        out_shape=[jax.ShapeDtypeStruct((n, 2 * w2), BF16), jax.ShapeDtypeStruct((D_FF, D), F32)],
        compiler_params=_cparams(("parallel", "arbitrary")),
    )(ucv, wd, df)


def conv_bwd(name, ducv, u, cw, tr, tps):
    n = u.shape[0]
    nt = n // tr
    w2 = 2 * FF_CHUNK

    def body(g_ref, gp_ref, gn_ref, u_ref, up_ref, un_ref, cw_ref, du_ref, dcw_ref, dcb_ref):
        c, j = pl.program_id(0), pl.program_id(1)
        g = g_ref[...].astype(F32)
        gdn, gup = _shifted(g, gp_ref[...].astype(F32), gn_ref[...].astype(F32), j, tps)
        uu = u_ref[...].astype(F32)
        udn, uup = _shifted(uu, up_ref[...].astype(F32), un_ref[...].astype(F32), j, tps)
        cwv = cw_ref[...]
        du_ref[...] = (cwv[0:1] * gup + cwv[1:2] * g + cwv[2:3] * gdn).astype(du_ref.dtype)
        rows = _iota((8, w2), 0)
        s = lambda z: jnp.sum(z, axis=0, keepdims=True)
        dcw = (jnp.where(rows == 0, s(g * udn), 0.0) + jnp.where(rows == 1, s(g * uu), 0.0)
               + jnp.where(rows == 2, s(g * uup), 0.0))

        @pl.when(j == 0)
        def _():
            dcw_ref[...] = jnp.zeros_like(dcw_ref)
            dcb_ref[...] = jnp.zeros_like(dcb_ref)

        dcw_ref[...] += dcw
        dcb_ref[...] += s(g)

    hs = _halo_specs(tr, w2, tps, nt)
    swap = lambda spec: pl.BlockSpec(spec.block_shape, lambda c, j, f=spec.index_map: f(j, c))
    return pl.pallas_call(
        body, name=name, grid=(2, nt),
        in_specs=[swap(s) for s in hs] * 2 + [pl.BlockSpec((8, w2), lambda c, j: (0, c))],
        out_specs=[pl.BlockSpec((tr, w2), lambda c, j: (j, c)), pl.BlockSpec((8, w2), lambda c, j: (0, c)),
                   pl.BlockSpec((1, w2), lambda c, j: (0, c))],
        out_shape=[jax.ShapeDtypeStruct((n, 2 * w2), BF16), jax.ShapeDtypeStruct((8, 2 * w2), F32),
                   jax.ShapeDtypeStruct((1, 2 * w2), F32)],
        compiler_params=_cparams(("parallel", "arbitrary")),
    )(ducv, ducv, ducv, u, u, u, cw)


def loss_head(name, xf, target, nb, tps, tr):
    n = xf.shape[0]

    def body(x_ref, t_ref, dy_ref, l_ref):
        lat = (pl.program_id(0) % tps > 0).astype(F32)
        err = (x_ref[...] - t_ref[...]) * lat
        dy_ref[...] = err * (1.0 / D)
        l_ref[...] = jnp.zeros_like(l_ref) + 0.5 * jnp.sum(err * err) * (1.0 / D)

    def tmap(j):
        return ((j // tps) * (tps - 1) + jnp.maximum(j % tps - 1, 0), 0)

    return pl.pallas_call(
        body, name=name, grid=(n // tr,),
        in_specs=[pl.BlockSpec((tr, D), lambda j: (j, 0)), pl.BlockSpec((tr, D), tmap)],
        out_specs=[pl.BlockSpec((tr, D), lambda j: (j, 0)), pl.BlockSpec((1, 8, 128), lambda j: (j, 0, 0))],
        out_shape=[jax.ShapeDtypeStruct((n, D), F32), jax.ShapeDtypeStruct((n // tr, 8, 128), F32)],
        compiler_params=_cparams(("arbitrary",)),
    )(xf, target)


ADAM_MAX_ROWS = 512


def adamw(name, w, m, v, g8):
    r, c = w.shape
    k = g8.shape[0]
    rows = max(b for b in range(8, ADAM_MAX_ROWS + 1, 8) if r % b == 0)
    bc1 = 1.0 - ADAM_B1 ** ADAM_STEP
    bc2 = 1.0 - ADAM_B2 ** ADAM_STEP

    def body(w_ref, m_ref, v_ref, g_ref, go_ref, d_ref, mo_ref, vo_ref):
        g = g_ref[0].astype(F32)
        for i in range(1, k):
            g = g + g_ref[i].astype(F32)
        mn = ADAM_B1 * m_ref[...] + (1.0 - ADAM_B1) * g
        vn = ADAM_B2 * v_ref[...] + (1.0 - ADAM_B2) * (g * g)
        go_ref[...] = g
        mo_ref[...] = mn
        vo_ref[...] = vn
        d_ref[...] = -ADAM_LR * ((mn / bc1) / (jnp.sqrt(vn / bc2) + ADAM_EPS) + ADAM_WD * w_ref[...])

    blk = pl.BlockSpec((rows, c), lambda i: (i, 0))
    return pl.pallas_call(
        body, name=name, grid=(r // rows,),
        in_specs=[blk, blk, blk, pl.BlockSpec((k, rows, c), lambda i: (0, i, 0))],
        out_specs=[blk] * 4, out_shape=[jax.ShapeDtypeStruct((r, c), F32)] * 4,
        compiler_params=_cparams(("parallel",)),
    )(w, m, v, g8)


def ada_fwd(name, s, aw, ab):
    nl, _, cw = aw.shape

    def body(s_ref, w_ref, b_ref, o_ref):
        o_ref[0] = hdot(s_ref[...], w_ref[0]) + b_ref[0]

    return pl.pallas_call(
        body, name=name, grid=(nl,),
        in_specs=[pl.BlockSpec(s.shape, lambda l: (0, 0)), pl.BlockSpec((1, D, cw), lambda l: (l, 0, 0)),
                  pl.BlockSpec((1, 1, cw), lambda l: (l, 0, 0))],
        out_specs=pl.BlockSpec((1, s.shape[0], cw), lambda l: (l, 0, 0)),
        out_shape=jax.ShapeDtypeStruct((nl, s.shape[0], cw), F32), compiler_params=_cparams(("arbitrary",)),
    )(s, aw, ab)


def ada_bwd(name, s, aw, dmod):
    nl, _, cw = aw.shape

    def body(s_ref, w_ref, d_ref, dw_ref, ds_ref):
        dw_ref[0] = lax.dot_general(s_ref[...], d_ref[0], (((0,), (0,)), ((), ())), precision=HI,
                                    preferred_element_type=F32)
        ds_ref[0] = lax.dot_general(d_ref[0], w_ref[0], (((1,), (1,)), ((), ())), precision=HI,
                                    preferred_element_type=F32)

    return pl.pallas_call(
        body, name=name, grid=(nl,),
        in_specs=[pl.BlockSpec(s.shape, lambda l: (0, 0)), pl.BlockSpec((1, D, cw), lambda l: (l, 0, 0)),
                  pl.BlockSpec((1, s.shape[0], cw), lambda l: (l, 0, 0))],
        out_specs=[pl.BlockSpec((1, D, cw), lambda l: (l, 0, 0)), pl.BlockSpec((1, s.shape[0], D), lambda l: (l, 0, 0))],
        out_shape=[jax.ShapeDtypeStruct((nl, D, cw), F32), jax.ShapeDtypeStruct((nl, s.shape[0], D), F32)],
        compiler_params=_cparams(("arbitrary",)),
    )(s, aw, dmod)


def _place():
    return lax.axis_index("x"), lax.axis_index("y"), lax.axis_index("c")


def all_gather(name, x, in_vmem):
    r, c = x.shape

    def body(x_ref, out_ref, send_sems, recv_sems, local_sem):
        px, py, pc = _place()
        me, sibling = (px, py, pc), (px, py, 1 - pc)
        chips = [(1 - px, py), (px, 1 - py), (1 - px, 1 - py)]

        def rows(qx, qy, qc):
            return out_ref.at[pl.ds((4 * qx + 2 * qy + qc) * r, r), :]

        def copy(k, block, to, src=None):
            return pltpu.make_async_remote_copy(
                src_ref=rows(*block) if src is None else src, dst_ref=rows(*block),
                send_sem=send_sems.at[k], recv_sem=recv_sems.at[k], device_id=to, device_id_type=MESH)

        mine = pltpu.make_async_copy(x_ref, rows(*me), local_sem)
        mine.start()
        first = [copy(0, me, sibling, src=x_ref)]
        first += [copy(1 + j, me, (*chip, pc), src=x_ref) for j, chip in enumerate(chips)]
        for cp in first:
            cp.start()
        passed = [copy(4 + j, (*chip, pc), sibling) for j, chip in enumerate(chips)]
        for j, chip in enumerate(chips):
            copy(1 + j, (*chip, pc), me).wait_recv()
            passed[j].start()
        copy(0, sibling, me).wait_recv()
        for j, chip in enumerate(chips):
            copy(4 + j, (*chip, 1 - pc), me).wait_recv()
        for cp in first + passed:
            cp.wait_send()
        mine.wait()

    space = pltpu.VMEM if in_vmem else pl.ANY
    return pl.pallas_call(
        body, name=name, out_shape=jax.ShapeDtypeStruct((N_DEV * r, c), x.dtype),
        in_specs=[pl.BlockSpec(memory_space=space)], out_specs=pl.BlockSpec(memory_space=space),
        scratch_shapes=[pltpu.SemaphoreType.DMA((7,)), pltpu.SemaphoreType.DMA((7,)), pltpu.SemaphoreType.DMA],
        compiler_params=pltpu.CompilerParams(vmem_limit_bytes=VMEM_LIMIT_BYTES),
    )(x)


_HBM = pl.BlockSpec(memory_space=pltpu.HBM)
_SEM = pl.BlockSpec(memory_space=pltpu.SEMAPHORE)
_EFFECT = pltpu.SideEffectType.DATAFLOW_SIDE_EFFECTING


def _partner(k):
    px, py, pc = _place()
    q = (px ^ (k >> 2 & 1), py ^ (k >> 1 & 1), pc ^ (k & 1))
    return q, 4 * q[0] + 2 * q[1] + q[2]


def xchg_start(name, x, per_peer):
    r, c = x.shape[-2:]

    def body(x_ref, land_ref, send_sems, recv_sems, x_thru, land_thru, token):
        px, py, pc = _place()
        my = 4 * px + 2 * py + pc
        for k in range(1, N_DEV):
            q, qi = _partner(k)
            pltpu.make_async_remote_copy(
                src_ref=x_ref.at[qi] if per_peer else x_ref, dst_ref=land_ref.at[my],
                send_sem=send_sems.at[k - 1], recv_sem=recv_sems.at[k - 1], device_id=q, device_id_type=MESH).start()
        token[...] = jnp.zeros_like(token)

    land = lax.empty((N_DEV, r, c), x.dtype)
    return pl.pallas_call(
        body, name=name,
        out_shape=(pltpu.SemaphoreType.DMA((N_DEV - 1,)), pltpu.SemaphoreType.DMA((N_DEV - 1,)),
                   pltpu.HBM(x.shape, x.dtype), pltpu.HBM(land.shape, land.dtype), jax.ShapeDtypeStruct((8, 128), F32)),
        in_specs=(_HBM, _HBM), out_specs=(_SEM, _SEM, _HBM, _HBM, pl.BlockSpec(memory_space=pltpu.VMEM)),
        input_output_aliases={0: 2, 1: 3}, compiler_params=pltpu.CompilerParams(has_side_effects=_EFFECT),
    )(pltpu.with_memory_space_constraint(x, pltpu.HBM), pltpu.with_memory_space_constraint(land, pltpu.HBM))


def xchg_wait(name, send_sems, recv_sems, x_thru, land_thru, after, per_peer):
    def body(x_ref, land_ref, send_sems, recv_sems, after_ref, x_out, land_out):
        for k in range(1, N_DEV):
            q, qi = _partner(k)
            cp = pltpu.make_async_remote_copy(
                src_ref=x_ref.at[qi] if per_peer else x_ref, dst_ref=land_ref.at[qi],
                send_sem=send_sems.at[k - 1], recv_sem=recv_sems.at[k - 1], device_id=q, device_id_type=MESH)
            cp.wait_send()
            cp.wait_recv()

    return pl.pallas_call(
        body, name=name,
        out_shape=(pltpu.HBM(x_thru.shape, x_thru.dtype), pltpu.HBM(land_thru.shape, land_thru.dtype)),
        in_specs=(_HBM, _HBM, _SEM, _SEM, pl.BlockSpec(memory_space=pl.ANY)), out_specs=(_HBM, _HBM),
        input_output_aliases={0: 0, 1: 1}, compiler_params=pltpu.CompilerParams(has_side_effects=_EFFECT),
    )(x_thru, land_thru, send_sems, recv_sems, after)


def _tables(seq, nctx_rows):
    f32 = np.float32
    pos = np.arange(seq, dtype=f32)
    ret_inv = (1.0 / (ROPE_BASE ** np.linspace(0.0, 1.0, 16, dtype=f32))).astype(f32)
    ang = pos[:, None] * ret_inv
    rc, rs = np.cos(ang).astype(f32), np.sin(ang).astype(f32)
    rcos = np.tile(np.concatenate([rc, rc], 1), (1, 4))
    rsin = np.tile(np.concatenate([-rs, rs], 1), (1, 4))
    rows = np.repeat(np.arange(seq // 64, dtype=f32), 64)
    cols = np.tile(np.arange(64, dtype=f32), seq // 64)
    ax_inv = (ROPE_BASE ** (-np.arange(8, dtype=f32) / 8)).astype(f32)
    ra, ca = rows[:, None] * ax_inv, cols[:, None] * ax_inv
    one, zero = np.ones((seq, 64), f32), np.zeros((seq, 64), f32)
    mcos = np.concatenate([one, np.cos(ra), np.cos(ra), np.cos(ca), np.cos(ca), one[:, :32]], 1)
    msin = np.concatenate([zero, -np.sin(ra), np.sin(ra), -np.sin(ca), np.sin(ca), zero[:, :32]], 1)
    ident = lambda t, v: np.concatenate([np.full((nctx_rows, 128), v, f32), t.astype(f32)], 0)
    return [jnp.asarray(ident(rcos, 1.0)), jnp.asarray(ident(rsin, 0.0)),
            jnp.asarray(ident(mcos, 1.0)), jnp.asarray(ident(msin, 0.0))]


def _prep_layer(w, l):
    z = lambda *s: jnp.zeros(s, F32)
    p = {}
    win = _rows(w["w_in_t"][l], W_IN_SEGS)
    p["w_in_t"] = jnp.concatenate([win, jnp.zeros((P_PAD - D_IN, D), win.dtype)], axis=0)
    p["w_up_t"] = w["ffn_up_t"][l]
    p["w_down"] = w["ffn_down"][l]
    p["w_out"] = w["w_out"][l]
    p["wuq"] = _pad_heads(w["mla_w_uq_t"][l], 96)
    p["wuk"] = _pad_heads(w["mla_w_uk_t"][l], 64)
    p["wuv"] = w["mla_w_uv_t"][l]
    gw = w["gla_gate_w"][l]
    p["w2f"] = z(128, 128).at[0:16].set(gw[0])
    p["w2b"] = z(128, 128).at[16:32].set(gw[1])
    p["b2f"], p["b2b"] = w["gla_gate_b"][l][0:1], w["gla_gate_b"][l][1:2]
    lg = jax.nn.log_sigmoid(w["ret_decay"][l])
    p["retf"], p["retb"] = jnp.repeat(lg[0], 32)[None], jnp.repeat(lg[1], 32)[None]
    p["qg"], p["kvg"] = w["mla_q_norm_g"][l][None], w["mla_kv_norm_g"][l][None]
    p["gng"] = jnp.tile(w["gla_norm_g"][l], 4)[None]
    p["ln1g"], p["ln1b"] = w["ln1_g"][l][None], w["ln1_b"][l][None]
    p["ln2g"], p["ln2b"] = w["ln2_g"][l][None], w["ln2_b"][l][None]
    p["cw"] = jnp.concatenate([_cols(w["ffn_conv_w"][l], FF_SEGS), z(5, 2 * D_FF)], axis=0)
    p["cb"] = _cols(w["ffn_conv_b"][l], FF_SEGS)[None]
    return p


def _pre_ins(pa, tabs, p):
    row = lambda w, cb: ("row", pa, w, cb)
    return [row(128, 0), row(128, 6), row(128, 7), row(256, 6), row(128, 14), row(128, 15)] + \
           [("pos", t, 128, 0) for t in tabs] + \
           [("par", p[k], 0, 0) for k in ("w2f", "w2b", "b2f", "b2b", "retf", "retb", "qg", "kvg", "wuq", "wuk", "wuv")]


_PRE_OUTS = [(128, F32)] * 7 + [(1024, BF16), (1024, BF16), (512, BF16)]
_PRE_WANT = [(i, F32) for i in range(6)] + [(i, F32) for i in range(10, 21)]


def _post_ins(ogf, ogb, orf, orb, om, pa, x, mod, p):
    return [("row", ogf, 256, 0), ("row", ogb, 256, 0), ("row", orf, 256, 0), ("row", orb, 256, 0),
            ("row", om, 512, 0), ("row", pa, 256, 2), ("row", pa, 256, 5), ("row", x, D, 0), ("tile", mod, 0, 0),
            ("par", p["gng"], 0, 0), ("par", p["w_out"], 0, 0), ("par", p["ln1g"], 0, 0), ("par", p["ln1b"], 0, 0)]


def layer_fwd(l, x, mod, p, tabs, dims):
    nb, tps, tr, nch, nctx = dims
    n = x.shape[0]
    pa = mm("proj", x, p["w_in_t"], F32, P_PAD, tr, mod=mod, sel=(1, 0))
    gq, af, ab, arf, arb, rq, rk, qa, ka, va = tile_fwd("mix_pre", pre_fn, _pre_ins(pa, tabs, p), _PRE_OUTS, n, tr, tps)
    ogf, ogb, gstf, gstb = scan_fwd("gla_scan", (gq, 128, 0), (pa, 128, 1), (pa, 256, 1), af, ab, nb, nch, nctx)
    orf, orb, rstf, rstb = scan_fwd("ret_scan", (rq, 128, 0), (rk, 128, 0), (pa, 256, 4), arf, arb, nb, nch, nctx)
    om, lse = mla_fwd("mla_attn", qa, ka, va, nb, tps, tr, nctx * CHUNK)
    (x1,) = tile_fwd("mix_post", post_fn, _post_ins(ogf, ogb, orf, orb, om, pa, x, mod, p), [(D, F32)], n, tr, tps)
    u = mm("ffn_up", x1, p["w_up_t"], BF16, 2 * D_FF, tr, mod=mod, sel=(4, 3))
    f, x2, ucv = ffn2_fwd("ffn_down", u, p["cw"], p["cb"], p["w_down"], x1, mod, p["ln2g"], p["ln2b"], tr, tps)
    saved = dict(x=x, pa=pa, gq=gq, af=af, ab=ab, arf=arf, arb=arb, rq=rq, rk=rk, qa=qa, ka=ka, va=va,
                 ogf=ogf, ogb=ogb, gstf=gstf, gstb=gstb, orf=orf, orb=orb, rstf=rstf, rstb=rstb, om=om, lse=lse,
                 x1=x1, u=u, ucv=ucv, f=f)
    return x2, saved


def layer_bwd(l, dx2, s, mod, p, tabs, dims):
    nb, tps, tr, nch, nctx = dims
    n = dx2.shape[0]
    g = {}
    ln2_ins = [("row", s["x1"], D, 0), ("row", s["f"], D, 0), ("tile", mod, 0, 0),
               ("par", p["ln2g"], 0, 0), ("par", p["ln2b"], 0, 0)]
    dx1a, df, dmod_a, g["ln2g"], g["ln2b"] = tile_bwd(
        "ln2_bwd", ln2_fn, ln2_ins, [dx2], [(0, F32), (1, F32), (2, F32), (3, F32), (4, F32)], n, tr, tps)
    ducv, g["w_down"] = ffn2_bwd("ffn_down_bwd", s["ucv"], p["w_down"], df, tr)
    du, g["cw"], g["cb"] = conv_bwd("conv_bwd", ducv, s["u"], p["cw"], tr, tps)
    g["w_up_t"] = mm_tn("ffn_up_dw", du, s["x1"], D_FF, tr, mod=mod, sel=(4, 3))
    dx1, dmod_b = mm_modbwd("ffn_up_dx", du, p["w_up_t"], s["x1"], mod, dx1a, (4, 3), tr)

    post_ins = _post_ins(s["ogf"], s["ogb"], s["orf"], s["orb"], s["om"], s["pa"], s["x"], mod, p)
    want = [(0, F32), (2, F32), (4, F32), (5, F32), (6, F32), (7, F32), (8, F32), (9, F32), (10, F32), (11, F32), (12, F32)]
    dog, dor, dom, dgg, drg, dxa, dmod_c, g["gng"], g["w_out"], g["ln1g"], g["ln1b"] = tile_bwd(
        "mix_post_bwd", post_fn, post_ins, [dx1], want, n, tr, tps)
    dqa, dka, dva = mla_bwd("mla_attn_bwd", s["qa"], s["ka"], s["va"], s["om"], s["lse"], dom, nb, tps, tr,
                            nctx * CHUNK)
    pa = s["pa"]
    gdqf, gdkf, gdvf, gdaf, gdqb, gdkb, gdvb, gdab = scan_bwd(
        "gla_scan_bwd", (s["gq"], 128, 0), (pa, 128, 1), (pa, 256, 1), s["af"], s["ab"], s["gstf"], s["gstb"], dog,
        nb, nch, nctx)
    rdqf, rdkf, rdvf, rdaf, rdqb, rdkb, rdvb, rdab = scan_bwd(
        "ret_scan_bwd", (s["rq"], 128, 0), (s["rk"], 128, 0), (pa, 256, 4), s["arf"], s["arb"], s["rstf"], s["rstb"],
        dor, nb, nch, nctx)

    pre_ins = _pre_ins(pa, tabs, p)
    extra = [gdqf, gdqb, rdqf, rdqb, rdkf, rdkb, gdkf, gdkb, gdvf, gdvb, rdvf, rdvb, dgg, drg]
    kinds = [i[0] for i in pre_ins]
    widx = [w[0] for w in _PRE_WANT]
    npre = len(pre_ins)

    def body(*refs):
        vals = [_load(k, r) for k, r in zip(kinds, refs[:npre])]
        rd = lambda i: refs[npre + i][...].astype(F32)
        cots = (rd(0) + rd(1), rd(14), rd(15), rd(16), rd(17), rd(2) + rd(3), rd(4) + rd(5), rd(18), rd(19), rd(20))

        def f(*dv):
            full = list(vals)
            for i, v in zip(widx, dv):
                full[i] = v
            return tuple(pre_fn(*full))

        _, vjp = jax.vjp(f, *[vals[i] for i in widx])
        grads = vjp(cots)
        dgq, drq, drk, dcq, dckv, dmisc = grads[:6]
        dp = jnp.concatenate([dgq, rd(6) + rd(7), rd(8) + rd(9), rd(12), drq, drk, rd(10) + rd(11), rd(13),
                              dcq, dckv, dmisc], axis=1)
        outs = refs[npre + 21:]
        outs[0][...] = dp.astype(BF16)
        first = pl.program_id(0) == 0
        for r, gr in zip(outs[1:], grads[6:]):
            @pl.when(first)
            def _(r=r):
                r[...] = jnp.zeros_like(r)
            r[...] += gr

    cot_arrays = extra + [gdaf, gdab, rdaf, rdab, dqa, dka, dva]
    par_arrays = [pre_ins[i][1] for i in range(10, 21)]
    res = pl.pallas_call(
        body, name="mix_pre_bwd", grid=(n // tr,),
        in_specs=[_spec(k, a, w, cb, tr, tps) for k, a, w, cb in pre_ins]
        + [pl.BlockSpec((tr, c.shape[1]), lambda j: (j, 0)) for c in cot_arrays],
        out_specs=[pl.BlockSpec((tr, P_PAD), lambda j: (j, 0))] + [pl.BlockSpec(a.shape, lambda j: (0, 0)) for a in par_arrays],
        out_shape=[jax.ShapeDtypeStruct((n, P_PAD), BF16)] + [jax.ShapeDtypeStruct(a.shape, F32) for a in par_arrays],
        compiler_params=_cparams(("arbitrary",)),
    )(*[i[1] for i in pre_ins], *cot_arrays)
    dp = res[0]
    for k, v in zip(("w2f", "w2b", "b2f", "b2b", "retf", "retb", "qg", "kvg", "wuq", "wuk", "wuv"), res[1:]):
        g[k] = v
    g["w_in_t"] = mm_tn("proj_dw", dp, s["x"], P_PAD, tr, mod=mod, sel=(1, 0))
    dx, dmod_d = mm_modbwd("proj_dx", dp, p["w_in_t"], s["x"], mod, dxa, (1, 0), tr)
    return dx, dmod_a + dmod_b + dmod_c + dmod_d, g


def _unprep_grads(g, w, l):
    o = {}
    o["w_in_t"] = _rows(g["w_in_t"], W_IN_INV_SEGS)
    o["ffn_up_t"] = g["w_up_t"]
    o["ffn_down"] = g["w_down"]
    o["w_out"] = g["w_out"]
    o["mla_w_uq_t"] = _unpad_heads(g["wuq"], 96)
    o["mla_w_uk_t"] = _unpad_heads(g["wuk"], 64)
    o["mla_w_uv_t"] = g["wuv"]
    o["gla_gate_w"] = jnp.stack([g["w2f"][0:16], g["w2b"][16:32]])
    o["gla_gate_b"] = jnp.concatenate([g["b2f"], g["b2b"]], axis=0)
    dlg = jnp.stack([g["retf"].reshape(4, 32).sum(-1), g["retb"].reshape(4, 32).sum(-1)])
    o["ret_decay"] = dlg * jax.nn.sigmoid(-w["ret_decay"][l])
    o["mla_q_norm_g"], o["mla_kv_norm_g"] = g["qg"][0], g["kvg"][0]
    o["gla_norm_g"] = g["gng"].reshape(4, 64).sum(0)
    o["ln1_g"], o["ln1_b"], o["ln2_g"], o["ln2_b"] = g["ln1g"][0], g["ln1b"][0], g["ln2g"][0], g["ln2b"][0]
    o["ffn_conv_w"] = _cols(g["cw"][0:3], FF_SEGS)
    o["ffn_conv_b"] = _cols(g["cb"][0], FF_SEGS)
    return o


def local_step(xs, target, modtab, layer_weights, dims, grads_ready=None):
    nb, tps, tr, nch, nctx = dims
    tabs = _tables((tps - 1) * tr, tr)
    x = xs
    saved, preps, ws = [], [], []
    for l in range(DEPTH):
        w = layer_weights(l, x)
        p = _prep_layer(w, 0)
        x, s = layer_fwd(l, x, modtab[l], p, tabs, dims)
        saved.append(s)
        preps.append(p)
        ws.append(w)
    dy, lpart = loss_head("loss_head", x, target, nb, tps, tr)
    loss = jnp.sum(lpart[:, 0, 0])
    dx = dy
    dmods, grads = [None] * DEPTH, [None] * DEPTH
    tok = None
    for l in reversed(range(DEPTH)):
        mod = modtab[l] if tok is None else modtab[l] + tok
        dx, dmods[l], g = layer_bwd(l, dx, saved[l], mod, preps[l], tabs, dims)
        grads[l] = _unprep_grads(g, ws[l], 0)
        tok = grads_ready(l, grads) if grads_ready is not None else None
    return loss, dx, jnp.stack(dmods), grads


BIG = [("ffn_up", 2), ("ffn_down", 1), ("w_out", 1), ("w_in", 2), ("mla_w_uq", 2), ("mla_w_uk", 2), ("mla_w_uv", 2)]
SMALL = ["ada_b", "gla_gate_w", "gla_gate_b", "gla_norm_g", "ret_decay", "mla_q_norm_g", "mla_kv_norm_g",
         "ln1_g", "ln1_b", "ffn_conv_b", "ln2_g", "ln2_b"]
PACK_C = 1024


def _big_key(k, axis):
    return k + "_t" if axis == 2 else k


def sum8(name, g8):
    k, r, c = g8.shape
    rows = max(b for b in range(16, ADAM_MAX_ROWS + 1, 16) if r % b == 0)

    def body(g_ref, o_ref):
        g = g_ref[0].astype(F32)
        for i in range(1, k):
            g = g + g_ref[i].astype(F32)
        o_ref[...] = g

    return pl.pallas_call(
        body, name=name, grid=(r // rows,), in_specs=[pl.BlockSpec((k, rows, c), lambda i: (0, i, 0))],
        out_specs=pl.BlockSpec((rows, c), lambda i: (i, 0)), out_shape=jax.ShapeDtypeStruct((r, c), F32),
        compiler_params=_cparams(("parallel",)),
    )(g8)


def _pack(arrs, dtype):
    flat = jnp.concatenate([a.reshape(-1).astype(dtype) for a in arrs])
    pad = (-flat.shape[0]) % (8 * PACK_C)
    return jnp.concatenate([flat, jnp.zeros((pad,), dtype)]).reshape(-1, PACK_C)


def _unpack(flat2d, shapes):
    flat = flat2d.reshape(-1)
    out, off = [], 0
    for s in shapes:
        sz = int(np.prod(s))
        out.append(flat[off:off + sz].reshape(s))
        off += sz
    return out


def _ff_order(b):
    return jnp.concatenate([b[0:2], b[4:6], b[2:4], b[6:8]], axis=0)


def _tile_pad(a):
    pad = (-a.shape[-2]) % HALO
    return a if pad == 0 else jnp.concatenate([a, jnp.zeros(a.shape[:-2] + (pad, a.shape[-1]), a.dtype)], axis=-2)


def kernel(x, c, ctx, c_ctx, ada_w, ada_b, w_in, gla_gate_w, gla_gate_b, gla_norm_g, ret_decay, mla_q_norm_g, mla_kv_norm_g, mla_w_uq, mla_w_uk, mla_w_uv, w_out, ln1_g, ln1_b, ffn_up, ffn_conv_w, ffn_conv_b, ffn_down, ln2_g, ln2_b, loss_target, m_c_ctx, m_ada_w, m_ada_b, m_w_in, m_gla_gate_w, m_gla_gate_b, m_gla_norm_g, m_ret_decay, m_mla_q_norm_g, m_mla_kv_norm_g, m_mla_w_uq, m_mla_w_uk, m_mla_w_uv, m_w_out, m_ln1_g, m_ln1_b, m_ffn_up, m_ffn_conv_w, m_ffn_conv_b, m_ffn_down, m_ln2_g, m_ln2_b, v_c_ctx, v_ada_w, v_ada_b, v_w_in, v_gla_gate_w, v_gla_gate_b, v_gla_norm_g, v_ret_decay, v_mla_q_norm_g, v_mla_kv_norm_g, v_mla_w_uq, v_mla_w_uk, v_mla_w_uv, v_w_out, v_ln1_g, v_ln1_b, v_ffn_up, v_ffn_conv_w, v_ffn_conv_b, v_ffn_down, v_ln2_g, v_ln2_b):
    names = ["c_ctx", "ada_w", "ada_b", "w_in", "gla_gate_w", "gla_gate_b", "gla_norm_g", "ret_decay", "mla_q_norm_g",
             "mla_kv_norm_g", "mla_w_uq", "mla_w_uk", "mla_w_uv", "w_out", "ln1_g", "ln1_b", "ffn_up", "ffn_conv_w",
             "ffn_conv_b", "ffn_down", "ln2_g", "ln2_b"]
    loc = locals()
    W = {k: loc[k] for k in names}
    M = {k: loc["m_" + k] for k in names}
    V = {k: loc["v_" + k] for k in names}

    nb, seq, _ = x.shape
    tr = ctx.shape[1]
    tps = 1 + seq // tr
    t = tps * tr
    n = nb * t
    nt = nb * tps
    dims = (nb, tps, tr, t // CHUNK, tr // CHUNK)
    px, py, pc = _place()
    me = 4 * px + 2 * py + pc
    ncol = ada_w.shape[2]

    cw_loc = ffn_conv_w.reshape(-1)
    g1 = jnp.concatenate([c.reshape(-1), cw_loc])
    g1 = jnp.concatenate([g1, jnp.zeros(((-g1.shape[0]) % (8 * PACK_C),), F32)]).reshape(-1, PACK_C)
    r1 = g1.shape[0]
    g1a = all_gather("gather_cond", g1, True).reshape(N_DEV, -1)
    c_all = g1a[:, :nb * D].reshape(N_DEV * nb, D)
    cw_all = g1a[:, nb * D:nb * D + cw_loc.shape[0]].reshape(N_DEV, DEPTH, 3, -1).transpose(1, 2, 0, 3).reshape(DEPTH, 3, -1)

    first, rest = [0], list(range(1, DEPTH))
    row_form = {k: (jnp.swapaxes(W[k], 1, 2) if ax == 2 else W[k]).astype(BF16) for k, ax in BIG}

    def part_rows(k):
        rows = int(np.prod(W[k].shape[1:])) // PACK_C
        return rows, -(-rows // HALO) * HALO

    def pack_rows(ls):
        return jnp.concatenate([_tile_pad(row_form[k][l].reshape(-1, PACK_C)) for k, _ in BIG for l in ls], axis=0)

    def whole_weights(wall, ls):
        out, off = {}, 0
        for k, ax in BIG:
            rows, padded = part_rows(k)
            _, r, c = row_form[k].shape
            order = _ff_order if k == "ffn_up" else (lambda b: b)
            out[_big_key(k, ax)] = [order(wall[:, off + i * padded:off + i * padded + rows]).reshape(1, N_DEV * r, c)
                                    for i in range(len(ls))]
            off += len(ls) * padded
        return out

    pack0 = pack_rows(first)
    whole0 = whole_weights(all_gather("gather_weights0", pack0, False).reshape(N_DEV, -1, PACK_C), first)
    pack_rest = pack_rows(rest)
    wsend, wrecv, wsrc, wland, wtok = xchg_start("gather_weights_start", pack_rest, False)
    small_w = {k: W[k] for k in SMALL[1:]}
    small_w["ffn_conv_w"] = cw_all
    later = {}

    def layer_weights(l, xin):
        if l >= 1 and not later:
            src, land = xchg_wait("gather_weights_wait", wsend, wrecv, wsrc, wland, xin, False)
            later.update(whole_weights(lax.dynamic_update_slice(land, src[None], (me, 0, 0)), rest))
        big = {k: v[0] for k, v in whole0.items()} if l == 0 else {k: v[l - 1] for k, v in later.items()}
        return {**big, **{k: v[l:l + 1] for k, v in small_w.items()}}

    srows = 40
    s_in = jnp.concatenate([c_all, c_ctx[None], jnp.zeros((srows - N_DEV * nb - 1, D), F32)], axis=0)
    s_act = _silu(s_in)
    ab_loc = lax.dynamic_slice_in_dim(ada_b, me * ncol, ncol, axis=1)[:, None, :]
    mod_part = ada_fwd("ada_fwd", s_act, ada_w, ab_loc)
    mod_all = all_gather("gather_mod", mod_part.reshape(-1, ncol), True).reshape(N_DEV, DEPTH, srows, ncol)
    mod_rows = mod_all.transpose(1, 2, 0, 3).reshape(DEPTH, srows, N_DEV * ncol)
    mod_l = lax.dynamic_slice_in_dim(mod_rows, me * nb, nb, axis=1).reshape(DEPTH, nb, 6, D)
    mod_c = mod_rows[:, N_DEV * nb].reshape(DEPTH, 1, 6, D)
    tile_is_ctx = (jnp.arange(tps) == 0)[None, None, :, None, None]
    modtab = jnp.where(tile_is_ctx, mod_c[:, :, None], mod_l[:, :, None])
    modtab = jnp.concatenate([modtab, jnp.zeros((DEPTH, nb, tps, 2, D), F32)], axis=3).reshape(DEPTH, nt, 8, D)
    modtab = modtab + wtok[0, 0]

    def grad_blocks(grads, ls):
        def blocks(k, ax, l):
            b = grads[l][_big_key(k, ax)].astype(BF16).reshape(N_DEV, -1, PACK_C)
            return _ff_order(b) if k == "ffn_up" else b

        return jnp.concatenate([_tile_pad(blocks(k, ax, l)) for k, ax in BIG for l in ls], axis=1)

    early = {}

    def grads_ready(l, grads):
        if l != 1:
            return None
        early["sems"] = xchg_start("grad_exchange_start", grad_blocks(grads, rest), True)
        return early["sems"][4][0, 0]

    xs = jnp.concatenate([ctx, x], axis=1).reshape(n, D)
    loss_loc, dxs, dmodtab, grads = local_step(xs, loss_target.reshape(nb * seq, D), modtab, layer_weights, dims,
                                               grads_ready)
    loss = lax.psum(loss_loc, ("x", "y", "c"))
    grad_x = dxs.reshape(nb, t, D)[:, tr:]
    gl = {k: jnp.stack([g[k] for g in grads]) for k in grads[0] if k in SMALL or k == "ffn_conv_w"}

    dm = dmodtab.reshape(DEPTH, nb, tps, 8, D)[:, :, :, :6]
    dmod_l = dm[:, :, 1:].sum(2).reshape(DEPTH, nb, 6 * D)
    dmod_c = dm[:, :, 0].sum(1).reshape(DEPTH, 1, 6 * D)
    gl["ada_b"] = dmod_l.sum(1) + dmod_c[:, 0]
    small_list = [gl[k] for k in SMALL] + [gl["ffn_conv_w"]]
    small_shapes = [a.shape for a in small_list]
    fsend, frecv, fsrc, fland, ftok = xchg_start("grad_exchange0_start", grad_blocks(grads, first), True)
    spack = _pack(small_list + [jnp.concatenate([dmod_l, dmod_c], axis=1)], F32) + ftok[0, 0]
    rs = spack.shape[0]
    sall = all_gather("gather_small_grads", spack, True).reshape(N_DEV, rs, PACK_C)
    nsmall = sum(int(np.prod(s)) for s in small_shapes)
    dmo = sall.reshape(N_DEV, -1)[:, nsmall:nsmall + DEPTH * (nb + 1) * 6 * D].reshape(N_DEV, DEPTH, nb + 1, 6 * D)
    dl_all = dmo[:, :, :nb].transpose(1, 0, 2, 3).reshape(DEPTH, N_DEV * nb, 6 * D)
    dc_all = dmo[:, :, nb].sum(0)[:, None]
    dmod_rows = jnp.concatenate([dl_all, dc_all, jnp.zeros((DEPTH, srows - N_DEV * nb - 1, 6 * D), F32)], axis=1)
    dmod_loc = lax.dynamic_slice_in_dim(dmod_rows.reshape(DEPTH, srows, N_DEV, ncol), me, 1, axis=2)[:, :, 0]
    d_ada_w, d_s = ada_bwd("ada_bwd", s_act, ada_w, dmod_loc)
    sg = jax.nn.sigmoid(c_ctx)
    dcc = d_s[:, N_DEV * nb].sum(0) * (sg * (1.0 + c_ctx * (1.0 - sg)))
    ccp = jnp.concatenate([dcc[None], jnp.zeros((7, D), F32)], axis=0)
    ccall = all_gather("gather_cctx", ccp, True).reshape(N_DEV, 8, D)

    esend, erecv, esrc, eland, _ = early["sems"]
    def landed(tag, sems, src, land, after):
        src, land = xchg_wait(tag + "_wait", sems[0], sems[1], src, land, after, True)
        mine = lax.dynamic_slice_in_dim(src, me, 1, axis=0)
        return sum8(tag + "_sum", lax.dynamic_update_slice(land, mine, (me, 0, 0)))

    gsum_rest = landed("grad_exchange", (esend, erecv), esrc, eland, ccall)
    gsum_first = landed("grad_exchange0", (fsend, frecv), fsrc, fland, gsum_rest)
    res = {}

    def update2d(tag, k, g):
        last = W[k].shape[-1]
        outs = adamw(tag, W[k].reshape(-1, last), M[k].reshape(-1, last), V[k].reshape(-1, last),
                     g.reshape(1, -1, last))
        res[k] = [a.reshape(W[k].shape) for a in outs]

    off0, off1 = 0, 0
    for k, ax in BIG:
        rows, padded = part_rows(k)
        _, r, c = row_form[k].shape
        parts = [gsum_first[off0:off0 + rows]] + [gsum_rest[off1 + i * padded:off1 + i * padded + rows]
                                                  for i in range(len(rest))]
        g = jnp.stack([p.reshape(r, c) for p in parts])
        update2d("adamw_" + k, k, jnp.swapaxes(g, 1, 2) if ax == 2 else g)
        off0 += padded
        off1 += len(rest) * padded

    def update(tag, keys, g8):
        outs = adamw(tag, _pack([W[k] for k in keys], F32), _pack([M[k] for k in keys], F32),
                     _pack([V[k] for k in keys], F32), g8)
        for i, arr in enumerate(outs):
            for k, a in zip(keys, _unpack(arr, [W[k].shape for k in keys])):
                res.setdefault(k, [None] * 4)[i] = a

    nrep = sum(int(np.prod(W[k].shape)) for k in SMALL)
    sflat = sall.reshape(N_DEV, -1)
    def pack8(a):
        a = a.reshape(N_DEV, -1)
        pad = (-a.shape[1]) % (8 * PACK_C)
        return jnp.concatenate([a, jnp.zeros((N_DEV, pad), F32)], axis=1).reshape(N_DEV, -1, PACK_C)

    update("adamw_small", SMALL, pack8(sflat[:, :nrep]))
    ncw = ffn_conv_w.shape[2]
    cw8 = sflat[:, nrep:nsmall].reshape(N_DEV, DEPTH, 3, N_DEV * ncw)
    cw8 = lax.dynamic_slice_in_dim(cw8, me * ncw, ncw, axis=3)
    update("adamw_conv", ["ffn_conv_w"], pack8(cw8))
    update2d("adamw_ada", "ada_w", d_ada_w)
    update("adamw_cctx", ["c_ctx"], ccall)

    out = [loss, grad_x]
    for i in range(4):
        out += [res[k][i] for k in names]
    return tuple(out)
```

```python
import functools
import math

import numpy as np
import jax
import jax.numpy as jnp
from jax import lax
from jax.experimental import pallas as pl
from jax.experimental.pallas import tpu as pltpu

F32 = jnp.float32
BF16 = jnp.bfloat16
HI = lax.Precision.HIGHEST
MESH = pl.DeviceIdType.MESH

N_DEV = 8
D = 1024
DEPTH = 4
CHUNK = 64
EPS = 1e-6
ALPHA = (2 * DEPTH) ** 0.25
GLA_TAU = 16.0
ROPE_BASE = 10000.0
MLA_SCALE = 96 ** -0.5
D_FF = 2816
FF_CHUNK = 1408
P_PAD = 2048
VMEM_LIMIT_BYTES = 56 << 20

ADAM_LR, ADAM_B1, ADAM_B2, ADAM_EPS, ADAM_WD, ADAM_STEP = 0.001, 0.9, 0.999, 1e-08, 0.01, 10

D_IN = 1984
W_IN_SEGS = [(0, 512), (544, 1408), (512, 32), (1952, 32)]
W_IN_INV_SEGS = [(0, 512), (1920, 32), (512, 1408), (1952, 32)]
FF_SEGS = [(0, FF_CHUNK), (D_FF, FF_CHUNK), (FF_CHUNK, FF_CHUNK), (D_FF + FF_CHUNK, FF_CHUNK)]


def _cols(a, segs):
    return jnp.concatenate([a[..., s:s + n] for s, n in segs], axis=-1)


def _rows(a, segs):
    return jnp.concatenate([a[s:s + n] for s, n in segs], axis=0)


def _pad_heads(wt, per_head):
    c = wt.shape[1]
    wt = wt.reshape(8, per_head, c)
    return jnp.concatenate([wt, jnp.zeros((8, 128 - per_head, c), wt.dtype)], axis=1).reshape(1024, c)


def _unpad_heads(g, per_head):
    c = g.shape[1]
    return g.reshape(8, 128, c)[:, :per_head].reshape(8 * per_head, c)


def _cparams(sem=None):
    return pltpu.CompilerParams(vmem_limit_bytes=VMEM_LIMIT_BYTES, dimension_semantics=sem)


@jax.custom_vjp
def bdot(a, w):
    return jnp.dot(a.astype(BF16), w.astype(BF16), preferred_element_type=F32)


def _bdot_fwd(a, w):
    return bdot(a, w), (a, w)


def _bdot_bwd(res, ct):
    a, w = res
    ctb = ct.astype(BF16)
    da = lax.dot_general(ctb, w.astype(BF16), (((1,), (1,)), ((), ())), preferred_element_type=F32)
    dw = lax.dot_general(a.astype(BF16), ctb, (((0,), (0,)), ((), ())), preferred_element_type=F32)
    return da.astype(a.dtype), dw.astype(w.dtype)


bdot.defvjp(_bdot_fwd, _bdot_bwd)


@jax.custom_vjp
def bdot_nt(a, wt):
    return lax.dot_general(a.astype(BF16), wt.astype(BF16), (((1,), (1,)), ((), ())), preferred_element_type=F32)


def _bdot_nt_fwd(a, wt):
    return bdot_nt(a, wt), (a, wt)


def _bdot_nt_bwd(res, ct):
    a, wt = res
    ctb = ct.astype(BF16)
    da = jnp.dot(ctb, wt.astype(BF16), preferred_element_type=F32)
    dwt = lax.dot_general(ctb, a.astype(BF16), (((0,), (0,)), ((), ())), preferred_element_type=F32)
    return da.astype(a.dtype), dwt.astype(wt.dtype)


bdot_nt.defvjp(_bdot_nt_fwd, _bdot_nt_bwd)


@jax.custom_vjp
def bdot_tn(a, b):
    return lax.dot_general(a.astype(BF16), b.astype(BF16), (((0,), (0,)), ((), ())), preferred_element_type=F32)


def _bdot_tn_fwd(a, b):
    return bdot_tn(a, b), (a, b)


def _bdot_tn_bwd(res, ct):
    a, b = res
    ctb = ct.astype(BF16)
    da = lax.dot_general(b.astype(BF16), ctb, (((1,), (1,)), ((), ())), preferred_element_type=F32)
    db = jnp.dot(a.astype(BF16), ctb, preferred_element_type=F32)
    return da.astype(a.dtype), db.astype(b.dtype)


bdot_tn.defvjp(_bdot_tn_fwd, _bdot_tn_bwd)


def _split3(x):
    x1 = x.astype(BF16)
    r1 = x - x1.astype(F32)
    x2 = r1.astype(BF16)
    return x1, x2, (r1 - x2.astype(F32)).astype(BF16)


@jax.custom_vjp
def xdot(x, m):
    mb = m.astype(BF16)
    return sum(jnp.dot(xi, mb, preferred_element_type=F32) for xi in _split3(x))


def _xdot_bwd(m, ct):
    mb = m.astype(BF16)
    dx = sum(lax.dot_general(ci, mb, (((1,), (1,)), ((), ())), preferred_element_type=F32) for ci in _split3(ct))
    return dx, jnp.zeros_like(m)


xdot.defvjp(lambda x, m: (xdot(x, m), m), _xdot_bwd)


@jax.custom_vjp
def xdot_l(m, x):
    mb = m.astype(BF16)
    return sum(jnp.dot(mb, xi, preferred_element_type=F32) for xi in _split3(x))


def _xdot_l_bwd(m, ct):
    mb = m.astype(BF16)
    dx = sum(lax.dot_general(mb, ci, (((0,), (0,)), ((), ())), preferred_element_type=F32) for ci in _split3(ct))
    return jnp.zeros_like(m), dx


xdot_l.defvjp(lambda m, x: (xdot_l(m, x), m), _xdot_l_bwd)


def _swap_fn(half):
    def swap(x):
        n = x.shape[1]
        first = (_iota(x.shape, 1) // half) % 2 == 0
        return jnp.where(first, pltpu.roll(x, n - half, 1), pltpu.roll(x, half, 1))

    f = jax.custom_vjp(swap)
    f.defvjp(lambda x: (swap(x), None), lambda _, ct: (swap(ct),))
    return f


_swap16 = _swap_fn(16)
_swap8 = _swap_fn(8)


def hdot(a, b):
    return jnp.dot(a, b, precision=HI, preferred_element_type=F32)


def _iota(shape, axis):
    return lax.broadcasted_iota(jnp.int32, shape, axis)


def _group_avg(n, g):
    return (_iota((n, n), 0) // g == _iota((n, n), 1) // g).astype(F32) * (1.0 / g)


def _silu(x):
    return x * jax.nn.sigmoid(x)


def _layer_norm(z, g, b):
    mu = jnp.mean(z, axis=-1, keepdims=True)
    zc = z - mu
    var = jnp.mean(zc * zc, axis=-1, keepdims=True)
    return zc * lax.rsqrt(var + EPS) * g + b


def _rms(x, g):
    return x * lax.rsqrt(jnp.mean(x * x, axis=-1, keepdims=True) + EPS) * g


def mm(name, a, wt, out_dtype, tn, tr, mod=None, sel=None):
    n, k = a.shape
    nw = wt.shape[0]

    def body(*refs):
        if mod is not None:
            a_ref, m_ref, w_ref, o_ref = refs
            m = m_ref[0]
            av = a_ref[...] * (1.0 + m[sel[0]:sel[0] + 1]) + m[sel[1]:sel[1] + 1]
        else:
            a_ref, w_ref, o_ref = refs
            av = a_ref[...]
        o_ref[...] = lax.dot_general(av.astype(BF16), w_ref[...], (((1,), (1,)), ((), ())),
                                     preferred_element_type=F32).astype(o_ref.dtype)

    in_specs = [pl.BlockSpec((tr, k), lambda c, j: (j, 0))]
    args = [a]
    if mod is not None:
        in_specs.append(pl.BlockSpec((1, 8, k), lambda c, j: (j, 0, 0)))
        args.append(mod)
    in_specs.append(pl.BlockSpec((tn, k), lambda c, j: (c, 0)))
    args.append(wt)
    return pl.pallas_call(
        body, name=name, grid=(nw // tn, n // tr), in_specs=in_specs,
        out_specs=pl.BlockSpec((tr, tn), lambda c, j: (j, c)),
        out_shape=jax.ShapeDtypeStruct((n, nw), out_dtype), compiler_params=_cparams(("parallel", "arbitrary")),
    )(*args)


def mm_tn(name, dc, a, tn, tr, mod=None, sel=None):
    n, k = a.shape
    nw = dc.shape[1]

    def body(*refs):
        if mod is not None:
            d_ref, a_ref, m_ref, o_ref = refs
            m = m_ref[0]
            av = a_ref[...] * (1.0 + m[sel[0]:sel[0] + 1]) + m[sel[1]:sel[1] + 1]
        else:
            d_ref, a_ref, o_ref = refs
            av = a_ref[...]

        @pl.when(pl.program_id(1) == 0)
        def _():
            o_ref[...] = jnp.zeros_like(o_ref)

        o_ref[...] += lax.dot_general(d_ref[...].astype(BF16), av.astype(BF16), (((0,), (0,)), ((), ())),
                                      preferred_element_type=F32)

    in_specs = [pl.BlockSpec((tr, tn), lambda c, j: (j, c)), pl.BlockSpec((tr, k), lambda c, j: (j, 0))]
    args = [dc, a]
    if mod is not None:
        in_specs.append(pl.BlockSpec((1, 8, k), lambda c, j: (j, 0, 0)))
        args.append(mod)
    return pl.pallas_call(
        body, name=name, grid=(nw // tn, n // tr), in_specs=in_specs,
        out_specs=pl.BlockSpec((tn, k), lambda c, j: (c, 0)),
        out_shape=jax.ShapeDtypeStruct((nw, k), F32), compiler_params=_cparams(("parallel", "arbitrary")),
    )(*args)


def mm_modbwd(name, dc, wt, x, mod, add, sel, tr):
    n, k = dc.shape
    dm = wt.shape[1]

    def body(dc_ref, wt_ref, x_ref, m_ref, add_ref, dx_ref, dm_ref):
        dh = jnp.dot(dc_ref[...].astype(BF16), wt_ref[...], preferred_element_type=F32)
        m = m_ref[0]
        dx_ref[...] = add_ref[...] + dh * (1.0 + m[sel[0]:sel[0] + 1])
        dsc = jnp.sum(dh * x_ref[...], axis=0, keepdims=True)
        dsh = jnp.sum(dh, axis=0, keepdims=True)
        rows = _iota((8, dm), 0)
        dm_ref[0] = jnp.where(rows == sel[0], dsc, 0.0) + jnp.where(rows == sel[1], dsh, 0.0)

    return pl.pallas_call(
        body, name=name, grid=(n // tr,),
        in_specs=[pl.BlockSpec((tr, k), lambda j: (j, 0)), pl.BlockSpec((k, dm), lambda j: (0, 0)),
                  pl.BlockSpec((tr, dm), lambda j: (j, 0)), pl.BlockSpec((1, 8, dm), lambda j: (j, 0, 0)),
                  pl.BlockSpec((tr, dm), lambda j: (j, 0))],
        out_specs=[pl.BlockSpec((tr, dm), lambda j: (j, 0)), pl.BlockSpec((1, 8, dm), lambda j: (j, 0, 0))],
        out_shape=[jax.ShapeDtypeStruct((n, dm), F32), jax.ShapeDtypeStruct((n // tr, 8, dm), F32)],
        compiler_params=_cparams(("arbitrary",)),
    )(dc, wt, x, mod, add)


def _spec(kind, arr, width, cb, tr, tps):
    if kind == "row":
        return pl.BlockSpec((tr, width), lambda j: (j, cb))
    if kind == "pos":
        return pl.BlockSpec((tr, width), lambda j: (j % tps, cb))
    if kind == "tile":
        return pl.BlockSpec((1,) + arr.shape[1:], lambda j: (j, 0, 0))
    if kind == "par":
        return pl.BlockSpec(arr.shape, lambda j: (0, 0))
    raise ValueError(kind)


def _load(kind, ref):
    v = ref[0] if kind == "tile" else ref[...]
    return v.astype(F32)


def tile_fwd(name, fn, ins, outs, n, tr, tps):
    kinds = [i[0] for i in ins]

    def body(*refs):
        vals = [_load(k, r) for k, r in zip(kinds, refs[:len(ins)])]
        res = fn(*vals)
        for r, o in zip(refs[len(ins):], res):
            r[...] = o.astype(r.dtype)

    return pl.pallas_call(
        body, name=name, grid=(n // tr,),
        in_specs=[_spec(k, a, w, cb, tr, tps) for k, a, w, cb in ins],
        out_specs=[pl.BlockSpec((tr, w), lambda j: (j, 0)) for w, _ in outs],
        out_shape=[jax.ShapeDtypeStruct((n, w), dt) for w, dt in outs],
        compiler_params=_cparams(("arbitrary",)),
    )(*[i[1] for i in ins])


def tile_bwd(name, fn, ins, cots, want, n, tr, tps):
    kinds = [i[0] for i in ins]
    widx = [w[0] for w in want]
    ni, nc = len(ins), len(cots)

    def body(*refs):
        vals = [_load(k, r) for k, r in zip(kinds, refs[:ni])]
        cvals = tuple(r[...].astype(F32) for r in refs[ni:ni + nc])

        def f(*dv):
            full = list(vals)
            for i, v in zip(widx, dv):
                full[i] = v
            return tuple(fn(*full))

        _, vjp = jax.vjp(f, *[vals[i] for i in widx])
        grads = vjp(cvals)
        first = pl.program_id(0) == 0
        for r, g, i in zip(refs[ni + nc:], grads, widx):
            if kinds[i] == "par":
                @pl.when(first)
                def _(r=r):
                    r[...] = jnp.zeros_like(r)
                r[...] += g
            elif kinds[i] == "tile":
                r[0] = g.astype(r.dtype)
            else:
                r[...] = g.astype(r.dtype)

    out_specs, out_shape = [], []
    for i, dt in want:
        k, a, w, cb = ins[i]
        if k == "par":
            out_specs.append(pl.BlockSpec(a.shape, lambda j: (0, 0)))
            out_shape.append(jax.ShapeDtypeStruct(a.shape, F32))
        elif k == "tile":
            out_specs.append(pl.BlockSpec((1,) + a.shape[1:], lambda j: (j, 0, 0)))
            out_shape.append(jax.ShapeDtypeStruct(a.shape, F32))
        else:
            out_specs.append(pl.BlockSpec((tr, w), lambda j: (j, 0)))
            out_shape.append(jax.ShapeDtypeStruct((n, w), dt))
    return pl.pallas_call(
        body, name=name, grid=(n // tr,),
        in_specs=[_spec(k, a, w, cb, tr, tps) for k, a, w, cb in ins]
        + [pl.BlockSpec((tr, c.shape[1]), lambda j: (j, 0)) for c in cots],
        out_specs=out_specs, out_shape=out_shape, compiler_params=_cparams(("arbitrary",)),
    )(*[i[1] for i in ins], *cots)


def pre_fn(p_gq, p_rq, p_rk, p_cq, p_ckv, p_misc, rcos, rsin, mcos, msin,
           w2f, w2b, b2f, b2b, retf, retb, qg, kvg, wuq, wuk, wuv, e2):
    tr = p_gq.shape[0]
    gq = p_gq * (32 ** -0.5)
    af = jax.nn.log_sigmoid(hdot(p_misc, w2f) + b2f) * (1.0 / GLA_TAU)
    ab = jax.nn.log_sigmoid(hdot(p_misc, w2b) + b2b) * (1.0 / GLA_TAU)
    arf = jnp.zeros((tr, 128), F32) + retf
    arb = jnp.zeros((tr, 128), F32) + retb
    rq = p_rq * rcos + _swap16(p_rq) * rsin
    rks = p_rk * (32 ** -0.5)
    rk = rks * rcos + _swap16(rks) * rsin
    qp = bdot_nt(_rms(p_cq, qg), wuq) * MLA_SCALE
    ckvn = _rms(p_ckv, kvg)
    kp = bdot_nt(ckvn, wuk) + xdot(p_misc, e2)
    mc, ms = jnp.tile(mcos, (1, 8)), jnp.tile(msin, (1, 8))
    v = bdot_nt(ckvn, wuv)
    return gq, af, ab, arf, arb, rq, rk, qp * mc + _swap8(qp) * ms, kp * mc + _swap8(kp) * ms, v


def post_fn(ogf, ogb, orf, orb, om, gg, rg, x, mod, gng, wout, lng, lnb):
    avg = _group_avg(256, 64)
    og = ogf + ogb
    mg = og * lax.rsqrt(xdot(og * og, avg) + EPS) * gng * _silu(gg)
    orr = orf + orb
    oc = orr - xdot(orr, avg)
    mr = oc * lax.rsqrt(xdot(oc * oc, avg) + EPS) * _silu(rg)
    m = jnp.concatenate([mg, mr, om], axis=1)
    y = bdot(m, wout)
    return (_layer_norm(ALPHA * x + mod[2:3] * y, lng, lnb),)


def ln2_fn(x1, f, mod, lng, lnb):
    return (_layer_norm(ALPHA * x1 + mod[5:6] * f, lng, lnb),)


def scan_step(q, k, v, a, st, rev):
    ii, jj = _iota((CHUNK, CHUNK), 0), _iota((CHUNK, CHUNK), 1)
    tri = ((jj >= ii) if rev else (jj <= ii)).astype(F32)
    b = xdot_l(tri, a)
    btot = jnp.sum(a, axis=0, keepdims=True)
    qe = q * jnp.exp(b - btot)
    ke = k * jnp.exp(btot - b)
    lane = _iota((1, 128), 1)
    q4 = jnp.concatenate([qe * (lane // 32 == h).astype(F32) for h in range(4)], axis=0)
    att = bdot_nt(q4, ke)
    att = jnp.where(jnp.concatenate([tri] * 4, axis=0) > 0, att, 0.0)
    r = bdot(att, v)
    col = _iota((1, 256), 1)
    o = bdot_nt(q * jnp.exp(b), st)
    for h in range(4):
        o = o + r[h * CHUNK:(h + 1) * CHUNK] * (col // 64 == h).astype(F32)
    vk = bdot_tn(v, ke)
    bd = (_iota((256, 128), 0) // 64 == _iota((256, 128), 1) // 32).astype(F32)
    return o, st * jnp.exp(btot) + vk * bd


def _chunk_maps(nch, nctx):
    def fwd(s):
        return s

    def bwd(s):
        return jnp.where(s < nctx, nctx - 1 - s, nch - 1 - (s - nctx))
    return fwd, bwd


def _per_sample(arr, nb):
    return arr.reshape(nb, arr.shape[0] // nb, arr.shape[1])


def scan_fwd(name, q, k, v, af, ab, nb, nch, nctx):
    n = af.shape[0]
    fmap, bmap = _chunk_maps(nch, nctx)

    def body(qf, kf, vf, a_f, qb, kb, vb, a_b, of_ref, ob_ref, stf_ref, stb_ref, s_scr):
        @pl.when(pl.program_id(0) == 0)
        def _():
            s_scr[...] = jnp.zeros_like(s_scr)

        for i in range(nb):
            stf_ref[0, i] = s_scr[2 * i]
            stb_ref[0, i] = s_scr[2 * i + 1]
            o, sn = scan_step(qf[i], kf[i], vf[i], a_f[i], s_scr[2 * i], False)
            of_ref[i] = o
            s_scr[2 * i] = sn
            o, sn = scan_step(qb[i], kb[i], vb[i], a_b[i], s_scr[2 * i + 1], True)
            ob_ref[i] = o
            s_scr[2 * i + 1] = sn

    def specs(m):
        return [pl.BlockSpec((nb, CHUNK, w), lambda s, cb=cb: (0, m(s), cb)) for _, w, cb in (q, k, v)] + \
               [pl.BlockSpec((nb, CHUNK, 128), lambda s: (0, m(s), 0))]

    ps = lambda a: _per_sample(a, nb)
    of, ob, stf, stb = pl.pallas_call(
        body, name=name, grid=(nch,), in_specs=specs(fmap) + specs(bmap),
        out_specs=[pl.BlockSpec((nb, CHUNK, 256), lambda s: (0, fmap(s), 0)),
                   pl.BlockSpec((nb, CHUNK, 256), lambda s: (0, bmap(s), 0)),
                   pl.BlockSpec((1, nb, 256, 128), lambda s: (s, 0, 0, 0)),
                   pl.BlockSpec((1, nb, 256, 128), lambda s: (s, 0, 0, 0))],
        out_shape=[jax.ShapeDtypeStruct((nb, n // nb, 256), F32)] * 2
        + [jax.ShapeDtypeStruct((nch, nb, 256, 128), F32)] * 2,
        scratch_shapes=[pltpu.VMEM((2 * nb, 256, 128), F32)], compiler_params=_cparams(("arbitrary",)),
    )(ps(q[0]), ps(k[0]), ps(v[0]), ps(af), ps(q[0]), ps(k[0]), ps(v[0]), ps(ab))
    return of.reshape(n, 256), ob.reshape(n, 256), stf, stb


def scan_bwd(name, q, k, v, af, ab, stf, stb, do, nb, nch, nctx):
    n = af.shape[0]
    fmap0, bmap0 = _chunk_maps(nch, nctx)
    fmap = lambda r: fmap0(nch - 1 - r)
    bmap = lambda r: bmap0(nch - 1 - r)

    def body(qf, kf, vf, a_f, sf, dof, qb, kb, vb, a_b, sb, dob,
             dqf, dkf, dvf, daf, dqb, dkb, dvb, dab, ds_scr):
        @pl.when(pl.program_id(0) == 0)
        def _():
            ds_scr[...] = jnp.zeros_like(ds_scr)

        for i in range(nb):
            for d, (qr, kr, vr, ar, sr, dor, outs) in enumerate(((qf, kf, vf, a_f, sf, dof, (dqf, dkf, dvf, daf)),
                                                                   (qb, kb, vb, a_b, sb, dob, (dqb, dkb, dvb, dab)))):
                _, vjp = jax.vjp(functools.partial(scan_step, rev=bool(d)), qr[i], kr[i], vr[i], ar[i], sr[0, i])
                dq, dk, dv, da, ds = vjp((dor[i], ds_scr[2 * i + d]))
                outs[0][i] = dq
                outs[1][i] = dk
                outs[2][i] = dv
                outs[3][i] = da
                ds_scr[2 * i + d] = ds

    def specs(m):
        return [pl.BlockSpec((nb, CHUNK, w), lambda r, cb=cb: (0, m(r), cb)) for _, w, cb in (q, k, v)] + \
               [pl.BlockSpec((nb, CHUNK, 128), lambda r: (0, m(r), 0)),
                pl.BlockSpec((1, nb, 256, 128), lambda r: (nch - 1 - r, 0, 0, 0)),
                pl.BlockSpec((nb, CHUNK, 256), lambda r: (0, m(r), 0))]

    def ospecs(m):
        return [pl.BlockSpec((nb, CHUNK, w), lambda r: (0, m(r), 0)) for w in (128, 128, 256, 128)]

    ps = lambda a: _per_sample(a, nb)
    oshape = [jax.ShapeDtypeStruct((nb, n // nb, w), F32) for w in (128, 128, 256, 128)]
    outs = pl.pallas_call(
        body, name=name, grid=(nch,), in_specs=specs(fmap) + specs(bmap),
        out_specs=ospecs(fmap) + ospecs(bmap), out_shape=oshape + oshape,
        scratch_shapes=[pltpu.VMEM((2 * nb, 256, 128), F32)], compiler_params=_cparams(("arbitrary",)),
    )(ps(q[0]), ps(k[0]), ps(v[0]), ps(af), stf, ps(do), ps(q[0]), ps(k[0]), ps(v[0]), ps(ab), stb, ps(do))
    return [o.reshape(n, o.shape[2]) for o in outs]


def mla_fwd(name, qa, ka, va, nb, tps, tr, nctx_rows):
    n = qa.shape[0]
    t = tps * tr

    def body(q_ref, k_ref, v_ref, o_ref, lse_ref):
        def attend(nk):
            vv = v_ref[0:nk, :]
            first = _iota(vv.shape, 1) < 64
            one = jnp.ones_like(vv)
            res, lses = [], []
            for h in range(2):
                s = lax.dot_general(q_ref[:, h * 128:(h + 1) * 128], k_ref[0:nk, h * 128:(h + 1) * 128],
                                    (((1,), (1,)), ((), ())), preferred_element_type=F32)
                m = jnp.max(s, axis=-1, keepdims=True)
                e = jnp.exp((s - m).astype(BF16))
                r = jnp.dot(e, jnp.where(first == (h == 0), vv, one), preferred_element_type=F32)
                l = r[:, 64:65] if h == 0 else r[:, 0:1]
                res.append(r / l)
                lses.append(m + jnp.log(l))
            lane = _iota((tr, 128), 1) < 64
            o_ref[...] = jnp.where(lane, res[0], res[1])
            lse_ref[...] = jnp.where(lane, lses[0], lses[1])

        @pl.when(pl.program_id(2) == 0)
        def _():
            attend(nctx_rows)

        @pl.when(pl.program_id(2) > 0)
        def _():
            attend(t)

    return pl.pallas_call(
        body, name=name, grid=(nb, 4, tps),
        in_specs=[pl.BlockSpec((tr, 256), lambda b, h, j: (b * tps + j, h)), pl.BlockSpec((t, 256), lambda b, h, j: (b, h)),
                  pl.BlockSpec((t, 128), lambda b, h, j: (b, h))],
        out_specs=[pl.BlockSpec((tr, 128), lambda b, h, j: (b * tps + j, h))] * 2,
        out_shape=[jax.ShapeDtypeStruct((n, 512), F32)] * 2,
        compiler_params=_cparams(("parallel", "parallel", "arbitrary")),
    )(qa, ka, va)


def mla_bwd(name, qa, ka, va, o, lse, do, nb, tps, tr, nctx_rows):
    n = qa.shape[0]
    t = tps * tr

    def body(q_ref, k_ref, v_ref, o_ref, lse_ref, do_ref, dq_ref, dk_ref, dv_ref, dkt, dvt):
        @pl.when(pl.program_id(2) == 0)
        def _():
            dkt[...] = jnp.zeros_like(dkt)
            dvt[...] = jnp.zeros_like(dvt)

        def attend(nk):
            dov = do_ref[...]
            oo = dov * o_ref[...]
            dob = dov.astype(BF16)
            first = _iota(dob.shape, 1) < 64
            dqs = []
            for h in range(2):
                hs = slice(h * 128, (h + 1) * 128)
                qh, kh = q_ref[:, hs], k_ref[0:nk, hs]
                mine = first == (h == 0)
                delta = jnp.sum(jnp.where(mine, oo, 0.0), axis=-1, keepdims=True)
                doh = jnp.where(mine, dob, jnp.zeros_like(dob))
                s = lax.dot_general(qh, kh, (((1,), (1,)), ((), ())), preferred_element_type=F32)
                p = jnp.exp((s - lse_ref[:, h * 64:h * 64 + 1]).astype(BF16))
                dp = lax.dot_general(doh, v_ref[0:nk, :], (((1,), (1,)), ((), ())), preferred_element_type=F32)
                ds = p * (dp - delta).astype(BF16)
                dqs.append(jnp.dot(ds, kh, preferred_element_type=F32))
                dkt[hs, 0:nk] += lax.dot_general(qh, ds, (((0,), (0,)), ((), ())), preferred_element_type=F32)
                dvt[:, 0:nk] += lax.dot_general(doh, p, (((0,), (0,)), ((), ())), preferred_element_type=F32)
            dq_ref[...] = jnp.concatenate(dqs, axis=1)

        @pl.when(pl.program_id(2) == 0)
        def _():
            attend(nctx_rows)

        @pl.when(pl.program_id(2) > 0)
        def _():
            attend(t)

        @pl.when(pl.program_id(2) == tps - 1)
        def _():
            dk_ref[...] = dkt[...].T
            dv_ref[...] = dvt[...].T

    qtile = pl.BlockSpec((tr, 128), lambda b, h, j: (b * tps + j, h))
    return pl.pallas_call(
        body, name=name, grid=(nb, 4, tps),
        in_specs=[pl.BlockSpec((tr, 256), lambda b, h, j: (b * tps + j, h)), pl.BlockSpec((t, 256), lambda b, h, j: (b, h)),
                  pl.BlockSpec((t, 128), lambda b, h, j: (b, h)), qtile, qtile, qtile],
        out_specs=[pl.BlockSpec((tr, 256), lambda b, h, j: (b * tps + j, h)), pl.BlockSpec((t, 256), lambda b, h, j: (b, h)),
                   pl.BlockSpec((t, 128), lambda b, h, j: (b, h))],
        out_shape=[jax.ShapeDtypeStruct((n, 1024), F32), jax.ShapeDtypeStruct((n, 1024), F32),
                   jax.ShapeDtypeStruct((n, 512), F32)],
        scratch_shapes=[pltpu.VMEM((256, t), F32), pltpu.VMEM((128, t), F32)],
        compiler_params=_cparams(("parallel", "parallel", "arbitrary")),
    )(qa, ka, va, o, lse, do)


HALO = 16


def _halo_specs(tr, width, tps, nt):
    r = tr // HALO
    return [pl.BlockSpec((tr, width), lambda j, c: (j, c)),
            pl.BlockSpec((HALO, width), lambda j, c: (jnp.maximum(j * r - 1, 0), c)),
            pl.BlockSpec((HALO, width), lambda j, c: (jnp.minimum((j + 1) * r, nt * r - 1), c))]


def _shifted(u, prev, nxt, j, tps):
    tr = u.shape[0]
    t = j % tps
    has_prev = (t >= 2).astype(F32)
    has_next = jnp.logical_and(t >= 1, t <= tps - 2).astype(F32)
    rows = _iota(u.shape, 0)
    dn = jnp.where(rows == 0, prev[HALO - 1:HALO] * has_prev, pltpu.roll(u, 1, 0))
    up = jnp.where(rows == tr - 1, nxt[0:1] * has_next, pltpu.roll(u, tr - 1, 0))
    return dn, up


def _ffn_act(ucv):
    return _silu(ucv[:, :FF_CHUNK]) * ucv[:, FF_CHUNK:]


def ffn2_fwd(name, u, cw, cb, wd, x1, mod, lng, lnb, tr, tps):
    n = u.shape[0]
    nt = n // tr
    w2 = 2 * FF_CHUNK

    def body(u_ref, up_ref, un_ref, cw_ref, cb_ref, wd_ref, x1_ref, m_ref, g_ref, b_ref, f_ref, x2_ref, ucv_ref, acc):
        j, c = pl.program_id(0), pl.program_id(1)
        uu = u_ref[...].astype(F32)
        dn, up = _shifted(uu, up_ref[...].astype(F32), un_ref[...].astype(F32), j, tps)
        cwv = cw_ref[...]
        ucv = cwv[0:1] * dn + cwv[1:2] * uu + cwv[2:3] * up + cb_ref[...]
        ucv_ref[...] = ucv.astype(ucv_ref.dtype)
        part = bdot(_ffn_act(ucv), wd_ref[...])

        @pl.when(c == 0)
        def _():
            acc[...] = part

        @pl.when(c == 1)
        def _():
            f = acc[...] + part
            f_ref[...] = f
            x2_ref[...] = ln2_fn(x1_ref[...], f, m_ref[0], g_ref[...], b_ref[...])[0]

    return pl.pallas_call(
        body, name=name, grid=(nt, 2),
        in_specs=_halo_specs(tr, w2, tps, nt) + [
            pl.BlockSpec((8, w2), lambda j, c: (0, c)), pl.BlockSpec((1, w2), lambda j, c: (0, c)),
            pl.BlockSpec((FF_CHUNK, D), lambda j, c: (c, 0)), pl.BlockSpec((tr, D), lambda j, c: (j, 0)),
            pl.BlockSpec((1, 8, D), lambda j, c: (j, 0, 0)), pl.BlockSpec((1, D), lambda j, c: (0, 0)),
            pl.BlockSpec((1, D), lambda j, c: (0, 0))],
        out_specs=[pl.BlockSpec((tr, D), lambda j, c: (j, 0)), pl.BlockSpec((tr, D), lambda j, c: (j, 0)),
                   pl.BlockSpec((tr, w2), lambda j, c: (j, c))],
        out_shape=[jax.ShapeDtypeStruct((n, D), F32)] * 2 + [jax.ShapeDtypeStruct((n, 2 * w2), BF16)],
        scratch_shapes=[pltpu.VMEM((tr, D), F32)], compiler_params=_cparams(("arbitrary", "arbitrary")),
    )(u, u, u, cw, cb, wd, x1, mod, lng, lnb)


def ffn2_bwd(name, ucv, wd, df, tr):
    n = ucv.shape[0]
    nt = n // tr
    w2 = 2 * FF_CHUNK

    def body(ucv_ref, wd_ref, df_ref, ducv_ref, dwd_ref):
        j = pl.program_id(1)
        a, g = ucv_ref[:, :FF_CHUNK].astype(F32), ucv_ref[:, FF_CHUNK:].astype(F32)
        sg = jax.nn.sigmoid(a)
        sa = a * sg
        dfb = df_ref[...].astype(BF16)
        dact = lax.dot_general(dfb, wd_ref[...], (((1,), (1,)), ((), ())), preferred_element_type=F32)
        ducv_ref[:, :FF_CHUNK] = (dact * g * (sg + sa * (1.0 - sg))).astype(ducv_ref.dtype)
        ducv_ref[:, FF_CHUNK:] = (dact * sa).astype(ducv_ref.dtype)
        dwd = lax.dot_general((sa * g).astype(BF16), dfb, (((0,), (0,)), ((), ())), preferred_element_type=F32)

        @pl.when(j == 0)
        def _():
            dwd_ref[...] = jnp.zeros_like(dwd_ref)

        dwd_ref[...] += dwd

    return pl.pallas_call(
        body, name=name, grid=(2, nt),
        in_specs=[pl.BlockSpec((tr, w2), lambda c, j: (j, c)), pl.BlockSpec((FF_CHUNK, D), lambda c, j: (c, 0)),
                  pl.BlockSpec((tr, D), lambda c, j: (j, 0))],
        out_specs=[pl.BlockSpec((tr, w2), lambda c, j: (j, c)), pl.BlockSpec((FF_CHUNK, D), lambda c, j: (c, 0))],
        out_shape=[jax.ShapeDtypeStruct((n, 2 * w2), BF16), jax.ShapeDtypeStruct((D_FF, D), F32)],
        compiler_params=_cparams(("parallel", "arbitrary")),
    )(ucv, wd, df)


def conv_bwd(name, ducv, u, cw, tr, tps):
    n = u.shape[0]
    nt = n // tr
    w2 = 2 * FF_CHUNK

    def body(g_ref, gp_ref, gn_ref, u_ref, up_ref, un_ref, cw_ref, du_ref, dcw_ref, dcb_ref):
        c, j = pl.program_id(0), pl.program_id(1)
        g = g_ref[...].astype(F32)
        gdn, gup = _shifted(g, gp_ref[...].astype(F32), gn_ref[...].astype(F32), j, tps)
        uu = u_ref[...].astype(F32)
        udn, uup = _shifted(uu, up_ref[...].astype(F32), un_ref[...].astype(F32), j, tps)
        cwv = cw_ref[...]
        du_ref[...] = (cwv[0:1] * gup + cwv[1:2] * g + cwv[2:3] * gdn).astype(du_ref.dtype)
        rows = _iota((8, w2), 0)
        s = lambda z: jnp.sum(z, axis=0, keepdims=True)
        dcw = (jnp.where(rows == 0, s(g * udn), 0.0) + jnp.where(rows == 1, s(g * uu), 0.0)
               + jnp.where(rows == 2, s(g * uup), 0.0))

        @pl.when(j == 0)
        def _():
            dcw_ref[...] = jnp.zeros_like(dcw_ref)
            dcb_ref[...] = jnp.zeros_like(dcb_ref)

        dcw_ref[...] += dcw
        dcb_ref[...] += s(g)

    hs = _halo_specs(tr, w2, tps, nt)
    swap = lambda spec: pl.BlockSpec(spec.block_shape, lambda c, j, f=spec.index_map: f(j, c))
    return pl.pallas_call(
        body, name=name, grid=(2, nt),
        in_specs=[swap(s) for s in hs] * 2 + [pl.BlockSpec((8, w2), lambda c, j: (0, c))],
        out_specs=[pl.BlockSpec((tr, w2), lambda c, j: (j, c)), pl.BlockSpec((8, w2), lambda c, j: (0, c)),
                   pl.BlockSpec((1, w2), lambda c, j: (0, c))],
        out_shape=[jax.ShapeDtypeStruct((n, 2 * w2), BF16), jax.ShapeDtypeStruct((8, 2 * w2), F32),
                   jax.ShapeDtypeStruct((1, 2 * w2), F32)],
        compiler_params=_cparams(("parallel", "arbitrary")),
    )(ducv, ducv, ducv, u, u, u, cw)


def loss_head(name, xf, target, nb, tps, tr):
    n = xf.shape[0]

    def body(x_ref, t_ref, dy_ref, l_ref):
        lat = (pl.program_id(0) % tps > 0).astype(F32)
        err = (x_ref[...] - t_ref[...]) * lat
        dy_ref[...] = err * (1.0 / D)
        l_ref[...] = jnp.zeros_like(l_ref) + 0.5 * jnp.sum(err * err) * (1.0 / D)

    def tmap(j):
        return ((j // tps) * (tps - 1) + jnp.maximum(j % tps - 1, 0), 0)

    return pl.pallas_call(
        body, name=name, grid=(n // tr,),
        in_specs=[pl.BlockSpec((tr, D), lambda j: (j, 0)), pl.BlockSpec((tr, D), tmap)],
        out_specs=[pl.BlockSpec((tr, D), lambda j: (j, 0)), pl.BlockSpec((1, 8, 128), lambda j: (j, 0, 0))],
        out_shape=[jax.ShapeDtypeStruct((n, D), F32), jax.ShapeDtypeStruct((n // tr, 8, 128), F32)],
        compiler_params=_cparams(("arbitrary",)),
    )(xf, target)


ADAM_MAX_ROWS = 512


def adamw(name, w, m, v, g8):
    r, c = w.shape
    k = g8.shape[0]
    rows = max(b for b in range(8, ADAM_MAX_ROWS + 1, 8) if r % b == 0)
    bc1 = 1.0 - ADAM_B1 ** ADAM_STEP
    bc2 = 1.0 - ADAM_B2 ** ADAM_STEP

    def body(w_ref, m_ref, v_ref, g_ref, go_ref, d_ref, mo_ref, vo_ref):
        g = g_ref[0].astype(F32)
        for i in range(1, k):
            g = g + g_ref[i].astype(F32)
        mn = ADAM_B1 * m_ref[...] + (1.0 - ADAM_B1) * g
        vn = ADAM_B2 * v_ref[...] + (1.0 - ADAM_B2) * (g * g)
        go_ref[...] = g
        mo_ref[...] = mn
        vo_ref[...] = vn
        d_ref[...] = -ADAM_LR * ((mn / bc1) / (jnp.sqrt(vn / bc2) + ADAM_EPS) + ADAM_WD * w_ref[...])

    blk = pl.BlockSpec((rows, c), lambda i: (i, 0))
    return pl.pallas_call(
        body, name=name, grid=(r // rows,),
        in_specs=[blk, blk, blk, pl.BlockSpec((k, rows, c), lambda i: (0, i, 0))],
        out_specs=[blk] * 4, out_shape=[jax.ShapeDtypeStruct((r, c), F32)] * 4,
        compiler_params=_cparams(("parallel",)),
    )(w, m, v, g8)


def ada_fwd(name, s, aw, ab):
    nl, _, cw = aw.shape

    def body(s_ref, w_ref, b_ref, o_ref):
        o_ref[0] = hdot(s_ref[...], w_ref[0]) + b_ref[0]

    return pl.pallas_call(
        body, name=name, grid=(nl,),
        in_specs=[pl.BlockSpec(s.shape, lambda l: (0, 0)), pl.BlockSpec((1, D, cw), lambda l: (l, 0, 0)),
                  pl.BlockSpec((1, 1, cw), lambda l: (l, 0, 0))],
        out_specs=pl.BlockSpec((1, s.shape[0], cw), lambda l: (l, 0, 0)),
        out_shape=jax.ShapeDtypeStruct((nl, s.shape[0], cw), F32), compiler_params=_cparams(("arbitrary",)),
    )(s, aw, ab)


def ada_bwd(name, s, aw, dmod):
    nl, _, cw = aw.shape

    def body(s_ref, w_ref, d_ref, dw_ref, ds_ref):
        dw_ref[0] = lax.dot_general(s_ref[...], d_ref[0], (((0,), (0,)), ((), ())), precision=HI,
                                    preferred_element_type=F32)
        ds_ref[0] = lax.dot_general(d_ref[0], w_ref[0], (((1,), (1,)), ((), ())), precision=HI,
                                    preferred_element_type=F32)

    return pl.pallas_call(
        body, name=name, grid=(nl,),
        in_specs=[pl.BlockSpec(s.shape, lambda l: (0, 0)), pl.BlockSpec((1, D, cw), lambda l: (l, 0, 0)),
                  pl.BlockSpec((1, s.shape[0], cw), lambda l: (l, 0, 0))],
        out_specs=[pl.BlockSpec((1, D, cw), lambda l: (l, 0, 0)), pl.BlockSpec((1, s.shape[0], D), lambda l: (l, 0, 0))],
        out_shape=[jax.ShapeDtypeStruct((nl, D, cw), F32), jax.ShapeDtypeStruct((nl, s.shape[0], D), F32)],
        compiler_params=_cparams(("arbitrary",)),
    )(s, aw, dmod)


def _place():
    return lax.axis_index("x"), lax.axis_index("y"), lax.axis_index("c")


def all_gather(name, x, in_vmem):
    r, c = x.shape

    def body(x_ref, out_ref, send_sems, recv_sems, local_sem):
        px, py, pc = _place()
        me, sibling = (px, py, pc), (px, py, 1 - pc)
        chips = [(1 - px, py), (px, 1 - py), (1 - px, 1 - py)]

        def rows(qx, qy, qc):
            return out_ref.at[pl.ds((4 * qx + 2 * qy + qc) * r, r), :]

        def copy(k, block, to, src=None):
            return pltpu.make_async_remote_copy(
                src_ref=rows(*block) if src is None else src, dst_ref=rows(*block),
                send_sem=send_sems.at[k], recv_sem=recv_sems.at[k], device_id=to, device_id_type=MESH)

        mine = pltpu.make_async_copy(x_ref, rows(*me), local_sem)
        mine.start()
        first = [copy(0, me, sibling, src=x_ref)]
        first += [copy(1 + j, me, (*chip, pc), src=x_ref) for j, chip in enumerate(chips)]
        for cp in first:
            cp.start()
        passed = [copy(4 + j, (*chip, pc), sibling) for j, chip in enumerate(chips)]
        for j, chip in enumerate(chips):
            copy(1 + j, (*chip, pc), me).wait_recv()
            passed[j].start()
        copy(0, sibling, me).wait_recv()
        for j, chip in enumerate(chips):
            copy(4 + j, (*chip, 1 - pc), me).wait_recv()
        for cp in first + passed:
            cp.wait_send()
        mine.wait()

    space = pltpu.VMEM if in_vmem else pl.ANY
    return pl.pallas_call(
        body, name=name, out_shape=jax.ShapeDtypeStruct((N_DEV * r, c), x.dtype),
        in_specs=[pl.BlockSpec(memory_space=space)], out_specs=pl.BlockSpec(memory_space=space),
        scratch_shapes=[pltpu.SemaphoreType.DMA((7,)), pltpu.SemaphoreType.DMA((7,)), pltpu.SemaphoreType.DMA],
        compiler_params=pltpu.CompilerParams(vmem_limit_bytes=VMEM_LIMIT_BYTES),
    )(x)


_HBM = pl.BlockSpec(memory_space=pltpu.HBM)
_SEM = pl.BlockSpec(memory_space=pltpu.SEMAPHORE)
_EFFECT = pltpu.SideEffectType.DATAFLOW_SIDE_EFFECTING


def _partner(k):
    px, py, pc = _place()
    q = (px ^ (k >> 2 & 1), py ^ (k >> 1 & 1), pc ^ (k & 1))
    return q, 4 * q[0] + 2 * q[1] + q[2]


def xchg_start(name, x, per_peer):
    r, c = x.shape[-2:]

    def body(x_ref, land_ref, send_sems, recv_sems, x_thru, land_thru, token):
        px, py, pc = _place()
        my = 4 * px + 2 * py + pc
        for k in range(1, N_DEV):
            q, qi = _partner(k)
            pltpu.make_async_remote_copy(
                src_ref=x_ref.at[qi] if per_peer else x_ref, dst_ref=land_ref.at[my],
                send_sem=send_sems.at[k - 1], recv_sem=recv_sems.at[k - 1], device_id=q, device_id_type=MESH).start()
        token[...] = jnp.zeros_like(token)

    land = lax.empty((N_DEV, r, c), x.dtype)
    return pl.pallas_call(
        body, name=name,
        out_shape=(pltpu.SemaphoreType.DMA((N_DEV - 1,)), pltpu.SemaphoreType.DMA((N_DEV - 1,)),
                   pltpu.HBM(x.shape, x.dtype), pltpu.HBM(land.shape, land.dtype), jax.ShapeDtypeStruct((8, 128), F32)),
        in_specs=(_HBM, _HBM), out_specs=(_SEM, _SEM, _HBM, _HBM, pl.BlockSpec(memory_space=pltpu.VMEM)),
        input_output_aliases={0: 2, 1: 3}, compiler_params=pltpu.CompilerParams(has_side_effects=_EFFECT),
    )(pltpu.with_memory_space_constraint(x, pltpu.HBM), pltpu.with_memory_space_constraint(land, pltpu.HBM))


def xchg_wait(name, send_sems, recv_sems, x_thru, land_thru, after, per_peer):
    def body(x_ref, land_ref, send_sems, recv_sems, after_ref, x_out, land_out):
        for k in range(1, N_DEV):
            q, qi = _partner(k)
            cp = pltpu.make_async_remote_copy(
                src_ref=x_ref.at[qi] if per_peer else x_ref, dst_ref=land_ref.at[qi],
                send_sem=send_sems.at[k - 1], recv_sem=recv_sems.at[k - 1], device_id=q, device_id_type=MESH)
            cp.wait_send()
            cp.wait_recv()

    return pl.pallas_call(
        body, name=name,
        out_shape=(pltpu.HBM(x_thru.shape, x_thru.dtype), pltpu.HBM(land_thru.shape, land_thru.dtype)),
        in_specs=(_HBM, _HBM, _SEM, _SEM, pl.BlockSpec(memory_space=pl.ANY)), out_specs=(_HBM, _HBM),
        input_output_aliases={0: 0, 1: 1}, compiler_params=pltpu.CompilerParams(has_side_effects=_EFFECT),
    )(x_thru, land_thru, send_sems, recv_sems, after)


def _tables(seq, nctx_rows):
    f32 = np.float32
    pos = np.arange(seq, dtype=f32)
    ret_inv = (1.0 / (ROPE_BASE ** np.linspace(0.0, 1.0, 16, dtype=f32))).astype(f32)
    ang = pos[:, None] * ret_inv
    rc, rs = np.cos(ang).astype(f32), np.sin(ang).astype(f32)
    rcos = np.tile(np.concatenate([rc, rc], 1), (1, 4))
    rsin = np.tile(np.concatenate([-rs, rs], 1), (1, 4))
    rows = np.repeat(np.arange(seq // 64, dtype=f32), 64)
    cols = np.tile(np.arange(64, dtype=f32), seq // 64)
    ax_inv = (ROPE_BASE ** (-np.arange(8, dtype=f32) / 8)).astype(f32)
    ra, ca = rows[:, None] * ax_inv, cols[:, None] * ax_inv
    one, zero = np.ones((seq, 64), f32), np.zeros((seq, 64), f32)
    mcos = np.concatenate([one, np.cos(ra), np.cos(ra), np.cos(ca), np.cos(ca), one[:, :32]], 1)
    msin = np.concatenate([zero, -np.sin(ra), np.sin(ra), -np.sin(ca), np.sin(ca), zero[:, :32]], 1)
    ident = lambda t, v: np.concatenate([np.full((nctx_rows, 128), v, f32), t.astype(f32)], 0)
    return [jnp.asarray(ident(rcos, 1.0)), jnp.asarray(ident(rsin, 0.0)),
            jnp.asarray(ident(mcos, 1.0)), jnp.asarray(ident(msin, 0.0))]


def _prep_layer(w, l):
    z = lambda *s: jnp.zeros(s, F32)
    p = {}
    win = _rows(w["w_in_t"][l], W_IN_SEGS)
    p["w_in_t"] = jnp.concatenate([win, jnp.zeros((P_PAD - D_IN, D), win.dtype)], axis=0)
    p["w_up_t"] = w["ffn_up_t"][l]
    p["w_down"] = w["ffn_down"][l]
    p["w_out"] = w["w_out"][l]
    p["wuq"] = _pad_heads(w["mla_w_uq_t"][l], 96)
    p["wuk"] = _pad_heads(w["mla_w_uk_t"][l], 64)
    p["wuv"] = w["mla_w_uv_t"][l]
    gw = w["gla_gate_w"][l]
    p["w2f"] = z(128, 128).at[0:16].set(gw[0])
    p["w2b"] = z(128, 128).at[16:32].set(gw[1])
    p["b2f"], p["b2b"] = w["gla_gate_b"][l][0:1], w["gla_gate_b"][l][1:2]
    lg = jax.nn.log_sigmoid(w["ret_decay"][l])
    p["retf"], p["retb"] = jnp.repeat(lg[0], 32)[None], jnp.repeat(lg[1], 32)[None]
    p["qg"], p["kvg"] = w["mla_q_norm_g"][l][None], w["mla_kv_norm_g"][l][None]
    p["gng"] = jnp.tile(w["gla_norm_g"][l], 4)[None]
    p["ln1g"], p["ln1b"] = w["ln1_g"][l][None], w["ln1_b"][l][None]
    p["ln2g"], p["ln2b"] = w["ln2_g"][l][None], w["ln2_b"][l][None]
    p["cw"] = jnp.concatenate([_cols(w["ffn_conv_w"][l], FF_SEGS), z(5, 2 * D_FF)], axis=0)
    p["cb"] = _cols(w["ffn_conv_b"][l], FF_SEGS)[None]
    e2 = np.zeros((128, 1024), np.float32)
    for h in range(8):
        e2[32 + np.arange(32), h * 128 + 64 + np.arange(32)] = 1.0
    p["e2"] = jnp.asarray(e2)
    return p


def _pre_ins(pa, tabs, p):
    row = lambda w, cb: ("row", pa, w, cb)
    return [row(128, 0), row(128, 6), row(128, 7), row(256, 6), row(128, 14), row(128, 15)] + \
           [("pos", t, 128, 0) for t in tabs] + \
           [("par", p[k], 0, 0) for k in ("w2f", "w2b", "b2f", "b2b", "retf", "retb", "qg", "kvg", "wuq", "wuk", "wuv", "e2")]


_PRE_OUTS = [(128, F32)] * 7 + [(1024, BF16), (1024, BF16), (512, BF16)]
_PRE_WANT = [(i, F32) for i in range(6)] + [(i, F32) for i in range(10, 21)]


def _post_ins(ogf, ogb, orf, orb, om, pa, x, mod, p):
    return [("row", ogf, 256, 0), ("row", ogb, 256, 0), ("row", orf, 256, 0), ("row", orb, 256, 0),
            ("row", om, 512, 0), ("row", pa, 256, 2), ("row", pa, 256, 5), ("row", x, D, 0), ("tile", mod, 0, 0),
            ("par", p["gng"], 0, 0), ("par", p["w_out"], 0, 0), ("par", p["ln1g"], 0, 0), ("par", p["ln1b"], 0, 0)]


def layer_fwd(l, x, mod, p, tabs, dims):
    nb, tps, tr, nch, nctx = dims
    n = x.shape[0]
    pa = mm("proj", x, p["w_in_t"], F32, P_PAD, tr, mod=mod, sel=(1, 0))
    gq, af, ab, arf, arb, rq, rk, qa, ka, va = tile_fwd("mix_pre", pre_fn, _pre_ins(pa, tabs, p), _PRE_OUTS, n, tr, tps)
    ogf, ogb, gstf, gstb = scan_fwd("gla_scan", (gq, 128, 0), (pa, 128, 1), (pa, 256, 1), af, ab, nb, nch, nctx)
    orf, orb, rstf, rstb = scan_fwd("ret_scan", (rq, 128, 0), (rk, 128, 0), (pa, 256, 4), arf, arb, nb, nch, nctx)
    om, lse = mla_fwd("mla_attn", qa, ka, va, nb, tps, tr, nctx * CHUNK)
    (x1,) = tile_fwd("mix_post", post_fn, _post_ins(ogf, ogb, orf, orb, om, pa, x, mod, p), [(D, F32)], n, tr, tps)
    u = mm("ffn_up", x1, p["w_up_t"], BF16, 2 * D_FF, tr, mod=mod, sel=(4, 3))
    f, x2, ucv = ffn2_fwd("ffn_down", u, p["cw"], p["cb"], p["w_down"], x1, mod, p["ln2g"], p["ln2b"], tr, tps)
    saved = dict(x=x, pa=pa, gq=gq, af=af, ab=ab, arf=arf, arb=arb, rq=rq, rk=rk, qa=qa, ka=ka, va=va,
                 ogf=ogf, ogb=ogb, gstf=gstf, gstb=gstb, orf=orf, orb=orb, rstf=rstf, rstb=rstb, om=om, lse=lse,
                 x1=x1, u=u, ucv=ucv, f=f)
    return x2, saved


def layer_bwd(l, dx2, s, mod, p, tabs, dims):
    nb, tps, tr, nch, nctx = dims
    n = dx2.shape[0]
    g = {}
    ln2_ins = [("row", s["x1"], D, 0), ("row", s["f"], D, 0), ("tile", mod, 0, 0),
               ("par", p["ln2g"], 0, 0), ("par", p["ln2b"], 0, 0)]
    dx1a, df, dmod_a, g["ln2g"], g["ln2b"] = tile_bwd(
        "ln2_bwd", ln2_fn, ln2_ins, [dx2], [(0, F32), (1, F32), (2, F32), (3, F32), (4, F32)], n, tr, tps)
    ducv, g["w_down"] = ffn2_bwd("ffn_down_bwd", s["ucv"], p["w_down"], df, tr)
    du, g["cw"], g["cb"] = conv_bwd("conv_bwd", ducv, s["u"], p["cw"], tr, tps)
    g["w_up_t"] = mm_tn("ffn_up_dw", du, s["x1"], D_FF, tr, mod=mod, sel=(4, 3))
    dx1, dmod_b = mm_modbwd("ffn_up_dx", du, p["w_up_t"], s["x1"], mod, dx1a, (4, 3), tr)

    post_ins = _post_ins(s["ogf"], s["ogb"], s["orf"], s["orb"], s["om"], s["pa"], s["x"], mod, p)
    want = [(0, F32), (2, F32), (4, F32), (5, F32), (6, F32), (7, F32), (8, F32), (9, F32), (10, F32), (11, F32), (12, F32)]
    dog, dor, dom, dgg, drg, dxa, dmod_c, g["gng"], g["w_out"], g["ln1g"], g["ln1b"] = tile_bwd(
        "mix_post_bwd", post_fn, post_ins, [dx1], want, n, tr, tps)
    dqa, dka, dva = mla_bwd("mla_attn_bwd", s["qa"], s["ka"], s["va"], s["om"], s["lse"], dom, nb, tps, tr,
                            nctx * CHUNK)
    pa = s["pa"]
    gdqf, gdkf, gdvf, gdaf, gdqb, gdkb, gdvb, gdab = scan_bwd(
        "gla_scan_bwd", (s["gq"], 128, 0), (pa, 128, 1), (pa, 256, 1), s["af"], s["ab"], s["gstf"], s["gstb"], dog,
        nb, nch, nctx)
    rdqf, rdkf, rdvf, rdaf, rdqb, rdkb, rdvb, rdab = scan_bwd(
        "ret_scan_bwd", (s["rq"], 128, 0), (s["rk"], 128, 0), (pa, 256, 4), s["arf"], s["arb"], s["rstf"], s["rstb"],
        dor, nb, nch, nctx)

    pre_ins = _pre_ins(pa, tabs, p)
    extra = [gdqf, gdqb, rdqf, rdqb, rdkf, rdkb, gdkf, gdkb, gdvf, gdvb, rdvf, rdvb, dgg, drg]
    kinds = [i[0] for i in pre_ins]
    widx = [w[0] for w in _PRE_WANT]
    npre = len(pre_ins)

    def body(*refs):
        vals = [_load(k, r) for k, r in zip(kinds, refs[:npre])]
        rd = lambda i: refs[npre + i][...].astype(F32)
        cots = (rd(0) + rd(1), rd(14), rd(15), rd(16), rd(17), rd(2) + rd(3), rd(4) + rd(5), rd(18), rd(19), rd(20))

        def f(*dv):
            full = list(vals)
            for i, v in zip(widx, dv):
                full[i] = v
            return tuple(pre_fn(*full))

        _, vjp = jax.vjp(f, *[vals[i] for i in widx])
        grads = vjp(cots)
        dgq, drq, drk, dcq, dckv, dmisc = grads[:6]
        dp = jnp.concatenate([dgq, rd(6) + rd(7), rd(8) + rd(9), rd(12), drq, drk, rd(10) + rd(11), rd(13),
                              dcq, dckv, dmisc], axis=1)
        outs = refs[npre + 21:]
        outs[0][...] = dp.astype(BF16)
        first = pl.program_id(0) == 0
        for r, gr in zip(outs[1:], grads[6:]):
            @pl.when(first)
            def _(r=r):
                r[...] = jnp.zeros_like(r)
            r[...] += gr

    cot_arrays = extra + [gdaf, gdab, rdaf, rdab, dqa, dka, dva]
    par_arrays = [pre_ins[i][1] for i in range(10, 21)]
    res = pl.pallas_call(
        body, name="mix_pre_bwd", grid=(n // tr,),
        in_specs=[_spec(k, a, w, cb, tr, tps) for k, a, w, cb in pre_ins]
        + [pl.BlockSpec((tr, c.shape[1]), lambda j: (j, 0)) for c in cot_arrays],
        out_specs=[pl.BlockSpec((tr, P_PAD), lambda j: (j, 0))] + [pl.BlockSpec(a.shape, lambda j: (0, 0)) for a in par_arrays],
        out_shape=[jax.ShapeDtypeStruct((n, P_PAD), BF16)] + [jax.ShapeDtypeStruct(a.shape, F32) for a in par_arrays],
        compiler_params=_cparams(("arbitrary",)),
    )(*[i[1] for i in pre_ins], *cot_arrays)
    dp = res[0]
    for k, v in zip(("w2f", "w2b", "b2f", "b2b", "retf", "retb", "qg", "kvg", "wuq", "wuk", "wuv"), res[1:]):
        g[k] = v
    g["w_in_t"] = mm_tn("proj_dw", dp, s["x"], P_PAD, tr, mod=mod, sel=(1, 0))
    dx, dmod_d = mm_modbwd("proj_dx", dp, p["w_in_t"], s["x"], mod, dxa, (1, 0), tr)
    return dx, dmod_a + dmod_b + dmod_c + dmod_d, g


def _unprep_grads(g, w, l):
    o = {}
    o["w_in_t"] = _rows(g["w_in_t"], W_IN_INV_SEGS)
    o["ffn_up_t"] = g["w_up_t"]
    o["ffn_down"] = g["w_down"]
    o["w_out"] = g["w_out"]
    o["mla_w_uq_t"] = _unpad_heads(g["wuq"], 96)
    o["mla_w_uk_t"] = _unpad_heads(g["wuk"], 64)
    o["mla_w_uv_t"] = g["wuv"]
    o["gla_gate_w"] = jnp.stack([g["w2f"][0:16], g["w2b"][16:32]])
    o["gla_gate_b"] = jnp.concatenate([g["b2f"], g["b2b"]], axis=0)
    dlg = jnp.stack([g["retf"].reshape(4, 32).sum(-1), g["retb"].reshape(4, 32).sum(-1)])
    o["ret_decay"] = dlg * jax.nn.sigmoid(-w["ret_decay"][l])
    o["mla_q_norm_g"], o["mla_kv_norm_g"] = g["qg"][0], g["kvg"][0]
    o["gla_norm_g"] = g["gng"].reshape(4, 64).sum(0)
    o["ln1_g"], o["ln1_b"], o["ln2_g"], o["ln2_b"] = g["ln1g"][0], g["ln1b"][0], g["ln2g"][0], g["ln2b"][0]
    o["ffn_conv_w"] = _cols(g["cw"][0:3], FF_SEGS)
    o["ffn_conv_b"] = _cols(g["cb"][0], FF_SEGS)
    return o


def local_step(xs, target, modtab, layer_weights, dims, grads_ready=None):
    nb, tps, tr, nch, nctx = dims
    tabs = _tables((tps - 1) * tr, tr)
    x = xs
    saved, preps, ws = [], [], []
    for l in range(DEPTH):
        w = layer_weights(l, x)
        p = _prep_layer(w, 0)
        x, s = layer_fwd(l, x, modtab[l], p, tabs, dims)
        saved.append(s)
        preps.append(p)
        ws.append(w)
    dy, lpart = loss_head("loss_head", x, target, nb, tps, tr)
    loss = jnp.sum(lpart[:, 0, 0])
    dx = dy
    dmods, grads = [None] * DEPTH, [None] * DEPTH
    tok = None
    for l in reversed(range(DEPTH)):
        mod = modtab[l] if tok is None else modtab[l] + tok
        dx, dmods[l], g = layer_bwd(l, dx, saved[l], mod, preps[l], tabs, dims)
        grads[l] = _unprep_grads(g, ws[l], 0)
        tok = grads_ready(l, grads) if grads_ready is not None else None
    return loss, dx, jnp.stack(dmods), grads


BIG = [("ffn_up", 2), ("ffn_down", 1), ("w_out", 1), ("w_in", 2), ("mla_w_uq", 2), ("mla_w_uk", 2), ("mla_w_uv", 2)]
SMALL = ["ada_b", "gla_gate_w", "gla_gate_b", "gla_norm_g", "ret_decay", "mla_q_norm_g", "mla_kv_norm_g",
         "ln1_g", "ln1_b", "ffn_conv_b", "ln2_g", "ln2_b"]
PACK_C = 1024


def _big_key(k, axis):
    return k + "_t" if axis == 2 else k


def sum8(name, g8):
    k, r, c = g8.shape
    rows = max(b for b in range(16, ADAM_MAX_ROWS + 1, 16) if r % b == 0)

    def body(g_ref, o_ref):
        g = g_ref[0].astype(F32)
        for i in range(1, k):
            g = g + g_ref[i].astype(F32)
        o_ref[...] = g

    return pl.pallas_call(
        body, name=name, grid=(r // rows,), in_specs=[pl.BlockSpec((k, rows, c), lambda i: (0, i, 0))],
        out_specs=pl.BlockSpec((rows, c), lambda i: (i, 0)), out_shape=jax.ShapeDtypeStruct((r, c), F32),
        compiler_params=_cparams(("parallel",)),
    )(g8)


def _pack(arrs, dtype):
    flat = jnp.concatenate([a.reshape(-1).astype(dtype) for a in arrs])
    pad = (-flat.shape[0]) % (8 * PACK_C)
    return jnp.concatenate([flat, jnp.zeros((pad,), dtype)]).reshape(-1, PACK_C)


def _unpack(flat2d, shapes):
    flat = flat2d.reshape(-1)
    out, off = [], 0
    for s in shapes:
        sz = int(np.prod(s))
        out.append(flat[off:off + sz].reshape(s))
        off += sz
    return out


def _ff_order(b):
    return jnp.concatenate([b[0:2], b[4:6], b[2:4], b[6:8]], axis=0)


def _tile_pad(a):
    pad = (-a.shape[-2]) % HALO
    return a if pad == 0 else jnp.concatenate([a, jnp.zeros(a.shape[:-2] + (pad, a.shape[-1]), a.dtype)], axis=-2)


def kernel(x, c, ctx, c_ctx, ada_w, ada_b, w_in, gla_gate_w, gla_gate_b, gla_norm_g, ret_decay, mla_q_norm_g, mla_kv_norm_g, mla_w_uq, mla_w_uk, mla_w_uv, w_out, ln1_g, ln1_b, ffn_up, ffn_conv_w, ffn_conv_b, ffn_down, ln2_g, ln2_b, loss_target, m_c_ctx, m_ada_w, m_ada_b, m_w_in, m_gla_gate_w, m_gla_gate_b, m_gla_norm_g, m_ret_decay, m_mla_q_norm_g, m_mla_kv_norm_g, m_mla_w_uq, m_mla_w_uk, m_mla_w_uv, m_w_out, m_ln1_g, m_ln1_b, m_ffn_up, m_ffn_conv_w, m_ffn_conv_b, m_ffn_down, m_ln2_g, m_ln2_b, v_c_ctx, v_ada_w, v_ada_b, v_w_in, v_gla_gate_w, v_gla_gate_b, v_gla_norm_g, v_ret_decay, v_mla_q_norm_g, v_mla_kv_norm_g, v_mla_w_uq, v_mla_w_uk, v_mla_w_uv, v_w_out, v_ln1_g, v_ln1_b, v_ffn_up, v_ffn_conv_w, v_ffn_conv_b, v_ffn_down, v_ln2_g, v_ln2_b):
    names = ["c_ctx", "ada_w", "ada_b", "w_in", "gla_gate_w", "gla_gate_b", "gla_norm_g", "ret_decay", "mla_q_norm_g",
             "mla_kv_norm_g", "mla_w_uq", "mla_w_uk", "mla_w_uv", "w_out", "ln1_g", "ln1_b", "ffn_up", "ffn_conv_w",
             "ffn_conv_b", "ffn_down", "ln2_g", "ln2_b"]
    loc = locals()
    W = {k: loc[k] for k in names}
    M = {k: loc["m_" + k] for k in names}
    V = {k: loc["v_" + k] for k in names}

    nb, seq, _ = x.shape
    tr = ctx.shape[1]
    tps = 1 + seq // tr
    t = tps * tr
    n = nb * t
    nt = nb * tps
    dims = (nb, tps, tr, t // CHUNK, tr // CHUNK)
    px, py, pc = _place()
    me = 4 * px + 2 * py + pc
    ncol = ada_w.shape[2]

    cw_loc = ffn_conv_w.reshape(-1)
    g1 = jnp.concatenate([c.reshape(-1), cw_loc])
    g1 = jnp.concatenate([g1, jnp.zeros(((-g1.shape[0]) % (8 * PACK_C),), F32)]).reshape(-1, PACK_C)
    r1 = g1.shape[0]
    g1a = all_gather("gather_cond", g1, True).reshape(N_DEV, -1)
    c_all = g1a[:, :nb * D].reshape(N_DEV * nb, D)
    cw_all = g1a[:, nb * D:nb * D + cw_loc.shape[0]].reshape(N_DEV, DEPTH, 3, -1).transpose(1, 2, 0, 3).reshape(DEPTH, 3, -1)

    first, rest = [0], list(range(1, DEPTH))
    row_form = {k: (jnp.swapaxes(W[k], 1, 2) if ax == 2 else W[k]).astype(BF16) for k, ax in BIG}

    def part_rows(k):
        rows = int(np.prod(W[k].shape[1:])) // PACK_C
        return rows, -(-rows // HALO) * HALO

    def pack_rows(ls):
        return jnp.concatenate([_tile_pad(row_form[k][l].reshape(-1, PACK_C)) for k, _ in BIG for l in ls], axis=0)

    def whole_weights(wall, ls):
        out, off = {}, 0
        for k, ax in BIG:
            rows, padded = part_rows(k)
            _, r, c = row_form[k].shape
            order = _ff_order if k == "ffn_up" else (lambda b: b)
            out[_big_key(k, ax)] = [order(wall[:, off + i * padded:off + i * padded + rows]).reshape(1, N_DEV * r, c)
                                    for i in range(len(ls))]
            off += len(ls) * padded
        return out

    pack0 = pack_rows(first)
    whole0 = whole_weights(all_gather("gather_weights0", pack0, False).reshape(N_DEV, -1, PACK_C), first)
    pack_rest = pack_rows(rest)
    wsend, wrecv, wsrc, wland, wtok = xchg_start("gather_weights_start", pack_rest, False)
    small_w = {k: W[k] for k in SMALL[1:]}
    small_w["ffn_conv_w"] = cw_all
    later = {}

    def layer_weights(l, xin):
        if l >= 1 and not later:
            src, land = xchg_wait("gather_weights_wait", wsend, wrecv, wsrc, wland, xin, False)
            later.update(whole_weights(lax.dynamic_update_slice(land, src[None], (me, 0, 0)), rest))
        big = {k: v[0] for k, v in whole0.items()} if l == 0 else {k: v[l - 1] for k, v in later.items()}
        return {**big, **{k: v[l:l + 1] for k, v in small_w.items()}}

    srows = 40
    s_in = jnp.concatenate([c_all, c_ctx[None], jnp.zeros((srows - N_DEV * nb - 1, D), F32)], axis=0)
    s_act = _silu(s_in)
    ab_loc = lax.dynamic_slice_in_dim(ada_b, me * ncol, ncol, axis=1)[:, None, :]
    mod_part = ada_fwd("ada_fwd", s_act, ada_w, ab_loc)
    mod_all = all_gather("gather_mod", mod_part.reshape(-1, ncol), True).reshape(N_DEV, DEPTH, srows, ncol)
    mod_rows = mod_all.transpose(1, 2, 0, 3).reshape(DEPTH, srows, N_DEV * ncol)
    mod_l = lax.dynamic_slice_in_dim(mod_rows, me * nb, nb, axis=1).reshape(DEPTH, nb, 6, D)
    mod_c = mod_rows[:, N_DEV * nb].reshape(DEPTH, 1, 6, D)
    tile_is_ctx = (jnp.arange(tps) == 0)[None, None, :, None, None]
    modtab = jnp.where(tile_is_ctx, mod_c[:, :, None], mod_l[:, :, None])
    modtab = jnp.concatenate([modtab, jnp.zeros((DEPTH, nb, tps, 2, D), F32)], axis=3).reshape(DEPTH, nt, 8, D)
    modtab = modtab + wtok[0, 0]

    def grad_blocks(grads, ls):
        def blocks(k, ax, l):
            b = grads[l][_big_key(k, ax)].astype(BF16).reshape(N_DEV, -1, PACK_C)
            return _ff_order(b) if k == "ffn_up" else b

        return jnp.concatenate([_tile_pad(blocks(k, ax, l)) for k, ax in BIG for l in ls], axis=1)

    early = {}

    def grads_ready(l, grads):
        if l != 1:
            return None
        early["sems"] = xchg_start("grad_exchange_start", grad_blocks(grads, rest), True)
        return early["sems"][4][0, 0]

    xs = jnp.concatenate([ctx, x], axis=1).reshape(n, D)
    loss_loc, dxs, dmodtab, grads = local_step(xs, loss_target.reshape(nb * seq, D), modtab, layer_weights, dims,
                                               grads_ready)
    loss = lax.psum(loss_loc, ("x", "y", "c"))
    grad_x = dxs.reshape(nb, t, D)[:, tr:]
    gl = {k: jnp.stack([g[k] for g in grads]) for k in grads[0] if k in SMALL or k == "ffn_conv_w"}

    dm = dmodtab.reshape(DEPTH, nb, tps, 8, D)[:, :, :, :6]
    dmod_l = dm[:, :, 1:].sum(2).reshape(DEPTH, nb, 6 * D)
    dmod_c = dm[:, :, 0].sum(1).reshape(DEPTH, 1, 6 * D)
    gl["ada_b"] = dmod_l.sum(1) + dmod_c[:, 0]
    small_list = [gl[k] for k in SMALL] + [gl["ffn_conv_w"]]
    small_shapes = [a.shape for a in small_list]
    spack = _pack(small_list + [jnp.concatenate([dmod_l, dmod_c], axis=1)], F32)
    rs = spack.shape[0]
    sall = all_gather("gather_small_grads", spack, True).reshape(N_DEV, rs, PACK_C)
    nsmall = sum(int(np.prod(s)) for s in small_shapes)
    dmo = sall.reshape(N_DEV, -1)[:, nsmall:nsmall + DEPTH * (nb + 1) * 6 * D].reshape(N_DEV, DEPTH, nb + 1, 6 * D)
    dl_all = dmo[:, :, :nb].transpose(1, 0, 2, 3).reshape(DEPTH, N_DEV * nb, 6 * D)
    dc_all = dmo[:, :, nb].sum(0)[:, None]
    dmod_rows = jnp.concatenate([dl_all, dc_all, jnp.zeros((DEPTH, srows - N_DEV * nb - 1, 6 * D), F32)], axis=1)
    dmod_loc = lax.dynamic_slice_in_dim(dmod_rows.reshape(DEPTH, srows, N_DEV, ncol), me, 1, axis=2)[:, :, 0]
    d_ada_w, d_s = ada_bwd("ada_bwd", s_act, ada_w, dmod_loc)
    sg = jax.nn.sigmoid(c_ctx)
    dcc = d_s[:, N_DEV * nb].sum(0) * (sg * (1.0 + c_ctx * (1.0 - sg)))
    ccp = jnp.concatenate([dcc[None], jnp.zeros((7, D), F32)], axis=0)
    ccall = all_gather("gather_cctx", ccp, True).reshape(N_DEV, 8, D)

    esend, erecv, esrc, eland, _ = early["sems"]
    def landed(tag, sems, src, land, after):
        src, land = xchg_wait(tag + "_wait", sems[0], sems[1], src, land, after, True)
        mine = lax.dynamic_slice_in_dim(src, me, 1, axis=0)
        return sum8(tag + "_sum", lax.dynamic_update_slice(land, mine, (me, 0, 0)))

    fblocks, ccall = lax.optimization_barrier((grad_blocks(grads, first), ccall))
    fsend, frecv, fsrc, fland, _ = xchg_start("grad_exchange0_start", fblocks, True)
    gsum_rest = landed("grad_exchange", (esend, erecv), esrc, eland, ccall)
    res = {}

    def update2d(tag, k, g):
        last = W[k].shape[-1]
        outs = adamw(tag, W[k].reshape(-1, last), M[k].reshape(-1, last), V[k].reshape(-1, last),
                     g.reshape(1, -1, last))
        res[k] = [a.reshape(W[k].shape) for a in outs]

    def update(tag, keys, g8):
        outs = adamw(tag, _pack([W[k] for k in keys], F32), _pack([M[k] for k in keys], F32),
                     _pack([V[k] for k in keys], F32), g8)
        for i, arr in enumerate(outs):
            for k, a in zip(keys, _unpack(arr, [W[k].shape for k in keys])):
                res.setdefault(k, [None] * 4)[i] = a

    nrep = sum(int(np.prod(W[k].shape)) for k in SMALL)
    sflat = sall.reshape(N_DEV, -1)
    def pack8(a):
        a = a.reshape(N_DEV, -1)
        pad = (-a.shape[1]) % (8 * PACK_C)
        return jnp.concatenate([a, jnp.zeros((N_DEV, pad), F32)], axis=1).reshape(N_DEV, -1, PACK_C)

    update("adamw_small", SMALL, pack8(sflat[:, :nrep]))
    ncw = ffn_conv_w.shape[2]
    cw8 = sflat[:, nrep:nsmall].reshape(N_DEV, DEPTH, 3, N_DEV * ncw)
    cw8 = lax.dynamic_slice_in_dim(cw8, me * ncw, ncw, axis=3)
    update("adamw_conv", ["ffn_conv_w"], pack8(cw8))
    update2d("adamw_ada", "ada_w", d_ada_w)
    update("adamw_cctx", ["c_ctx"], ccall)

    gsum_first = landed("grad_exchange0", (fsend, frecv), fsrc, fland, res["ada_w"][1])
    off0, off1 = 0, 0
    for k, ax in BIG:
        rows, padded = part_rows(k)
        _, r, c = row_form[k].shape
        parts = [gsum_first[off0:off0 + rows]] + [gsum_rest[off1 + i * padded:off1 + i * padded + rows]
                                                  for i in range(len(rest))]
        g = jnp.stack([p.reshape(r, c) for p in parts])
        update2d("adamw_" + k, k, jnp.swapaxes(g, 1, 2) if ax == 2 else g)
        off0 += padded
        off1 += len(rest) * padded

    out = [loss, grad_x]
    for i in range(4):
        out += [res[k][i] for k in names]
    return tuple(out)
```

```python
import functools
import math

import numpy as np
import jax
import jax.numpy as jnp
from jax import lax
from jax.experimental import pallas as pl
from jax.experimental.pallas import tpu as pltpu

F32 = jnp.float32
BF16 = jnp.bfloat16
HI = lax.Precision.HIGHEST
MESH = pl.DeviceIdType.MESH

N_DEV = 8
D = 1024
DEPTH = 4
CHUNK = 64
EPS = 1e-6
ALPHA = (2 * DEPTH) ** 0.25
GLA_TAU = 16.0
ROPE_BASE = 10000.0
MLA_SCALE = 96 ** -0.5
D_FF = 2816
FF_CHUNK = 1408
P_PAD = 2048
VMEM_LIMIT_BYTES = 56 << 20

ADAM_LR, ADAM_B1, ADAM_B2, ADAM_EPS, ADAM_WD, ADAM_STEP = 0.001, 0.9, 0.999, 1e-08, 0.01, 10

D_IN = 1984
W_IN_SEGS = [(0, 512), (544, 1408), (512, 32), (1952, 32)]
W_IN_INV_SEGS = [(0, 512), (1920, 32), (512, 1408), (1952, 32)]
FF_SEGS = [(0, FF_CHUNK), (D_FF, FF_CHUNK), (FF_CHUNK, FF_CHUNK), (D_FF + FF_CHUNK, FF_CHUNK)]


def _cols(a, segs):
    return jnp.concatenate([a[..., s:s + n] for s, n in segs], axis=-1)


def _rows(a, segs):
    return jnp.concatenate([a[s:s + n] for s, n in segs], axis=0)


def _pad_heads(wt, per_head):
    c = wt.shape[1]
    wt = wt.reshape(8, per_head, c)
    return jnp.concatenate([wt, jnp.zeros((8, 128 - per_head, c), wt.dtype)], axis=1).reshape(1024, c)


def _unpad_heads(g, per_head):
    c = g.shape[1]
    return g.reshape(8, 128, c)[:, :per_head].reshape(8 * per_head, c)


def _cparams(sem=None):
    return pltpu.CompilerParams(vmem_limit_bytes=VMEM_LIMIT_BYTES, dimension_semantics=sem)


@jax.custom_vjp
def bdot(a, w):
    return jnp.dot(a.astype(BF16), w.astype(BF16), preferred_element_type=F32)


def _bdot_fwd(a, w):
    return bdot(a, w), (a, w)


def _bdot_bwd(res, ct):
    a, w = res
    ctb = ct.astype(BF16)
    da = lax.dot_general(ctb, w.astype(BF16), (((1,), (1,)), ((), ())), preferred_element_type=F32)
    dw = lax.dot_general(a.astype(BF16), ctb, (((0,), (0,)), ((), ())), preferred_element_type=F32)
    return da.astype(a.dtype), dw.astype(w.dtype)


bdot.defvjp(_bdot_fwd, _bdot_bwd)


@jax.custom_vjp
def bdot_nt(a, wt):
    return lax.dot_general(a.astype(BF16), wt.astype(BF16), (((1,), (1,)), ((), ())), preferred_element_type=F32)


def _bdot_nt_fwd(a, wt):
    return bdot_nt(a, wt), (a, wt)


def _bdot_nt_bwd(res, ct):
    a, wt = res
    ctb = ct.astype(BF16)
    da = jnp.dot(ctb, wt.astype(BF16), preferred_element_type=F32)
    dwt = lax.dot_general(ctb, a.astype(BF16), (((0,), (0,)), ((), ())), preferred_element_type=F32)
    return da.astype(a.dtype), dwt.astype(wt.dtype)


bdot_nt.defvjp(_bdot_nt_fwd, _bdot_nt_bwd)


@jax.custom_vjp
def bdot_tn(a, b):
    return lax.dot_general(a.astype(BF16), b.astype(BF16), (((0,), (0,)), ((), ())), preferred_element_type=F32)


def _bdot_tn_fwd(a, b):
    return bdot_tn(a, b), (a, b)


def _bdot_tn_bwd(res, ct):
    a, b = res
    ctb = ct.astype(BF16)
    da = lax.dot_general(b.astype(BF16), ctb, (((1,), (1,)), ((), ())), preferred_element_type=F32)
    db = jnp.dot(a.astype(BF16), ctb, preferred_element_type=F32)
    return da.astype(a.dtype), db.astype(b.dtype)


bdot_tn.defvjp(_bdot_tn_fwd, _bdot_tn_bwd)


def _split3(x):
    x1 = x.astype(BF16)
    r1 = x - x1.astype(F32)
    x2 = r1.astype(BF16)
    return x1, x2, (r1 - x2.astype(F32)).astype(BF16)


@jax.custom_vjp
def xdot(x, m):
    mb = m.astype(BF16)
    return sum(jnp.dot(xi, mb, preferred_element_type=F32) for xi in _split3(x))


def _xdot_bwd(m, ct):
    mb = m.astype(BF16)
    dx = sum(lax.dot_general(ci, mb, (((1,), (1,)), ((), ())), preferred_element_type=F32) for ci in _split3(ct))
    return dx, jnp.zeros_like(m)


xdot.defvjp(lambda x, m: (xdot(x, m), m), _xdot_bwd)


@jax.custom_vjp
def xdot_l(m, x):
    mb = m.astype(BF16)
    return sum(jnp.dot(mb, xi, preferred_element_type=F32) for xi in _split3(x))


def _xdot_l_bwd(m, ct):
    mb = m.astype(BF16)
    dx = sum(lax.dot_general(mb, ci, (((0,), (0,)), ((), ())), preferred_element_type=F32) for ci in _split3(ct))
    return jnp.zeros_like(m), dx


xdot_l.defvjp(lambda m, x: (xdot_l(m, x), m), _xdot_l_bwd)


def _swap_fn(half):
    def swap(x):
        n = x.shape[1]
        first = (_iota(x.shape, 1) // half) % 2 == 0
        return jnp.where(first, pltpu.roll(x, n - half, 1), pltpu.roll(x, half, 1))

    f = jax.custom_vjp(swap)
    f.defvjp(lambda x: (swap(x), None), lambda _, ct: (swap(ct),))
    return f


_swap16 = _swap_fn(16)
_swap8 = _swap_fn(8)


def hdot(a, b):
    return jnp.dot(a, b, precision=HI, preferred_element_type=F32)


def _iota(shape, axis):
    return lax.broadcasted_iota(jnp.int32, shape, axis)


def _group_avg(n, g):
    return (_iota((n, n), 0) // g == _iota((n, n), 1) // g).astype(F32) * (1.0 / g)


def _silu(x):
    return x * jax.nn.sigmoid(x)


def _layer_norm(z, g, b):
    mu = jnp.mean(z, axis=-1, keepdims=True)
    zc = z - mu
    var = jnp.mean(zc * zc, axis=-1, keepdims=True)
    return zc * lax.rsqrt(var + EPS) * g + b


def _rms(x, g):
    return x * lax.rsqrt(jnp.mean(x * x, axis=-1, keepdims=True) + EPS) * g


def mm(name, a, wt, out_dtype, tn, tr, mod=None, sel=None):
    n, k = a.shape
    nw = wt.shape[0]

    def body(*refs):
        if mod is not None:
            a_ref, m_ref, w_ref, o_ref = refs
            m = m_ref[0]
            av = a_ref[...] * (1.0 + m[sel[0]:sel[0] + 1]) + m[sel[1]:sel[1] + 1]
        else:
            a_ref, w_ref, o_ref = refs
            av = a_ref[...]
        o_ref[...] = lax.dot_general(av.astype(BF16), w_ref[...], (((1,), (1,)), ((), ())),
                                     preferred_element_type=F32).astype(o_ref.dtype)

    in_specs = [pl.BlockSpec((tr, k), lambda c, j: (j, 0))]
    args = [a]
    if mod is not None:
        in_specs.append(pl.BlockSpec((1, 8, k), lambda c, j: (j, 0, 0)))
        args.append(mod)
    in_specs.append(pl.BlockSpec((tn, k), lambda c, j: (c, 0)))
    args.append(wt)
    return pl.pallas_call(
        body, name=name, grid=(nw // tn, n // tr), in_specs=in_specs,
        out_specs=pl.BlockSpec((tr, tn), lambda c, j: (j, c)),
        out_shape=jax.ShapeDtypeStruct((n, nw), out_dtype), compiler_params=_cparams(("parallel", "arbitrary")),
    )(*args)


def mm_tn(name, dc, a, tn, tr, mod=None, sel=None):
    n, k = a.shape
    nw = dc.shape[1]

    def body(*refs):
        if mod is not None:
            d_ref, a_ref, m_ref, o_ref = refs
            m = m_ref[0]
            av = a_ref[...] * (1.0 + m[sel[0]:sel[0] + 1]) + m[sel[1]:sel[1] + 1]
        else:
            d_ref, a_ref, o_ref = refs
            av = a_ref[...]

        @pl.when(pl.program_id(1) == 0)
        def _():
            o_ref[...] = jnp.zeros_like(o_ref)

        o_ref[...] += lax.dot_general(d_ref[...].astype(BF16), av.astype(BF16), (((0,), (0,)), ((), ())),
                                      preferred_element_type=F32)

    in_specs = [pl.BlockSpec((tr, tn), lambda c, j: (j, c)), pl.BlockSpec((tr, k), lambda c, j: (j, 0))]
    args = [dc, a]
    if mod is not None:
        in_specs.append(pl.BlockSpec((1, 8, k), lambda c, j: (j, 0, 0)))
        args.append(mod)
    return pl.pallas_call(
        body, name=name, grid=(nw // tn, n // tr), in_specs=in_specs,
        out_specs=pl.BlockSpec((tn, k), lambda c, j: (c, 0)),
        out_shape=jax.ShapeDtypeStruct((nw, k), F32), compiler_params=_cparams(("parallel", "arbitrary")),
    )(*args)


def mm_modbwd(name, dc, wt, x, mod, add, sel, tr):
    n, k = dc.shape
    dm = wt.shape[1]

    def body(dc_ref, wt_ref, x_ref, m_ref, add_ref, dx_ref, dm_ref):
        dh = jnp.dot(dc_ref[...].astype(BF16), wt_ref[...], preferred_element_type=F32)
        m = m_ref[0]
        dx_ref[...] = add_ref[...] + dh * (1.0 + m[sel[0]:sel[0] + 1])
        dsc = jnp.sum(dh * x_ref[...], axis=0, keepdims=True)
        dsh = jnp.sum(dh, axis=0, keepdims=True)
        rows = _iota((8, dm), 0)
        dm_ref[0] = jnp.where(rows == sel[0], dsc, 0.0) + jnp.where(rows == sel[1], dsh, 0.0)

    return pl.pallas_call(
        body, name=name, grid=(n // tr,),
        in_specs=[pl.BlockSpec((tr, k), lambda j: (j, 0)), pl.BlockSpec((k, dm), lambda j: (0, 0)),
                  pl.BlockSpec((tr, dm), lambda j: (j, 0)), pl.BlockSpec((1, 8, dm), lambda j: (j, 0, 0)),
                  pl.BlockSpec((tr, dm), lambda j: (j, 0))],
        out_specs=[pl.BlockSpec((tr, dm), lambda j: (j, 0)), pl.BlockSpec((1, 8, dm), lambda j: (j, 0, 0))],
        out_shape=[jax.ShapeDtypeStruct((n, dm), F32), jax.ShapeDtypeStruct((n // tr, 8, dm), F32)],
        compiler_params=_cparams(("arbitrary",)),
    )(dc, wt, x, mod, add)


def _spec(kind, arr, width, cb, tr, tps):
    if kind == "row":
        return pl.BlockSpec((tr, width), lambda j: (j, cb))
    if kind == "pos":
        return pl.BlockSpec((tr, width), lambda j: (j % tps, cb))
    if kind == "tile":
        return pl.BlockSpec((1,) + arr.shape[1:], lambda j: (j, 0, 0))
    if kind == "par":
        return pl.BlockSpec(arr.shape, lambda j: (0, 0))
    raise ValueError(kind)


def _load(kind, ref):
    v = ref[0] if kind == "tile" else ref[...]
    return v.astype(F32)


def tile_fwd(name, fn, ins, outs, n, tr, tps):
    kinds = [i[0] for i in ins]

    def body(*refs):
        vals = [_load(k, r) for k, r in zip(kinds, refs[:len(ins)])]
        res = fn(*vals)
        for r, o in zip(refs[len(ins):], res):
            r[...] = o.astype(r.dtype)

    return pl.pallas_call(
        body, name=name, grid=(n // tr,),
        in_specs=[_spec(k, a, w, cb, tr, tps) for k, a, w, cb in ins],
        out_specs=[pl.BlockSpec((tr, w), lambda j: (j, 0)) for w, _ in outs],
        out_shape=[jax.ShapeDtypeStruct((n, w), dt) for w, dt in outs],
        compiler_params=_cparams(("arbitrary",)),
    )(*[i[1] for i in ins])


def tile_bwd(name, fn, ins, cots, want, n, tr, tps):
    kinds = [i[0] for i in ins]
    widx = [w[0] for w in want]
    ni, nc = len(ins), len(cots)

    def body(*refs):
        vals = [_load(k, r) for k, r in zip(kinds, refs[:ni])]
        cvals = tuple(r[...].astype(F32) for r in refs[ni:ni + nc])

        def f(*dv):
            full = list(vals)
            for i, v in zip(widx, dv):
                full[i] = v
            return tuple(fn(*full))

        _, vjp = jax.vjp(f, *[vals[i] for i in widx])
        grads = vjp(cvals)
        first = pl.program_id(0) == 0
        for r, g, i in zip(refs[ni + nc:], grads, widx):
            if kinds[i] == "par":
                @pl.when(first)
                def _(r=r):
                    r[...] = jnp.zeros_like(r)
                r[...] += g
            elif kinds[i] == "tile":
                r[0] = g.astype(r.dtype)
            else:
                r[...] = g.astype(r.dtype)

    out_specs, out_shape = [], []
    for i, dt in want:
        k, a, w, cb = ins[i]
        if k == "par":
            out_specs.append(pl.BlockSpec(a.shape, lambda j: (0, 0)))
            out_shape.append(jax.ShapeDtypeStruct(a.shape, F32))
        elif k == "tile":
            out_specs.append(pl.BlockSpec((1,) + a.shape[1:], lambda j: (j, 0, 0)))
            out_shape.append(jax.ShapeDtypeStruct(a.shape, F32))
        else:
            out_specs.append(pl.BlockSpec((tr, w), lambda j: (j, 0)))
            out_shape.append(jax.ShapeDtypeStruct((n, w), dt))
    return pl.pallas_call(
        body, name=name, grid=(n // tr,),
        in_specs=[_spec(k, a, w, cb, tr, tps) for k, a, w, cb in ins]
        + [pl.BlockSpec((tr, c.shape[1]), lambda j: (j, 0)) for c in cots],
        out_specs=out_specs, out_shape=out_shape, compiler_params=_cparams(("arbitrary",)),
    )(*[i[1] for i in ins], *cots)


def pre_fn(p_gq, p_rq, p_rk, p_cq, p_ckv, p_misc, rcos, rsin, mcos, msin,
           w2f, w2b, b2f, b2b, retf, retb, qg, kvg, wuq, wuk, wuv, e2):
    tr = p_gq.shape[0]
    gq = p_gq * (32 ** -0.5)
    af = jax.nn.log_sigmoid(hdot(p_misc, w2f) + b2f) * (1.0 / GLA_TAU)
    ab = jax.nn.log_sigmoid(hdot(p_misc, w2b) + b2b) * (1.0 / GLA_TAU)
    arf = jnp.zeros((tr, 128), F32) + retf
    arb = jnp.zeros((tr, 128), F32) + retb
    rq = p_rq * rcos + _swap16(p_rq) * rsin
    rks = p_rk * (32 ** -0.5)
    rk = rks * rcos + _swap16(rks) * rsin
    qp = bdot_nt(_rms(p_cq, qg), wuq) * MLA_SCALE
    ckvn = _rms(p_ckv, kvg)
    kp = bdot_nt(ckvn, wuk) + xdot(p_misc, e2)
    mc, ms = jnp.tile(mcos, (1, 8)), jnp.tile(msin, (1, 8))
    v = bdot_nt(ckvn, wuv)
    return gq, af, ab, arf, arb, rq, rk, qp * mc + _swap8(qp) * ms, kp * mc + _swap8(kp) * ms, v


def post_fn(ogf, ogb, orf, orb, om, gg, rg, x, mod, gng, wout, lng, lnb):
    avg = _group_avg(256, 64)
    og = ogf + ogb
    mg = og * lax.rsqrt(xdot(og * og, avg) + EPS) * gng * _silu(gg)
    orr = orf + orb
    oc = orr - xdot(orr, avg)
    mr = oc * lax.rsqrt(xdot(oc * oc, avg) + EPS) * _silu(rg)
    m = jnp.concatenate([mg, mr, om], axis=1)
    y = bdot(m, wout)
    return (_layer_norm(ALPHA * x + mod[2:3] * y, lng, lnb),)


def ln2_fn(x1, f, mod, lng, lnb):
    return (_layer_norm(ALPHA * x1 + mod[5:6] * f, lng, lnb),)


def scan_step(q, k, v, a, st, rev):
    ii, jj = _iota((CHUNK, CHUNK), 0), _iota((CHUNK, CHUNK), 1)
    tri = ((jj >= ii) if rev else (jj <= ii)).astype(F32)
    b = xdot_l(tri, a)
    btot = jnp.sum(a, axis=0, keepdims=True)
    qe = q * jnp.exp(b - btot)
    ke = k * jnp.exp(btot - b)
    lane = _iota((1, 128), 1)
    q4 = jnp.concatenate([qe * (lane // 32 == h).astype(F32) for h in range(4)], axis=0)
    att = bdot_nt(q4, ke)
    att = jnp.where(jnp.concatenate([tri] * 4, axis=0) > 0, att, 0.0)
    r = bdot(att, v)
    col = _iota((1, 256), 1)
    o = bdot_nt(q * jnp.exp(b), st)
    for h in range(4):
        o = o + r[h * CHUNK:(h + 1) * CHUNK] * (col // 64 == h).astype(F32)
    vk = bdot_tn(v, ke)
    bd = (_iota((256, 128), 0) // 64 == _iota((256, 128), 1) // 32).astype(F32)
    return o, st * jnp.exp(btot) + vk * bd


def _chunk_maps(nch, nctx):
    def fwd(s):
        return s

    def bwd(s):
        return jnp.where(s < nctx, nctx - 1 - s, nch - 1 - (s - nctx))
    return fwd, bwd


def _per_sample(arr, nb):
    return arr.reshape(nb, arr.shape[0] // nb, arr.shape[1])


def scan_fwd(name, q, k, v, af, ab, nb, nch, nctx):
    n = af.shape[0]
    fmap, bmap = _chunk_maps(nch, nctx)

    def body(qf, kf, vf, a_f, qb, kb, vb, a_b, of_ref, ob_ref, stf_ref, stb_ref, s_scr):
        @pl.when(pl.program_id(0) == 0)
        def _():
            s_scr[...] = jnp.zeros_like(s_scr)

        for i in range(nb):
            stf_ref[0, i] = s_scr[2 * i]
            stb_ref[0, i] = s_scr[2 * i + 1]
            o, sn = scan_step(qf[i], kf[i], vf[i], a_f[i], s_scr[2 * i], False)
            of_ref[i] = o
            s_scr[2 * i] = sn
            o, sn = scan_step(qb[i], kb[i], vb[i], a_b[i], s_scr[2 * i + 1], True)
            ob_ref[i] = o
            s_scr[2 * i + 1] = sn

    def specs(m):
        return [pl.BlockSpec((nb, CHUNK, w), lambda s, cb=cb: (0, m(s), cb)) for _, w, cb in (q, k, v)] + \
               [pl.BlockSpec((nb, CHUNK, 128), lambda s: (0, m(s), 0))]

    ps = lambda a: _per_sample(a, nb)
    of, ob, stf, stb = pl.pallas_call(
        body, name=name, grid=(nch,), in_specs=specs(fmap) + specs(bmap),
        out_specs=[pl.BlockSpec((nb, CHUNK, 256), lambda s: (0, fmap(s), 0)),
                   pl.BlockSpec((nb, CHUNK, 256), lambda s: (0, bmap(s), 0)),
                   pl.BlockSpec((1, nb, 256, 128), lambda s: (s, 0, 0, 0)),
                   pl.BlockSpec((1, nb, 256, 128), lambda s: (s, 0, 0, 0))],
        out_shape=[jax.ShapeDtypeStruct((nb, n // nb, 256), F32)] * 2
        + [jax.ShapeDtypeStruct((nch, nb, 256, 128), F32)] * 2,
        scratch_shapes=[pltpu.VMEM((2 * nb, 256, 128), F32)], compiler_params=_cparams(("arbitrary",)),
    )(ps(q[0]), ps(k[0]), ps(v[0]), ps(af), ps(q[0]), ps(k[0]), ps(v[0]), ps(ab))
    return of.reshape(n, 256), ob.reshape(n, 256), stf, stb


def scan_bwd(name, q, k, v, af, ab, stf, stb, do, nb, nch, nctx):
    n = af.shape[0]
    fmap0, bmap0 = _chunk_maps(nch, nctx)
    fmap = lambda r: fmap0(nch - 1 - r)
    bmap = lambda r: bmap0(nch - 1 - r)

    def body(qf, kf, vf, a_f, sf, dof, qb, kb, vb, a_b, sb, dob,
             dqf, dkf, dvf, daf, dqb, dkb, dvb, dab, ds_scr):
        @pl.when(pl.program_id(0) == 0)
        def _():
            ds_scr[...] = jnp.zeros_like(ds_scr)

        for i in range(nb):
            for d, (qr, kr, vr, ar, sr, dor, outs) in enumerate(((qf, kf, vf, a_f, sf, dof, (dqf, dkf, dvf, daf)),
                                                                   (qb, kb, vb, a_b, sb, dob, (dqb, dkb, dvb, dab)))):
                _, vjp = jax.vjp(functools.partial(scan_step, rev=bool(d)), qr[i], kr[i], vr[i], ar[i], sr[0, i])
                dq, dk, dv, da, ds = vjp((dor[i], ds_scr[2 * i + d]))
                outs[0][i] = dq
                outs[1][i] = dk
                outs[2][i] = dv
                outs[3][i] = da
                ds_scr[2 * i + d] = ds

    def specs(m):
        return [pl.BlockSpec((nb, CHUNK, w), lambda r, cb=cb: (0, m(r), cb)) for _, w, cb in (q, k, v)] + \
               [pl.BlockSpec((nb, CHUNK, 128), lambda r: (0, m(r), 0)),
                pl.BlockSpec((1, nb, 256, 128), lambda r: (nch - 1 - r, 0, 0, 0)),
                pl.BlockSpec((nb, CHUNK, 256), lambda r: (0, m(r), 0))]

    def ospecs(m):
        return [pl.BlockSpec((nb, CHUNK, w), lambda r: (0, m(r), 0)) for w in (128, 128, 256, 128)]

    ps = lambda a: _per_sample(a, nb)
    oshape = [jax.ShapeDtypeStruct((nb, n // nb, w), F32) for w in (128, 128, 256, 128)]
    outs = pl.pallas_call(
        body, name=name, grid=(nch,), in_specs=specs(fmap) + specs(bmap),
        out_specs=ospecs(fmap) + ospecs(bmap), out_shape=oshape + oshape,
        scratch_shapes=[pltpu.VMEM((2 * nb, 256, 128), F32)], compiler_params=_cparams(("arbitrary",)),
    )(ps(q[0]), ps(k[0]), ps(v[0]), ps(af), stf, ps(do), ps(q[0]), ps(k[0]), ps(v[0]), ps(ab), stb, ps(do))
    return [o.reshape(n, o.shape[2]) for o in outs]


def mla_fwd(name, qa, ka, va, nb, tps, tr, nctx_rows):
    n = qa.shape[0]
    t = tps * tr

    def body(q_ref, k_ref, v_ref, o_ref, lse_ref):
        def attend(nk):
            vv = v_ref[0:nk, :]
            first = _iota(vv.shape, 1) < 64
            one = jnp.ones_like(vv)
            res, lses = [], []
            for h in range(2):
                s = lax.dot_general(q_ref[:, h * 128:(h + 1) * 128], k_ref[0:nk, h * 128:(h + 1) * 128],
                                    (((1,), (1,)), ((), ())), preferred_element_type=F32)
                m = jnp.max(s, axis=-1, keepdims=True)
                e = jnp.exp((s - m).astype(BF16))
                r = jnp.dot(e, jnp.where(first == (h == 0), vv, one), preferred_element_type=F32)
                l = r[:, 64:65] if h == 0 else r[:, 0:1]
                res.append(r / l)
                lses.append(m + jnp.log(l))
            lane = _iota((tr, 128), 1) < 64
            o_ref[...] = jnp.where(lane, res[0], res[1])
            lse_ref[...] = jnp.where(lane, lses[0], lses[1])

        @pl.when(pl.program_id(2) == 0)
        def _():
            attend(nctx_rows)

        @pl.when(pl.program_id(2) > 0)
        def _():
            attend(t)

    return pl.pallas_call(
        body, name=name, grid=(nb, 4, tps),
        in_specs=[pl.BlockSpec((tr, 256), lambda b, h, j: (b * tps + j, h)), pl.BlockSpec((t, 256), lambda b, h, j: (b, h)),
                  pl.BlockSpec((t, 128), lambda b, h, j: (b, h))],
        out_specs=[pl.BlockSpec((tr, 128), lambda b, h, j: (b * tps + j, h))] * 2,
        out_shape=[jax.ShapeDtypeStruct((n, 512), F32)] * 2,
        compiler_params=_cparams(("parallel", "parallel", "arbitrary")),
    )(qa, ka, va)


def mla_bwd(name, qa, ka, va, o, lse, do, nb, tps, tr, nctx_rows):
    n = qa.shape[0]
    t = tps * tr

    def body(q_ref, k_ref, v_ref, o_ref, lse_ref, do_ref, dq_ref, dk_ref, dv_ref, dkt, dvt):
        @pl.when(pl.program_id(2) == 0)
        def _():
            dkt[...] = jnp.zeros_like(dkt)
            dvt[...] = jnp.zeros_like(dvt)

        def attend(nk):
            dov = do_ref[...]
            oo = dov * o_ref[...]
            dob = dov.astype(BF16)
            first = _iota(dob.shape, 1) < 64
            dqs = []
            for h in range(2):
                hs = slice(h * 128, (h + 1) * 128)
                qh, kh = q_ref[:, hs], k_ref[0:nk, hs]
                mine = first == (h == 0)
                delta = jnp.sum(jnp.where(mine, oo, 0.0), axis=-1, keepdims=True)
                doh = jnp.where(mine, dob, jnp.zeros_like(dob))
                s = lax.dot_general(qh, kh, (((1,), (1,)), ((), ())), preferred_element_type=F32)
                p = jnp.exp((s - lse_ref[:, h * 64:h * 64 + 1]).astype(BF16))
                dp = lax.dot_general(doh, v_ref[0:nk, :], (((1,), (1,)), ((), ())), preferred_element_type=F32)
                ds = p * (dp - delta).astype(BF16)
                dqs.append(jnp.dot(ds, kh, preferred_element_type=F32))
                dkt[hs, 0:nk] += lax.dot_general(qh, ds, (((0,), (0,)), ((), ())), preferred_element_type=F32)
                dvt[:, 0:nk] += lax.dot_general(doh, p, (((0,), (0,)), ((), ())), preferred_element_type=F32)
            dq_ref[...] = jnp.concatenate(dqs, axis=1)

        @pl.when(pl.program_id(2) == 0)
        def _():
            attend(nctx_rows)

        @pl.when(pl.program_id(2) > 0)
        def _():
            attend(t)

        @pl.when(pl.program_id(2) == tps - 1)
        def _():
            dk_ref[...] = dkt[...].T
            dv_ref[...] = dvt[...].T

    qtile = pl.BlockSpec((tr, 128), lambda b, h, j: (b * tps + j, h))
    return pl.pallas_call(
        body, name=name, grid=(nb, 4, tps),
        in_specs=[pl.BlockSpec((tr, 256), lambda b, h, j: (b * tps + j, h)), pl.BlockSpec((t, 256), lambda b, h, j: (b, h)),
                  pl.BlockSpec((t, 128), lambda b, h, j: (b, h)), qtile, qtile, qtile],
        out_specs=[pl.BlockSpec((tr, 256), lambda b, h, j: (b * tps + j, h)), pl.BlockSpec((t, 256), lambda b, h, j: (b, h)),
                   pl.BlockSpec((t, 128), lambda b, h, j: (b, h))],
        out_shape=[jax.ShapeDtypeStruct((n, 1024), F32), jax.ShapeDtypeStruct((n, 1024), F32),
                   jax.ShapeDtypeStruct((n, 512), F32)],
        scratch_shapes=[pltpu.VMEM((256, t), F32), pltpu.VMEM((128, t), F32)],
        compiler_params=_cparams(("parallel", "parallel", "arbitrary")),
    )(qa, ka, va, o, lse, do)


HALO = 16


def _halo_specs(tr, width, tps, nt):
    r = tr // HALO
    return [pl.BlockSpec((tr, width), lambda j, c: (j, c)),
            pl.BlockSpec((HALO, width), lambda j, c: (jnp.maximum(j * r - 1, 0), c)),
            pl.BlockSpec((HALO, width), lambda j, c: (jnp.minimum((j + 1) * r, nt * r - 1), c))]


def _shifted(u, prev, nxt, j, tps):
    tr = u.shape[0]
    t = j % tps
    has_prev = (t >= 2).astype(F32)
    has_next = jnp.logical_and(t >= 1, t <= tps - 2).astype(F32)
    rows = _iota(u.shape, 0)
    dn = jnp.where(rows == 0, prev[HALO - 1:HALO] * has_prev, pltpu.roll(u, 1, 0))
    up = jnp.where(rows == tr - 1, nxt[0:1] * has_next, pltpu.roll(u, tr - 1, 0))
    return dn, up


def _sigmoid_eup(x):
    return pl.reciprocal(1.0 + jnp.exp(-x), approx=True)


def _ffn_act(ucv):
    a = ucv[:, :FF_CHUNK]
    return a * _sigmoid_eup(a) * ucv[:, FF_CHUNK:]


def ffn2_fwd(name, u, cw, cb, wd, x1, mod, lng, lnb, tr, tps):
    n = u.shape[0]
    nt = n // tr
    w2 = 2 * FF_CHUNK

    def body(u_ref, up_ref, un_ref, cw_ref, cb_ref, wd_ref, x1_ref, m_ref, g_ref, b_ref, f_ref, x2_ref, ucv_ref, acc):
        j, c = pl.program_id(0), pl.program_id(1)
        uu = u_ref[...].astype(F32)
        dn, up = _shifted(uu, up_ref[...].astype(F32), un_ref[...].astype(F32), j, tps)
        cwv = cw_ref[...]
        ucv = cwv[0:1] * dn + cwv[1:2] * uu + cwv[2:3] * up + cb_ref[...]
        ucv_ref[...] = ucv.astype(ucv_ref.dtype)
        part = bdot(_ffn_act(ucv), wd_ref[...])

        @pl.when(c == 0)
        def _():
            acc[...] = part

        @pl.when(c == 1)
        def _():
            f = acc[...] + part
            f_ref[...] = f
            x2_ref[...] = ln2_fn(x1_ref[...], f, m_ref[0], g_ref[...], b_ref[...])[0]

    return pl.pallas_call(
        body, name=name, grid=(nt, 2),
        in_specs=_halo_specs(tr, w2, tps, nt) + [
            pl.BlockSpec((8, w2), lambda j, c: (0, c)), pl.BlockSpec((1, w2), lambda j, c: (0, c)),
            pl.BlockSpec((FF_CHUNK, D), lambda j, c: (c, 0)), pl.BlockSpec((tr, D), lambda j, c: (j, 0)),
            pl.BlockSpec((1, 8, D), lambda j, c: (j, 0, 0)), pl.BlockSpec((1, D), lambda j, c: (0, 0)),
            pl.BlockSpec((1, D), lambda j, c: (0, 0))],
        out_specs=[pl.BlockSpec((tr, D), lambda j, c: (j, 0)), pl.BlockSpec((tr, D), lambda j, c: (j, 0)),
                   pl.BlockSpec((tr, w2), lambda j, c: (j, c))],
        out_shape=[jax.ShapeDtypeStruct((n, D), F32)] * 2 + [jax.ShapeDtypeStruct((n, 2 * w2), BF16)],
        scratch_shapes=[pltpu.VMEM((tr, D), F32)], compiler_params=_cparams(("arbitrary", "arbitrary")),
    )(u, u, u, cw, cb, wd, x1, mod, lng, lnb)


def ffn2_bwd(name, ucv, wd, df, tr):
    n = ucv.shape[0]
    nt = n // tr
    w2 = 2 * FF_CHUNK

    def body(ucv_ref, wd_ref, df_ref, ducv_ref, dwd_ref):
        j = pl.program_id(1)
        a, g = ucv_ref[:, :FF_CHUNK].astype(F32), ucv_ref[:, FF_CHUNK:].astype(F32)
        sg = _sigmoid_eup(a)
        sa = a * sg
        dfb = df_ref[...].astype(BF16)
        dact = lax.dot_general(dfb, wd_ref[...], (((1,), (1,)), ((), ())), preferred_element_type=F32)
        ducv_ref[:, :FF_CHUNK] = (dact * g * (sg + sa * (1.0 - sg))).astype(ducv_ref.dtype)
        ducv_ref[:, FF_CHUNK:] = (dact * sa).astype(ducv_ref.dtype)
        dwd = lax.dot_general((sa * g).astype(BF16), dfb, (((0,), (0,)), ((), ())), preferred_element_type=F32)

        @pl.when(j == 0)
        def _():
            dwd_ref[...] = jnp.zeros_like(dwd_ref)

        dwd_ref[...] += dwd

    return pl.pallas_call(
        body, name=name, grid=(2, nt),
        in_specs=[pl.BlockSpec((tr, w2), lambda c, j: (j, c)), pl.BlockSpec((FF_CHUNK, D), lambda c, j: (c, 0)),
                  pl.BlockSpec((tr, D), lambda c, j: (j, 0))],
        out_specs=[pl.BlockSpec((tr, w2), lambda c, j: (j, c)), pl.BlockSpec((FF_CHUNK, D), lambda c, j: (c, 0))],
        out_shape=[jax.ShapeDtypeStruct((n, 2 * w2), BF16), jax.ShapeDtypeStruct((D_FF, D), F32)],
        compiler_params=_cparams(("parallel", "arbitrary")),
    )(ucv, wd, df)


def conv_bwd(name, ducv, u, cw, tr, tps):
    n = u.shape[0]
    nt = n // tr
    w2 = 2 * FF_CHUNK

    def body(g_ref, gp_ref, gn_ref, u_ref, up_ref, un_ref, cw_ref, du_ref, dcw_ref, dcb_ref):
        c, j = pl.program_id(0), pl.program_id(1)
        g = g_ref[...].astype(F32)
        gdn, gup = _shifted(g, gp_ref[...].astype(F32), gn_ref[...].astype(F32), j, tps)
        uu = u_ref[...].astype(F32)
        udn, uup = _shifted(uu, up_ref[...].astype(F32), un_ref[...].astype(F32), j, tps)
        cwv = cw_ref[...]
        du_ref[...] = (cwv[0:1] * gup + cwv[1:2] * g + cwv[2:3] * gdn).astype(du_ref.dtype)
        rows = _iota((8, w2), 0)
        s = lambda z: jnp.sum(z, axis=0, keepdims=True)
        dcw = (jnp.where(rows == 0, s(g * udn), 0.0) + jnp.where(rows == 1, s(g * uu), 0.0)
               + jnp.where(rows == 2, s(g * uup), 0.0))

        @pl.when(j == 0)
        def _():
            dcw_ref[...] = jnp.zeros_like(dcw_ref)
            dcb_ref[...] = jnp.zeros_like(dcb_ref)

        dcw_ref[...] += dcw
        dcb_ref[...] += s(g)

    hs = _halo_specs(tr, w2, tps, nt)
    swap = lambda spec: pl.BlockSpec(spec.block_shape, lambda c, j, f=spec.index_map: f(j, c))
    return pl.pallas_call(
        body, name=name, grid=(2, nt),
        in_specs=[swap(s) for s in hs] * 2 + [pl.BlockSpec((8, w2), lambda c, j: (0, c))],
        out_specs=[pl.BlockSpec((tr, w2), lambda c, j: (j, c)), pl.BlockSpec((8, w2), lambda c, j: (0, c)),
                   pl.BlockSpec((1, w2), lambda c, j: (0, c))],
        out_shape=[jax.ShapeDtypeStruct((n, 2 * w2), BF16), jax.ShapeDtypeStruct((8, 2 * w2), F32),
                   jax.ShapeDtypeStruct((1, 2 * w2), F32)],
        compiler_params=_cparams(("parallel", "arbitrary")),
    )(ducv, ducv, ducv, u, u, u, cw)


def loss_head(name, xf, target, nb, tps, tr):
    n = xf.shape[0]

    def body(x_ref, t_ref, dy_ref, l_ref):
        lat = (pl.program_id(0) % tps > 0).astype(F32)
        err = (x_ref[...] - t_ref[...]) * lat
        dy_ref[...] = err * (1.0 / D)
        l_ref[...] = jnp.zeros_like(l_ref) + 0.5 * jnp.sum(err * err) * (1.0 / D)

    def tmap(j):
        return ((j // tps) * (tps - 1) + jnp.maximum(j % tps - 1, 0), 0)

    return pl.pallas_call(
        body, name=name, grid=(n // tr,),
        in_specs=[pl.BlockSpec((tr, D), lambda j: (j, 0)), pl.BlockSpec((tr, D), tmap)],
        out_specs=[pl.BlockSpec((tr, D), lambda j: (j, 0)), pl.BlockSpec((1, 8, 128), lambda j: (j, 0, 0))],
        out_shape=[jax.ShapeDtypeStruct((n, D), F32), jax.ShapeDtypeStruct((n // tr, 8, 128), F32)],
        compiler_params=_cparams(("arbitrary",)),
    )(xf, target)


ADAM_MAX_ROWS = 512


def adamw(name, w, m, v, g8):
    r, c = w.shape
    k = g8.shape[0]
    rows = max(b for b in range(8, ADAM_MAX_ROWS + 1, 8) if r % b == 0)
    bc1 = 1.0 - ADAM_B1 ** ADAM_STEP
    bc2 = 1.0 - ADAM_B2 ** ADAM_STEP

    def body(w_ref, m_ref, v_ref, g_ref, go_ref, d_ref, mo_ref, vo_ref):
        g = g_ref[0].astype(F32)
        for i in range(1, k):
            g = g + g_ref[i].astype(F32)
        mn = ADAM_B1 * m_ref[...] + (1.0 - ADAM_B1) * g
        vn = ADAM_B2 * v_ref[...] + (1.0 - ADAM_B2) * (g * g)
        go_ref[...] = g
        mo_ref[...] = mn
        vo_ref[...] = vn
        d_ref[...] = -ADAM_LR * ((mn / bc1) / (jnp.sqrt(vn / bc2) + ADAM_EPS) + ADAM_WD * w_ref[...])

    blk = pl.BlockSpec((rows, c), lambda i: (i, 0))
    return pl.pallas_call(
        body, name=name, grid=(r // rows,),
        in_specs=[blk, blk, blk, pl.BlockSpec((k, rows, c), lambda i: (0, i, 0))],
        out_specs=[blk] * 4, out_shape=[jax.ShapeDtypeStruct((r, c), F32)] * 4,
        compiler_params=_cparams(("parallel",)),
    )(w, m, v, g8)


def ada_fwd(name, s, aw, ab):
    nl, _, cw = aw.shape

    def body(s_ref, w_ref, b_ref, o_ref):
        o_ref[0] = hdot(s_ref[...], w_ref[0]) + b_ref[0]

    return pl.pallas_call(
        body, name=name, grid=(nl,),
        in_specs=[pl.BlockSpec(s.shape, lambda l: (0, 0)), pl.BlockSpec((1, D, cw), lambda l: (l, 0, 0)),
                  pl.BlockSpec((1, 1, cw), lambda l: (l, 0, 0))],
        out_specs=pl.BlockSpec((1, s.shape[0], cw), lambda l: (l, 0, 0)),
        out_shape=jax.ShapeDtypeStruct((nl, s.shape[0], cw), F32), compiler_params=_cparams(("arbitrary",)),
    )(s, aw, ab)


def ada_bwd(name, s, aw, dmod):
    nl, _, cw = aw.shape

    def body(s_ref, w_ref, d_ref, dw_ref, ds_ref):
        dw_ref[0] = lax.dot_general(s_ref[...], d_ref[0], (((0,), (0,)), ((), ())), precision=HI,
                                    preferred_element_type=F32)
        ds_ref[0] = lax.dot_general(d_ref[0], w_ref[0], (((1,), (1,)), ((), ())), precision=HI,
                                    preferred_element_type=F32)

    return pl.pallas_call(
        body, name=name, grid=(nl,),
        in_specs=[pl.BlockSpec(s.shape, lambda l: (0, 0)), pl.BlockSpec((1, D, cw), lambda l: (l, 0, 0)),
                  pl.BlockSpec((1, s.shape[0], cw), lambda l: (l, 0, 0))],
        out_specs=[pl.BlockSpec((1, D, cw), lambda l: (l, 0, 0)), pl.BlockSpec((1, s.shape[0], D), lambda l: (l, 0, 0))],
        out_shape=[jax.ShapeDtypeStruct((nl, D, cw), F32), jax.ShapeDtypeStruct((nl, s.shape[0], D), F32)],
        compiler_params=_cparams(("arbitrary",)),
    )(s, aw, dmod)


def _place():
    return lax.axis_index("x"), lax.axis_index("y"), lax.axis_index("c")


def all_gather(name, x, in_vmem):
    r, c = x.shape

    def body(x_ref, out_ref, send_sems, recv_sems, local_sem):
        px, py, pc = _place()
        me, sibling = (px, py, pc), (px, py, 1 - pc)
        chips = [(1 - px, py), (px, 1 - py), (1 - px, 1 - py)]

        def rows(qx, qy, qc):
            return out_ref.at[pl.ds((4 * qx + 2 * qy + qc) * r, r), :]

        def copy(k, block, to, src=None):
            return pltpu.make_async_remote_copy(
                src_ref=rows(*block) if src is None else src, dst_ref=rows(*block),
                send_sem=send_sems.at[k], recv_sem=recv_sems.at[k], device_id=to, device_id_type=MESH)

        mine = pltpu.make_async_copy(x_ref, rows(*me), local_sem)
        mine.start()
        first = [copy(0, me, sibling, src=x_ref)]
        first += [copy(1 + j, me, (*chip, pc), src=x_ref) for j, chip in enumerate(chips)]
        for cp in first:
            cp.start()
        passed = [copy(4 + j, (*chip, pc), sibling) for j, chip in enumerate(chips)]
        for j, chip in enumerate(chips):
            copy(1 + j, (*chip, pc), me).wait_recv()
            passed[j].start()
        copy(0, sibling, me).wait_recv()
        for j, chip in enumerate(chips):
            copy(4 + j, (*chip, 1 - pc), me).wait_recv()
        for cp in first + passed:
            cp.wait_send()
        mine.wait()

    space = pltpu.VMEM if in_vmem else pl.ANY
    return pl.pallas_call(
        body, name=name, out_shape=jax.ShapeDtypeStruct((N_DEV * r, c), x.dtype),
        in_specs=[pl.BlockSpec(memory_space=space)], out_specs=pl.BlockSpec(memory_space=space),
        scratch_shapes=[pltpu.SemaphoreType.DMA((7,)), pltpu.SemaphoreType.DMA((7,)), pltpu.SemaphoreType.DMA],
        compiler_params=pltpu.CompilerParams(vmem_limit_bytes=VMEM_LIMIT_BYTES),
    )(x)


_HBM = pl.BlockSpec(memory_space=pltpu.HBM)
_SEM = pl.BlockSpec(memory_space=pltpu.SEMAPHORE)
_EFFECT = pltpu.SideEffectType.DATAFLOW_SIDE_EFFECTING


def _partner(k):
    px, py, pc = _place()
    q = (px ^ (k >> 2 & 1), py ^ (k >> 1 & 1), pc ^ (k & 1))
    return q, 4 * q[0] + 2 * q[1] + q[2]


def xchg_start(name, x, per_peer):
    r, c = x.shape[-2:]

    def body(x_ref, land_ref, send_sems, recv_sems, x_thru, land_thru, token):
        px, py, pc = _place()
        my = 4 * px + 2 * py + pc
        for k in range(1, N_DEV):
            q, qi = _partner(k)
            pltpu.make_async_remote_copy(
                src_ref=x_ref.at[qi] if per_peer else x_ref, dst_ref=land_ref.at[my],
                send_sem=send_sems.at[k - 1], recv_sem=recv_sems.at[k - 1], device_id=q, device_id_type=MESH).start()
        token[...] = jnp.zeros_like(token)

    land = lax.empty((N_DEV, r, c), x.dtype)
    return pl.pallas_call(
        body, name=name,
        out_shape=(pltpu.SemaphoreType.DMA((N_DEV - 1,)), pltpu.SemaphoreType.DMA((N_DEV - 1,)),
                   pltpu.HBM(x.shape, x.dtype), pltpu.HBM(land.shape, land.dtype), jax.ShapeDtypeStruct((8, 128), F32)),
        in_specs=(_HBM, _HBM), out_specs=(_SEM, _SEM, _HBM, _HBM, pl.BlockSpec(memory_space=pltpu.VMEM)),
        input_output_aliases={0: 2, 1: 3}, compiler_params=pltpu.CompilerParams(has_side_effects=_EFFECT),
    )(pltpu.with_memory_space_constraint(x, pltpu.HBM), pltpu.with_memory_space_constraint(land, pltpu.HBM))


def xchg_wait(name, send_sems, recv_sems, x_thru, land_thru, after, per_peer):
    def body(x_ref, land_ref, send_sems, recv_sems, after_ref, x_out, land_out):
        for k in range(1, N_DEV):
            q, qi = _partner(k)
            cp = pltpu.make_async_remote_copy(
                src_ref=x_ref.at[qi] if per_peer else x_ref, dst_ref=land_ref.at[qi],
                send_sem=send_sems.at[k - 1], recv_sem=recv_sems.at[k - 1], device_id=q, device_id_type=MESH)
            cp.wait_send()
            cp.wait_recv()

    return pl.pallas_call(
        body, name=name,
        out_shape=(pltpu.HBM(x_thru.shape, x_thru.dtype), pltpu.HBM(land_thru.shape, land_thru.dtype)),
        in_specs=(_HBM, _HBM, _SEM, _SEM, pl.BlockSpec(memory_space=pl.ANY)), out_specs=(_HBM, _HBM),
        input_output_aliases={0: 0, 1: 1}, compiler_params=pltpu.CompilerParams(has_side_effects=_EFFECT),
    )(x_thru, land_thru, send_sems, recv_sems, after)


def _tables(seq, nctx_rows):
    f32 = np.float32
    pos = np.arange(seq, dtype=f32)
    ret_inv = (1.0 / (ROPE_BASE ** np.linspace(0.0, 1.0, 16, dtype=f32))).astype(f32)
    ang = pos[:, None] * ret_inv
    rc, rs = np.cos(ang).astype(f32), np.sin(ang).astype(f32)
    rcos = np.tile(np.concatenate([rc, rc], 1), (1, 4))
    rsin = np.tile(np.concatenate([-rs, rs], 1), (1, 4))
    rows = np.repeat(np.arange(seq // 64, dtype=f32), 64)
    cols = np.tile(np.arange(64, dtype=f32), seq // 64)
    ax_inv = (ROPE_BASE ** (-np.arange(8, dtype=f32) / 8)).astype(f32)
    ra, ca = rows[:, None] * ax_inv, cols[:, None] * ax_inv
    one, zero = np.ones((seq, 64), f32), np.zeros((seq, 64), f32)
    mcos = np.concatenate([one, np.cos(ra), np.cos(ra), np.cos(ca), np.cos(ca), one[:, :32]], 1)
    msin = np.concatenate([zero, -np.sin(ra), np.sin(ra), -np.sin(ca), np.sin(ca), zero[:, :32]], 1)
    ident = lambda t, v: np.concatenate([np.full((nctx_rows, 128), v, f32), t.astype(f32)], 0)
    return [jnp.asarray(ident(rcos, 1.0)), jnp.asarray(ident(rsin, 0.0)),
            jnp.asarray(ident(mcos, 1.0)), jnp.asarray(ident(msin, 0.0))]


def _prep_layer(w, l):
    z = lambda *s: jnp.zeros(s, F32)
    p = {}
    win = _rows(w["w_in_t"][l], W_IN_SEGS)
    p["w_in_t"] = jnp.concatenate([win, jnp.zeros((P_PAD - D_IN, D), win.dtype)], axis=0)
    p["w_up_t"] = w["ffn_up_t"][l]
    p["w_down"] = w["ffn_down"][l]
    p["w_out"] = w["w_out"][l]
    p["wuq"] = _pad_heads(w["mla_w_uq_t"][l], 96)
    p["wuk"] = _pad_heads(w["mla_w_uk_t"][l], 64)
    p["wuv"] = w["mla_w_uv_t"][l]
    gw = w["gla_gate_w"][l]
    p["w2f"] = z(128, 128).at[0:16].set(gw[0])
    p["w2b"] = z(128, 128).at[16:32].set(gw[1])
    p["b2f"], p["b2b"] = w["gla_gate_b"][l][0:1], w["gla_gate_b"][l][1:2]
    lg = jax.nn.log_sigmoid(w["ret_decay"][l])
    p["retf"], p["retb"] = jnp.repeat(lg[0], 32)[None], jnp.repeat(lg[1], 32)[None]
    p["qg"], p["kvg"] = w["mla_q_norm_g"][l][None], w["mla_kv_norm_g"][l][None]
    p["gng"] = jnp.tile(w["gla_norm_g"][l], 4)[None]
    p["ln1g"], p["ln1b"] = w["ln1_g"][l][None], w["ln1_b"][l][None]
    p["ln2g"], p["ln2b"] = w["ln2_g"][l][None], w["ln2_b"][l][None]
    p["cw"] = jnp.concatenate([_cols(w["ffn_conv_w"][l], FF_SEGS), z(5, 2 * D_FF)], axis=0)
    p["cb"] = _cols(w["ffn_conv_b"][l], FF_SEGS)[None]
    e2 = np.zeros((128, 1024), np.float32)
    for h in range(8):
        e2[32 + np.arange(32), h * 128 + 64 + np.arange(32)] = 1.0
    p["e2"] = jnp.asarray(e2)
    return p


def _pre_ins(pa, tabs, p):
    row = lambda w, cb: ("row", pa, w, cb)
    return [row(128, 0), row(128, 6), row(128, 7), row(256, 6), row(128, 14), row(128, 15)] + \
           [("pos", t, 128, 0) for t in tabs] + \
           [("par", p[k], 0, 0) for k in ("w2f", "w2b", "b2f", "b2b", "retf", "retb", "qg", "kvg", "wuq", "wuk", "wuv", "e2")]


_PRE_OUTS = [(128, F32)] * 7 + [(1024, BF16), (1024, BF16), (512, BF16)]
_PRE_WANT = [(i, F32) for i in range(6)] + [(i, F32) for i in range(10, 21)]


def _post_ins(ogf, ogb, orf, orb, om, pa, x, mod, p):
    return [("row", ogf, 256, 0), ("row", ogb, 256, 0), ("row", orf, 256, 0), ("row", orb, 256, 0),
            ("row", om, 512, 0), ("row", pa, 256, 2), ("row", pa, 256, 5), ("row", x, D, 0), ("tile", mod, 0, 0),
            ("par", p["gng"], 0, 0), ("par", p["w_out"], 0, 0), ("par", p["ln1g"], 0, 0), ("par", p["ln1b"], 0, 0)]


def layer_fwd(l, x, mod, p, tabs, dims):
    nb, tps, tr, nch, nctx = dims
    n = x.shape[0]
    pa = mm("proj", x, p["w_in_t"], F32, P_PAD, tr, mod=mod, sel=(1, 0))
    gq, af, ab, arf, arb, rq, rk, qa, ka, va = tile_fwd("mix_pre", pre_fn, _pre_ins(pa, tabs, p), _PRE_OUTS, n, tr, tps)
    ogf, ogb, gstf, gstb = scan_fwd("gla_scan", (gq, 128, 0), (pa, 128, 1), (pa, 256, 1), af, ab, nb, nch, nctx)
    orf, orb, rstf, rstb = scan_fwd("ret_scan", (rq, 128, 0), (rk, 128, 0), (pa, 256, 4), arf, arb, nb, nch, nctx)
    om, lse = mla_fwd("mla_attn", qa, ka, va, nb, tps, tr, nctx * CHUNK)
    (x1,) = tile_fwd("mix_post", post_fn, _post_ins(ogf, ogb, orf, orb, om, pa, x, mod, p), [(D, F32)], n, tr, tps)
    u = mm("ffn_up", x1, p["w_up_t"], BF16, 2 * D_FF, tr, mod=mod, sel=(4, 3))
    f, x2, ucv = ffn2_fwd("ffn_down", u, p["cw"], p["cb"], p["w_down"], x1, mod, p["ln2g"], p["ln2b"], tr, tps)
    saved = dict(x=x, pa=pa, gq=gq, af=af, ab=ab, arf=arf, arb=arb, rq=rq, rk=rk, qa=qa, ka=ka, va=va,
                 ogf=ogf, ogb=ogb, gstf=gstf, gstb=gstb, orf=orf, orb=orb, rstf=rstf, rstb=rstb, om=om, lse=lse,
                 x1=x1, u=u, ucv=ucv, f=f)
    return x2, saved


def layer_bwd(l, dx2, s, mod, p, tabs, dims):
    nb, tps, tr, nch, nctx = dims
    n = dx2.shape[0]
    g = {}
    ln2_ins = [("row", s["x1"], D, 0), ("row", s["f"], D, 0), ("tile", mod, 0, 0),
               ("par", p["ln2g"], 0, 0), ("par", p["ln2b"], 0, 0)]
    dx1a, df, dmod_a, g["ln2g"], g["ln2b"] = tile_bwd(
        "ln2_bwd", ln2_fn, ln2_ins, [dx2], [(0, F32), (1, F32), (2, F32), (3, F32), (4, F32)], n, tr, tps)
    ducv, g["w_down"] = ffn2_bwd("ffn_down_bwd", s["ucv"], p["w_down"], df, tr)
    du, g["cw"], g["cb"] = conv_bwd("conv_bwd", ducv, s["u"], p["cw"], tr, tps)
    g["w_up_t"] = mm_tn("ffn_up_dw", du, s["x1"], D_FF, tr, mod=mod, sel=(4, 3))
    dx1, dmod_b = mm_modbwd("ffn_up_dx", du, p["w_up_t"], s["x1"], mod, dx1a, (4, 3), tr)

    post_ins = _post_ins(s["ogf"], s["ogb"], s["orf"], s["orb"], s["om"], s["pa"], s["x"], mod, p)
    want = [(0, F32), (2, F32), (4, F32), (5, F32), (6, F32), (7, F32), (8, F32), (9, F32), (10, F32), (11, F32), (12, F32)]
    dog, dor, dom, dgg, drg, dxa, dmod_c, g["gng"], g["w_out"], g["ln1g"], g["ln1b"] = tile_bwd(
        "mix_post_bwd", post_fn, post_ins, [dx1], want, n, tr, tps)
    dqa, dka, dva = mla_bwd("mla_attn_bwd", s["qa"], s["ka"], s["va"], s["om"], s["lse"], dom, nb, tps, tr,
                            nctx * CHUNK)
    pa = s["pa"]
    gdqf, gdkf, gdvf, gdaf, gdqb, gdkb, gdvb, gdab = scan_bwd(
        "gla_scan_bwd", (s["gq"], 128, 0), (pa, 128, 1), (pa, 256, 1), s["af"], s["ab"], s["gstf"], s["gstb"], dog,
        nb, nch, nctx)
    rdqf, rdkf, rdvf, rdaf, rdqb, rdkb, rdvb, rdab = scan_bwd(
        "ret_scan_bwd", (s["rq"], 128, 0), (s["rk"], 128, 0), (pa, 256, 4), s["arf"], s["arb"], s["rstf"], s["rstb"],
        dor, nb, nch, nctx)

    pre_ins = _pre_ins(pa, tabs, p)
    extra = [gdqf, gdqb, rdqf, rdqb, rdkf, rdkb, gdkf, gdkb, gdvf, gdvb, rdvf, rdvb, dgg, drg]
    kinds = [i[0] for i in pre_ins]
    widx = [w[0] for w in _PRE_WANT]
    npre = len(pre_ins)

    def body(*refs):
        vals = [_load(k, r) for k, r in zip(kinds, refs[:npre])]
        rd = lambda i: refs[npre + i][...].astype(F32)
        cots = (rd(0) + rd(1), rd(14), rd(15), rd(16), rd(17), rd(2) + rd(3), rd(4) + rd(5), rd(18), rd(19), rd(20))

        def f(*dv):
            full = list(vals)
            for i, v in zip(widx, dv):
                full[i] = v
            return tuple(pre_fn(*full))

        _, vjp = jax.vjp(f, *[vals[i] for i in widx])
        grads = vjp(cots)
        dgq, drq, drk, dcq, dckv, dmisc = grads[:6]
        dp = jnp.concatenate([dgq, rd(6) + rd(7), rd(8) + rd(9), rd(12), drq, drk, rd(10) + rd(11), rd(13),
                              dcq, dckv, dmisc], axis=1)
        outs = refs[npre + 21:]
        outs[0][...] = dp.astype(BF16)
        first = pl.program_id(0) == 0
        for r, gr in zip(outs[1:], grads[6:]):
            @pl.when(first)
            def _(r=r):
                r[...] = jnp.zeros_like(r)
            r[...] += gr

    cot_arrays = extra + [gdaf, gdab, rdaf, rdab, dqa, dka, dva]
    par_arrays = [pre_ins[i][1] for i in range(10, 21)]
    res = pl.pallas_call(
        body, name="mix_pre_bwd", grid=(n // tr,),
        in_specs=[_spec(k, a, w, cb, tr, tps) for k, a, w, cb in pre_ins]
        + [pl.BlockSpec((tr, c.shape[1]), lambda j: (j, 0)) for c in cot_arrays],
        out_specs=[pl.BlockSpec((tr, P_PAD), lambda j: (j, 0))] + [pl.BlockSpec(a.shape, lambda j: (0, 0)) for a in par_arrays],
        out_shape=[jax.ShapeDtypeStruct((n, P_PAD), BF16)] + [jax.ShapeDtypeStruct(a.shape, F32) for a in par_arrays],
        compiler_params=_cparams(("arbitrary",)),
    )(*[i[1] for i in pre_ins], *cot_arrays)
    dp = res[0]
    for k, v in zip(("w2f", "w2b", "b2f", "b2b", "retf", "retb", "qg", "kvg", "wuq", "wuk", "wuv"), res[1:]):
        g[k] = v
    g["w_in_t"] = mm_tn("proj_dw", dp, s["x"], P_PAD, tr, mod=mod, sel=(1, 0))
    dx, dmod_d = mm_modbwd("proj_dx", dp, p["w_in_t"], s["x"], mod, dxa, (1, 0), tr)
    return dx, dmod_a + dmod_b + dmod_c + dmod_d, g


def _unprep_grads(g, w, l):
    o = {}
    o["w_in_t"] = _rows(g["w_in_t"], W_IN_INV_SEGS)
    o["ffn_up_t"] = g["w_up_t"]
    o["ffn_down"] = g["w_down"]
    o["w_out"] = g["w_out"]
    o["mla_w_uq_t"] = _unpad_heads(g["wuq"], 96)
    o["mla_w_uk_t"] = _unpad_heads(g["wuk"], 64)
    o["mla_w_uv_t"] = g["wuv"]
    o["gla_gate_w"] = jnp.stack([g["w2f"][0:16], g["w2b"][16:32]])
    o["gla_gate_b"] = jnp.concatenate([g["b2f"], g["b2b"]], axis=0)
    dlg = jnp.stack([g["retf"].reshape(4, 32).sum(-1), g["retb"].reshape(4, 32).sum(-1)])
    o["ret_decay"] = dlg * jax.nn.sigmoid(-w["ret_decay"][l])
    o["mla_q_norm_g"], o["mla_kv_norm_g"] = g["qg"][0], g["kvg"][0]
    o["gla_norm_g"] = g["gng"].reshape(4, 64).sum(0)
    o["ln1_g"], o["ln1_b"], o["ln2_g"], o["ln2_b"] = g["ln1g"][0], g["ln1b"][0], g["ln2g"][0], g["ln2b"][0]
    o["ffn_conv_w"] = _cols(g["cw"][0:3], FF_SEGS)
    o["ffn_conv_b"] = _cols(g["cb"][0], FF_SEGS)
    return o


def local_step(xs, target, modtab, layer_weights, dims, grads_ready=None):
    nb, tps, tr, nch, nctx = dims
    tabs = _tables((tps - 1) * tr, tr)
    x = xs
    saved, preps, ws = [], [], []
    for l in range(DEPTH):
        w = layer_weights(l, x)
        p = _prep_layer(w, 0)
        x, s = layer_fwd(l, x, modtab[l], p, tabs, dims)
        saved.append(s)
        preps.append(p)
        ws.append(w)
    dy, lpart = loss_head("loss_head", x, target, nb, tps, tr)
    loss = jnp.sum(lpart[:, 0, 0])
    dx = dy
    dmods, grads = [None] * DEPTH, [None] * DEPTH
    tok = None
    for l in reversed(range(DEPTH)):
        mod = modtab[l] if tok is None else modtab[l] + tok
        dx, dmods[l], g = layer_bwd(l, dx, saved[l], mod, preps[l], tabs, dims)
        grads[l] = _unprep_grads(g, ws[l], 0)
        tok = grads_ready(l, grads) if grads_ready is not None else None
    return loss, dx, jnp.stack(dmods), grads


BIG = [("ffn_up", 2), ("ffn_down", 1), ("w_out", 1), ("w_in", 2), ("mla_w_uq", 2), ("mla_w_uk", 2), ("mla_w_uv", 2)]
SMALL = ["ada_b", "gla_gate_w", "gla_gate_b", "gla_norm_g", "ret_decay", "mla_q_norm_g", "mla_kv_norm_g",
         "ln1_g", "ln1_b", "ffn_conv_b", "ln2_g", "ln2_b"]
PACK_C = 1024


def _big_key(k, axis):
    return k + "_t" if axis == 2 else k


def sum8(name, g8):
    k, r, c = g8.shape
    rows = max(b for b in range(16, ADAM_MAX_ROWS + 1, 16) if r % b == 0)

    def body(g_ref, o_ref):
        g = g_ref[0].astype(F32)
        for i in range(1, k):
            g = g + g_ref[i].astype(F32)
        o_ref[...] = g

    return pl.pallas_call(
        body, name=name, grid=(r // rows,), in_specs=[pl.BlockSpec((k, rows, c), lambda i: (0, i, 0))],
        out_specs=pl.BlockSpec((rows, c), lambda i: (i, 0)), out_shape=jax.ShapeDtypeStruct((r, c), F32),
        compiler_params=_cparams(("parallel",)),
    )(g8)


def _pack(arrs, dtype):
    flat = jnp.concatenate([a.reshape(-1).astype(dtype) for a in arrs])
    pad = (-flat.shape[0]) % (8 * PACK_C)
    return jnp.concatenate([flat, jnp.zeros((pad,), dtype)]).reshape(-1, PACK_C)


def _unpack(flat2d, shapes):
    flat = flat2d.reshape(-1)
    out, off = [], 0
    for s in shapes:
        sz = int(np.prod(s))
        out.append(flat[off:off + sz].reshape(s))
        off += sz
    return out


def _ff_order(b):
    return jnp.concatenate([b[0:2], b[4:6], b[2:4], b[6:8]], axis=0)


def _tile_pad(a):
    pad = (-a.shape[-2]) % HALO
    return a if pad == 0 else jnp.concatenate([a, jnp.zeros(a.shape[:-2] + (pad, a.shape[-1]), a.dtype)], axis=-2)


def kernel(x, c, ctx, c_ctx, ada_w, ada_b, w_in, gla_gate_w, gla_gate_b, gla_norm_g, ret_decay, mla_q_norm_g, mla_kv_norm_g, mla_w_uq, mla_w_uk, mla_w_uv, w_out, ln1_g, ln1_b, ffn_up, ffn_conv_w, ffn_conv_b, ffn_down, ln2_g, ln2_b, loss_target, m_c_ctx, m_ada_w, m_ada_b, m_w_in, m_gla_gate_w, m_gla_gate_b, m_gla_norm_g, m_ret_decay, m_mla_q_norm_g, m_mla_kv_norm_g, m_mla_w_uq, m_mla_w_uk, m_mla_w_uv, m_w_out, m_ln1_g, m_ln1_b, m_ffn_up, m_ffn_conv_w, m_ffn_conv_b, m_ffn_down, m_ln2_g, m_ln2_b, v_c_ctx, v_ada_w, v_ada_b, v_w_in, v_gla_gate_w, v_gla_gate_b, v_gla_norm_g, v_ret_decay, v_mla_q_norm_g, v_mla_kv_norm_g, v_mla_w_uq, v_mla_w_uk, v_mla_w_uv, v_w_out, v_ln1_g, v_ln1_b, v_ffn_up, v_ffn_conv_w, v_ffn_conv_b, v_ffn_down, v_ln2_g, v_ln2_b):
    names = ["c_ctx", "ada_w", "ada_b", "w_in", "gla_gate_w", "gla_gate_b", "gla_norm_g", "ret_decay", "mla_q_norm_g",
             "mla_kv_norm_g", "mla_w_uq", "mla_w_uk", "mla_w_uv", "w_out", "ln1_g", "ln1_b", "ffn_up", "ffn_conv_w",
             "ffn_conv_b", "ffn_down", "ln2_g", "ln2_b"]
    loc = locals()
    W = {k: loc[k] for k in names}
    M = {k: loc["m_" + k] for k in names}
    V = {k: loc["v_" + k] for k in names}

    nb, seq, _ = x.shape
    tr = ctx.shape[1]
    tps = 1 + seq // tr
    t = tps * tr
    n = nb * t
    nt = nb * tps
    dims = (nb, tps, tr, t // CHUNK, tr // CHUNK)
    px, py, pc = _place()
    me = 4 * px + 2 * py + pc
    ncol = ada_w.shape[2]

    cw_loc = ffn_conv_w.reshape(-1)
    g1 = jnp.concatenate([c.reshape(-1), cw_loc])
    g1 = jnp.concatenate([g1, jnp.zeros(((-g1.shape[0]) % (8 * PACK_C),), F32)]).reshape(-1, PACK_C)
    r1 = g1.shape[0]
    g1a = all_gather("gather_cond", g1, True).reshape(N_DEV, -1)
    c_all = g1a[:, :nb * D].reshape(N_DEV * nb, D)
    cw_all = g1a[:, nb * D:nb * D + cw_loc.shape[0]].reshape(N_DEV, DEPTH, 3, -1).transpose(1, 2, 0, 3).reshape(DEPTH, 3, -1)

    first, rest = [0], list(range(1, DEPTH))
    row_form = {k: (jnp.swapaxes(W[k], 1, 2) if ax == 2 else W[k]).astype(BF16) for k, ax in BIG}

    def part_rows(k):
        rows = int(np.prod(W[k].shape[1:])) // PACK_C
        return rows, -(-rows // HALO) * HALO

    def pack_rows(ls):
        return jnp.concatenate([_tile_pad(row_form[k][l].reshape(-1, PACK_C)) for k, _ in BIG for l in ls], axis=0)

    def whole_weights(wall, ls):
        out, off = {}, 0
        for k, ax in BIG:
            rows, padded = part_rows(k)
            _, r, c = row_form[k].shape
            order = _ff_order if k == "ffn_up" else (lambda b: b)
            out[_big_key(k, ax)] = [order(wall[:, off + i * padded:off + i * padded + rows]).reshape(1, N_DEV * r, c)
                                    for i in range(len(ls))]
            off += len(ls) * padded
        return out

    pack0 = pack_rows(first)
    whole0 = whole_weights(all_gather("gather_weights0", pack0, False).reshape(N_DEV, -1, PACK_C), first)
    pack_rest = pack_rows(rest)
    wsend, wrecv, wsrc, wland, wtok = xchg_start("gather_weights_start", pack_rest, False)
    small_w = {k: W[k] for k in SMALL[1:]}
    small_w["ffn_conv_w"] = cw_all
    later = {}

    def layer_weights(l, xin):
        if l >= 1 and not later:
            src, land = xchg_wait("gather_weights_wait", wsend, wrecv, wsrc, wland, xin, False)
            later.update(whole_weights(lax.dynamic_update_slice(land, src[None], (me, 0, 0)), rest))
        big = {k: v[0] for k, v in whole0.items()} if l == 0 else {k: v[l - 1] for k, v in later.items()}
        return {**big, **{k: v[l:l + 1] for k, v in small_w.items()}}

    srows = 40
    s_in = jnp.concatenate([c_all, c_ctx[None], jnp.zeros((srows - N_DEV * nb - 1, D), F32)], axis=0)
    s_act = _silu(s_in)
    ab_loc = lax.dynamic_slice_in_dim(ada_b, me * ncol, ncol, axis=1)[:, None, :]
    mod_part = ada_fwd("ada_fwd", s_act, ada_w, ab_loc)
    mod_all = all_gather("gather_mod", mod_part.reshape(-1, ncol), True).reshape(N_DEV, DEPTH, srows, ncol)
    mod_rows = mod_all.transpose(1, 2, 0, 3).reshape(DEPTH, srows, N_DEV * ncol)
    mod_l = lax.dynamic_slice_in_dim(mod_rows, me * nb, nb, axis=1).reshape(DEPTH, nb, 6, D)
    mod_c = mod_rows[:, N_DEV * nb].reshape(DEPTH, 1, 6, D)
    tile_is_ctx = (jnp.arange(tps) == 0)[None, None, :, None, None]
    modtab = jnp.where(tile_is_ctx, mod_c[:, :, None], mod_l[:, :, None])
    modtab = jnp.concatenate([modtab, jnp.zeros((DEPTH, nb, tps, 2, D), F32)], axis=3).reshape(DEPTH, nt, 8, D)
    modtab = modtab + wtok[0, 0]

    def grad_blocks(grads, ls):
        def blocks(k, ax, l):
            b = grads[l][_big_key(k, ax)].astype(BF16).reshape(N_DEV, -1, PACK_C)
            return _ff_order(b) if k == "ffn_up" else b

        return jnp.concatenate([_tile_pad(blocks(k, ax, l)) for k, ax in BIG for l in ls], axis=1)

    early = {}

    def grads_ready(l, grads):
        if l != 1:
            return None
        early["sems"] = xchg_start("grad_exchange_start", grad_blocks(grads, rest), True)
        return early["sems"][4][0, 0]

    xs = jnp.concatenate([ctx, x], axis=1).reshape(n, D)
    loss_loc, dxs, dmodtab, grads = local_step(xs, loss_target.reshape(nb * seq, D), modtab, layer_weights, dims,
                                               grads_ready)
    loss = lax.psum(loss_loc, ("x", "y", "c"))
    grad_x = dxs.reshape(nb, t, D)[:, tr:]
    gl = {k: jnp.stack([g[k] for g in grads]) for k in grads[0] if k in SMALL or k == "ffn_conv_w"}

    dm = dmodtab.reshape(DEPTH, nb, tps, 8, D)[:, :, :, :6]
    dmod_l = dm[:, :, 1:].sum(2).reshape(DEPTH, nb, 6 * D)
    dmod_c = dm[:, :, 0].sum(1).reshape(DEPTH, 1, 6 * D)
    gl["ada_b"] = dmod_l.sum(1) + dmod_c[:, 0]
    small_list = [gl[k] for k in SMALL] + [gl["ffn_conv_w"]]
    small_shapes = [a.shape for a in small_list]
    spack = _pack(small_list + [jnp.concatenate([dmod_l, dmod_c], axis=1)], F32)
    rs = spack.shape[0]
    sall = all_gather("gather_small_grads", spack, True).reshape(N_DEV, rs, PACK_C)
    nsmall = sum(int(np.prod(s)) for s in small_shapes)
    dmo = sall.reshape(N_DEV, -1)[:, nsmall:nsmall + DEPTH * (nb + 1) * 6 * D].reshape(N_DEV, DEPTH, nb + 1, 6 * D)
    dl_all = dmo[:, :, :nb].transpose(1, 0, 2, 3).reshape(DEPTH, N_DEV * nb, 6 * D)
    dc_all = dmo[:, :, nb].sum(0)[:, None]
    dmod_rows = jnp.concatenate([dl_all, dc_all, jnp.zeros((DEPTH, srows - N_DEV * nb - 1, 6 * D), F32)], axis=1)
    dmod_loc = lax.dynamic_slice_in_dim(dmod_rows.reshape(DEPTH, srows, N_DEV, ncol), me, 1, axis=2)[:, :, 0]
    d_ada_w, d_s = ada_bwd("ada_bwd", s_act, ada_w, dmod_loc)
    sg = jax.nn.sigmoid(c_ctx)
    dcc = d_s[:, N_DEV * nb].sum(0) * (sg * (1.0 + c_ctx * (1.0 - sg)))
    ccp = jnp.concatenate([dcc[None], jnp.zeros((7, D), F32)], axis=0)
    ccall = all_gather("gather_cctx", ccp, True).reshape(N_DEV, 8, D)

    esend, erecv, esrc, eland, _ = early["sems"]
    def landed(tag, sems, src, land, after):
        src, land = xchg_wait(tag + "_wait", sems[0], sems[1], src, land, after, True)
        mine = lax.dynamic_slice_in_dim(src, me, 1, axis=0)
        return sum8(tag + "_sum", lax.dynamic_update_slice(land, mine, (me, 0, 0)))

    fblocks, ccall = lax.optimization_barrier((grad_blocks(grads, first), ccall))
    fsend, frecv, fsrc, fland, _ = xchg_start("grad_exchange0_start", fblocks, True)
    gsum_rest = landed("grad_exchange", (esend, erecv), esrc, eland, ccall)
    res = {}

    def update2d(tag, k, g):
        last = W[k].shape[-1]
        outs = adamw(tag, W[k].reshape(-1, last), M[k].reshape(-1, last), V[k].reshape(-1, last),
                     g.reshape(1, -1, last))
        res[k] = [a.reshape(W[k].shape) for a in outs]

    def update(tag, keys, g8):
        outs = adamw(tag, _pack([W[k] for k in keys], F32), _pack([M[k] for k in keys], F32),
                     _pack([V[k] for k in keys], F32), g8)
        for i, arr in enumerate(outs):
            for k, a in zip(keys, _unpack(arr, [W[k].shape for k in keys])):
                res.setdefault(k, [None] * 4)[i] = a

    nrep = sum(int(np.prod(W[k].shape)) for k in SMALL)
    sflat = sall.reshape(N_DEV, -1)
    def pack8(a):
        a = a.reshape(N_DEV, -1)
        pad = (-a.shape[1]) % (8 * PACK_C)
        return jnp.concatenate([a, jnp.zeros((N_DEV, pad), F32)], axis=1).reshape(N_DEV, -1, PACK_C)

    update("adamw_small", SMALL, pack8(sflat[:, :nrep]))
    ncw = ffn_conv_w.shape[2]
    cw8 = sflat[:, nrep:nsmall].reshape(N_DEV, DEPTH, 3, N_DEV * ncw)
    cw8 = lax.dynamic_slice_in_dim(cw8, me * ncw, ncw, axis=3)
    update("adamw_conv", ["ffn_conv_w"], pack8(cw8))
    update2d("adamw_ada", "ada_w", d_ada_w)
    update("adamw_cctx", ["c_ctx"], ccall)

    gsum_first = landed("grad_exchange0", (fsend, frecv), fsrc, fland, res["ada_w"][1])
    off0, off1 = 0, 0
    for k, ax in BIG:
        rows, padded = part_rows(k)
        _, r, c = row_form[k].shape
        parts = [gsum_first[off0:off0 + rows]] + [gsum_rest[off1 + i * padded:off1 + i * padded + rows]
                                                  for i in range(len(rest))]
        g = jnp.stack([p.reshape(r, c) for p in parts])
        update2d("adamw_" + k, k, jnp.swapaxes(g, 1, 2) if ax == 2 else g)
        off0 += padded
        off1 += len(rest) * padded

    out = [loss, grad_x]
    for i in range(4):
        out += [res[k][i] for k in names]
    return tuple(out)
```
